```python
import math
import jax, jax.numpy as jnp
from jax import lax
import numpy as np

D_MODEL = 1024
BATCH = 8
SEQ = 2048
DEPTH = 4
DEC_BATCH = 128
DEC_SEQ = 8
PAST_LEN = 16384
PAGE_SIZE = 128

MIX_DIM = 2 * D_MODEL
CONV_WIDTH = 4
SSD_DIM = D_MODEL
SSD_HEADDIM = 64
SSD_HEADS = SSD_DIM // SSD_HEADDIM
SSD_GROUPS = 4
SSD_HPG = SSD_HEADS // SSD_GROUPS
SSD_STATE = 128
SSD_CONV_DIM = SSD_DIM + 2 * SSD_GROUPS * SSD_STATE
SSD_CHUNK = 128
LRU_DIM = D_MODEL // 2
LRU_BLOCKS = 8
LRU_BLOCK_DIM = LRU_DIM // LRU_BLOCKS
LRU_C = 8.0
S5_DIM = D_MODEL // 2
S5_GROUP = 16
S5_NGROUPS = S5_DIM // S5_GROUP
S5_STATE = 64
EPS = 1e-6

OFF_XBC = SSD_DIM
OFF_DT = OFF_XBC + SSD_CONV_DIM
OFF_LRU = OFF_DT + SSD_HEADS
OFF_LRU_G = OFF_LRU + LRU_DIM
OFF_S5 = OFF_LRU_G + LRU_DIM
OFF_S5_G = OFF_S5 + S5_DIM
IN_DIM = OFF_S5_G + S5_DIM
SPLITS = (OFF_XBC, OFF_DT, OFF_LRU, OFF_LRU_G, OFF_S5, OFF_S5_G)

kernel_name = "hymba_ssd_rglru_s5_step"


def rmsnorm(x, g):
    xf = x.astype(jnp.float32)
    return xf * lax.rsqrt(jnp.mean(xf * xf, axis=-1, keepdims=True) + EPS) * g


def causal_conv(x, prev, w, b):
    l = x.shape[1]
    xp = jnp.concatenate([prev, x], axis=1)
    y = b + sum(w[k] * xp[:, k:k + l] for k in range(CONV_WIDTH))
    return y, xp[:, l:]


def ssd_scan(x, dt, a, bmat, cmat, h0):
    bsz, l = x.shape[:2]
    q = l if l <= SSD_CHUNK else math.gcd(l, SSD_CHUNK)
    nc = l // q
    xc = x.reshape(bsz, nc, q, SSD_GROUPS, SSD_HPG, SSD_HEADDIM)
    dtc = dt.reshape(bsz, nc, q, SSD_GROUPS, SSD_HPG)
    bc = bmat.reshape(bsz, nc, q, SSD_GROUPS, SSD_STATE)
    cc = cmat.reshape(bsz, nc, q, SSD_GROUPS, SSD_STATE)
    acum = jnp.moveaxis(jnp.cumsum(dtc * a.reshape(SSD_GROUPS, SSD_HPG), axis=2), 2, -1)
    dt_t = jnp.moveaxis(dtc, 2, -1)
    diff = acum[..., :, None] - acum[..., None, :]
    causal = jnp.tril(jnp.ones((q, q), dtype=bool))
    decay = jnp.where(causal, jnp.exp(jnp.where(causal, diff, 0.0)), 0.0)
    scores = jnp.einsum('bcqgn,bcsgn->bcgqs', cc, bc)
    y_diag = jnp.einsum('bcgqs,bcgrqs,bcgrs,bcsgrp->bcqgrp', scores, decay, dt_t, xc)
    to_end = jnp.exp(acum[..., -1:] - acum)
    chunk_states = jnp.einsum('bcsgn,bcgrs,bcsgrp->bcgrpn', bc, to_end * dt_t, xc)
    chunk_decay = jnp.exp(acum[..., -1])

    def step(h, inp):
        dec, st = inp
        return dec[..., None, None] * h + st, h

    h_last, h_prev = lax.scan(
        step, h0.reshape(bsz, SSD_GROUPS, SSD_HPG, SSD_HEADDIM, SSD_STATE),
        (jnp.moveaxis(chunk_decay, 1, 0), jnp.moveaxis(chunk_states, 1, 0)))
    h_prev = jnp.moveaxis(h_prev, 0, 1)
    y_off = jnp.einsum('bcqgn,bcgrpn,bcgrq->bcqgrp', cc, h_prev, jnp.exp(acum))
    y = (y_diag + y_off).reshape(bsz, l, SSD_HEADS, SSD_HEADDIM)
    return y, h_last.reshape(bsz, SSD_HEADS, SSD_HEADDIM, SSD_STATE)


def linear_scan(a, b, h0):
    b = b.at[:, 0].add(a[:, 0] * h0)

    def combine(e1, e2):
        a1, b1 = e1
        a2, b2 = e2
        return a1 * a2, a2 * b1 + b2

    _, h = lax.associative_scan(combine, (a, b), axis=1)
    return h


def complex_linear_scan(ar, ai, br, bi, h0r, h0i):
    br = br.at[:, 0].add(ar[:, 0] * h0r - ai[:, 0] * h0i)
    bi = bi.at[:, 0].add(ar[:, 0] * h0i + ai[:, 0] * h0r)

    def combine(e1, e2):
        ar1, ai1, br1, bi1 = e1
        ar2, ai2, br2, bi2 = e2
        return (ar1 * ar2 - ai1 * ai2, ar1 * ai2 + ai1 * ar2,
                ar2 * br1 - ai2 * bi1 + br2, ar2 * bi1 + ai2 * br1 + bi2)

    _, _, hr, hi = lax.associative_scan(combine, (ar, ai, br, bi), axis=1)
    return hr, hi


def s5_mixer(u, lam_re, lam_im, log_dt, b_re, b_im, c_re, c_im, d, h0r, h0i):
    bsz, l, _ = u.shape
    ug = u.reshape(bsz, l, S5_NGROUPS, S5_GROUP)
    delta = jnp.exp(log_dt)[:, None]
    mag = jnp.exp(lam_re * delta)
    abar_re = mag * jnp.cos(lam_im * delta)
    abar_im = mag * jnp.sin(lam_im * delta)
    denom = lam_re * lam_re + lam_im * lam_im
    nr = abar_re - 1.0
    ni = abar_im
    coef_re = (nr * lam_re + ni * lam_im) / denom
    coef_im = (ni * lam_re - nr * lam_im) / denom
    bbar_re = coef_re[..., None] * b_re - coef_im[..., None] * b_im
    bbar_im = coef_re[..., None] * b_im + coef_im[..., None] * b_re
    bu_re = jnp.einsum('blgh,gph->blgp', ug, bbar_re)
    bu_im = jnp.einsum('blgh,gph->blgp', ug, bbar_im)
    ar = jnp.broadcast_to(abar_re, bu_re.shape)
    ai = jnp.broadcast_to(abar_im, bu_re.shape)
    hr, hi = complex_linear_scan(ar, ai, bu_re, bu_im, h0r, h0i)
    y = jnp.einsum('blgp,ghp->blgh', hr, c_re) - jnp.einsum('blgp,ghp->blgh', hi, c_im)
    y = y.reshape(bsz, l, S5_DIM) + d * u
    return y, hr[:, -1], hi[:, -1]


def mixer_layer(x, states, p):
    ssd_h0, ssd_conv0, lru_h0, lru_conv0, s5_h0r, s5_h0i = states
    bsz, l, _ = x.shape
    h = rmsnorm(x, p['norm_g'])
    proj = h @ p['w_in']
    z, xbc, dt_raw, lru_x, lru_gate, s5_u, s5_gate = jnp.split(proj, SPLITS, axis=-1)

    xbc, ssd_conv_new = causal_conv(xbc, ssd_conv0, p['ssd_conv_w'], p['ssd_conv_b'])
    xbc = jax.nn.silu(xbc)
    xs, bm, cm = jnp.split(xbc, (SSD_DIM, SSD_DIM + SSD_GROUPS * SSD_STATE), axis=-1)
    dt = jax.nn.softplus(dt_raw + p['ssd_dt_bias'])
    a = -jnp.exp(p['ssd_a_log'])
    xh = xs.reshape(bsz, l, SSD_HEADS, SSD_HEADDIM)
    y, ssd_h = ssd_scan(xh, dt, a, bm.reshape(bsz, l, SSD_GROUPS, SSD_STATE),
                        cm.reshape(bsz, l, SSD_GROUPS, SSD_STATE), ssd_h0)
    y = y + p['ssd_d'][:, None] * xh
    y_ssd = rmsnorm(y.reshape(bsz, l, SSD_DIM) * jax.nn.silu(z), p['ssd_norm_g'])

    xr, lru_conv_new = causal_conv(lru_x, lru_conv0, p['lru_conv_w'], p['lru_conv_b'])
    xb = xr.reshape(bsz, l, LRU_BLOCKS, LRU_BLOCK_DIM)
    r = jax.nn.sigmoid(jnp.einsum('blki,kij->blkj', xb, p['lru_wa']).reshape(bsz, l, LRU_DIM) + p['lru_ba'])
    gi = jax.nn.sigmoid(jnp.einsum('blki,kij->blkj', xb, p['lru_wx']).reshape(bsz, l, LRU_DIM) + p['lru_bx'])
    log_a = -LRU_C * r * jax.nn.softplus(-p['lru_lambda'])
    a_t = jnp.exp(log_a)
    gain = jnp.sqrt(jnp.maximum(-jnp.expm1(2.0 * log_a), 0.0))
    hs = linear_scan(a_t, gain * gi * xr, lru_h0)
    y_lru = hs * jax.nn.silu(lru_gate)

    ys5, s5_hr, s5_hi = s5_mixer(s5_u, p['s5_lambda_re'], p['s5_lambda_im'], p['s5_log_dt'],
                                 p['s5_b_re'], p['s5_b_im'], p['s5_c_re'], p['s5_c_im'],
                                 p['s5_d'], s5_h0r, s5_h0i)
    ys5 = jax.nn.gelu(ys5)
    ys5 = ys5 * jax.nn.sigmoid(ys5 @ p['s5_glu_w'] + p['s5_glu_b'])
    y_s5 = ys5 * jax.nn.silu(s5_gate)

    out = jnp.concatenate([y_ssd, y_lru, y_s5], axis=-1) @ p['w_out']
    return x + out, (ssd_h, ssd_conv_new, hs[:, -1], lru_conv_new, s5_hr, s5_hi)


def setup_inputs(seed: int = 0) -> dict:
    key = jax.random.key(seed)
    ks = iter(jax.random.split(key, 48))
    f32 = jnp.float32

    def nrm(shape, scale):
        return scale * jax.random.normal(next(ks), shape, f32)

    def uni(shape, lo, hi):
        return jax.random.uniform(next(ks), shape, f32, lo, hi)

    x_prompt = nrm((BATCH, SEQ, D_MODEL), 1.0)
    x_sample = nrm((DEC_BATCH, DEC_SEQ, D_MODEL), 1.0)
    state_ssd = nrm((DEPTH, DEC_BATCH, SSD_HEADS, SSD_HEADDIM, SSD_STATE), 0.1)
    state_ssd_conv = nrm((DEPTH, DEC_BATCH, CONV_WIDTH - 1, SSD_CONV_DIM), 1.0)
    state_lru = nrm((DEPTH, DEC_BATCH, LRU_DIM), 0.5)
    state_lru_conv = nrm((DEPTH, DEC_BATCH, CONV_WIDTH - 1, LRU_DIM), 1.0)
    state_s5_re = nrm((DEPTH, DEC_BATCH, S5_NGROUPS, S5_STATE), 0.5)
    state_s5_im = nrm((DEPTH, DEC_BATCH, S5_NGROUPS, S5_STATE), 0.5)

    norm_g = 1.0 + nrm((DEPTH, D_MODEL), 0.02)
    w_in = nrm((DEPTH, D_MODEL, IN_DIM), D_MODEL ** -0.5)
    ssd_conv_w = nrm((DEPTH, CONV_WIDTH, SSD_CONV_DIM), CONV_WIDTH ** -0.5)
    ssd_conv_b = nrm((DEPTH, SSD_CONV_DIM), 0.02)
    dt0 = jnp.exp(uni((DEPTH, SSD_HEADS), math.log(1e-3), math.log(1e-1)))
    ssd_dt_bias = dt0 + jnp.log(-jnp.expm1(-dt0))
    ssd_a_log = jnp.log(uni((DEPTH, SSD_HEADS), 1.0, 16.0))
    ssd_d = 1.0 + nrm((DEPTH, SSD_HEADS), 0.02)
    ssd_norm_g = 1.0 + nrm((DEPTH, SSD_DIM), 0.02)
    lru_conv_w = nrm((DEPTH, CONV_WIDTH, LRU_DIM), CONV_WIDTH ** -0.5)
    lru_conv_b = nrm((DEPTH, LRU_DIM), 0.02)
    lru_wa = nrm((DEPTH, LRU_BLOCKS, LRU_BLOCK_DIM, LRU_BLOCK_DIM), LRU_BLOCK_DIM ** -0.5)
    lru_ba = nrm((DEPTH, LRU_DIM), 0.02)
    lru_wx = nrm((DEPTH, LRU_BLOCKS, LRU_BLOCK_DIM, LRU_BLOCK_DIM), LRU_BLOCK_DIM ** -0.5)
    lru_bx = nrm((DEPTH, LRU_DIM), 0.02)
    u_a = uni((DEPTH, LRU_DIM), 0.9, 0.999)
    lru_lambda = jnp.log(u_a) - jnp.log1p(-u_a)
    s5_lambda_re = -0.5 + nrm((DEPTH, S5_NGROUPS, S5_STATE), 0.01)
    s5_lambda_im = jnp.pi * jnp.arange(S5_STATE, dtype=f32) + nrm((DEPTH, S5_NGROUPS, S5_STATE), 0.01)
    s5_log_dt = uni((DEPTH, S5_NGROUPS), math.log(1e-3), math.log(1e-1))
    s5_b_re = nrm((DEPTH, S5_NGROUPS, S5_STATE, S5_GROUP), (2 * S5_GROUP) ** -0.5)
    s5_b_im = nrm((DEPTH, S5_NGROUPS, S5_STATE, S5_GROUP), (2 * S5_GROUP) ** -0.5)
    s5_c_re = nrm((DEPTH, S5_NGROUPS, S5_GROUP, S5_STATE), S5_STATE ** -0.5)
    s5_c_im = nrm((DEPTH, S5_NGROUPS, S5_GROUP, S5_STATE), S5_STATE ** -0.5)
    s5_d = nrm((DEPTH, S5_DIM), 1.0)
    s5_glu_w = nrm((DEPTH, S5_DIM, S5_DIM), S5_DIM ** -0.5)
    s5_glu_b = nrm((DEPTH, S5_DIM), 0.02)
    w_out = nrm((DEPTH, MIX_DIM, D_MODEL), MIX_DIM ** -0.5)
    final_norm_g = 1.0 + nrm((D_MODEL,), 0.02)
    return {
        "x_prompt": x_prompt, "x_sample": x_sample,
        "state_ssd": state_ssd, "state_ssd_conv": state_ssd_conv,
        "state_lru": state_lru, "state_lru_conv": state_lru_conv,
        "state_s5_re": state_s5_re, "state_s5_im": state_s5_im,
        "norm_g": norm_g, "w_in": w_in,
        "ssd_conv_w": ssd_conv_w, "ssd_conv_b": ssd_conv_b, "ssd_dt_bias": ssd_dt_bias,
        "ssd_a_log": ssd_a_log, "ssd_d": ssd_d, "ssd_norm_g": ssd_norm_g,
        "lru_conv_w": lru_conv_w, "lru_conv_b": lru_conv_b, "lru_wa": lru_wa, "lru_ba": lru_ba,
        "lru_wx": lru_wx, "lru_bx": lru_bx, "lru_lambda": lru_lambda,
        "s5_lambda_re": s5_lambda_re, "s5_lambda_im": s5_lambda_im, "s5_log_dt": s5_log_dt,
        "s5_b_re": s5_b_re, "s5_b_im": s5_b_im, "s5_c_re": s5_c_re, "s5_c_im": s5_c_im,
        "s5_d": s5_d, "s5_glu_w": s5_glu_w, "s5_glu_b": s5_glu_b,
        "w_out": w_out, "final_norm_g": final_norm_g,
    }


def reference(x_prompt, x_sample, state_ssd, state_ssd_conv, state_lru, state_lru_conv,
              state_s5_re, state_s5_im, norm_g, w_in, ssd_conv_w, ssd_conv_b, ssd_dt_bias,
              ssd_a_log, ssd_d, ssd_norm_g, lru_conv_w, lru_conv_b, lru_wa, lru_ba, lru_wx,
              lru_bx, lru_lambda, s5_lambda_re, s5_lambda_im, s5_log_dt, s5_b_re, s5_b_im,
              s5_c_re, s5_c_im, s5_d, s5_glu_w, s5_glu_b, w_out, final_norm_g):
    f32 = jnp.float32
    bp = x_prompt.shape[0]
    xp = x_prompt.astype(f32)
    xs = x_sample.astype(f32)
    zero_states = (
        jnp.zeros((bp, SSD_HEADS, SSD_HEADDIM, SSD_STATE), f32),
        jnp.zeros((bp, CONV_WIDTH - 1, SSD_CONV_DIM), f32),
        jnp.zeros((bp, LRU_DIM), f32),
        jnp.zeros((bp, CONV_WIDTH - 1, LRU_DIM), f32),
        jnp.zeros((bp, S5_NGROUPS, S5_STATE), f32),
        jnp.zeros((bp, S5_NGROUPS, S5_STATE), f32),
    )
    new_p = ([], [], [], [], [], [])
    new_s = ([], [], [], [], [], [])
    for i in range(DEPTH):
        layer = {
            'norm_g': norm_g[i], 'w_in': w_in[i],
            'ssd_conv_w': ssd_conv_w[i], 'ssd_conv_b': ssd_conv_b[i], 'ssd_dt_bias': ssd_dt_bias[i],
            'ssd_a_log': ssd_a_log[i], 'ssd_d': ssd_d[i], 'ssd_norm_g': ssd_norm_g[i],
            'lru_conv_w': lru_conv_w[i], 'lru_conv_b': lru_conv_b[i], 'lru_wa': lru_wa[i],
            'lru_ba': lru_ba[i], 'lru_wx': lru_wx[i], 'lru_bx': lru_bx[i], 'lru_lambda': lru_lambda[i],
            's5_lambda_re': s5_lambda_re[i], 's5_lambda_im': s5_lambda_im[i], 's5_log_dt': s5_log_dt[i],
            's5_b_re': s5_b_re[i], 's5_b_im': s5_b_im[i], 's5_c_re': s5_c_re[i], 's5_c_im': s5_c_im[i],
            's5_d': s5_d[i], 's5_glu_w': s5_glu_w[i], 's5_glu_b': s5_glu_b[i], 'w_out': w_out[i],
        }
        p = {k: v.astype(f32) for k, v in layer.items()}
        sample_states = (state_ssd[i].astype(f32), state_ssd_conv[i].astype(f32),
                         state_lru[i].astype(f32), state_lru_conv[i].astype(f32),
                         state_s5_re[i].astype(f32), state_s5_im[i].astype(f32))
        xp, sp = mixer_layer(xp, zero_states, p)
        xs, ss = mixer_layer(xs, sample_states, p)
        for j in range(6):
            new_p[j].append(sp[j])
            new_s[j].append(ss[j])
    fg = final_norm_g.astype(f32)
    y_prompt = rmsnorm(xp, fg).astype(x_prompt.dtype)
    y_sample = rmsnorm(xs, fg).astype(x_sample.dtype)
    ssd_p = jnp.stack(new_p[0]).astype(state_ssd.dtype)
    ssd_s = jnp.stack(new_s[0]).astype(state_ssd.dtype)
    ssd_conv_p = jnp.stack(new_p[1]).astype(state_ssd_conv.dtype)
    ssd_conv_s = jnp.stack(new_s[1]).astype(state_ssd_conv.dtype)
    lru_p = jnp.stack(new_p[2]).astype(state_lru.dtype)
    lru_s = jnp.stack(new_s[2]).astype(state_lru.dtype)
    lru_conv_p = jnp.stack(new_p[3]).astype(state_lru_conv.dtype)
    lru_conv_s = jnp.stack(new_s[3]).astype(state_lru_conv.dtype)
    s5_re_p = jnp.stack(new_p[4]).astype(state_s5_re.dtype)
    s5_re_s = jnp.stack(new_s[4]).astype(state_s5_re.dtype)
    s5_im_p = jnp.stack(new_p[5]).astype(state_s5_im.dtype)
    s5_im_s = jnp.stack(new_s[5]).astype(state_s5_im.dtype)
    return (y_prompt, y_sample, ssd_p, ssd_s, ssd_conv_p, ssd_conv_s, lru_p, lru_s,
            lru_conv_p, lru_conv_s, s5_re_p, s5_re_s, s5_im_p, s5_im_s)
```

```python
import functools

import jax
import jax.numpy as jnp
from jax import lax
from jax.experimental import pallas as pl
from jax.experimental.pallas import tpu as pltpu

f32 = jnp.float32
bf16 = jnp.bfloat16

D_MODEL = 1024
CONV_WIDTH = 4
SSD_DIM = 1024
SSD_HEADDIM = 64
SSD_HEADS = 16
SSD_GROUPS = 4
SSD_HPG = 4
SSD_STATE = 128
SSD_CONV_DIM = SSD_DIM + 2 * SSD_GROUPS * SSD_STATE
LRU_DIM = 512
LRU_BLOCKS = 8
LRU_C = 8.0
S5_DIM = 512
S5_GROUP = 16
S5_NGROUPS = 32
S5_STATE = 64
S5_FLAT = S5_NGROUPS * S5_STATE
EPS = 1e-6

LANES = 128
SUBLANES = 8
TILE = 128
TILE_S = 64
SEQ_S = 8
SEQ_SHIFT = 3
NEG = -1e30

C_Z = 0
C_XBC = C_Z + SSD_DIM
C_LRU = C_XBC + SSD_CONV_DIM
C_LRU_G = C_LRU + LRU_DIM
C_S5 = C_LRU_G + LRU_DIM
C_S5_G = C_S5 + S5_DIM
C_DT = C_S5_G + S5_DIM
IN_COLS = C_DT + LANES

VMEM_LIMIT_BYTES = 56 * 1024 * 1024


def _rms(x, g):
    return x * lax.rsqrt(jnp.mean(x * x, axis=-1, keepdims=True) + EPS) * g


def _silu(x):
    return x * jax.nn.sigmoid(x)


def _dot(a, b):
    return jnp.dot(a, b, preferred_element_type=f32)


def _dot_nt(a, b):
    return lax.dot_general(a, b, (((1,), (1,)), ((), ())), preferred_element_type=f32)


def _dot_tn(a, b):
    return lax.dot_general(a, b, (((0,), (0,)), ((), ())), preferred_element_type=f32)


def _dot_exact(a, b):
    return jnp.dot(a, b, preferred_element_type=f32, precision=lax.Precision.HIGHEST)


def _pair_expand(v, j, lane_lo):
    t = v.shape[0]
    lo = jnp.broadcast_to(v[:, 2 * j:2 * j + 1], (t, LANES))
    hi = jnp.broadcast_to(v[:, 2 * j + 1:2 * j + 2], (t, LANES))
    return jnp.where(lane_lo, lo, hi)


def _s5_prep_body(lre_ref, lim_ref, ldt_ref, lre_rep_ref, lim_rep_ref, ldt_rep_ref,
                  bre_ref, bim_ref, tre_ref, tim_ref, bbre_ref, bbim_ref):
    def abar(lre, lim, ldt):
        delta = jnp.exp(ldt)
        mag = jnp.exp(lre * delta)
        return mag * jnp.cos(lim * delta), mag * jnp.sin(lim * delta)

    ar, ai = abar(lre_ref[...], lim_ref[...], ldt_ref[...])
    pr, pi = [ar], [ai]
    for _ in range(SUBLANES - 1):
        pr, pi = pr + [pr[-1] * ar - pi[-1] * ai], pi + [pr[-1] * ai + pi[-1] * ar]
    zero = jnp.zeros_like(ar)
    for t, d in enumerate((1, 2, 4)):
        for r in range(SUBLANES):
            tre_ref[t * SUBLANES + r] = pr[d - 1] if r >= d else zero
            tim_ref[t * SUBLANES + r] = pi[d - 1] if r >= d else zero
    for r in range(SUBLANES):
        tre_ref[3 * SUBLANES + r] = pr[r]
        tim_ref[3 * SUBLANES + r] = pi[r]

    lre, lim = lre_rep_ref[...], lim_rep_ref[...]
    ar, ai = abar(lre, lim, ldt_rep_ref[...])
    denom = lre * lre + lim * lim
    nr = ar - 1.0
    ni = ai
    coef_re = (nr * lre + ni * lim) / denom
    coef_im = (ni * lre - nr * lim) / denom
    bre, bim = bre_ref[...], bim_ref[...]
    bbre_ref[...] = coef_re * bre - coef_im * bim
    bbim_ref[...] = coef_re * bim + coef_im * bre


def _s5_prep(lam_re, lam_im, log_dt, b_re, b_im):
    depth = lam_re.shape[0]
    rows_c = S5_FLAT // LANES
    rows_r = S5_DIM * S5_STATE // LANES
    ldt = jnp.broadcast_to(log_dt[:, :, None], (depth, S5_NGROUPS, S5_STATE))

    def rep(v):
        return jnp.broadcast_to(v[:, :, None, :], (depth, S5_NGROUPS, S5_GROUP, S5_STATE)).reshape(depth, rows_r, LANES)

    def bt(v):
        return jnp.transpose(v, (0, 1, 3, 2)).reshape(depth, rows_r, LANES)

    cspec = pl.BlockSpec((None, rows_c, LANES), lambda i: (i, 0, 0))
    rspec = pl.BlockSpec((None, rows_r, LANES), lambda i: (i, 0, 0))
    tspec = pl.BlockSpec((None, 4 * SUBLANES, rows_c, LANES), lambda i: (i, 0, 0, 0))
    tre, tim, bbre, bbim = pl.pallas_call(
        _s5_prep_body,
        grid=(depth,),
        in_specs=[cspec, cspec, cspec, rspec, rspec, rspec, rspec, rspec],
        out_specs=[tspec, tspec, rspec, rspec],
        out_shape=[jax.ShapeDtypeStruct((depth, 4 * SUBLANES, rows_c, LANES), f32)] * 2
        + [jax.ShapeDtypeStruct((depth, rows_r, LANES), f32)] * 2,
        name="s5_prep",
    )(lam_re.reshape(depth, rows_c, LANES), lam_im.reshape(depth, rows_c, LANES),
      ldt.reshape(depth, rows_c, LANES), rep(lam_re), rep(lam_im), rep(ldt), bt(b_re), bt(b_im))
    tab = jnp.stack([tre, tim], axis=1).reshape(depth, 2, 4, SUBLANES, S5_FLAT)
    return tab, bbre.reshape(depth, S5_DIM, S5_STATE), bbim.reshape(depth, S5_DIM, S5_STATE)


def _layer_body(mode, final, *refs):
    prompt = mode == "prompt"
    T = TILE if prompt else TILE_S
    nseq = T // SEQ_S
    it = iter(refs)
    x_ref = next(it)
    if not prompt:
        h0_ssd_ref, c0_ssd_ref, h0_lru_ref, c0_lru_ref, h0_s5r_ref, h0_s5i_ref = (next(it) for _ in range(6))
    (ng_ref, w_in_ref, cw_ssd_ref, cb_ssd_ref, dtb_ref, alog_ref, dfull_ref, sng_ref,
     cw_lru_ref, cb_lru_ref, lru_w_ref, lru_b_ref, lam_ref,
     tab_ref, bbd_ref, cbd_ref, s5d_ref, glu_w_ref, glu_b_ref, w_out_ref, fg_ref) = (next(it) for _ in range(21))
    y_ref, o_ssd_ref, o_cssd_ref, o_lru_ref, o_clru_ref, o_s5r_ref, o_s5i_ref = (next(it) for _ in range(7))
    if prompt:
        h_ssd, xpad_ssd, xpad_lru, a_s, b_s, bur_s, bui_s, lru_c, s5_cr, s5_ci = (next(it) for _ in range(10))
    else:
        xpad_ssd, xpad_lru, a_s, b_s, bur_s, bui_s, c_s, bm_s, xw_s, yoff_s, eac_s = (next(it) for _ in range(11))

    if prompt:
        c_idx = pl.program_id(1)
        last = c_idx == pl.num_programs(1) - 1

        @pl.when(c_idx == 0)
        def _():
            h_ssd[...] = jnp.zeros_like(h_ssd)
            xpad_ssd[0:SUBLANES, :] = jnp.zeros((SUBLANES, SSD_CONV_DIM), f32)
            xpad_lru[0:SUBLANES, :] = jnp.zeros((SUBLANES, LRU_DIM), f32)
            lru_c[...] = jnp.zeros_like(lru_c)
            s5_cr[...] = jnp.zeros_like(s5_cr)
            s5_ci[...] = jnp.zeros_like(s5_ci)

    x = x_ref[...]
    hn = _rms(x, ng_ref[...]).astype(bf16)

    def proj(lo, hi):
        return _dot(hn, w_in_ref[:, lo:hi])

    def conv(raw, xpad, c0_ref, cw_ref, cb_ref, o_ref):
        cdim = raw.shape[1]
        if prompt:
            xpad[SUBLANES:SUBLANES + T, :] = raw
            acc = cb_ref[...] + cw_ref[3:4, :] * raw
            for j in range(1, CONV_WIDTH):
                acc = acc + cw_ref[3 - j:4 - j, :] * xpad[pl.ds(SUBLANES - j, T), :]
            o_ref[...] = xpad[pl.ds(SUBLANES + T - 3, 3), :]
            xpad[0:SUBLANES, :] = xpad[pl.ds(T, SUBLANES), :]
        else:
            xpad[:, pl.ds(SUBLANES - 3, 3), :] = c0_ref[...]
            xpad[:, pl.ds(SUBLANES, SEQ_S), :] = raw.reshape(nseq, SEQ_S, cdim)
            acc = cb_ref[...] + cw_ref[3:4, :] * raw
            for j in range(1, CONV_WIDTH):
                acc = acc + cw_ref[3 - j:4 - j, :] * xpad[:, pl.ds(SUBLANES - j, SEQ_S), :].reshape(T, cdim)
            o_ref[...] = xpad[:, pl.ds(SUBLANES + SEQ_S - 3, 3), :]
        return acc

    row = lax.broadcasted_iota(jnp.int32, (T, T), 0)
    col = lax.broadcasted_iota(jnp.int32, (T, T), 1)
    causal = row >= col
    if not prompt:
        causal = jnp.logical_and(causal, jnp.right_shift(row, SEQ_SHIFT) == jnp.right_shift(col, SEQ_SHIFT))
    lane_lo = lax.broadcasted_iota(jnp.int32, (T, LANES), 1) < SSD_HEADDIM

    xbc = _silu(conv(proj(C_XBC, C_LRU), xpad_ssd, None if prompt else c0_ssd_ref,
                     cw_ssd_ref, cb_ssd_ref, o_cssd_ref))
    xs = xbc[:, :SSD_DIM]
    bm = xbc[:, SSD_DIM:SSD_DIM + SSD_GROUPS * SSD_STATE]
    cm = xbc[:, SSD_DIM + SSD_GROUPS * SSD_STATE:]
    bm_b = bm.astype(bf16)
    cm_b = cm.astype(bf16)
    dt = jax.nn.softplus(proj(C_DT, IN_COLS) + dtb_ref[...])
    a_neg = -jnp.exp(alog_ref[...])
    acum = _dot_exact(jnp.where(causal, 1.0, 0.0), dt * a_neg)
    acum_row = acum.T
    dt_row = dt.T
    eac = jnp.exp(acum)
    if prompt:
        acum_end = acum[T - 1:T, :]
    else:
        sel = jnp.where(col == jnp.bitwise_or(row, SEQ_S - 1), 1.0, 0.0)
        acum_end = _dot_exact(sel, acum)
    wgt = jnp.exp(acum_end - acum) * dt

    scores = [_dot_nt(cm_b[:, LANES * g:LANES * (g + 1)], bm_b[:, LANES * g:LANES * (g + 1)])
              for g in range(SSD_GROUPS)]
    y_pairs, xw_pairs, ecol_pairs = [], [], []
    for j in range(SSD_HEADS // 2):
        g = (2 * j) // SSD_HPG
        ms = []
        for h in (2 * j, 2 * j + 1):
            diff = acum[:, h:h + 1] - acum_row[h:h + 1, :]
            decay = jnp.exp(jnp.where(causal, diff, NEG))
            ms.append((scores[g] * decay * dt_row[h:h + 1, :]).astype(bf16))
        xp = xs[:, LANES * j:LANES * (j + 1)]
        xbd = jnp.concatenate([jnp.where(lane_lo, xp, 0.0), jnp.where(lane_lo, 0.0, xp)], axis=0).astype(bf16)
        y_pairs.append(_dot(jnp.concatenate(ms, axis=1), xbd))
        xw_pairs.append(xp * _pair_expand(wgt, j, lane_lo))
        ecol_pairs.append(_pair_expand(eac, j, lane_lo))
    y_diag = jnp.concatenate(y_pairs, axis=1)
    xw = jnp.concatenate(xw_pairs, axis=1)
    ecol = jnp.concatenate(ecol_pairs, axis=1)

    def ssd_state_io(rows_c, rows_xw, rows_b, e_last, h_get, h_set):
        outs = []
        for g in range(SSD_GROUPS):
            hp = h_get(g)
            outs.append(_dot_nt(rows_c[:, LANES * g:LANES * (g + 1)].astype(bf16), hp.astype(bf16)))
            sg = _dot_tn(rows_xw[:, 2 * LANES * g:2 * LANES * (g + 1)].astype(bf16),
                         rows_b[:, LANES * g:LANES * (g + 1)].astype(bf16))
            dec = jnp.concatenate(
                [jnp.broadcast_to(e_last[:, SSD_HPG * g + k:SSD_HPG * g + k + 1], (SSD_HEADDIM, SSD_STATE))
                 for k in range(SSD_HPG)], axis=0)
            h_set(g, dec * hp + sg)
        return jnp.concatenate(outs, axis=1)

    gsz = SSD_HPG * SSD_HEADDIM
    if prompt:
        def h_get(g):
            return h_ssd[gsz * g:gsz * (g + 1), :]

        def h_set(g, v):
            h_ssd[gsz * g:gsz * (g + 1), :] = v

        y_off = ssd_state_io(cm, xw, bm, eac[T - 1:T, :], h_get, h_set)

        @pl.when(last)
        def _():
            o_ssd_ref[...] = h_ssd[...]
    else:
        c_s[...] = cm
        bm_s[...] = bm
        xw_s[...] = xw
        eac_s[...] = eac

        def seq_step(i, carry):
            r0 = pl.multiple_of(i * SEQ_S, SEQ_S)

            def h_get(g):
                return h0_ssd_ref[i, pl.ds(gsz * g, gsz), :]

            def h_set(g, v):
                o_ssd_ref[i, pl.ds(gsz * g, gsz), :] = v

            yoff_s[pl.ds(r0, SEQ_S), :] = ssd_state_io(
                c_s[pl.ds(r0, SEQ_S), :], xw_s[pl.ds(r0, SEQ_S), :], bm_s[pl.ds(r0, SEQ_S), :],
                eac_s[pl.ds(r0 + SEQ_S - 1, 1), :], h_get, h_set)
            return carry

        lax.fori_loop(0, nseq, seq_step, 0)
        y_off = yoff_s[...]

    y = y_diag + y_off * ecol + dfull_ref[...] * xs
    y_ssd = _rms(y * _silu(proj(C_Z, C_XBC)), sng_ref[...])

    xr = conv(proj(C_LRU, C_LRU_G), xpad_lru, None if prompt else c0_lru_ref,
              cw_lru_ref, cb_lru_ref, o_clru_ref)
    gates = _dot(xr.astype(bf16), lru_w_ref[...]) + lru_b_ref[...]
    r_gate = jax.nn.sigmoid(gates[:, :LRU_DIM])
    i_gate = jax.nn.sigmoid(gates[:, LRU_DIM:])
    log_a = -LRU_C * r_gate * jax.nn.softplus(-lam_ref[...])
    a_t = jnp.exp(log_a)
    gain = jnp.sqrt(jnp.maximum(-jnp.tanh(log_a) * (a_t * a_t + 1.0), 0.0))
    a_s[...] = a_t
    b_s[...] = gain * i_gate * xr
    sub = lax.broadcasted_iota(jnp.int32, (SUBLANES, LRU_DIM), 0)

    def lru_step(i, carry):
        r0 = pl.multiple_of(i * SUBLANES, SUBLANES)
        a = a_s[pl.ds(r0, SUBLANES), :]
        b = b_s[pl.ds(r0, SUBLANES), :]
        for d in (1, 2, 4):
            a_sh = jnp.where(sub >= d, pltpu.roll(a, d, 0), 1.0)
            b_sh = jnp.where(sub >= d, pltpu.roll(b, d, 0), 0.0)
            b = a * b_sh + b
            a = a * a_sh
        h_in = carry if prompt else h0_lru_ref[pl.ds(i, 1), :]
        h = a * jnp.broadcast_to(h_in, (SUBLANES, LRU_DIM)) + b
        b_s[pl.ds(r0, SUBLANES), :] = h
        if prompt:
            return h[SUBLANES - 1:SUBLANES, :]
        o_lru_ref[pl.ds(i, 1), :] = h[SUBLANES - 1:SUBLANES, :]
        return carry

    if prompt:
        lru_c[...] = lax.fori_loop(0, T // SUBLANES, lru_step, lru_c[...])
        o_lru_ref[...] = lru_c[...]
    else:
        lax.fori_loop(0, T // SUBLANES, lru_step, 0)
    y_lru = b_s[...] * _silu(proj(C_LRU_G, C_S5))

    u = proj(C_S5, C_S5_G)
    bu = _dot(u.astype(bf16), bbd_ref[...])
    bur_s[...] = bu[:, :S5_FLAT]
    bui_s[...] = bu[:, S5_FLAT:]

    def s5_step(i, carry):
        r0 = pl.multiple_of(i * SUBLANES, SUBLANES)
        hr = bur_s[pl.ds(r0, SUBLANES), :]
        hi = bui_s[pl.ds(r0, SUBLANES), :]
        for t, d in enumerate((1, 2, 4)):
            cr, ci = tab_ref[0, t], tab_ref[1, t]
            sr, si = pltpu.roll(hr, d, 0), pltpu.roll(hi, d, 0)
            hr, hi = hr + (cr * sr - ci * si), hi + (cr * si + ci * sr)
        if prompt:
            c_r, c_i = carry
        else:
            c_r, c_i = h0_s5r_ref[pl.ds(i, 1), :], h0_s5i_ref[pl.ds(i, 1), :]
        c_r = jnp.broadcast_to(c_r, (SUBLANES, S5_FLAT))
        c_i = jnp.broadcast_to(c_i, (SUBLANES, S5_FLAT))
        pr, pi = tab_ref[0, 3], tab_ref[1, 3]
        hr, hi = hr + (pr * c_r - pi * c_i), hi + (pr * c_i + pi * c_r)
        bur_s[pl.ds(r0, SUBLANES), :] = hr
        bui_s[pl.ds(r0, SUBLANES), :] = hi
        if prompt:
            return hr[SUBLANES - 1:SUBLANES, :], hi[SUBLANES - 1:SUBLANES, :]
        o_s5r_ref[pl.ds(i, 1), :] = hr[SUBLANES - 1:SUBLANES, :]
        o_s5i_ref[pl.ds(i, 1), :] = hi[SUBLANES - 1:SUBLANES, :]
        return carry

    if prompt:
        c_r, c_i = lax.fori_loop(0, T // SUBLANES, s5_step, (s5_cr[...], s5_ci[...]))
        s5_cr[...] = c_r
        s5_ci[...] = c_i
        o_s5r_ref[...] = c_r
        o_s5i_ref[...] = c_i
    else:
        lax.fori_loop(0, T // SUBLANES, s5_step, 0)
    hcat = jnp.concatenate([bur_s[...].astype(bf16), bui_s[...].astype(bf16)], axis=1)
    ys5 = _dot(hcat, cbd_ref[...]) + s5d_ref[...] * u
    ys5 = jax.nn.gelu(ys5)
    ys5 = ys5 * jax.nn.sigmoid(_dot(ys5.astype(bf16), glu_w_ref[...]) + glu_b_ref[...])
    y_s5 = ys5 * _silu(proj(C_S5_G, C_DT))

    ycat = jnp.concatenate([y_ssd.astype(bf16), y_lru.astype(bf16), y_s5.astype(bf16)], axis=1)
    out = x + _dot(ycat, w_out_ref[...])
    if final:
        out = _rms(out, fg_ref[...])
    y_ref[...] = out


def _const_spec(shape):
    nd = len(shape)
    return pl.BlockSpec(shape, lambda *_: (0,) * nd, pipeline_mode=pl.Buffered(1))


def _layer_call(mode, final, x, states, weights):
    prompt = mode == "prompt"
    T = TILE if prompt else TILE_S
    wspecs = [_const_spec(w.shape) for w in weights]
    if prompt:
        nb, seq, _ = x.shape
        grid = (nb, seq // T)
        x_spec = pl.BlockSpec((None, T, D_MODEL), lambda b, c: (b, c, 0))
        in_specs = [x_spec] + wspecs
        args = [x] + list(weights)

        def st(shape):
            nd = len(shape)
            return pl.BlockSpec((None,) + shape, lambda b, c: (b,) + (0,) * nd)

        out_specs = [x_spec, st((SSD_DIM, SSD_STATE)), st((CONV_WIDTH - 1, SSD_CONV_DIM)), st((1, LRU_DIM)),
                     st((CONV_WIDTH - 1, LRU_DIM)), st((1, S5_FLAT)), st((1, S5_FLAT))]
        out_shape = [jax.ShapeDtypeStruct(x.shape, f32),
                     jax.ShapeDtypeStruct((nb, SSD_DIM, SSD_STATE), f32),
                     jax.ShapeDtypeStruct((nb, CONV_WIDTH - 1, SSD_CONV_DIM), f32),
                     jax.ShapeDtypeStruct((nb, 1, LRU_DIM), f32),
                     jax.ShapeDtypeStruct((nb, CONV_WIDTH - 1, LRU_DIM), f32),
                     jax.ShapeDtypeStruct((nb, 1, S5_FLAT), f32),
                     jax.ShapeDtypeStruct((nb, 1, S5_FLAT), f32)]
        scratch = [pltpu.VMEM((SSD_DIM, SSD_STATE), f32),
                   pltpu.VMEM((T + SUBLANES, SSD_CONV_DIM), f32),
                   pltpu.VMEM((T + SUBLANES, LRU_DIM), f32),
                   pltpu.VMEM((T, LRU_DIM), f32), pltpu.VMEM((T, LRU_DIM), f32),
                   pltpu.VMEM((T, S5_FLAT), f32), pltpu.VMEM((T, S5_FLAT), f32),
                   pltpu.VMEM((1, LRU_DIM), f32),
                   pltpu.VMEM((1, S5_FLAT), f32), pltpu.VMEM((1, S5_FLAT), f32)]
        sem = ("arbitrary", "arbitrary")
    else:
        rows = x.shape[0]
        nseq = T // SEQ_S
        nb = rows // SEQ_S
        grid = (rows // T,)
        x_spec = pl.BlockSpec((T, D_MODEL), lambda i: (i, 0))

        def st(shape):
            nd = len(shape)
            return pl.BlockSpec((nseq,) + shape, lambda i: (i,) + (0,) * nd)

        st_specs = [st((SSD_DIM, SSD_STATE)), st((CONV_WIDTH - 1, SSD_CONV_DIM)), st((LRU_DIM,)),
                    st((CONV_WIDTH - 1, LRU_DIM)), st((S5_FLAT,)), st((S5_FLAT,))]
        in_specs = [x_spec] + st_specs + wspecs
        args = [x] + list(states) + list(weights)
        out_specs = [x_spec] + st_specs
        out_shape = [jax.ShapeDtypeStruct(x.shape, f32)] + [jax.ShapeDtypeStruct(s.shape, f32) for s in states]
        scratch = [pltpu.VMEM((nseq, 2 * SUBLANES, SSD_CONV_DIM), f32),
                   pltpu.VMEM((nseq, 2 * SUBLANES, LRU_DIM), f32),
                   pltpu.VMEM((T, LRU_DIM), f32), pltpu.VMEM((T, LRU_DIM), f32),
                   pltpu.VMEM((T, S5_FLAT), f32), pltpu.VMEM((T, S5_FLAT), f32),
                   pltpu.VMEM((T, SSD_GROUPS * SSD_STATE), f32), pltpu.VMEM((T, SSD_GROUPS * SSD_STATE), f32),
                   pltpu.VMEM((T, SSD_DIM), f32), pltpu.VMEM((T, SSD_DIM), f32),
                   pltpu.VMEM((T, LANES), f32)]
        sem = ("arbitrary",)
    return pl.pallas_call(
        functools.partial(_layer_body, mode, final),
        grid=grid, in_specs=in_specs, out_specs=out_specs, out_shape=out_shape,
        scratch_shapes=scratch,
        compiler_params=pltpu.CompilerParams(dimension_semantics=sem, vmem_limit_bytes=VMEM_LIMIT_BYTES),
        name=f"layer_{mode}",
    )(*args)


def _block_diag(blocks):
    n, r, c = blocks.shape
    eye = jnp.eye(n, dtype=blocks.dtype)
    return (blocks[:, :, None, :] * eye[:, None, :, None]).reshape(n * r, n * c)


def _pad_lanes(v):
    return jnp.pad(v, ((0, 0), (0, LANES - v.shape[-1])))


def kernel(x_prompt, x_sample, state_ssd, state_ssd_conv, state_lru, state_lru_conv, state_s5_re, state_s5_im, norm_g, w_in, ssd_conv_w, ssd_conv_b, ssd_dt_bias, ssd_a_log, ssd_d, ssd_norm_g, lru_conv_w, lru_conv_b, lru_wa, lru_ba, lru_wx, lru_bx, lru_lambda, s5_lambda_re, s5_lambda_im, s5_log_dt, s5_b_re, s5_b_im, s5_c_re, s5_c_im, s5_d, s5_glu_w, s5_glu_b, w_out, final_norm_g):
    depth = w_in.shape[0]
    nbp = x_prompt.shape[0]
    nbs, ls, _ = x_sample.shape
    assert ls == SEQ_S and x_prompt.shape[1] % TILE == 0 and (nbs * ls) % TILE_S == 0

    tab, bbar_re, bbar_im = _s5_prep(s5_lambda_re.astype(f32), s5_lambda_im.astype(f32), s5_log_dt.astype(f32),
                                     s5_b_re.astype(f32), s5_b_im.astype(f32))
    fg = final_norm_g.astype(f32).reshape(1, D_MODEL)

    xp = x_prompt.astype(f32)
    xs = x_sample.astype(f32).reshape(nbs * ls, D_MODEL)
    outs_p = [[] for _ in range(6)]
    outs_s = [[] for _ in range(6)]
    for i in range(depth):
        wi = w_in[i].astype(f32)
        w_in_r = jnp.concatenate(
            [wi[:, 0:1024], wi[:, 1024:3072], wi[:, 3088:3600], wi[:, 3600:4112], wi[:, 4112:4624],
             wi[:, 4624:5136], _pad_lanes(wi[:, 3072:3088])], axis=1).astype(bf16)
        bbd = jnp.concatenate([_block_diag(bbar_re[i].reshape(S5_NGROUPS, S5_GROUP, S5_STATE)),
                               _block_diag(bbar_im[i].reshape(S5_NGROUPS, S5_GROUP, S5_STATE))], axis=1).astype(bf16)
        cbd = jnp.concatenate([_block_diag(jnp.transpose(s5_c_re[i].astype(f32), (0, 2, 1))),
                               -_block_diag(jnp.transpose(s5_c_im[i].astype(f32), (0, 2, 1)))], axis=0).astype(bf16)
        weights = (
            norm_g[i].astype(f32).reshape(1, D_MODEL), w_in_r,
            ssd_conv_w[i].astype(f32), ssd_conv_b[i].astype(f32).reshape(1, SSD_CONV_DIM),
            _pad_lanes(ssd_dt_bias[i].astype(f32).reshape(1, SSD_HEADS)),
            _pad_lanes(ssd_a_log[i].astype(f32).reshape(1, SSD_HEADS)),
            jnp.repeat(ssd_d[i].astype(f32), SSD_HEADDIM).reshape(1, SSD_DIM),
            ssd_norm_g[i].astype(f32).reshape(1, SSD_DIM),
            lru_conv_w[i].astype(f32), lru_conv_b[i].astype(f32).reshape(1, LRU_DIM),
            jnp.concatenate([_block_diag(lru_wa[i].astype(f32)), _block_diag(lru_wx[i].astype(f32))], axis=1).astype(bf16),
            jnp.concatenate([lru_ba[i], lru_bx[i]]).astype(f32).reshape(1, 2 * LRU_DIM),
            lru_lambda[i].astype(f32).reshape(1, LRU_DIM),
            tab[i], bbd, cbd, s5_d[i].astype(f32).reshape(1, S5_DIM),
            s5_glu_w[i].astype(bf16), s5_glu_b[i].astype(f32).reshape(1, S5_DIM),
            w_out[i].astype(bf16), fg,
        )
        final = i == depth - 1
        res_p = _layer_call("prompt", final, xp, None, weights)
        xp = res_p[0]
        st_s = (state_ssd[i].astype(f32).reshape(nbs, SSD_DIM, SSD_STATE), state_ssd_conv[i].astype(f32),
                state_lru[i].astype(f32), state_lru_conv[i].astype(f32),
                state_s5_re[i].astype(f32).reshape(nbs, S5_FLAT), state_s5_im[i].astype(f32).reshape(nbs, S5_FLAT))
        res_s = _layer_call("sample", final, xs, st_s, weights)
        xs = res_s[0]
        for j in range(6):
            outs_p[j].append(res_p[1 + j])
            outs_s[j].append(res_s[1 + j])

    def stack(lst, shape, dtype):
        return jnp.stack(lst).reshape((depth,) + shape).astype(dtype)

    y_prompt = xp.astype(x_prompt.dtype)
    y_sample = xs.reshape(nbs, ls, D_MODEL).astype(x_sample.dtype)
    ssd_shape = (SSD_HEADS, SSD_HEADDIM, SSD_STATE)
    s5_shape = (S5_NGROUPS, S5_STATE)
    return (
        y_prompt, y_sample,
        stack(outs_p[0], (nbp,) + ssd_shape, state_ssd.dtype), stack(outs_s[0], (nbs,) + ssd_shape, state_ssd.dtype),
        stack(outs_p[1], (nbp, CONV_WIDTH - 1, SSD_CONV_DIM), state_ssd_conv.dtype),
        stack(outs_s[1], (nbs, CONV_WIDTH - 1, SSD_CONV_DIM), state_ssd_conv.dtype),
        stack(outs_p[2], (nbp, LRU_DIM), state_lru.dtype), stack(outs_s[2], (nbs, LRU_DIM), state_lru.dtype),
        stack(outs_p[3], (nbp, CONV_WIDTH - 1, LRU_DIM), state_lru_conv.dtype),
        stack(outs_s[3], (nbs, CONV_WIDTH - 1, LRU_DIM), state_lru_conv.dtype),
        stack(outs_p[4], (nbp,) + s5_shape, state_s5_re.dtype), stack(outs_s[4], (nbs,) + s5_shape, state_s5_re.dtype),
        stack(outs_p[5], (nbp,) + s5_shape, state_s5_im.dtype), stack(outs_s[5], (nbs,) + s5_shape, state_s5_im.dtype),
    )
```

```python
import functools

import jax
import jax.numpy as jnp
from jax import lax
from jax.experimental import pallas as pl
from jax.experimental.pallas import tpu as pltpu

f32 = jnp.float32
bf16 = jnp.bfloat16

D_MODEL = 1024
CONV_WIDTH = 4
SSD_DIM = 1024
SSD_HEADDIM = 64
SSD_HEADS = 16
SSD_GROUPS = 4
SSD_HPG = 4
SSD_STATE = 128
SSD_BC = SSD_GROUPS * SSD_STATE
SSD_CONV_DIM = SSD_DIM + 2 * SSD_BC
LRU_DIM = 512
LRU_C = 8.0
S5_DIM = 512
S5_GROUP = 16
S5_NGROUPS = 32
S5_STATE = 64
S5_FLAT = S5_NGROUPS * S5_STATE
S5_HALF = S5_FLAT // 2
EPS = 1e-6

LANES = 128
SUBLANES = 8
CHUNK = 128
TILE_P = 256
TILE_S = 64
SEQ_S = 8
SEQ_SHIFT = 3
NEG = -1e30

C_Z = 0
C_XBC = C_Z + SSD_DIM
C_LRU = C_XBC + SSD_CONV_DIM
C_LRU_G = C_LRU + LRU_DIM
C_S5 = C_LRU_G + LRU_DIM
C_S5_G = C_S5 + S5_DIM
C_DT = C_S5_G + S5_DIM
IN_COLS = C_DT + LANES

VMEM_LIMIT_BYTES = 56 * 1024 * 1024

N_WEIGHTS = 21


def _rms(x, g):
    return x * lax.rsqrt(jnp.mean(x * x, axis=-1, keepdims=True) + EPS) * g


def _silu(x):
    return x * jax.nn.sigmoid(x)


def _dot(a, b):
    return jnp.dot(a, b, preferred_element_type=f32)


def _dot_nt(a, b):
    return lax.dot_general(a, b, (((1,), (1,)), ((), ())), preferred_element_type=f32)


def _dot_tn(a, b):
    return lax.dot_general(a, b, (((0,), (0,)), ((), ())), preferred_element_type=f32)


def _dot_exact(a, b):
    return jnp.dot(a, b, preferred_element_type=f32, precision=lax.Precision.HIGHEST)


def _pair_expand(v, j, lane_lo):
    q = v.shape[0]
    lo = jnp.broadcast_to(v[:, 2 * j:2 * j + 1], (q, LANES))
    hi = jnp.broadcast_to(v[:, 2 * j + 1:2 * j + 2], (q, LANES))
    return jnp.where(lane_lo, lo, hi)


def _s5_prep_body(lre_ref, lim_ref, ldt_ref, lre_rep_ref, lim_rep_ref, ldt_rep_ref,
                  bre_ref, bim_ref, tre_ref, tim_ref, bbre_ref, bbim_ref):
    def abar(lre, lim, ldt):
        delta = jnp.exp(ldt)
        mag = jnp.exp(lre * delta)
        return mag * jnp.cos(lim * delta), mag * jnp.sin(lim * delta)

    ar, ai = abar(lre_ref[...], lim_ref[...], ldt_ref[...])
    pr, pi = [ar], [ai]
    for _ in range(SUBLANES - 1):
        pr, pi = pr + [pr[-1] * ar - pi[-1] * ai], pi + [pr[-1] * ai + pi[-1] * ar]
    zero = jnp.zeros_like(ar)
    for t, d in enumerate((1, 2, 4)):
        for r in range(SUBLANES):
            tre_ref[t * SUBLANES + r] = pr[d - 1] if r >= d else zero
            tim_ref[t * SUBLANES + r] = pi[d - 1] if r >= d else zero
    for r in range(SUBLANES):
        tre_ref[3 * SUBLANES + r] = pr[r]
        tim_ref[3 * SUBLANES + r] = pi[r]

    lre, lim = lre_rep_ref[...], lim_rep_ref[...]
    ar, ai = abar(lre, lim, ldt_rep_ref[...])
    denom = lre * lre + lim * lim
    nr = ar - 1.0
    ni = ai
    coef_re = (nr * lre + ni * lim) / denom
    coef_im = (ni * lre - nr * lim) / denom
    bre, bim = bre_ref[...], bim_ref[...]
    bbre_ref[...] = coef_re * bre - coef_im * bim
    bbim_ref[...] = coef_re * bim + coef_im * bre


def _s5_prep(lam_re, lam_im, log_dt, b_re, b_im):
    depth = lam_re.shape[0]
    rows_c = S5_FLAT // LANES
    rows_r = S5_DIM * S5_STATE // LANES
    ldt = jnp.broadcast_to(log_dt[:, :, None], (depth, S5_NGROUPS, S5_STATE))

    def rep(v):
        return jnp.broadcast_to(v[:, :, None, :], (depth, S5_NGROUPS, S5_GROUP, S5_STATE)).reshape(depth, rows_r, LANES)

    def bt(v):
        return jnp.transpose(v, (0, 1, 3, 2)).reshape(depth, rows_r, LANES)

    cspec = pl.BlockSpec((None, rows_c, LANES), lambda i: (i, 0, 0))
    rspec = pl.BlockSpec((None, rows_r, LANES), lambda i: (i, 0, 0))
    tspec = pl.BlockSpec((None, 4 * SUBLANES, rows_c, LANES), lambda i: (i, 0, 0, 0))
    tre, tim, bbre, bbim = pl.pallas_call(
        _s5_prep_body,
        grid=(depth,),
        in_specs=[cspec, cspec, cspec, rspec, rspec, rspec, rspec, rspec],
        out_specs=[tspec, tspec, rspec, rspec],
        out_shape=[jax.ShapeDtypeStruct((depth, 4 * SUBLANES, rows_c, LANES), f32)] * 2
        + [jax.ShapeDtypeStruct((depth, rows_r, LANES), f32)] * 2,
        name="s5_prep",
    )(lam_re.reshape(depth, rows_c, LANES), lam_im.reshape(depth, rows_c, LANES),
      ldt.reshape(depth, rows_c, LANES), rep(lam_re), rep(lam_im), rep(ldt), bt(b_re), bt(b_im))
    tab = jnp.stack([tre, tim], axis=1).reshape(depth, 2, 4, SUBLANES, S5_FLAT)
    return tab, bbre.reshape(depth, S5_DIM, S5_STATE), bbim.reshape(depth, S5_DIM, S5_STATE)


def _layer_math(prompt, T, x, w, st, o, scr):
    (ng_ref, w_in_ref, cw_ssd_ref, cb_ssd_ref, dtb_ref, alog_ref, dfull_ref, sng_ref,
     cw_lru_ref, cb_lru_ref, lru_w_ref, lru_b_ref, lam_ref,
     tab_ref, bb_ref, cc_ref, s5d_ref, glu_w_ref, glu_b_ref, w_out_ref, _) = w
    o_ssd_ref, o_cssd_ref, o_lru_ref, o_clru_ref, o_s5r_ref, o_s5i_ref = o
    if prompt:
        h_ssd, xpad_ssd, xpad_lru, a_s, b_s, bur_s, bui_s, lru_c, s5_cr, s5_ci = scr
        last = pl.program_id(1) == pl.num_programs(1) - 1
    else:
        h0_ssd_ref, c0_ssd_ref, h0_lru_ref, c0_lru_ref, h0_s5r_ref, h0_s5i_ref = st
        xpad_ssd, xpad_lru, a_s, b_s, bur_s, bui_s, c_s, bm_s, xw_s, yoff_s, eac_s = scr
    nseq = T // SEQ_S
    Q = CHUNK if prompt else T

    hn = _rms(x, ng_ref[...]).astype(bf16)

    def proj(lo, hi):
        return _dot(hn, w_in_ref[:, lo:hi])

    def conv(raw, xpad, c0_ref, cw_ref, cb_ref, o_ref):
        cdim = raw.shape[1]
        acc = cb_ref[...] + cw_ref[3:4, :] * raw
        if prompt:
            xpad[SUBLANES:SUBLANES + T, :] = raw
            for j in range(1, CONV_WIDTH):
                acc = acc + cw_ref[3 - j:4 - j, :] * xpad[pl.ds(SUBLANES - j, T), :]
            o_ref[...] = xpad[pl.ds(SUBLANES + T - 3, 3), :]
            xpad[0:SUBLANES, :] = xpad[pl.ds(T, SUBLANES), :]
        else:
            xpad[:, pl.ds(SUBLANES - 3, 3), :] = c0_ref[...]
            xpad[:, pl.ds(SUBLANES, SEQ_S), :] = raw.reshape(nseq, SEQ_S, cdim)
            for j in range(1, CONV_WIDTH):
                acc = acc + cw_ref[3 - j:4 - j, :] * xpad[:, pl.ds(SUBLANES - j, SEQ_S), :].reshape(T, cdim)
            o_ref[...] = xpad[:, pl.ds(SUBLANES + SEQ_S - 3, 3), :]
        return acc

    row = lax.broadcasted_iota(jnp.int32, (Q, Q), 0)
    col = lax.broadcasted_iota(jnp.int32, (Q, Q), 1)
    causal = row >= col
    if not prompt:
        causal = jnp.logical_and(causal, jnp.right_shift(row, SEQ_SHIFT) == jnp.right_shift(col, SEQ_SHIFT))
    tril = jnp.where(causal, 1.0, 0.0)
    lane_lo = lax.broadcasted_iota(jnp.int32, (Q, LANES), 1) < SSD_HEADDIM
    gsz = SSD_HPG * SSD_HEADDIM

    xbc = _silu(conv(proj(C_XBC, C_LRU), xpad_ssd, None if prompt else c0_ssd_ref,
                     cw_ssd_ref, cb_ssd_ref, o_cssd_ref))
    dt_all = jax.nn.softplus(proj(C_DT, IN_COLS) + dtb_ref[...])
    a_neg = -jnp.exp(alog_ref[...])

    def ssd_state_io(rows_c, rows_xw, rows_b, e_last, h_get, h_set):
        outs = []
        for g in range(SSD_GROUPS):
            hp = h_get(g)
            outs.append(_dot_nt(rows_c[:, LANES * g:LANES * (g + 1)].astype(bf16), hp.astype(bf16)))
            sg = _dot_tn(rows_xw[:, gsz * g:gsz * (g + 1)].astype(bf16),
                         rows_b[:, LANES * g:LANES * (g + 1)].astype(bf16))
            dec = jnp.concatenate(
                [jnp.broadcast_to(e_last[:, SSD_HPG * g + k:SSD_HPG * g + k + 1], (SSD_HEADDIM, SSD_STATE))
                 for k in range(SSD_HPG)], axis=0)
            h_set(g, dec * hp + sg)
        return jnp.concatenate(outs, axis=1)

    def ssd_chunk(r0):
        xs = xbc[r0:r0 + Q, :SSD_DIM]
        bm = xbc[r0:r0 + Q, SSD_DIM:SSD_DIM + SSD_BC]
        cm = xbc[r0:r0 + Q, SSD_DIM + SSD_BC:]
        dt = dt_all[r0:r0 + Q, :]
        bm_b = bm.astype(bf16)
        cm_b = cm.astype(bf16)
        acum = _dot_exact(tril, dt * a_neg)
        acum_row = acum.T
        dt_row = dt.T
        eac = jnp.exp(acum)
        if prompt:
            acum_end = acum[Q - 1:Q, :]
        else:
            sel = jnp.where(col == jnp.bitwise_or(row, SEQ_S - 1), 1.0, 0.0)
            acum_end = _dot_exact(sel, acum)
        wgt = jnp.exp(acum_end - acum) * dt

        scores = [_dot_nt(cm_b[:, LANES * g:LANES * (g + 1)], bm_b[:, LANES * g:LANES * (g + 1)])
                  for g in range(SSD_GROUPS)]
        y_pairs, xw_pairs, ecol_pairs = [], [], []
        for j in range(SSD_HEADS // 2):
            g = (2 * j) // SSD_HPG
            ms = []
            for h in (2 * j, 2 * j + 1):
                diff = acum[:, h:h + 1] - acum_row[h:h + 1, :]
                decay = jnp.exp(jnp.where(causal, diff, NEG))
                ms.append((scores[g] * decay * dt_row[h:h + 1, :]).astype(bf16))
            xp = xs[:, LANES * j:LANES * (j + 1)]
            xbd = jnp.concatenate([jnp.where(lane_lo, xp, 0.0), jnp.where(lane_lo, 0.0, xp)], axis=0).astype(bf16)
            y_pairs.append(_dot(jnp.concatenate(ms, axis=1), xbd))
            xw_pairs.append(xp * _pair_expand(wgt, j, lane_lo))
            ecol_pairs.append(_pair_expand(eac, j, lane_lo))
        y_diag = jnp.concatenate(y_pairs, axis=1)
        xw = jnp.concatenate(xw_pairs, axis=1)
        ecol = jnp.concatenate(ecol_pairs, axis=1)

        if prompt:
            def h_get(g):
                return h_ssd[gsz * g:gsz * (g + 1), :]

            def h_set(g, v):
                h_ssd[gsz * g:gsz * (g + 1), :] = v

            y_off = ssd_state_io(cm, xw, bm, eac[Q - 1:Q, :], h_get, h_set)
        else:
            c_s[...] = cm
            bm_s[...] = bm
            xw_s[...] = xw
            eac_s[...] = eac

            def seq_step(i, carry):
                s0 = pl.multiple_of(i * SEQ_S, SEQ_S)

                def h_get(g):
                    return h0_ssd_ref[i, pl.ds(gsz * g, gsz), :]

                def h_set(g, v):
                    o_ssd_ref[i, pl.ds(gsz * g, gsz), :] = v

                yoff_s[pl.ds(s0, SEQ_S), :] = ssd_state_io(
                    c_s[pl.ds(s0, SEQ_S), :], xw_s[pl.ds(s0, SEQ_S), :], bm_s[pl.ds(s0, SEQ_S), :],
                    eac_s[pl.ds(s0 + SEQ_S - 1, 1), :], h_get, h_set)
                return carry

            lax.fori_loop(0, nseq, seq_step, 0)
            y_off = yoff_s[...]
        return y_diag + y_off * ecol + dfull_ref[...] * xs

    y = jnp.concatenate([ssd_chunk(r0) for r0 in range(0, T, Q)], axis=0) if T > Q else ssd_chunk(0)
    if prompt:
        @pl.when(last)
        def _():
            o_ssd_ref[...] = h_ssd[...]
    y_ssd = _rms(y * _silu(proj(C_Z, C_XBC)), sng_ref[...])

    xr = conv(proj(C_LRU, C_LRU_G), xpad_lru, None if prompt else c0_lru_ref,
              cw_lru_ref, cb_lru_ref, o_clru_ref)
    gates = _dot(xr.astype(bf16), lru_w_ref[...]) + lru_b_ref[...]
    r_gate = jax.nn.sigmoid(gates[:, :LRU_DIM])
    i_gate = jax.nn.sigmoid(gates[:, LRU_DIM:])
    log_a = -LRU_C * r_gate * jax.nn.softplus(-lam_ref[...])
    a_t = jnp.exp(log_a)
    gain = jnp.sqrt(jnp.maximum(-jnp.tanh(log_a) * (a_t * a_t + 1.0), 0.0))
    a_s[...] = a_t
    b_s[...] = gain * i_gate * xr
    sub = lax.broadcasted_iota(jnp.int32, (SUBLANES, LRU_DIM), 0)

    def lru_step(i, carry):
        r0 = pl.multiple_of(i * SUBLANES, SUBLANES)
        a = a_s[pl.ds(r0, SUBLANES), :]
        b = b_s[pl.ds(r0, SUBLANES), :]
        for d in (1, 2, 4):
            a_sh = jnp.where(sub >= d, pltpu.roll(a, d, 0), 1.0)
            b_sh = jnp.where(sub >= d, pltpu.roll(b, d, 0), 0.0)
            b = a * b_sh + b
            a = a * a_sh
        h_in = carry if prompt else h0_lru_ref[pl.ds(i, 1), :]
        h = a * jnp.broadcast_to(h_in, (SUBLANES, LRU_DIM)) + b
        b_s[pl.ds(r0, SUBLANES), :] = h
        if prompt:
            return h[SUBLANES - 1:SUBLANES, :]
        o_lru_ref[pl.ds(i, 1), :] = h[SUBLANES - 1:SUBLANES, :]
        return carry

    if prompt:
        lru_c[...] = lax.fori_loop(0, T // SUBLANES, lru_step, lru_c[...], unroll=True)
        o_lru_ref[...] = lru_c[...]
    else:
        lax.fori_loop(0, T // SUBLANES, lru_step, 0, unroll=True)
    y_lru = b_s[...] * _silu(proj(C_LRU_G, C_S5))

    u = proj(C_S5, C_S5_G)
    u_b = u.astype(bf16)
    half = S5_DIM // 2
    for k in range(2):
        uk = u_b[:, half * k:half * (k + 1)]
        bur_s[:, S5_HALF * k:S5_HALF * (k + 1)] = _dot(uk, bb_ref[k])
        bui_s[:, S5_HALF * k:S5_HALF * (k + 1)] = _dot(uk, bb_ref[2 + k])

    def s5_step(i, carry):
        r0 = pl.multiple_of(i * SUBLANES, SUBLANES)
        hr = bur_s[pl.ds(r0, SUBLANES), :]
        hi = bui_s[pl.ds(r0, SUBLANES), :]
        for t, d in enumerate((1, 2, 4)):
            cr, ci = tab_ref[0, t], tab_ref[1, t]
            sr, si = pltpu.roll(hr, d, 0), pltpu.roll(hi, d, 0)
            hr, hi = hr + (cr * sr - ci * si), hi + (cr * si + ci * sr)
        if prompt:
            c_r, c_i = carry
        else:
            c_r, c_i = h0_s5r_ref[pl.ds(i, 1), :], h0_s5i_ref[pl.ds(i, 1), :]
        c_r = jnp.broadcast_to(c_r, (SUBLANES, S5_FLAT))
        c_i = jnp.broadcast_to(c_i, (SUBLANES, S5_FLAT))
        pr, pi = tab_ref[0, 3], tab_ref[1, 3]
        hr, hi = hr + (pr * c_r - pi * c_i), hi + (pr * c_i + pi * c_r)
        bur_s[pl.ds(r0, SUBLANES), :] = hr
        bui_s[pl.ds(r0, SUBLANES), :] = hi
        if prompt:
            return hr[SUBLANES - 1:SUBLANES, :], hi[SUBLANES - 1:SUBLANES, :]
        o_s5r_ref[pl.ds(i, 1), :] = hr[SUBLANES - 1:SUBLANES, :]
        o_s5i_ref[pl.ds(i, 1), :] = hi[SUBLANES - 1:SUBLANES, :]
        return carry

    if prompt:
        c_r, c_i = lax.fori_loop(0, T // SUBLANES, s5_step, (s5_cr[...], s5_ci[...]), unroll=True)
        s5_cr[...] = c_r
        s5_ci[...] = c_i
        o_s5r_ref[...] = c_r
        o_s5i_ref[...] = c_i
    else:
        lax.fori_loop(0, T // SUBLANES, s5_step, 0, unroll=True)
    ys = []
    for k in range(2):
        hk = jnp.concatenate([bur_s[:, S5_HALF * k:S5_HALF * (k + 1)].astype(bf16),
                              bui_s[:, S5_HALF * k:S5_HALF * (k + 1)].astype(bf16)], axis=1)
        ys.append(_dot(hk, cc_ref[k]))
    ys5 = jnp.concatenate(ys, axis=1) + s5d_ref[...] * u
    ys5 = jax.nn.gelu(ys5)
    ys5 = ys5 * jax.nn.sigmoid(_dot(ys5.astype(bf16), glu_w_ref[...]) + glu_b_ref[...])
    y_s5 = ys5 * _silu(proj(C_S5_G, C_DT))

    ycat = jnp.concatenate([y_ssd.astype(bf16), y_lru.astype(bf16), y_s5.astype(bf16)], axis=1)
    return x + _dot(ycat, w_out_ref[...])


def _prompt_body(final, *refs):
    x_ref = refs[0]
    w = refs[1:1 + N_WEIGHTS]
    y_ref = refs[1 + N_WEIGHTS]
    o = refs[2 + N_WEIGHTS:8 + N_WEIGHTS]
    scr = refs[8 + N_WEIGHTS:]
    h_ssd, xpad_ssd, xpad_lru, _, _, _, _, lru_c, s5_cr, s5_ci = scr

    @pl.when(pl.program_id(1) == 0)
    def _():
        h_ssd[...] = jnp.zeros_like(h_ssd)
        xpad_ssd[0:SUBLANES, :] = jnp.zeros((SUBLANES, SSD_CONV_DIM), f32)
        xpad_lru[0:SUBLANES, :] = jnp.zeros((SUBLANES, LRU_DIM), f32)
        lru_c[...] = jnp.zeros_like(lru_c)
        s5_cr[...] = jnp.zeros_like(s5_cr)
        s5_ci[...] = jnp.zeros_like(s5_ci)

    out = _layer_math(True, TILE_P, x_ref[...], w, None, o, scr)
    if final:
        out = _rms(out, w[-1][...])
    y_ref[...] = out


def _sample_body(*refs):
    x_ref = refs[0]
    st = refs[1:7]
    w = refs[7:7 + N_WEIGHTS]
    y_ref = refs[7 + N_WEIGHTS]
    o = refs[8 + N_WEIGHTS:14 + N_WEIGHTS]
    x_all = refs[14 + N_WEIGHTS]
    scr = refs[15 + N_WEIGHTS:]
    layer = pl.program_id(0)
    last_layer = layer == pl.num_programs(0) - 1
    r0 = pl.multiple_of(pl.program_id(1) * TILE_S, TILE_S)

    @pl.when(layer == 0)
    def _():
        x_all[pl.ds(r0, TILE_S), :] = x_ref[...]

    out = _layer_math(False, TILE_S, x_all[pl.ds(r0, TILE_S), :], w, st, o, scr)
    x_all[pl.ds(r0, TILE_S), :] = out

    @pl.when(last_layer)
    def _():
        y_ref[...] = _rms(out, w[-1][...])

    @pl.when(jnp.logical_not(last_layer))
    def _():
        y_ref[...] = out


def _prompt_call(layer, final, x, weights):
    T = TILE_P
    nb, seq, _ = x.shape

    def wspec(a):
        nd = a.ndim - 1
        return pl.BlockSpec((None,) + a.shape[1:], lambda b, c: (layer,) + (0,) * nd, pipeline_mode=pl.Buffered(1))

    def st(shape):
        nd = len(shape)
        return pl.BlockSpec((None,) + shape, lambda b, c: (b,) + (0,) * nd)

    x_spec = pl.BlockSpec((None, T, D_MODEL), lambda b, c: (b, c, 0))
    out_specs = [x_spec, st((SSD_DIM, SSD_STATE)), st((CONV_WIDTH - 1, SSD_CONV_DIM)), st((1, LRU_DIM)),
                 st((CONV_WIDTH - 1, LRU_DIM)), st((1, S5_FLAT)), st((1, S5_FLAT))]
    out_shape = [jax.ShapeDtypeStruct(x.shape, f32),
                 jax.ShapeDtypeStruct((nb, SSD_DIM, SSD_STATE), f32),
                 jax.ShapeDtypeStruct((nb, CONV_WIDTH - 1, SSD_CONV_DIM), f32),
                 jax.ShapeDtypeStruct((nb, 1, LRU_DIM), f32),
                 jax.ShapeDtypeStruct((nb, CONV_WIDTH - 1, LRU_DIM), f32),
                 jax.ShapeDtypeStruct((nb, 1, S5_FLAT), f32),
                 jax.ShapeDtypeStruct((nb, 1, S5_FLAT), f32)]
    scratch = [pltpu.VMEM((SSD_DIM, SSD_STATE), f32),
               pltpu.VMEM((T + SUBLANES, SSD_CONV_DIM), f32),
               pltpu.VMEM((T + SUBLANES, LRU_DIM), f32),
               pltpu.VMEM((T, LRU_DIM), f32), pltpu.VMEM((T, LRU_DIM), f32),
               pltpu.VMEM((T, S5_FLAT), f32), pltpu.VMEM((T, S5_FLAT), f32),
               pltpu.VMEM((1, LRU_DIM), f32),
               pltpu.VMEM((1, S5_FLAT), f32), pltpu.VMEM((1, S5_FLAT), f32)]
    return pl.pallas_call(
        functools.partial(_prompt_body, final),
        grid=(nb, seq // T), in_specs=[x_spec] + [wspec(a) for a in weights],
        out_specs=out_specs, out_shape=out_shape, scratch_shapes=scratch,
        compiler_params=pltpu.CompilerParams(dimension_semantics=("arbitrary", "arbitrary"),
                                             vmem_limit_bytes=VMEM_LIMIT_BYTES),
        name="layer_prompt",
    )(x, *weights)


def _sample_call(x, states, weights):
    T = TILE_S
    rows = x.shape[0]
    depth = weights[0].shape[0]
    nseq = T // SEQ_S

    def wspec(a):
        nd = a.ndim - 1
        return pl.BlockSpec((None,) + a.shape[1:], lambda l, i: (l,) + (0,) * nd, pipeline_mode=pl.Buffered(1))

    def st(a):
        nd = a.ndim - 2
        return pl.BlockSpec((None, nseq) + a.shape[2:], lambda l, i: (l, i) + (0,) * nd)

    x_spec = pl.BlockSpec((T, D_MODEL), lambda l, i: (i, 0))
    st_specs = [st(a) for a in states]
    scratch = [pltpu.VMEM((rows, D_MODEL), f32),
               pltpu.VMEM((nseq, 2 * SUBLANES, SSD_CONV_DIM), f32),
               pltpu.VMEM((nseq, 2 * SUBLANES, LRU_DIM), f32),
               pltpu.VMEM((T, LRU_DIM), f32), pltpu.VMEM((T, LRU_DIM), f32),
               pltpu.VMEM((T, S5_FLAT), f32), pltpu.VMEM((T, S5_FLAT), f32),
               pltpu.VMEM((T, SSD_BC), f32), pltpu.VMEM((T, SSD_BC), f32),
               pltpu.VMEM((T, SSD_DIM), f32), pltpu.VMEM((T, SSD_DIM), f32),
               pltpu.VMEM((T, LANES), f32)]
    return pl.pallas_call(
        _sample_body,
        grid=(depth, rows // T), in_specs=[x_spec] + st_specs + [wspec(a) for a in weights],
        out_specs=[pl.BlockSpec((None, T, D_MODEL), lambda l, i: (l, i, 0))] + st_specs,
        out_shape=[jax.ShapeDtypeStruct((depth,) + x.shape, f32)] + [jax.ShapeDtypeStruct(a.shape, f32) for a in states],
        scratch_shapes=scratch,
        compiler_params=pltpu.CompilerParams(dimension_semantics=("arbitrary", "arbitrary"),
                                             vmem_limit_bytes=VMEM_LIMIT_BYTES),
        name="layers_sample",
    )(x, *states, *weights)


def _block_diag(blocks):
    *lead, n, r, c = blocks.shape
    eye = jnp.eye(n, dtype=blocks.dtype)
    return (blocks[..., :, :, None, :] * eye[:, None, :, None]).reshape(*lead, n * r, n * c)


def _pad_lanes(v):
    return jnp.pad(v, [(0, 0)] * (v.ndim - 1) + [(0, LANES - v.shape[-1])])


def kernel(x_prompt, x_sample, state_ssd, state_ssd_conv, state_lru, state_lru_conv, state_s5_re, state_s5_im, norm_g, w_in, ssd_conv_w, ssd_conv_b, ssd_dt_bias, ssd_a_log, ssd_d, ssd_norm_g, lru_conv_w, lru_conv_b, lru_wa, lru_ba, lru_wx, lru_bx, lru_lambda, s5_lambda_re, s5_lambda_im, s5_log_dt, s5_b_re, s5_b_im, s5_c_re, s5_c_im, s5_d, s5_glu_w, s5_glu_b, w_out, final_norm_g):
    depth = w_in.shape[0]
    nbp = x_prompt.shape[0]
    nbs, ls, _ = x_sample.shape
    assert ls == SEQ_S and x_prompt.shape[1] % TILE_P == 0 and (nbs * ls) % TILE_S == 0

    tab, bbar_re, bbar_im = _s5_prep(s5_lambda_re.astype(f32), s5_lambda_im.astype(f32), s5_log_dt.astype(f32),
                                     s5_b_re.astype(f32), s5_b_im.astype(f32))

    def row(v, n):
        return v.astype(f32).reshape(depth, 1, n)

    wi = w_in.astype(f32)
    w_in_r = jnp.concatenate(
        [wi[..., 0:1024], wi[..., 1024:3072], wi[..., 3088:3600], wi[..., 3600:4112], wi[..., 4112:4624],
         wi[..., 4624:5136], _pad_lanes(wi[..., 3072:3088])], axis=-1).astype(bf16)

    def halves(v):
        return _block_diag(v.reshape(depth, 2, S5_NGROUPS // 2, S5_GROUP, S5_STATE))

    bb = jnp.concatenate([halves(bbar_re), halves(bbar_im)], axis=1).astype(bf16)

    def chalves(v):
        return _block_diag(jnp.transpose(v.astype(f32), (0, 1, 3, 2)).reshape(depth, 2, S5_NGROUPS // 2, S5_STATE, S5_GROUP))

    cc = jnp.concatenate([chalves(s5_c_re), -chalves(s5_c_im)], axis=2).astype(bf16)
    weights = (
        row(norm_g, D_MODEL), w_in_r,
        ssd_conv_w.astype(f32), row(ssd_conv_b, SSD_CONV_DIM),
        _pad_lanes(row(ssd_dt_bias, SSD_HEADS)), _pad_lanes(row(ssd_a_log, SSD_HEADS)),
        jnp.repeat(ssd_d.astype(f32), SSD_HEADDIM, axis=-1).reshape(depth, 1, SSD_DIM),
        row(ssd_norm_g, SSD_DIM),
        lru_conv_w.astype(f32), row(lru_conv_b, LRU_DIM),
        jnp.concatenate([_block_diag(lru_wa.astype(f32)), _block_diag(lru_wx.astype(f32))], axis=-1).astype(bf16),
        jnp.concatenate([lru_ba, lru_bx], axis=-1).astype(f32).reshape(depth, 1, 2 * LRU_DIM),
        row(lru_lambda, LRU_DIM),
        tab, bb, cc, row(s5_d, S5_DIM),
        s5_glu_w.astype(bf16), row(s5_glu_b, S5_DIM),
        w_out.astype(bf16),
        jnp.broadcast_to(final_norm_g.astype(f32).reshape(1, 1, D_MODEL), (depth, 1, D_MODEL)),
    )

    xp = x_prompt.astype(f32)
    outs_p = [[] for _ in range(6)]
    for i in range(depth):
        res = _prompt_call(i, i == depth - 1, xp, weights)
        xp = res[0]
        for j in range(6):
            outs_p[j].append(res[1 + j])

    states_s = (state_ssd.astype(f32).reshape(depth, nbs, SSD_DIM, SSD_STATE), state_ssd_conv.astype(f32),
                state_lru.astype(f32), state_lru_conv.astype(f32),
                state_s5_re.astype(f32).reshape(depth, nbs, S5_FLAT), state_s5_im.astype(f32).reshape(depth, nbs, S5_FLAT))
    res_s = _sample_call(x_sample.astype(f32).reshape(nbs * ls, D_MODEL), states_s, weights)

    def stack(lst, shape, dtype):
        return jnp.stack(lst).reshape((depth,) + shape).astype(dtype)

    ssd_shape = (SSD_HEADS, SSD_HEADDIM, SSD_STATE)
    s5_shape = (S5_NGROUPS, S5_STATE)
    return (
        xp.astype(x_prompt.dtype), res_s[0][depth - 1].reshape(nbs, ls, D_MODEL).astype(x_sample.dtype),
        stack(outs_p[0], (nbp,) + ssd_shape, state_ssd.dtype),
        res_s[1].reshape((depth, nbs) + ssd_shape).astype(state_ssd.dtype),
        stack(outs_p[1], (nbp, CONV_WIDTH - 1, SSD_CONV_DIM), state_ssd_conv.dtype),
        res_s[2].astype(state_ssd_conv.dtype),
        stack(outs_p[2], (nbp, LRU_DIM), state_lru.dtype), res_s[3].astype(state_lru.dtype),
        stack(outs_p[3], (nbp, CONV_WIDTH - 1, LRU_DIM), state_lru_conv.dtype),
        res_s[4].astype(state_lru_conv.dtype),
        stack(outs_p[4], (nbp,) + s5_shape, state_s5_re.dtype),
        res_s[5].reshape((depth, nbs) + s5_shape).astype(state_s5_re.dtype),
        stack(outs_p[5], (nbp,) + s5_shape, state_s5_im.dtype),
        res_s[6].reshape((depth, nbs) + s5_shape).astype(state_s5_im.dtype),
    )
```

```python
import functools

import jax
import jax.numpy as jnp
from jax import lax
from jax.experimental import pallas as pl
from jax.experimental.pallas import tpu as pltpu

f32 = jnp.float32
bf16 = jnp.bfloat16

D_MODEL = 1024
CONV_WIDTH = 4
SSD_DIM = 1024
SSD_HEADDIM = 64
SSD_HEADS = 16
SSD_GROUPS = 4
SSD_HPG = 4
SSD_STATE = 128
SSD_BC = SSD_GROUPS * SSD_STATE
SSD_CONV_DIM = SSD_DIM + 2 * SSD_BC
LRU_DIM = 512
LRU_C = 8.0
S5_DIM = 512
S5_GROUP = 16
S5_NGROUPS = 32
S5_STATE = 64
S5_FLAT = S5_NGROUPS * S5_STATE
S5_HALF = S5_FLAT // 2
EPS = 1e-6

LANES = 128
SUBLANES = 8
CHUNK = 128
TILE_P = 256
TILE_S = 64
SEQ_S = 8
NEG = -1e30

SEG = CHUNK // SUBLANES
SEG_SHIFT = 4
SUB_SHIFT = 3
HALO = (CONV_WIDTH - 1) * SUBLANES

TAB_A = 0
TAB_Q = TAB_A + SUBLANES
TAB_ASEG = TAB_Q + 3 * SUBLANES
TAB_PW = TAB_ASEG + SUBLANES
TAB_ROWS = TAB_PW + SEG * SUBLANES

C_Z = 0
C_XBC = C_Z + SSD_DIM
C_LRU = C_XBC + SSD_CONV_DIM
C_LRU_G = C_LRU + LRU_DIM
C_S5 = C_LRU_G + LRU_DIM
C_S5_G = C_S5 + S5_DIM
C_DT = C_S5_G + S5_DIM
IN_COLS = C_DT + LANES

VMEM_LIMIT_BYTES = 56 * 1024 * 1024

N_WEIGHTS = 21


def _rms(x, g):
    return x * lax.rsqrt(jnp.mean(x * x, axis=-1, keepdims=True) + EPS) * g


def _silu(x):
    return x * jax.nn.sigmoid(x)


def _dot(a, b):
    return jnp.dot(a, b, preferred_element_type=f32)


def _dot_nt(a, b):
    return lax.dot_general(a, b, (((1,), (1,)), ((), ())), preferred_element_type=f32)


def _dot_tn(a, b):
    return lax.dot_general(a, b, (((0,), (0,)), ((), ())), preferred_element_type=f32)


def _dot_exact(a, b):
    return jnp.dot(a, b, preferred_element_type=f32, precision=lax.Precision.HIGHEST)


def _pair_expand(v, j, lane_lo):
    q = v.shape[0]
    lo = jnp.broadcast_to(v[:, 2 * j:2 * j + 1], (q, LANES))
    hi = jnp.broadcast_to(v[:, 2 * j + 1:2 * j + 2], (q, LANES))
    return jnp.where(lane_lo, lo, hi)


def _s5_prep_body(lre_ref, lim_ref, ldt_ref, lre_rep_ref, lim_rep_ref, ldt_rep_ref,
                  bre_ref, bim_ref, tre_ref, tim_ref, bbre_ref, bbim_ref):
    def abar(lre, lim, ldt):
        delta = jnp.exp(ldt)
        mag = jnp.exp(lre * delta)
        return mag * jnp.cos(lim * delta), mag * jnp.sin(lim * delta)

    ar, ai = abar(lre_ref[...], lim_ref[...], ldt_ref[...])

    def cmul(xr, xi, yr, yi):
        return xr * yr - xi * yi, xr * yi + xi * yr

    pw = [(ar, ai)]
    for _ in range(SEG - 1):
        pw.append(cmul(*pw[-1], ar, ai))
    seg = [pw[SEG - 1]]
    for _ in range(2):
        seg.append(cmul(*seg[-1], *seg[-1]))
    zero = jnp.zeros_like(ar)

    def put(i, v):
        tre_ref[i] = v[0]
        tim_ref[i] = v[1]

    for r in range(SUBLANES):
        put(TAB_A + r, pw[0])
        put(TAB_ASEG + r, seg[0])
        for t, d in enumerate((1, 2, 4)):
            put(TAB_Q + t * SUBLANES + r, seg[t] if r >= d else (zero, zero))
        for k in range(SEG):
            put(TAB_PW + k * SUBLANES + r, pw[k])

    lre, lim = lre_rep_ref[...], lim_rep_ref[...]
    ar, ai = abar(lre, lim, ldt_rep_ref[...])
    denom = lre * lre + lim * lim
    nr = ar - 1.0
    ni = ai
    coef_re = (nr * lre + ni * lim) / denom
    coef_im = (ni * lre - nr * lim) / denom
    bre, bim = bre_ref[...], bim_ref[...]
    bbre_ref[...] = coef_re * bre - coef_im * bim
    bbim_ref[...] = coef_re * bim + coef_im * bre


def _s5_prep(lam_re, lam_im, log_dt, b_re, b_im):
    depth = lam_re.shape[0]
    rows_c = S5_FLAT // LANES
    rows_r = S5_DIM * S5_STATE // LANES
    ldt = jnp.broadcast_to(log_dt[:, :, None], (depth, S5_NGROUPS, S5_STATE))

    def rep(v):
        return jnp.broadcast_to(v[:, :, None, :], (depth, S5_NGROUPS, S5_GROUP, S5_STATE)).reshape(depth, rows_r, LANES)

    def bt(v):
        return jnp.transpose(v, (0, 1, 3, 2)).reshape(depth, rows_r, LANES)

    cspec = pl.BlockSpec((None, rows_c, LANES), lambda i: (i, 0, 0))
    rspec = pl.BlockSpec((None, rows_r, LANES), lambda i: (i, 0, 0))
    tspec = pl.BlockSpec((None, TAB_ROWS, rows_c, LANES), lambda i: (i, 0, 0, 0))
    tre, tim, bbre, bbim = pl.pallas_call(
        _s5_prep_body,
        grid=(depth,),
        in_specs=[cspec, cspec, cspec, rspec, rspec, rspec, rspec, rspec],
        out_specs=[tspec, tspec, rspec, rspec],
        out_shape=[jax.ShapeDtypeStruct((depth, TAB_ROWS, rows_c, LANES), f32)] * 2
        + [jax.ShapeDtypeStruct((depth, rows_r, LANES), f32)] * 2,
        name="s5_prep",
    )(lam_re.reshape(depth, rows_c, LANES), lam_im.reshape(depth, rows_c, LANES),
      ldt.reshape(depth, rows_c, LANES), rep(lam_re), rep(lam_im), rep(ldt), bt(b_re), bt(b_im))
    tab = jnp.stack([tre, tim], axis=1).reshape(depth, 2, TAB_ROWS, S5_FLAT)
    return tab, bbre.reshape(depth, S5_DIM, S5_STATE), bbim.reshape(depth, S5_DIM, S5_STATE)


def _layer_math(prompt, T, x, w, st, o, scr):
    (ng_ref, w_in_ref, cw_ssd_ref, cb_ssd_ref, dtb_ref, alog_ref, dfull_ref, sng_ref,
     cw_lru_ref, cb_lru_ref, lru_w_ref, lru_b_ref, lam_ref,
     tab_ref, bb_ref, cc_ref, s5d_ref, glu_w_ref, glu_b_ref, w_out_ref, _) = w
    o_ssd_ref, o_cssd_ref, o_lru_ref, o_clru_ref, o_s5r_ref, o_s5i_ref = o
    if prompt:
        h_ssd, prev_ssd, prev_lru, a_s, b_s, bur_s, bui_s, lru_c, s5_cr, s5_ci = scr
        last = pl.program_id(1) == pl.num_programs(1) - 1
    else:
        h0_ssd_ref, c0_ssd_ref, h0_lru_ref, c0_lru_ref, h0_s5r_ref, h0_s5i_ref = st
        a_s, b_s, bur_s, bui_s, c_s, bm_s, xw_s, yoff_s, eac_s = scr
    nseq = T // SEQ_S
    Q = CHUNK if prompt else T

    hn = _rms(x, ng_ref[...]).astype(bf16)

    def proj(lo, hi):
        return _dot(hn, w_in_ref[:, lo:hi])

    def sub_iota(n):
        return lax.broadcasted_iota(jnp.int32, (SUBLANES, n), 0)

    def conv_taps(halo, rs, cw_ref, cb_ref):
        ext = jnp.concatenate([halo, rs], axis=0)
        n = rs.shape[0]
        acc = cb_ref[...] + cw_ref[3:4, :] * rs
        for j in range(1, CONV_WIDTH):
            acc = acc + cw_ref[3 - j:4 - j, :] * ext[HALO - SUBLANES * j:HALO - SUBLANES * j + n, :]
        return acc

    def conv(raw, prev_ref, c0_ref, cw_ref, cb_ref, o_ref):
        cdim = raw.shape[1]
        if not prompt:
            o_ref[...] = raw[T - HALO:, :].reshape(CONV_WIDTH - 1, SUBLANES, cdim)
            return conv_taps(c0_ref[...].reshape(HALO, cdim), raw, cw_ref, cb_ref)
        first = sub_iota(cdim) == 0
        tail = prev_ref[...]
        outs = []
        for r0 in range(0, T, CHUNK):
            rs = raw[r0:r0 + CHUNK, :]
            cur = rs[CHUNK - HALO:, :]
            halo = jnp.concatenate(
                [jnp.where(first, pltpu.roll(tail[SUBLANES * k:SUBLANES * (k + 1), :], 1, 0),
                           pltpu.roll(cur[SUBLANES * k:SUBLANES * (k + 1), :], 1, 0))
                 for k in range(CONV_WIDTH - 1)], axis=0)
            outs.append(conv_taps(halo, rs, cw_ref, cb_ref))
            tail = cur
        prev_ref[...] = tail
        for k in range(CONV_WIDTH - 1):
            o_ref[k:k + 1, :] = tail[SUBLANES * k + SUBLANES - 1:SUBLANES * (k + 1), :]
        return jnp.concatenate(outs, axis=0)

    row = lax.broadcasted_iota(jnp.int32, (Q, Q), 0)
    col = lax.broadcasted_iota(jnp.int32, (Q, Q), 1)
    if prompt:
        def local_time(i):
            return jnp.bitwise_or(jnp.left_shift(jnp.bitwise_and(i, SUBLANES - 1), SEG_SHIFT),
                                  jnp.right_shift(i, SUB_SHIFT))
        causal = local_time(row) >= local_time(col)
    else:
        same_seq = jnp.bitwise_and(row, SUBLANES - 1) == jnp.bitwise_and(col, SUBLANES - 1)
        causal = jnp.logical_and(same_seq, jnp.right_shift(row, SUB_SHIFT) >= jnp.right_shift(col, SUB_SHIFT))
    tril = jnp.where(causal, 1.0, 0.0)
    lane_lo = lax.broadcasted_iota(jnp.int32, (Q, LANES), 1) < SSD_HEADDIM
    gsz = SSD_HPG * SSD_HEADDIM

    xbc = _silu(conv(proj(C_XBC, C_LRU), prev_ssd if prompt else None, None if prompt else c0_ssd_ref,
                     cw_ssd_ref, cb_ssd_ref, o_cssd_ref))
    dt_all = jax.nn.softplus(proj(C_DT, IN_COLS) + dtb_ref[...])
    a_neg = -jnp.exp(alog_ref[...])

    def ssd_state_io(rows_c, rows_xw, rows_b, e_last, h_get, h_set):
        outs = []
        for g in range(SSD_GROUPS):
            hp = h_get(g)
            outs.append(_dot_nt(rows_c[:, LANES * g:LANES * (g + 1)].astype(bf16), hp.astype(bf16)))
            sg = _dot_tn(rows_xw[:, gsz * g:gsz * (g + 1)].astype(bf16),
                         rows_b[:, LANES * g:LANES * (g + 1)].astype(bf16))
            dec = jnp.concatenate(
                [jnp.broadcast_to(e_last[:, SSD_HPG * g + k:SSD_HPG * g + k + 1], (SSD_HEADDIM, SSD_STATE))
                 for k in range(SSD_HPG)], axis=0)
            h_set(g, dec * hp + sg)
        return jnp.concatenate(outs, axis=1)

    def ssd_chunk(r0):
        xs = xbc[r0:r0 + Q, :SSD_DIM]
        bm = xbc[r0:r0 + Q, SSD_DIM:SSD_DIM + SSD_BC]
        cm = xbc[r0:r0 + Q, SSD_DIM + SSD_BC:]
        dt = dt_all[r0:r0 + Q, :]
        bm_b = bm.astype(bf16)
        cm_b = cm.astype(bf16)
        acum = _dot_exact(tril, dt * a_neg)
        acum_row = acum.T
        dt_row = dt.T
        eac = jnp.exp(acum)
        if prompt:
            acum_end = acum[Q - 1:Q, :]
        else:
            sel = jnp.where(col == jnp.bitwise_and(row, SUBLANES - 1) + (Q - SUBLANES), 1.0, 0.0)
            acum_end = _dot_exact(sel, acum)
        wgt = jnp.exp(acum_end - acum) * dt

        scores = [_dot_nt(cm_b[:, LANES * g:LANES * (g + 1)], bm_b[:, LANES * g:LANES * (g + 1)])
                  for g in range(SSD_GROUPS)]
        y_pairs, xw_pairs, ecol_pairs = [], [], []
        for j in range(SSD_HEADS // 2):
            g = (2 * j) // SSD_HPG
            ms = []
            for h in (2 * j, 2 * j + 1):
                diff = acum[:, h:h + 1] - acum_row[h:h + 1, :]
                decay = jnp.exp(jnp.where(causal, diff, NEG))
                ms.append((scores[g] * decay * dt_row[h:h + 1, :]).astype(bf16))
            xp = xs[:, LANES * j:LANES * (j + 1)]
            xbd = jnp.concatenate([jnp.where(lane_lo, xp, 0.0), jnp.where(lane_lo, 0.0, xp)], axis=0).astype(bf16)
            y_pairs.append(_dot(jnp.concatenate(ms, axis=1), xbd))
            xw_pairs.append(xp * _pair_expand(wgt, j, lane_lo))
            ecol_pairs.append(_pair_expand(eac, j, lane_lo))
        y_diag = jnp.concatenate(y_pairs, axis=1)
        xw = jnp.concatenate(xw_pairs, axis=1)
        ecol = jnp.concatenate(ecol_pairs, axis=1)

        if prompt:
            def h_get(g):
                return h_ssd[gsz * g:gsz * (g + 1), :]

            def h_set(g, v):
                h_ssd[gsz * g:gsz * (g + 1), :] = v

            y_off = ssd_state_io(cm, xw, bm, eac[Q - 1:Q, :], h_get, h_set)
        else:
            to_seq = jnp.bitwise_or(jnp.left_shift(jnp.bitwise_and(row, SUBLANES - 1), SUB_SHIFT),
                                    jnp.right_shift(row, SUB_SHIFT)) == col
            perm_b = jnp.where(to_seq, 1.0, 0.0).astype(bf16)
            c_s[...] = _dot(perm_b, cm_b)
            bm_s[...] = _dot(perm_b, bm_b)
            xw_s[...] = _dot(perm_b, xw.astype(bf16))
            eac_s[...] = eac

            def seq_step(i, carry):
                s0 = pl.multiple_of(i * SEQ_S, SEQ_S)

                def h_get(g):
                    return h0_ssd_ref[i, pl.ds(gsz * g, gsz), :]

                def h_set(g, v):
                    o_ssd_ref[i, pl.ds(gsz * g, gsz), :] = v

                yoff_s[pl.ds(s0, SEQ_S), :] = ssd_state_io(
                    c_s[pl.ds(s0, SEQ_S), :], xw_s[pl.ds(s0, SEQ_S), :], bm_s[pl.ds(s0, SEQ_S), :],
                    eac_s[pl.ds(Q - SUBLANES + i, 1), :], h_get, h_set)
                return carry

            lax.fori_loop(0, nseq, seq_step, 0)
            y_off = _dot_exact(jnp.where(to_seq, 1.0, 0.0), yoff_s[...])
        return y_diag + y_off * ecol + dfull_ref[...] * xs

    y = jnp.concatenate([ssd_chunk(r0) for r0 in range(0, T, Q)], axis=0) if T > Q else ssd_chunk(0)
    if prompt:
        @pl.when(last)
        def _():
            o_ssd_ref[...] = h_ssd[...]
    y_ssd = _rms(y * _silu(proj(C_Z, C_XBC)), sng_ref[...])

    xr = conv(proj(C_LRU, C_LRU_G), prev_lru if prompt else None, None if prompt else c0_lru_ref,
              cw_lru_ref, cb_lru_ref, o_clru_ref)
    gates = _dot(xr.astype(bf16), lru_w_ref[...]) + lru_b_ref[...]
    r_gate = jax.nn.sigmoid(gates[:, :LRU_DIM])
    i_gate = jax.nn.sigmoid(gates[:, LRU_DIM:])
    log_a = -LRU_C * r_gate * jax.nn.softplus(-lam_ref[...])
    a_t = jnp.exp(log_a)
    gain = jnp.sqrt(jnp.maximum(-jnp.tanh(log_a) * (a_t * a_t + 1.0), 0.0))
    a_s[...] = a_t
    b_s[...] = gain * i_gate * xr

    def vrow(ref, r0, k):
        return ref[r0 + SUBLANES * k:r0 + SUBLANES * (k + 1), :]

    def set_vrow(ref, r0, k, v):
        ref[r0 + SUBLANES * k:r0 + SUBLANES * (k + 1), :] = v

    if prompt:
        sub = sub_iota(LRU_DIM)
        carry = lru_c[...]
        for r0 in range(0, T, CHUNK):
            acc_a, acc_h = vrow(a_s, r0, 0), vrow(b_s, r0, 0)
            for k in range(1, SEG):
                a_k = vrow(a_s, r0, k)
                acc_h = a_k * acc_h + vrow(b_s, r0, k)
                acc_a = a_k * acc_a
                set_vrow(a_s, r0, k, acc_a)
                set_vrow(b_s, r0, k, acc_h)
            alpha = jnp.where(sub == 0, 0.0, pltpu.roll(acc_a, 1, 0))
            beta = jnp.where(sub == 0, jnp.broadcast_to(carry, (SUBLANES, LRU_DIM)), pltpu.roll(acc_h, 1, 0))
            for d in (1, 2, 4):
                a_sh = jnp.where(sub >= d, pltpu.roll(alpha, d, 0), 1.0)
                b_sh = jnp.where(sub >= d, pltpu.roll(beta, d, 0), 0.0)
                beta = alpha * b_sh + beta
                alpha = alpha * a_sh
            carry = (acc_a * beta + acc_h)[SUBLANES - 1:SUBLANES, :]
            for k in range(SEG):
                set_vrow(b_s, r0, k, vrow(b_s, r0, k) + vrow(a_s, r0, k) * beta)
        lru_c[...] = carry
        o_lru_ref[...] = carry
    else:
        h = h0_lru_ref[...]
        for k in range(T // SUBLANES):
            h = vrow(a_s, 0, k) * h + vrow(b_s, 0, k)
            set_vrow(b_s, 0, k, h)
        o_lru_ref[...] = h
    y_lru = b_s[...] * _silu(proj(C_LRU_G, C_S5))

    u = proj(C_S5, C_S5_G)
    u_b = u.astype(bf16)
    half = S5_DIM // 2
    for k in range(2):
        uk = u_b[:, half * k:half * (k + 1)]
        bur_s[:, S5_HALF * k:S5_HALF * (k + 1)] = _dot(uk, bb_ref[k])
        bui_s[:, S5_HALF * k:S5_HALF * (k + 1)] = _dot(uk, bb_ref[2 + k])

    def tab(r0):
        return tab_ref[0, r0:r0 + SUBLANES, :], tab_ref[1, r0:r0 + SUBLANES, :]

    def cmul_add(pr, pi, xr, xi, yr, yi):
        return pr * xr - pi * xi + yr, pr * xi + pi * xr + yi

    ar, ai = tab(TAB_A)
    if prompt:
        sub = sub_iota(S5_FLAT)
        c_r, c_i = s5_cr[...], s5_ci[...]
        for r0 in range(0, T, CHUNK):
            hr, hi = vrow(bur_s, r0, 0), vrow(bui_s, r0, 0)
            for k in range(1, SEG):
                hr, hi = cmul_add(ar, ai, hr, hi, vrow(bur_s, r0, k), vrow(bui_s, r0, k))
                set_vrow(bur_s, r0, k, hr)
                set_vrow(bui_s, r0, k, hi)
            er = jnp.where(sub == 0, jnp.broadcast_to(c_r, (SUBLANES, S5_FLAT)), pltpu.roll(hr, 1, 0))
            ei = jnp.where(sub == 0, jnp.broadcast_to(c_i, (SUBLANES, S5_FLAT)), pltpu.roll(hi, 1, 0))
            for t, d in enumerate((1, 2, 4)):
                qr, qi = tab(TAB_Q + t * SUBLANES)
                er, ei = cmul_add(qr, qi, pltpu.roll(er, d, 0), pltpu.roll(ei, d, 0), er, ei)
            sr, si = tab(TAB_ASEG)
            nr, ni = cmul_add(sr, si, er, ei, hr, hi)
            c_r, c_i = nr[SUBLANES - 1:SUBLANES, :], ni[SUBLANES - 1:SUBLANES, :]
            for k in range(SEG):
                pr, pi = tab(TAB_PW + k * SUBLANES)
                vr, vi = cmul_add(pr, pi, er, ei, vrow(bur_s, r0, k), vrow(bui_s, r0, k))
                set_vrow(bur_s, r0, k, vr)
                set_vrow(bui_s, r0, k, vi)
        s5_cr[...] = c_r
        s5_ci[...] = c_i
        o_s5r_ref[...] = c_r
        o_s5i_ref[...] = c_i
    else:
        hr, hi = h0_s5r_ref[...], h0_s5i_ref[...]
        for k in range(T // SUBLANES):
            hr, hi = cmul_add(ar, ai, hr, hi, vrow(bur_s, 0, k), vrow(bui_s, 0, k))
            set_vrow(bur_s, 0, k, hr)
            set_vrow(bui_s, 0, k, hi)
        o_s5r_ref[...] = hr
        o_s5i_ref[...] = hi
    ys = []
    for k in range(2):
        hk = jnp.concatenate([bur_s[:, S5_HALF * k:S5_HALF * (k + 1)].astype(bf16),
                              bui_s[:, S5_HALF * k:S5_HALF * (k + 1)].astype(bf16)], axis=1)
        ys.append(_dot(hk, cc_ref[k]))
    ys5 = jnp.concatenate(ys, axis=1) + s5d_ref[...] * u
    ys5 = jax.nn.gelu(ys5)
    ys5 = ys5 * jax.nn.sigmoid(_dot(ys5.astype(bf16), glu_w_ref[...]) + glu_b_ref[...])
    y_s5 = ys5 * _silu(proj(C_S5_G, C_DT))

    ycat = jnp.concatenate([y_ssd.astype(bf16), y_lru.astype(bf16), y_s5.astype(bf16)], axis=1)
    return x + _dot(ycat, w_out_ref[...])


def _prompt_body(final, *refs):
    x_ref = refs[0]
    w = refs[1:1 + N_WEIGHTS]
    y_ref = refs[1 + N_WEIGHTS]
    o = refs[2 + N_WEIGHTS:8 + N_WEIGHTS]
    scr = refs[8 + N_WEIGHTS:]
    h_ssd, prev_ssd, prev_lru, _, _, _, _, lru_c, s5_cr, s5_ci = scr

    @pl.when(pl.program_id(1) == 0)
    def _():
        h_ssd[...] = jnp.zeros_like(h_ssd)
        prev_ssd[...] = jnp.zeros_like(prev_ssd)
        prev_lru[...] = jnp.zeros_like(prev_lru)
        lru_c[...] = jnp.zeros_like(lru_c)
        s5_cr[...] = jnp.zeros_like(s5_cr)
        s5_ci[...] = jnp.zeros_like(s5_ci)

    out = _layer_math(True, TILE_P, x_ref[...], w, None, o, scr)
    if final:
        out = _rms(out, w[-1][...])
    y_ref[...] = out


def _sample_body(*refs):
    x_ref = refs[0]
    st = refs[1:7]
    w = refs[7:7 + N_WEIGHTS]
    y_ref = refs[7 + N_WEIGHTS]
    o = refs[8 + N_WEIGHTS:14 + N_WEIGHTS]
    x_all = refs[14 + N_WEIGHTS]
    scr = refs[15 + N_WEIGHTS:]
    layer = pl.program_id(0)
    last_layer = layer == pl.num_programs(0) - 1
    r0 = pl.multiple_of(pl.program_id(1) * TILE_S, TILE_S)

    @pl.when(layer == 0)
    def _():
        x_all[pl.ds(r0, TILE_S), :] = x_ref[...]

    out = _layer_math(False, TILE_S, x_all[pl.ds(r0, TILE_S), :], w, st, o, scr)
    x_all[pl.ds(r0, TILE_S), :] = out

    @pl.when(last_layer)
    def _():
        y_ref[...] = _rms(out, w[-1][...])

    @pl.when(jnp.logical_not(last_layer))
    def _():
        y_ref[...] = out


def _prompt_call(layer, final, x, weights):
    T = TILE_P
    nb, seq, _ = x.shape

    def wspec(a):
        nd = a.ndim - 1
        return pl.BlockSpec((None,) + a.shape[1:], lambda b, c: (layer,) + (0,) * nd, pipeline_mode=pl.Buffered(1))

    def st(shape):
        nd = len(shape)
        return pl.BlockSpec((None,) + shape, lambda b, c: (b,) + (0,) * nd)

    x_spec = pl.BlockSpec((None, T, D_MODEL), lambda b, c: (b, c, 0))
    out_specs = [x_spec, st((SSD_DIM, SSD_STATE)), st((CONV_WIDTH - 1, SSD_CONV_DIM)), st((1, LRU_DIM)),
                 st((CONV_WIDTH - 1, LRU_DIM)), st((1, S5_FLAT)), st((1, S5_FLAT))]
    out_shape = [jax.ShapeDtypeStruct(x.shape, f32),
                 jax.ShapeDtypeStruct((nb, SSD_DIM, SSD_STATE), f32),
                 jax.ShapeDtypeStruct((nb, CONV_WIDTH - 1, SSD_CONV_DIM), f32),
                 jax.ShapeDtypeStruct((nb, 1, LRU_DIM), f32),
                 jax.ShapeDtypeStruct((nb, CONV_WIDTH - 1, LRU_DIM), f32),
                 jax.ShapeDtypeStruct((nb, 1, S5_FLAT), f32),
                 jax.ShapeDtypeStruct((nb, 1, S5_FLAT), f32)]
    scratch = [pltpu.VMEM((SSD_DIM, SSD_STATE), f32),
               pltpu.VMEM((HALO, SSD_CONV_DIM), f32),
               pltpu.VMEM((HALO, LRU_DIM), f32),
               pltpu.VMEM((T, LRU_DIM), f32), pltpu.VMEM((T, LRU_DIM), f32),
               pltpu.VMEM((T, S5_FLAT), f32), pltpu.VMEM((T, S5_FLAT), f32),
               pltpu.VMEM((1, LRU_DIM), f32),
               pltpu.VMEM((1, S5_FLAT), f32), pltpu.VMEM((1, S5_FLAT), f32)]
    return pl.pallas_call(
        functools.partial(_prompt_body, final),
        grid=(nb, seq // T), in_specs=[x_spec] + [wspec(a) for a in weights],
        out_specs=out_specs, out_shape=out_shape, scratch_shapes=scratch,
        compiler_params=pltpu.CompilerParams(dimension_semantics=("arbitrary", "arbitrary"),
                                             vmem_limit_bytes=VMEM_LIMIT_BYTES),
        name="layer_prompt",
    )(x, *weights)


def _sample_call(x, states, weights):
    T = TILE_S
    rows = x.shape[0]
    depth = weights[0].shape[0]
    nseq = T // SEQ_S

    def wspec(a):
        nd = a.ndim - 1
        return pl.BlockSpec((None,) + a.shape[1:], lambda l, i: (l,) + (0,) * nd, pipeline_mode=pl.Buffered(1))

    def st(a):
        nd = a.ndim - 2
        if a.ndim == 5:
            return pl.BlockSpec((None, None) + a.shape[2:], lambda l, i: (l, i) + (0,) * nd)
        return pl.BlockSpec((None, nseq) + a.shape[2:], lambda l, i: (l, i) + (0,) * nd)

    x_spec = pl.BlockSpec((T, D_MODEL), lambda l, i: (i, 0))
    st_specs = [st(a) for a in states]
    scratch = [pltpu.VMEM((rows, D_MODEL), f32),
               pltpu.VMEM((T, LRU_DIM), f32), pltpu.VMEM((T, LRU_DIM), f32),
               pltpu.VMEM((T, S5_FLAT), f32), pltpu.VMEM((T, S5_FLAT), f32),
               pltpu.VMEM((T, SSD_BC), f32), pltpu.VMEM((T, SSD_BC), f32),
               pltpu.VMEM((T, SSD_DIM), f32), pltpu.VMEM((T, SSD_DIM), f32),
               pltpu.VMEM((T, LANES), f32)]
    return pl.pallas_call(
        _sample_body,
        grid=(depth, rows // T), in_specs=[x_spec] + st_specs + [wspec(a) for a in weights],
        out_specs=[pl.BlockSpec((None, T, D_MODEL), lambda l, i: (l, i, 0))] + st_specs,
        out_shape=[jax.ShapeDtypeStruct((depth,) + x.shape, f32)] + [jax.ShapeDtypeStruct(a.shape, f32) for a in states],
        scratch_shapes=scratch,
        compiler_params=pltpu.CompilerParams(dimension_semantics=("arbitrary", "arbitrary"),
                                             vmem_limit_bytes=VMEM_LIMIT_BYTES),
        name="layers_sample",
    )(x, *states, *weights)


def _block_diag(blocks):
    *lead, n, r, c = blocks.shape
    eye = jnp.eye(n, dtype=blocks.dtype)
    return (blocks[..., :, :, None, :] * eye[:, None, :, None]).reshape(*lead, n * r, n * c)


def _pad_lanes(v):
    return jnp.pad(v, [(0, 0)] * (v.ndim - 1) + [(0, LANES - v.shape[-1])])


def kernel(x_prompt, x_sample, state_ssd, state_ssd_conv, state_lru, state_lru_conv, state_s5_re, state_s5_im, norm_g, w_in, ssd_conv_w, ssd_conv_b, ssd_dt_bias, ssd_a_log, ssd_d, ssd_norm_g, lru_conv_w, lru_conv_b, lru_wa, lru_ba, lru_wx, lru_bx, lru_lambda, s5_lambda_re, s5_lambda_im, s5_log_dt, s5_b_re, s5_b_im, s5_c_re, s5_c_im, s5_d, s5_glu_w, s5_glu_b, w_out, final_norm_g):
    depth = w_in.shape[0]
    nbp = x_prompt.shape[0]
    nbs, ls, _ = x_sample.shape
    assert ls == SEQ_S and x_prompt.shape[1] % TILE_P == 0 and (nbs * ls) % TILE_S == 0

    tab, bbar_re, bbar_im = _s5_prep(s5_lambda_re.astype(f32), s5_lambda_im.astype(f32), s5_log_dt.astype(f32),
                                     s5_b_re.astype(f32), s5_b_im.astype(f32))

    def row(v, n):
        return v.astype(f32).reshape(depth, 1, n)

    wi = w_in.astype(f32)
    w_in_r = jnp.concatenate(
        [wi[..., 0:1024], wi[..., 1024:3072], wi[..., 3088:3600], wi[..., 3600:4112], wi[..., 4112:4624],
         wi[..., 4624:5136], _pad_lanes(wi[..., 3072:3088])], axis=-1).astype(bf16)

    def halves(v):
        return _block_diag(v.reshape(depth, 2, S5_NGROUPS // 2, S5_GROUP, S5_STATE))

    bb = jnp.concatenate([halves(bbar_re), halves(bbar_im)], axis=1).astype(bf16)

    def chalves(v):
        return _block_diag(jnp.transpose(v.astype(f32), (0, 1, 3, 2)).reshape(depth, 2, S5_NGROUPS // 2, S5_STATE, S5_GROUP))

    cc = jnp.concatenate([chalves(s5_c_re), -chalves(s5_c_im)], axis=2).astype(bf16)
    weights = (
        row(norm_g, D_MODEL), w_in_r,
        ssd_conv_w.astype(f32), row(ssd_conv_b, SSD_CONV_DIM),
        _pad_lanes(row(ssd_dt_bias, SSD_HEADS)), _pad_lanes(row(ssd_a_log, SSD_HEADS)),
        jnp.repeat(ssd_d.astype(f32), SSD_HEADDIM, axis=-1).reshape(depth, 1, SSD_DIM),
        row(ssd_norm_g, SSD_DIM),
        lru_conv_w.astype(f32), row(lru_conv_b, LRU_DIM),
        jnp.concatenate([_block_diag(lru_wa.astype(f32)), _block_diag(lru_wx.astype(f32))], axis=-1).astype(bf16),
        jnp.concatenate([lru_ba, lru_bx], axis=-1).astype(f32).reshape(depth, 1, 2 * LRU_DIM),
        row(lru_lambda, LRU_DIM),
        tab, bb, cc, row(s5_d, S5_DIM),
        s5_glu_w.astype(bf16), row(s5_glu_b, S5_DIM),
        w_out.astype(bf16),
        jnp.broadcast_to(final_norm_g.astype(f32).reshape(1, 1, D_MODEL), (depth, 1, D_MODEL)),
    )

    seq_p = x_prompt.shape[1]
    xp = jnp.swapaxes(x_prompt.astype(f32).reshape(nbp, seq_p // CHUNK, SUBLANES, SEG, D_MODEL), 2, 3)
    xp = xp.reshape(nbp, seq_p, D_MODEL)
    outs_p = [[] for _ in range(6)]
    for i in range(depth):
        res = _prompt_call(i, i == depth - 1, xp, weights)
        xp = res[0]
        for j in range(6):
            outs_p[j].append(res[1 + j])
    y_prompt = jnp.swapaxes(xp.reshape(nbp, seq_p // CHUNK, SEG, SUBLANES, D_MODEL), 2, 3).reshape(nbp, seq_p, D_MODEL)

    ntile = nbs // SUBLANES

    def conv_in(v):
        return jnp.swapaxes(v.astype(f32).reshape(depth, ntile, SUBLANES, CONV_WIDTH - 1, v.shape[-1]), 2, 3)

    def conv_out(v, dtype):
        return jnp.swapaxes(v, 2, 3).reshape(depth, nbs, CONV_WIDTH - 1, v.shape[-1]).astype(dtype)

    xs = jnp.swapaxes(x_sample.astype(f32).reshape(ntile, SUBLANES, ls, D_MODEL), 1, 2).reshape(nbs * ls, D_MODEL)
    states_s = (state_ssd.astype(f32).reshape(depth, nbs, SSD_DIM, SSD_STATE), conv_in(state_ssd_conv),
                state_lru.astype(f32), conv_in(state_lru_conv),
                state_s5_re.astype(f32).reshape(depth, nbs, S5_FLAT), state_s5_im.astype(f32).reshape(depth, nbs, S5_FLAT))
    res_s = _sample_call(xs, states_s, weights)
    y_sample = jnp.swapaxes(res_s[0][depth - 1].reshape(ntile, ls, SUBLANES, D_MODEL), 1, 2).reshape(nbs, ls, D_MODEL)

    def stack(lst, shape, dtype):
        return jnp.stack(lst).reshape((depth,) + shape).astype(dtype)

    ssd_shape = (SSD_HEADS, SSD_HEADDIM, SSD_STATE)
    s5_shape = (S5_NGROUPS, S5_STATE)
    return (
        y_prompt.astype(x_prompt.dtype), y_sample.astype(x_sample.dtype),
        stack(outs_p[0], (nbp,) + ssd_shape, state_ssd.dtype),
        res_s[1].reshape((depth, nbs) + ssd_shape).astype(state_ssd.dtype),
        stack(outs_p[1], (nbp, CONV_WIDTH - 1, SSD_CONV_DIM), state_ssd_conv.dtype),
        conv_out(res_s[2], state_ssd_conv.dtype),
        stack(outs_p[2], (nbp, LRU_DIM), state_lru.dtype), res_s[3].astype(state_lru.dtype),
        stack(outs_p[3], (nbp, CONV_WIDTH - 1, LRU_DIM), state_lru_conv.dtype),
        conv_out(res_s[4], state_lru_conv.dtype),
        stack(outs_p[4], (nbp,) + s5_shape, state_s5_re.dtype),
        res_s[5].reshape((depth, nbs) + s5_shape).astype(state_s5_re.dtype),
        stack(outs_p[5], (nbp,) + s5_shape, state_s5_im.dtype),
        res_s[6].reshape((depth, nbs) + s5_shape).astype(state_s5_im.dtype),
    )
```

```python
import functools

import jax
import jax.numpy as jnp
from jax import lax
from jax.experimental import pallas as pl
from jax.experimental.pallas import tpu as pltpu

f32 = jnp.float32
bf16 = jnp.bfloat16

D_MODEL = 1024
CONV_WIDTH = 4
SSD_DIM = 1024
SSD_HEADDIM = 64
SSD_HEADS = 16
SSD_GROUPS = 4
SSD_HPG = 4
SSD_STATE = 128
SSD_BC = SSD_GROUPS * SSD_STATE
SSD_CONV_DIM = SSD_DIM + 2 * SSD_BC
LRU_DIM = 512
LRU_C = 8.0
S5_DIM = 512
S5_GROUP = 16
S5_NGROUPS = 32
S5_STATE = 64
S5_FLAT = S5_NGROUPS * S5_STATE
S5_HALF = S5_FLAT // 2
EPS = 1e-6

LANES = 128
SUBLANES = 8
CHUNK = 128
TILE_P = 256
TILE_S = 64
SEQ_S = 8
NEG = -1e30

SEG = CHUNK // SUBLANES
SEG_SHIFT = 4
SUB_SHIFT = 3
HALO = (CONV_WIDTH - 1) * SUBLANES

TAB_A = 0
TAB_Q = TAB_A + SUBLANES
TAB_ASEG = TAB_Q + 3 * SUBLANES
TAB_PW = TAB_ASEG + SUBLANES
TAB_ROWS = TAB_PW + SEG * SUBLANES

C_Z = 0
C_XBC = C_Z + SSD_DIM
C_LRU = C_XBC + SSD_CONV_DIM
C_LRU_G = C_LRU + LRU_DIM
C_S5 = C_LRU_G + LRU_DIM
C_S5_G = C_S5 + S5_DIM
C_DT = C_S5_G + S5_DIM
IN_COLS = C_DT + LANES

VMEM_LIMIT_BYTES = 56 * 1024 * 1024

N_WEIGHTS = 21


def _rms(x, g):
    return x * lax.rsqrt(jnp.mean(x * x, axis=-1, keepdims=True) + EPS) * g


def _silu(x):
    return x * jax.nn.sigmoid(x)


def _dot(a, b):
    return jnp.dot(a, b, preferred_element_type=f32)


def _dot_nt(a, b):
    return lax.dot_general(a, b, (((1,), (1,)), ((), ())), preferred_element_type=f32)


def _dot_tn(a, b):
    return lax.dot_general(a, b, (((0,), (0,)), ((), ())), preferred_element_type=f32)


def _dot_exact(a, b):
    return jnp.dot(a, b, preferred_element_type=f32, precision=lax.Precision.HIGHEST)


def _pair_expand(v, j, lane_lo):
    q = v.shape[0]
    lo = jnp.broadcast_to(v[:, 2 * j:2 * j + 1], (q, LANES))
    hi = jnp.broadcast_to(v[:, 2 * j + 1:2 * j + 2], (q, LANES))
    return jnp.where(lane_lo, lo, hi)


def _s5_prep_body(lre_ref, lim_ref, ldt_ref, lre_rep_ref, lim_rep_ref, ldt_rep_ref,
                  bre_ref, bim_ref, tre_ref, tim_ref, bbre_ref, bbim_ref):
    def abar(lre, lim, ldt):
        delta = jnp.exp(ldt)
        mag = jnp.exp(lre * delta)
        return mag * jnp.cos(lim * delta), mag * jnp.sin(lim * delta)

    ar, ai = abar(lre_ref[...], lim_ref[...], ldt_ref[...])

    def cmul(xr, xi, yr, yi):
        return xr * yr - xi * yi, xr * yi + xi * yr

    pw = [(ar, ai)]
    for _ in range(SEG - 1):
        pw.append(cmul(*pw[-1], ar, ai))
    seg = [pw[SEG - 1]]
    for _ in range(2):
        seg.append(cmul(*seg[-1], *seg[-1]))
    zero = jnp.zeros_like(ar)

    def put(i, v):
        tre_ref[i] = v[0]
        tim_ref[i] = v[1]

    for r in range(SUBLANES):
        put(TAB_A + r, pw[0])
        put(TAB_ASEG + r, seg[0])
        for t, d in enumerate((1, 2, 4)):
            put(TAB_Q + t * SUBLANES + r, seg[t] if r >= d else (zero, zero))
        for k in range(SEG):
            put(TAB_PW + k * SUBLANES + r, pw[k])

    lre, lim = lre_rep_ref[...], lim_rep_ref[...]
    ar, ai = abar(lre, lim, ldt_rep_ref[...])
    denom = lre * lre + lim * lim
    nr = ar - 1.0
    ni = ai
    coef_re = (nr * lre + ni * lim) / denom
    coef_im = (ni * lre - nr * lim) / denom
    bre, bim = bre_ref[...], bim_ref[...]
    bbre_ref[...] = coef_re * bre - coef_im * bim
    bbim_ref[...] = coef_re * bim + coef_im * bre


def _s5_prep(lam_re, lam_im, log_dt, b_re, b_im):
    depth = lam_re.shape[0]
    rows_c = S5_FLAT // LANES
    rows_r = S5_DIM * S5_STATE // LANES
    ldt = jnp.broadcast_to(log_dt[:, :, None], (depth, S5_NGROUPS, S5_STATE))

    def rep(v):
        return jnp.broadcast_to(v[:, :, None, :], (depth, S5_NGROUPS, S5_GROUP, S5_STATE)).reshape(depth, rows_r, LANES)

    def bt(v):
        return jnp.transpose(v, (0, 1, 3, 2)).reshape(depth, rows_r, LANES)

    cspec = pl.BlockSpec((None, rows_c, LANES), lambda i: (i, 0, 0))
    rspec = pl.BlockSpec((None, rows_r, LANES), lambda i: (i, 0, 0))
    tspec = pl.BlockSpec((None, TAB_ROWS, rows_c, LANES), lambda i: (i, 0, 0, 0))
    tre, tim, bbre, bbim = pl.pallas_call(
        _s5_prep_body,
        grid=(depth,),
        in_specs=[cspec, cspec, cspec, rspec, rspec, rspec, rspec, rspec],
        out_specs=[tspec, tspec, rspec, rspec],
        out_shape=[jax.ShapeDtypeStruct((depth, TAB_ROWS, rows_c, LANES), f32)] * 2
        + [jax.ShapeDtypeStruct((depth, rows_r, LANES), f32)] * 2,
        name="s5_prep",
    )(lam_re.reshape(depth, rows_c, LANES), lam_im.reshape(depth, rows_c, LANES),
      ldt.reshape(depth, rows_c, LANES), rep(lam_re), rep(lam_im), rep(ldt), bt(b_re), bt(b_im))
    tab = jnp.stack([tre, tim], axis=1).reshape(depth, 2, TAB_ROWS, S5_FLAT)
    return tab, bbre.reshape(depth, S5_DIM, S5_STATE), bbim.reshape(depth, S5_DIM, S5_STATE)


def _layer_math(prompt, T, x, w, st, o, scr):
    (ng_ref, w_in_ref, cw_ssd_ref, cb_ssd_ref, dtb_ref, alog_ref, dfull_ref, sng_ref,
     cw_lru_ref, cb_lru_ref, lru_w_ref, lru_b_ref, lam_ref,
     tab_ref, bb_ref, cc_ref, s5d_ref, glu_w_ref, glu_b_ref, w_out_ref, _) = w
    o_ssd_ref, o_cssd_ref, o_lru_ref, o_clru_ref, o_s5r_ref, o_s5i_ref = o
    if prompt:
        h_ssd, prev_ssd, prev_lru, a_s, b_s, bur_s, bui_s, lru_c, s5_cr, s5_ci = scr
        last = pl.program_id(1) == pl.num_programs(1) - 1
    else:
        h0_ssd_ref, c0_ssd_ref, h0_lru_ref, c0_lru_ref, h0_s5r_ref, h0_s5i_ref = st
        a_s, b_s, bur_s, bui_s, c_s, bm_s, xw_s, yoff_s, eac_s = scr
    nseq = T // SEQ_S
    Q = CHUNK if prompt else T

    hn = _rms(x, ng_ref[...]).astype(bf16)

    def proj(lo, hi):
        return _dot(hn, w_in_ref[:, lo:hi])

    def sub_iota(n):
        return lax.broadcasted_iota(jnp.int32, (SUBLANES, n), 0)

    def conv_taps(halo, rs, cw_ref, cb_ref):
        ext = jnp.concatenate([halo, rs], axis=0)
        n = rs.shape[0]
        acc = cb_ref[...] + cw_ref[3:4, :] * rs
        for j in range(1, CONV_WIDTH):
            acc = acc + cw_ref[3 - j:4 - j, :] * ext[HALO - SUBLANES * j:HALO - SUBLANES * j + n, :]
        return acc

    def conv(raw, prev_ref, c0_ref, cw_ref, cb_ref, o_ref):
        cdim = raw.shape[1]
        if not prompt:
            o_ref[...] = raw[T - HALO:, :].reshape(CONV_WIDTH - 1, SUBLANES, cdim)
            return conv_taps(c0_ref[...].reshape(HALO, cdim), raw, cw_ref, cb_ref)
        first = sub_iota(cdim) == 0
        tail = prev_ref[...]
        outs = []
        for r0 in range(0, T, CHUNK):
            rs = raw[r0:r0 + CHUNK, :]
            cur = rs[CHUNK - HALO:, :]
            halo = jnp.concatenate(
                [jnp.where(first, pltpu.roll(tail[SUBLANES * k:SUBLANES * (k + 1), :], 1, 0),
                           pltpu.roll(cur[SUBLANES * k:SUBLANES * (k + 1), :], 1, 0))
                 for k in range(CONV_WIDTH - 1)], axis=0)
            outs.append(conv_taps(halo, rs, cw_ref, cb_ref))
            tail = cur
        prev_ref[...] = tail
        for k in range(CONV_WIDTH - 1):
            o_ref[k:k + 1, :] = tail[SUBLANES * k + SUBLANES - 1:SUBLANES * (k + 1), :]
        return jnp.concatenate(outs, axis=0)

    row = lax.broadcasted_iota(jnp.int32, (Q, Q), 0)
    col = lax.broadcasted_iota(jnp.int32, (Q, Q), 1)
    if prompt:
        def local_time(i):
            return jnp.bitwise_or(jnp.left_shift(jnp.bitwise_and(i, SUBLANES - 1), SEG_SHIFT),
                                  jnp.right_shift(i, SUB_SHIFT))
        causal = local_time(row) >= local_time(col)
    else:
        same_seq = jnp.bitwise_and(row, SUBLANES - 1) == jnp.bitwise_and(col, SUBLANES - 1)
        causal = jnp.logical_and(same_seq, jnp.right_shift(row, SUB_SHIFT) >= jnp.right_shift(col, SUB_SHIFT))
    tril = jnp.where(causal, 1.0, 0.0)
    lane_lo = lax.broadcasted_iota(jnp.int32, (Q, LANES), 1) < SSD_HEADDIM
    gsz = SSD_HPG * SSD_HEADDIM

    xbc = _silu(conv(proj(C_XBC, C_LRU), prev_ssd if prompt else None, None if prompt else c0_ssd_ref,
                     cw_ssd_ref, cb_ssd_ref, o_cssd_ref))
    dt_all = jax.nn.softplus(proj(C_DT, IN_COLS) + dtb_ref[...])
    a_neg = -jnp.exp(alog_ref[...])

    def ssd_state_io(rows_c, rows_xw, rows_b, e_last, h_get, h_set):
        outs = []
        for g in range(SSD_GROUPS):
            hp = h_get(g)
            outs.append(_dot_nt(rows_c[:, LANES * g:LANES * (g + 1)].astype(bf16), hp.astype(bf16)))
            sg = _dot_tn(rows_xw[:, gsz * g:gsz * (g + 1)].astype(bf16),
                         rows_b[:, LANES * g:LANES * (g + 1)].astype(bf16))
            dec = jnp.concatenate(
                [jnp.broadcast_to(e_last[:, SSD_HPG * g + k:SSD_HPG * g + k + 1], (SSD_HEADDIM, SSD_STATE))
                 for k in range(SSD_HPG)], axis=0)
            h_set(g, dec * hp + sg)
        return jnp.concatenate(outs, axis=1)

    def ssd_chunk(r0):
        xs = xbc[r0:r0 + Q, :SSD_DIM]
        bm = xbc[r0:r0 + Q, SSD_DIM:SSD_DIM + SSD_BC]
        cm = xbc[r0:r0 + Q, SSD_DIM + SSD_BC:]
        dt = dt_all[r0:r0 + Q, :]
        bm_b = bm.astype(bf16)
        cm_b = cm.astype(bf16)
        acum = _dot_exact(tril, dt * a_neg)
        acum_row = acum.T
        dt_row = dt.T
        eac = jnp.exp(acum)
        if prompt:
            acum_end = acum[Q - 1:Q, :]
        else:
            sel = jnp.where(col == jnp.bitwise_and(row, SUBLANES - 1) + (Q - SUBLANES), 1.0, 0.0)
            acum_end = _dot_exact(sel, acum)
        wgt = jnp.exp(acum_end - acum) * dt

        scores = [_dot_nt(cm_b[:, LANES * g:LANES * (g + 1)], bm_b[:, LANES * g:LANES * (g + 1)])
                  for g in range(SSD_GROUPS)]
        y_pairs, xw_pairs, ecol_pairs = [], [], []
        for j in range(SSD_HEADS // 2):
            g = (2 * j) // SSD_HPG
            ms = []
            for h in (2 * j, 2 * j + 1):
                diff = acum[:, h:h + 1] - acum_row[h:h + 1, :]
                decay = jnp.exp(jnp.where(causal, diff, NEG))
                ms.append((scores[g] * decay * dt_row[h:h + 1, :]).astype(bf16))
            xp = xs[:, LANES * j:LANES * (j + 1)]
            xbd = jnp.concatenate([jnp.where(lane_lo, xp, 0.0), jnp.where(lane_lo, 0.0, xp)], axis=0).astype(bf16)
            y_pairs.append(_dot(jnp.concatenate(ms, axis=1), xbd))
            xw_pairs.append(xp * _pair_expand(wgt, j, lane_lo))
            ecol_pairs.append(_pair_expand(eac, j, lane_lo))
        y_diag = jnp.concatenate(y_pairs, axis=1)
        xw = jnp.concatenate(xw_pairs, axis=1)
        ecol = jnp.concatenate(ecol_pairs, axis=1)

        if prompt:
            def h_get(g):
                return h_ssd[gsz * g:gsz * (g + 1), :]

            def h_set(g, v):
                h_ssd[gsz * g:gsz * (g + 1), :] = v

            y_off = ssd_state_io(cm, xw, bm, eac[Q - 1:Q, :], h_get, h_set)
        else:
            to_seq = jnp.bitwise_or(jnp.left_shift(jnp.bitwise_and(row, SUBLANES - 1), SUB_SHIFT),
                                    jnp.right_shift(row, SUB_SHIFT)) == col
            perm_b = jnp.where(to_seq, 1.0, 0.0).astype(bf16)
            c_s[...] = _dot(perm_b, cm_b)
            bm_s[...] = _dot(perm_b, bm_b)
            xw_s[...] = _dot(perm_b, xw.astype(bf16))
            eac_s[...] = eac

            def seq_step(i, carry):
                s0 = pl.multiple_of(i * SEQ_S, SEQ_S)

                def h_get(g):
                    return h0_ssd_ref[i, pl.ds(gsz * g, gsz), :]

                def h_set(g, v):
                    o_ssd_ref[i, pl.ds(gsz * g, gsz), :] = v

                yoff_s[pl.ds(s0, SEQ_S), :] = ssd_state_io(
                    c_s[pl.ds(s0, SEQ_S), :], xw_s[pl.ds(s0, SEQ_S), :], bm_s[pl.ds(s0, SEQ_S), :],
                    eac_s[pl.ds(Q - SUBLANES + i, 1), :], h_get, h_set)
                return carry

            lax.fori_loop(0, nseq, seq_step, 0)
            y_off = _dot_exact(jnp.where(to_seq, 1.0, 0.0), yoff_s[...])
        return y_diag + y_off * ecol + dfull_ref[...] * xs

    y_chunks = [ssd_chunk(r0) for r0 in range(0, T, Q)]
    y = jnp.concatenate(y_chunks, axis=0) if len(y_chunks) > 1 else y_chunks[0]
    if prompt:
        o_ssd_ref[...] = h_ssd[...]
    y_ssd = _rms(y * _silu(proj(C_Z, C_XBC)), sng_ref[...])

    xr = conv(proj(C_LRU, C_LRU_G), prev_lru if prompt else None, None if prompt else c0_lru_ref,
              cw_lru_ref, cb_lru_ref, o_clru_ref)
    xr_b = xr.astype(bf16)
    hl = LRU_DIM // 2
    gates = jnp.concatenate([_dot(xr_b[:, hl * (k % 2):hl * (k % 2 + 1)], lru_w_ref[k]) for k in range(4)],
                            axis=1) + lru_b_ref[...]
    r_gate = jax.nn.sigmoid(gates[:, :LRU_DIM])
    i_gate = jax.nn.sigmoid(gates[:, LRU_DIM:])
    log_a = -LRU_C * r_gate * jax.nn.softplus(-lam_ref[...])
    a_t = jnp.exp(log_a)
    gain = jnp.sqrt(jnp.maximum(-jnp.tanh(log_a) * (a_t * a_t + 1.0), 0.0))
    a_s[...] = a_t
    b_s[...] = gain * i_gate * xr

    def vrow(ref, r0, k):
        return ref[r0 + SUBLANES * k:r0 + SUBLANES * (k + 1), :]

    def set_vrow(ref, r0, k, v):
        ref[r0 + SUBLANES * k:r0 + SUBLANES * (k + 1), :] = v

    if prompt:
        sub = sub_iota(LRU_DIM)
        carry = lru_c[...]
        for r0 in range(0, T, CHUNK):
            acc_a, acc_h = vrow(a_s, r0, 0), vrow(b_s, r0, 0)
            for k in range(1, SEG):
                a_k = vrow(a_s, r0, k)
                acc_h = a_k * acc_h + vrow(b_s, r0, k)
                acc_a = a_k * acc_a
                set_vrow(a_s, r0, k, acc_a)
                set_vrow(b_s, r0, k, acc_h)
            alpha = jnp.where(sub == 0, 0.0, pltpu.roll(acc_a, 1, 0))
            beta = jnp.where(sub == 0, jnp.broadcast_to(carry, (SUBLANES, LRU_DIM)), pltpu.roll(acc_h, 1, 0))
            for d in (1, 2, 4):
                a_sh = jnp.where(sub >= d, pltpu.roll(alpha, d, 0), 1.0)
                b_sh = jnp.where(sub >= d, pltpu.roll(beta, d, 0), 0.0)
                beta = alpha * b_sh + beta
                alpha = alpha * a_sh
            carry = (acc_a * beta + acc_h)[SUBLANES - 1:SUBLANES, :]
            for k in range(SEG):
                set_vrow(b_s, r0, k, vrow(b_s, r0, k) + vrow(a_s, r0, k) * beta)
        lru_c[...] = carry
        o_lru_ref[...] = carry
    else:
        h = h0_lru_ref[...]
        for k in range(T // SUBLANES):
            h = vrow(a_s, 0, k) * h + vrow(b_s, 0, k)
            set_vrow(b_s, 0, k, h)
        o_lru_ref[...] = h
    y_lru = b_s[...] * _silu(proj(C_LRU_G, C_S5))

    u = proj(C_S5, C_S5_G)
    u_b = u.astype(bf16)
    half = S5_DIM // 2
    for k in range(2):
        uk = u_b[:, half * k:half * (k + 1)]
        bur_s[:, S5_HALF * k:S5_HALF * (k + 1)] = _dot(uk, bb_ref[k])
        bui_s[:, S5_HALF * k:S5_HALF * (k + 1)] = _dot(uk, bb_ref[2 + k])

    def tab(r0):
        return tab_ref[0, r0:r0 + SUBLANES, :], tab_ref[1, r0:r0 + SUBLANES, :]

    def cmul_add(pr, pi, xr, xi, yr, yi):
        return pr * xr - pi * xi + yr, pr * xi + pi * xr + yi

    ar, ai = tab(TAB_A)
    if prompt:
        sub = sub_iota(S5_FLAT)
        c_r, c_i = s5_cr[...], s5_ci[...]
        for r0 in range(0, T, CHUNK):
            hr, hi = vrow(bur_s, r0, 0), vrow(bui_s, r0, 0)
            for k in range(1, SEG):
                hr, hi = cmul_add(ar, ai, hr, hi, vrow(bur_s, r0, k), vrow(bui_s, r0, k))
                set_vrow(bur_s, r0, k, hr)
                set_vrow(bui_s, r0, k, hi)
            er = jnp.where(sub == 0, jnp.broadcast_to(c_r, (SUBLANES, S5_FLAT)), pltpu.roll(hr, 1, 0))
            ei = jnp.where(sub == 0, jnp.broadcast_to(c_i, (SUBLANES, S5_FLAT)), pltpu.roll(hi, 1, 0))
            for t, d in enumerate((1, 2, 4)):
                qr, qi = tab(TAB_Q + t * SUBLANES)
                er, ei = cmul_add(qr, qi, pltpu.roll(er, d, 0), pltpu.roll(ei, d, 0), er, ei)
            sr, si = tab(TAB_ASEG)
            nr, ni = cmul_add(sr, si, er, ei, hr, hi)
            c_r, c_i = nr[SUBLANES - 1:SUBLANES, :], ni[SUBLANES - 1:SUBLANES, :]
            for k in range(SEG):
                pr, pi = tab(TAB_PW + k * SUBLANES)
                vr, vi = cmul_add(pr, pi, er, ei, vrow(bur_s, r0, k), vrow(bui_s, r0, k))
                set_vrow(bur_s, r0, k, vr)
                set_vrow(bui_s, r0, k, vi)
        s5_cr[...] = c_r
        s5_ci[...] = c_i
        o_s5r_ref[...] = c_r
        o_s5i_ref[...] = c_i
    else:
        hr, hi = h0_s5r_ref[...], h0_s5i_ref[...]
        for k in range(T // SUBLANES):
            hr, hi = cmul_add(ar, ai, hr, hi, vrow(bur_s, 0, k), vrow(bui_s, 0, k))
            set_vrow(bur_s, 0, k, hr)
            set_vrow(bui_s, 0, k, hi)
        o_s5r_ref[...] = hr
        o_s5i_ref[...] = hi
    ys = []
    for k in range(2):
        hk = jnp.concatenate([bur_s[:, S5_HALF * k:S5_HALF * (k + 1)].astype(bf16),
                              bui_s[:, S5_HALF * k:S5_HALF * (k + 1)].astype(bf16)], axis=1)
        ys.append(_dot(hk, cc_ref[k]))
    ys5 = jnp.concatenate(ys, axis=1) + s5d_ref[...] * u
    ys5 = jax.nn.gelu(ys5)
    ys5 = ys5 * jax.nn.sigmoid(_dot(ys5.astype(bf16), glu_w_ref[...]) + glu_b_ref[...])
    y_s5 = ys5 * _silu(proj(C_S5_G, C_DT))

    ycat = jnp.concatenate([y_ssd.astype(bf16), y_lru.astype(bf16), y_s5.astype(bf16)], axis=1)
    return x + _dot(ycat, w_out_ref[...])


def _prompt_body(final, *refs):
    x_ref = refs[0]
    w = refs[1:1 + N_WEIGHTS]
    y_ref = refs[1 + N_WEIGHTS]
    o = refs[2 + N_WEIGHTS:8 + N_WEIGHTS]
    scr = refs[8 + N_WEIGHTS:]
    h_ssd, prev_ssd, prev_lru, _, _, _, _, lru_c, s5_cr, s5_ci = scr

    @pl.when(pl.program_id(1) == 0)
    def _():
        h_ssd[...] = jnp.zeros_like(h_ssd)
        prev_ssd[...] = jnp.zeros_like(prev_ssd)
        prev_lru[...] = jnp.zeros_like(prev_lru)
        lru_c[...] = jnp.zeros_like(lru_c)
        s5_cr[...] = jnp.zeros_like(s5_cr)
        s5_ci[...] = jnp.zeros_like(s5_ci)

    out = _layer_math(True, TILE_P, x_ref[...], w, None, o, scr)
    if final:
        out = _rms(out, w[-1][...])
    y_ref[...] = out


def _sample_body(*refs):
    x_ref = refs[0]
    st = refs[1:7]
    w = refs[7:7 + N_WEIGHTS]
    y_ref = refs[7 + N_WEIGHTS]
    o = refs[8 + N_WEIGHTS:14 + N_WEIGHTS]
    x_all = refs[14 + N_WEIGHTS]
    scr = refs[15 + N_WEIGHTS:]
    layer = pl.program_id(0)
    last_layer = layer == pl.num_programs(0) - 1
    r0 = pl.multiple_of(pl.program_id(1) * TILE_S, TILE_S)

    @pl.when(layer == 0)
    def _():
        x_all[pl.ds(r0, TILE_S), :] = x_ref[...]

    out = _layer_math(False, TILE_S, x_all[pl.ds(r0, TILE_S), :], w, st, o, scr)
    x_all[pl.ds(r0, TILE_S), :] = out

    @pl.when(last_layer)
    def _():
        y_ref[...] = _rms(out, w[-1][...])

    @pl.when(jnp.logical_not(last_layer))
    def _():
        y_ref[...] = out


def _prompt_call(layer, final, x, weights):
    T = TILE_P
    nb, seq, _ = x.shape

    def wspec(a):
        nd = a.ndim - 1
        return pl.BlockSpec((None,) + a.shape[1:], lambda b, c: (layer,) + (0,) * nd, pipeline_mode=pl.Buffered(1))

    def st(shape):
        nd = len(shape)
        return pl.BlockSpec((None,) + shape, lambda b, c: (b,) + (0,) * nd)

    x_spec = pl.BlockSpec((None, T, D_MODEL), lambda b, c: (b, c, 0))
    out_specs = [x_spec, st((SSD_DIM, SSD_STATE)), st((CONV_WIDTH - 1, SSD_CONV_DIM)), st((1, LRU_DIM)),
                 st((CONV_WIDTH - 1, LRU_DIM)), st((1, S5_FLAT)), st((1, S5_FLAT))]
    out_shape = [jax.ShapeDtypeStruct(x.shape, f32),
                 jax.ShapeDtypeStruct((nb, SSD_DIM, SSD_STATE), f32),
                 jax.ShapeDtypeStruct((nb, CONV_WIDTH - 1, SSD_CONV_DIM), f32),
                 jax.ShapeDtypeStruct((nb, 1, LRU_DIM), f32),
                 jax.ShapeDtypeStruct((nb, CONV_WIDTH - 1, LRU_DIM), f32),
                 jax.ShapeDtypeStruct((nb, 1, S5_FLAT), f32),
                 jax.ShapeDtypeStruct((nb, 1, S5_FLAT), f32)]
    scratch = [pltpu.VMEM((SSD_DIM, SSD_STATE), f32),
               pltpu.VMEM((HALO, SSD_CONV_DIM), f32),
               pltpu.VMEM((HALO, LRU_DIM), f32),
               pltpu.VMEM((T, LRU_DIM), f32), pltpu.VMEM((T, LRU_DIM), f32),
               pltpu.VMEM((T, S5_FLAT), f32), pltpu.VMEM((T, S5_FLAT), f32),
               pltpu.VMEM((1, LRU_DIM), f32),
               pltpu.VMEM((1, S5_FLAT), f32), pltpu.VMEM((1, S5_FLAT), f32)]
    return pl.pallas_call(
        functools.partial(_prompt_body, final),
        grid=(nb, seq // T), in_specs=[x_spec] + [wspec(a) for a in weights],
        out_specs=out_specs, out_shape=out_shape, scratch_shapes=scratch,
        compiler_params=pltpu.CompilerParams(dimension_semantics=("arbitrary", "arbitrary"),
                                             vmem_limit_bytes=VMEM_LIMIT_BYTES),
        name="layer_prompt",
    )(x, *weights)


def _sample_call(x, states, weights):
    T = TILE_S
    rows = x.shape[0]
    depth = weights[0].shape[0]
    nseq = T // SEQ_S

    def wspec(a):
        nd = a.ndim - 1
        return pl.BlockSpec((None,) + a.shape[1:], lambda l, i: (l,) + (0,) * nd, pipeline_mode=pl.Buffered(1))

    def st(a):
        nd = a.ndim - 2
        if a.ndim == 5:
            return pl.BlockSpec((None, None) + a.shape[2:], lambda l, i: (l, i) + (0,) * nd)
        return pl.BlockSpec((None, nseq) + a.shape[2:], lambda l, i: (l, i) + (0,) * nd)

    x_spec = pl.BlockSpec((T, D_MODEL), lambda l, i: (i, 0))
    st_specs = [st(a) for a in states]
    scratch = [pltpu.VMEM((rows, D_MODEL), f32),
               pltpu.VMEM((T, LRU_DIM), f32), pltpu.VMEM((T, LRU_DIM), f32),
               pltpu.VMEM((T, S5_FLAT), f32), pltpu.VMEM((T, S5_FLAT), f32),
               pltpu.VMEM((T, SSD_BC), f32), pltpu.VMEM((T, SSD_BC), f32),
               pltpu.VMEM((T, SSD_DIM), f32), pltpu.VMEM((T, SSD_DIM), f32),
               pltpu.VMEM((T, LANES), f32)]
    return pl.pallas_call(
        _sample_body,
        grid=(depth, rows // T), in_specs=[x_spec] + st_specs + [wspec(a) for a in weights],
        out_specs=[pl.BlockSpec((None, T, D_MODEL), lambda l, i: (l, i, 0))] + st_specs,
        out_shape=[jax.ShapeDtypeStruct((depth,) + x.shape, f32)] + [jax.ShapeDtypeStruct(a.shape, f32) for a in states],
        scratch_shapes=scratch,
        compiler_params=pltpu.CompilerParams(dimension_semantics=("arbitrary", "arbitrary"),
                                             vmem_limit_bytes=VMEM_LIMIT_BYTES),
        name="layers_sample",
    )(x, *states, *weights)


def _block_diag(blocks):
    *lead, n, r, c = blocks.shape
    eye = jnp.eye(n, dtype=blocks.dtype)
    return (blocks[..., :, :, None, :] * eye[:, None, :, None]).reshape(*lead, n * r, n * c)


def _pad_lanes(v):
    return jnp.pad(v, [(0, 0)] * (v.ndim - 1) + [(0, LANES - v.shape[-1])])


def kernel(x_prompt, x_sample, state_ssd, state_ssd_conv, state_lru, state_lru_conv, state_s5_re, state_s5_im, norm_g, w_in, ssd_conv_w, ssd_conv_b, ssd_dt_bias, ssd_a_log, ssd_d, ssd_norm_g, lru_conv_w, lru_conv_b, lru_wa, lru_ba, lru_wx, lru_bx, lru_lambda, s5_lambda_re, s5_lambda_im, s5_log_dt, s5_b_re, s5_b_im, s5_c_re, s5_c_im, s5_d, s5_glu_w, s5_glu_b, w_out, final_norm_g):
    depth = w_in.shape[0]
    nbp = x_prompt.shape[0]
    nbs, ls, _ = x_sample.shape
    assert ls == SEQ_S and x_prompt.shape[1] % TILE_P == 0 and (nbs * ls) % TILE_S == 0

    tab, bbar_re, bbar_im = _s5_prep(s5_lambda_re.astype(f32), s5_lambda_im.astype(f32), s5_log_dt.astype(f32),
                                     s5_b_re.astype(f32), s5_b_im.astype(f32))

    def row(v, n):
        return v.astype(f32).reshape(depth, 1, n)

    wi = w_in.astype(f32)
    w_in_r = jnp.concatenate(
        [wi[..., 0:1024], wi[..., 1024:3072], wi[..., 3088:3600], wi[..., 3600:4112], wi[..., 4112:4624],
         wi[..., 4624:5136], _pad_lanes(wi[..., 3072:3088])], axis=-1).astype(bf16)

    def halves(v):
        return _block_diag(v.reshape(depth, 2, S5_NGROUPS // 2, S5_GROUP, S5_STATE))

    bb = jnp.concatenate([halves(bbar_re), halves(bbar_im)], axis=1).astype(bf16)

    def chalves(v):
        return _block_diag(jnp.transpose(v.astype(f32), (0, 1, 3, 2)).reshape(depth, 2, S5_NGROUPS // 2, S5_STATE, S5_GROUP))

    cc = jnp.concatenate([chalves(s5_c_re), -chalves(s5_c_im)], axis=2).astype(bf16)
    weights = (
        row(norm_g, D_MODEL), w_in_r,
        ssd_conv_w.astype(f32), row(ssd_conv_b, SSD_CONV_DIM),
        _pad_lanes(row(ssd_dt_bias, SSD_HEADS)), _pad_lanes(row(ssd_a_log, SSD_HEADS)),
        jnp.repeat(ssd_d.astype(f32), SSD_HEADDIM, axis=-1).reshape(depth, 1, SSD_DIM),
        row(ssd_norm_g, SSD_DIM),
        lru_conv_w.astype(f32), row(lru_conv_b, LRU_DIM),
        jnp.concatenate([_block_diag(v.astype(f32).reshape(depth, 2, v.shape[1] // 2, *v.shape[2:]))
                         for v in (lru_wa, lru_wx)], axis=1).astype(bf16),
        jnp.concatenate([lru_ba, lru_bx], axis=-1).astype(f32).reshape(depth, 1, 2 * LRU_DIM),
        row(lru_lambda, LRU_DIM),
        tab, bb, cc, row(s5_d, S5_DIM),
        s5_glu_w.astype(bf16), row(s5_glu_b, S5_DIM),
        w_out.astype(bf16),
        jnp.broadcast_to(final_norm_g.astype(f32).reshape(1, 1, D_MODEL), (depth, 1, D_MODEL)),
    )

    seq_p = x_prompt.shape[1]
    xp = jnp.swapaxes(x_prompt.astype(f32).reshape(nbp, seq_p // CHUNK, SUBLANES, SEG, D_MODEL), 2, 3)
    xp = xp.reshape(nbp, seq_p, D_MODEL)
    outs_p = [[] for _ in range(6)]
    for i in range(depth):
        res = _prompt_call(i, i == depth - 1, xp, weights)
        xp = res[0]
        for j in range(6):
            outs_p[j].append(res[1 + j])
    y_prompt = jnp.swapaxes(xp.reshape(nbp, seq_p // CHUNK, SEG, SUBLANES, D_MODEL), 2, 3).reshape(nbp, seq_p, D_MODEL)

    ntile = nbs // SUBLANES

    def conv_in(v):
        return jnp.swapaxes(v.astype(f32).reshape(depth, ntile, SUBLANES, CONV_WIDTH - 1, v.shape[-1]), 2, 3)

    def conv_out(v, dtype):
        return jnp.swapaxes(v, 2, 3).reshape(depth, nbs, CONV_WIDTH - 1, v.shape[-1]).astype(dtype)

    xs = jnp.swapaxes(x_sample.astype(f32).reshape(ntile, SUBLANES, ls, D_MODEL), 1, 2).reshape(nbs * ls, D_MODEL)
    states_s = (state_ssd.astype(f32).reshape(depth, nbs, SSD_DIM, SSD_STATE), conv_in(state_ssd_conv),
                state_lru.astype(f32), conv_in(state_lru_conv),
                state_s5_re.astype(f32).reshape(depth, nbs, S5_FLAT), state_s5_im.astype(f32).reshape(depth, nbs, S5_FLAT))
    res_s = _sample_call(xs, states_s, weights)
    y_sample = jnp.swapaxes(res_s[0][depth - 1].reshape(ntile, ls, SUBLANES, D_MODEL), 1, 2).reshape(nbs, ls, D_MODEL)

    def stack(lst, shape, dtype):
        return jnp.stack(lst).reshape((depth,) + shape).astype(dtype)

    ssd_shape = (SSD_HEADS, SSD_HEADDIM, SSD_STATE)
    s5_shape = (S5_NGROUPS, S5_STATE)
    return (
        y_prompt.astype(x_prompt.dtype), y_sample.astype(x_sample.dtype),
        stack(outs_p[0], (nbp,) + ssd_shape, state_ssd.dtype),
        res_s[1].reshape((depth, nbs) + ssd_shape).astype(state_ssd.dtype),
        stack(outs_p[1], (nbp, CONV_WIDTH - 1, SSD_CONV_DIM), state_ssd_conv.dtype),
        conv_out(res_s[2], state_ssd_conv.dtype),
        stack(outs_p[2], (nbp, LRU_DIM), state_lru.dtype), res_s[3].astype(state_lru.dtype),
        stack(outs_p[3], (nbp, CONV_WIDTH - 1, LRU_DIM), state_lru_conv.dtype),
        conv_out(res_s[4], state_lru_conv.dtype),
        stack(outs_p[4], (nbp,) + s5_shape, state_s5_re.dtype),
        res_s[5].reshape((depth, nbs) + s5_shape).astype(state_s5_re.dtype),
        stack(outs_p[5], (nbp,) + s5_shape, state_s5_im.dtype),
        res_s[6].reshape((depth, nbs) + s5_shape).astype(state_s5_im.dtype),
    )
```

```python
import functools

import jax
import jax.numpy as jnp
from jax import lax
from jax.experimental import pallas as pl
from jax.experimental.pallas import tpu as pltpu

f32 = jnp.float32
bf16 = jnp.bfloat16

D_MODEL = 1024
CONV_WIDTH = 4
SSD_DIM = 1024
SSD_HEADDIM = 64
SSD_HEADS = 16
SSD_GROUPS = 4
SSD_HPG = 4
SSD_STATE = 128
SSD_BC = SSD_GROUPS * SSD_STATE
SSD_CONV_DIM = SSD_DIM + 2 * SSD_BC
LRU_DIM = 512
LRU_C = 8.0
S5_DIM = 512
S5_GROUP = 16
S5_NGROUPS = 32
S5_STATE = 64
S5_FLAT = S5_NGROUPS * S5_STATE
S5_HALF = S5_FLAT // 2
EPS = 1e-6

LANES = 128
SUBLANES = 8
CHUNK = 128
TILE_P = 256
TILE_S = 256
SUBTILE_S = 64
SUBTILE_SHIFT = 6
SEQ_S = 8
NEG = -1e30

SEG = CHUNK // SUBLANES
SEG_SHIFT = 4
SUB_SHIFT = 3
HALO = (CONV_WIDTH - 1) * SUBLANES

TAB_A = 0
TAB_Q = TAB_A + SUBLANES
TAB_ASEG = TAB_Q + 3 * SUBLANES
TAB_PW = TAB_ASEG + SUBLANES
TAB_ROWS = TAB_PW + SEG * SUBLANES

C_Z = 0
C_XBC = C_Z + SSD_DIM
C_LRU = C_XBC + SSD_CONV_DIM
C_LRU_G = C_LRU + LRU_DIM
C_S5 = C_LRU_G + LRU_DIM
C_S5_G = C_S5 + S5_DIM
C_DT = C_S5_G + S5_DIM
IN_COLS = C_DT + LANES

VMEM_LIMIT_BYTES = 56 * 1024 * 1024

N_WEIGHTS = 21


def _rms(x, g):
    return x * lax.rsqrt(jnp.mean(x * x, axis=-1, keepdims=True) + EPS) * g


def _silu(x):
    return x * jax.nn.sigmoid(x)


def _dot(a, b):
    return jnp.dot(a, b, preferred_element_type=f32)


def _dot_nt(a, b):
    return lax.dot_general(a, b, (((1,), (1,)), ((), ())), preferred_element_type=f32)


def _dot_tn(a, b):
    return lax.dot_general(a, b, (((0,), (0,)), ((), ())), preferred_element_type=f32)


def _dot_exact(a, b):
    return jnp.dot(a, b, preferred_element_type=f32, precision=lax.Precision.HIGHEST)


def _pair_expand(v, j, lane_lo):
    q = v.shape[0]
    lo = jnp.broadcast_to(v[:, 2 * j:2 * j + 1], (q, LANES))
    hi = jnp.broadcast_to(v[:, 2 * j + 1:2 * j + 2], (q, LANES))
    return jnp.where(lane_lo, lo, hi)


def _s5_prep_body(lre_ref, lim_ref, ldt_ref, lre_rep_ref, lim_rep_ref, ldt_rep_ref,
                  bre_ref, bim_ref, tre_ref, tim_ref, bbre_ref, bbim_ref):
    def abar(lre, lim, ldt):
        delta = jnp.exp(ldt)
        mag = jnp.exp(lre * delta)
        return mag * jnp.cos(lim * delta), mag * jnp.sin(lim * delta)

    ar, ai = abar(lre_ref[...], lim_ref[...], ldt_ref[...])

    def cmul(xr, xi, yr, yi):
        return xr * yr - xi * yi, xr * yi + xi * yr

    pw = [(ar, ai)]
    for _ in range(SEG - 1):
        pw.append(cmul(*pw[-1], ar, ai))
    seg = [pw[SEG - 1]]
    for _ in range(2):
        seg.append(cmul(*seg[-1], *seg[-1]))
    zero = jnp.zeros_like(ar)

    def put(i, v):
        tre_ref[i] = v[0]
        tim_ref[i] = v[1]

    for r in range(SUBLANES):
        put(TAB_A + r, pw[0])
        put(TAB_ASEG + r, seg[0])
        for t, d in enumerate((1, 2, 4)):
            put(TAB_Q + t * SUBLANES + r, seg[t] if r >= d else (zero, zero))
        for k in range(SEG):
            put(TAB_PW + k * SUBLANES + r, pw[k])

    lre, lim = lre_rep_ref[...], lim_rep_ref[...]
    ar, ai = abar(lre, lim, ldt_rep_ref[...])
    denom = lre * lre + lim * lim
    nr = ar - 1.0
    ni = ai
    coef_re = (nr * lre + ni * lim) / denom
    coef_im = (ni * lre - nr * lim) / denom
    bre, bim = bre_ref[...], bim_ref[...]
    bbre_ref[...] = coef_re * bre - coef_im * bim
    bbim_ref[...] = coef_re * bim + coef_im * bre


def _s5_prep(lam_re, lam_im, log_dt, b_re, b_im):
    depth = lam_re.shape[0]
    rows_c = S5_FLAT // LANES
    rows_r = S5_DIM * S5_STATE // LANES
    ldt = jnp.broadcast_to(log_dt[:, :, None], (depth, S5_NGROUPS, S5_STATE))

    def rep(v):
        return jnp.broadcast_to(v[:, :, None, :], (depth, S5_NGROUPS, S5_GROUP, S5_STATE)).reshape(depth, rows_r, LANES)

    def bt(v):
        return jnp.transpose(v, (0, 1, 3, 2)).reshape(depth, rows_r, LANES)

    cspec = pl.BlockSpec((None, rows_c, LANES), lambda i: (i, 0, 0))
    rspec = pl.BlockSpec((None, rows_r, LANES), lambda i: (i, 0, 0))
    tspec = pl.BlockSpec((None, TAB_ROWS, rows_c, LANES), lambda i: (i, 0, 0, 0))
    tre, tim, bbre, bbim = pl.pallas_call(
        _s5_prep_body,
        grid=(depth,),
        in_specs=[cspec, cspec, cspec, rspec, rspec, rspec, rspec, rspec],
        out_specs=[tspec, tspec, rspec, rspec],
        out_shape=[jax.ShapeDtypeStruct((depth, TAB_ROWS, rows_c, LANES), f32)] * 2
        + [jax.ShapeDtypeStruct((depth, rows_r, LANES), f32)] * 2,
        name="s5_prep",
    )(lam_re.reshape(depth, rows_c, LANES), lam_im.reshape(depth, rows_c, LANES),
      ldt.reshape(depth, rows_c, LANES), rep(lam_re), rep(lam_im), rep(ldt), bt(b_re), bt(b_im))
    tab = jnp.stack([tre, tim], axis=1).reshape(depth, 2, TAB_ROWS, S5_FLAT)
    return tab, bbre.reshape(depth, S5_DIM, S5_STATE), bbim.reshape(depth, S5_DIM, S5_STATE)


def _layer_math(prompt, T, x, w, st, o, scr):
    (ng_ref, w_in_ref, cw_ssd_ref, cb_ssd_ref, dtb_ref, alog_ref, dfull_ref, sng_ref,
     cw_lru_ref, cb_lru_ref, lru_w_ref, lru_b_ref, lam_ref,
     tab_ref, bb_ref, cc_ref, s5d_ref, glu_w_ref, glu_b_ref, w_out_ref, _) = w
    o_ssd_ref, o_cssd_ref, o_lru_ref, o_clru_ref, o_s5r_ref, o_s5i_ref = o
    if prompt:
        h_ssd, prev_ssd, prev_lru, a_s, b_s, bur_s, bui_s, lru_c, s5_cr, s5_ci = scr
    else:
        h0_ssd_hbm, c0_ssd_ref, h0_lru_ref, c0_lru_ref, h0_s5r_ref, h0_s5i_ref = st
        a_s, b_s, bur_s, bui_s, c_s, bm_s, xw_s, yoff_s, eac_s, h_in, h_out, sem_in, sem_out = scr
    nseq = T // SEQ_S
    Q = CHUNK if prompt else SUBTILE_S

    hn = _rms(x, ng_ref[...]).astype(bf16)

    def proj(lo, hi):
        return _dot(hn, w_in_ref[:, lo:hi])

    def sub_iota(n):
        return lax.broadcasted_iota(jnp.int32, (SUBLANES, n), 0)

    def conv_taps(halo, rs, cw_ref, cb_ref):
        ext = jnp.concatenate([halo, rs], axis=0)
        n = rs.shape[0]
        acc = cb_ref[...] + cw_ref[3:4, :] * rs
        for j in range(1, CONV_WIDTH):
            acc = acc + cw_ref[3 - j:4 - j, :] * ext[HALO - SUBLANES * j:HALO - SUBLANES * j + n, :]
        return acc

    def conv(raw, prev_ref, c0_ref, cw_ref, cb_ref, o_ref):
        cdim = raw.shape[1]
        if not prompt:
            outs = []
            for s in range(T // Q):
                rs = raw[Q * s:Q * (s + 1), :]
                o_ref[s] = rs[Q - HALO:, :].reshape(CONV_WIDTH - 1, SUBLANES, cdim)
                outs.append(conv_taps(c0_ref[s].reshape(HALO, cdim), rs, cw_ref, cb_ref))
            return jnp.concatenate(outs, axis=0)
        first = sub_iota(cdim) == 0
        tail = prev_ref[...]
        outs = []
        for r0 in range(0, T, CHUNK):
            rs = raw[r0:r0 + CHUNK, :]
            cur = rs[CHUNK - HALO:, :]
            halo = jnp.concatenate(
                [jnp.where(first, pltpu.roll(tail[SUBLANES * k:SUBLANES * (k + 1), :], 1, 0),
                           pltpu.roll(cur[SUBLANES * k:SUBLANES * (k + 1), :], 1, 0))
                 for k in range(CONV_WIDTH - 1)], axis=0)
            outs.append(conv_taps(halo, rs, cw_ref, cb_ref))
            tail = cur
        prev_ref[...] = tail
        for k in range(CONV_WIDTH - 1):
            o_ref[k:k + 1, :] = tail[SUBLANES * k + SUBLANES - 1:SUBLANES * (k + 1), :]
        return jnp.concatenate(outs, axis=0)

    row = lax.broadcasted_iota(jnp.int32, (Q, Q), 0)
    col = lax.broadcasted_iota(jnp.int32, (Q, Q), 1)
    if prompt:
        def local_time(i):
            return jnp.bitwise_or(jnp.left_shift(jnp.bitwise_and(i, SUBLANES - 1), SEG_SHIFT),
                                  jnp.right_shift(i, SUB_SHIFT))
        causal = local_time(row) >= local_time(col)
    else:
        same_seq = jnp.bitwise_and(row, SUBLANES - 1) == jnp.bitwise_and(col, SUBLANES - 1)
        causal = jnp.logical_and(same_seq, jnp.right_shift(row, SUB_SHIFT) >= jnp.right_shift(col, SUB_SHIFT))
    tril = jnp.where(causal, 1.0, 0.0)
    lane_lo = lax.broadcasted_iota(jnp.int32, (Q, LANES), 1) < SSD_HEADDIM
    gsz = SSD_HPG * SSD_HEADDIM

    xbc = _silu(conv(proj(C_XBC, C_LRU), prev_ssd if prompt else None, None if prompt else c0_ssd_ref,
                     cw_ssd_ref, cb_ssd_ref, o_cssd_ref))
    dt_all = jax.nn.softplus(proj(C_DT, IN_COLS) + dtb_ref[...])
    a_neg = -jnp.exp(alog_ref[...])

    def ssd_state_io(rows_c, rows_xw, rows_b, e_last, h_get, h_set):
        outs = []
        for g in range(SSD_GROUPS):
            hp = h_get(g)
            outs.append(_dot_nt(rows_c[:, LANES * g:LANES * (g + 1)].astype(bf16), hp.astype(bf16)))
            sg = _dot_tn(rows_xw[:, gsz * g:gsz * (g + 1)].astype(bf16),
                         rows_b[:, LANES * g:LANES * (g + 1)].astype(bf16))
            dec = jnp.concatenate(
                [jnp.broadcast_to(e_last[:, SSD_HPG * g + k:SSD_HPG * g + k + 1], (SSD_HEADDIM, SSD_STATE))
                 for k in range(SSD_HPG)], axis=0)
            h_set(g, dec * hp + sg)
        return jnp.concatenate(outs, axis=1)

    def ssd_chunk(r0):
        xs = xbc[r0:r0 + Q, :SSD_DIM]
        bm = xbc[r0:r0 + Q, SSD_DIM:SSD_DIM + SSD_BC]
        cm = xbc[r0:r0 + Q, SSD_DIM + SSD_BC:]
        dt = dt_all[r0:r0 + Q, :]
        bm_b = bm.astype(bf16)
        cm_b = cm.astype(bf16)
        acum = _dot_exact(tril, dt * a_neg)
        acum_row = acum.T
        dt_row = dt.T
        eac = jnp.exp(acum)
        if prompt:
            acum_end = acum[Q - 1:Q, :]
        else:
            sel = jnp.where(col == jnp.bitwise_and(row, SUBLANES - 1) + (Q - SUBLANES), 1.0, 0.0)
            acum_end = _dot_exact(sel, acum)
        wgt = jnp.exp(acum_end - acum) * dt

        scores = [_dot_nt(cm_b[:, LANES * g:LANES * (g + 1)], bm_b[:, LANES * g:LANES * (g + 1)])
                  for g in range(SSD_GROUPS)]
        y_pairs, xw_pairs, ecol_pairs = [], [], []
        for j in range(SSD_HEADS // 2):
            g = (2 * j) // SSD_HPG
            ms = []
            for h in (2 * j, 2 * j + 1):
                diff = acum[:, h:h + 1] - acum_row[h:h + 1, :]
                decay = jnp.exp(jnp.where(causal, diff, NEG))
                ms.append((scores[g] * decay * dt_row[h:h + 1, :]).astype(bf16))
            xp = xs[:, LANES * j:LANES * (j + 1)]
            xbd = jnp.concatenate([jnp.where(lane_lo, xp, 0.0), jnp.where(lane_lo, 0.0, xp)], axis=0).astype(bf16)
            y_pairs.append(_dot(jnp.concatenate(ms, axis=1), xbd))
            xw_pairs.append(xp * _pair_expand(wgt, j, lane_lo))
            ecol_pairs.append(_pair_expand(eac, j, lane_lo))
        y_diag = jnp.concatenate(y_pairs, axis=1)
        xw = jnp.concatenate(xw_pairs, axis=1)
        ecol = jnp.concatenate(ecol_pairs, axis=1)

        if prompt:
            def h_get(g):
                return h_ssd[gsz * g:gsz * (g + 1), :]

            def h_set(g, v):
                h_ssd[gsz * g:gsz * (g + 1), :] = v

            y_off = ssd_state_io(cm, xw, bm, eac[Q - 1:Q, :], h_get, h_set)
            return y_diag + y_off * ecol + dfull_ref[...] * xs
        c_s[r0:r0 + Q, :] = _dot(perm_b, cm_b)
        bm_s[r0:r0 + Q, :] = _dot(perm_b, bm_b)
        xw_s[r0:r0 + Q, :] = _dot(perm_b, xw.astype(bf16))
        eac_s[r0:r0 + Q, :] = eac
        return y_diag + dfull_ref[...] * xs, ecol

    if prompt:
        y = jnp.concatenate([ssd_chunk(r0) for r0 in range(0, T, Q)], axis=0)
        o_ssd_ref[...] = h_ssd[...]
    else:
        to_seq = jnp.bitwise_or(jnp.left_shift(jnp.bitwise_and(row, SUBLANES - 1), SUB_SHIFT),
                                jnp.right_shift(row, SUB_SHIFT)) == col
        perm_b = jnp.where(to_seq, 1.0, 0.0).astype(bf16)
        parts = [ssd_chunk(r0) for r0 in range(0, T, Q)]
        layer = pl.program_id(0)
        seq0 = pl.program_id(1) * nseq

        def in_copy(i, slot):
            return pltpu.make_async_copy(h0_ssd_hbm.at[layer, seq0 + i], h_in.at[slot], sem_in.at[slot])

        def out_copy(i, slot):
            return pltpu.make_async_copy(h_out.at[slot], o_ssd_ref.at[layer, seq0 + i], sem_out.at[slot])

        in_copy(0, 0).start()

        def seq_step(i, carry):
            slot = jnp.bitwise_and(i, 1)
            s0 = pl.multiple_of(i * SEQ_S, SEQ_S)

            @pl.when(i + 1 < nseq)
            def _():
                in_copy(i + 1, 1 - slot).start()

            in_copy(i, slot).wait()

            @pl.when(i >= 2)
            def _():
                out_copy(i - 2, slot).wait()

            def h_get(g):
                return h_in[slot, pl.ds(gsz * g, gsz), :]

            def h_set(g, v):
                h_out[slot, pl.ds(gsz * g, gsz), :] = v

            e_row = jnp.left_shift(jnp.right_shift(i, SUB_SHIFT), SUBTILE_SHIFT) + (Q - SUBLANES) \
                + jnp.bitwise_and(i, SUBLANES - 1)
            yoff_s[pl.ds(s0, SEQ_S), :] = ssd_state_io(
                c_s[pl.ds(s0, SEQ_S), :], xw_s[pl.ds(s0, SEQ_S), :], bm_s[pl.ds(s0, SEQ_S), :],
                eac_s[pl.ds(e_row, 1), :], h_get, h_set)
            out_copy(i, slot).start()
            return carry

        lax.fori_loop(0, nseq, seq_step, 0)
        out_copy(nseq - 2, 0).wait()
        out_copy(nseq - 1, 1).wait()
        perm_f = jnp.where(to_seq, 1.0, 0.0)
        y = jnp.concatenate(
            [part + _dot_exact(perm_f, yoff_s[r0:r0 + Q, :]) * ecol
             for r0, (part, ecol) in zip(range(0, T, Q), parts)], axis=0)
    y_ssd = _rms(y * _silu(proj(C_Z, C_XBC)), sng_ref[...])

    xr = conv(proj(C_LRU, C_LRU_G), prev_lru if prompt else None, None if prompt else c0_lru_ref,
              cw_lru_ref, cb_lru_ref, o_clru_ref)
    xr_b = xr.astype(bf16)
    hl = LRU_DIM // 2
    gates = jnp.concatenate([_dot(xr_b[:, hl * (k % 2):hl * (k % 2 + 1)], lru_w_ref[k]) for k in range(4)],
                            axis=1) + lru_b_ref[...]
    r_gate = jax.nn.sigmoid(gates[:, :LRU_DIM])
    i_gate = jax.nn.sigmoid(gates[:, LRU_DIM:])
    log_a = -LRU_C * r_gate * jax.nn.softplus(-lam_ref[...])
    a_t = jnp.exp(log_a)
    gain = jnp.sqrt(jnp.maximum(-jnp.tanh(log_a) * (a_t * a_t + 1.0), 0.0))
    a_s[...] = a_t
    b_s[...] = gain * i_gate * xr

    def vrow(ref, r0, k):
        return ref[r0 + SUBLANES * k:r0 + SUBLANES * (k + 1), :]

    def set_vrow(ref, r0, k, v):
        ref[r0 + SUBLANES * k:r0 + SUBLANES * (k + 1), :] = v

    if prompt:
        sub = sub_iota(LRU_DIM)
        carry = lru_c[...]
        for r0 in range(0, T, CHUNK):
            acc_a, acc_h = vrow(a_s, r0, 0), vrow(b_s, r0, 0)
            for k in range(1, SEG):
                a_k = vrow(a_s, r0, k)
                acc_h = a_k * acc_h + vrow(b_s, r0, k)
                acc_a = a_k * acc_a
                set_vrow(a_s, r0, k, acc_a)
                set_vrow(b_s, r0, k, acc_h)
            alpha = jnp.where(sub == 0, 0.0, pltpu.roll(acc_a, 1, 0))
            beta = jnp.where(sub == 0, jnp.broadcast_to(carry, (SUBLANES, LRU_DIM)), pltpu.roll(acc_h, 1, 0))
            for d in (1, 2, 4):
                a_sh = jnp.where(sub >= d, pltpu.roll(alpha, d, 0), 1.0)
                b_sh = jnp.where(sub >= d, pltpu.roll(beta, d, 0), 0.0)
                beta = alpha * b_sh + beta
                alpha = alpha * a_sh
            carry = (acc_a * beta + acc_h)[SUBLANES - 1:SUBLANES, :]
            for k in range(SEG):
                set_vrow(b_s, r0, k, vrow(b_s, r0, k) + vrow(a_s, r0, k) * beta)
        lru_c[...] = carry
        o_lru_ref[...] = carry
    else:
        for s in range(T // Q):
            h = h0_lru_ref[SUBLANES * s:SUBLANES * (s + 1), :]
            for k in range(Q // SUBLANES):
                h = vrow(a_s, Q * s, k) * h + vrow(b_s, Q * s, k)
                set_vrow(b_s, Q * s, k, h)
            o_lru_ref[SUBLANES * s:SUBLANES * (s + 1), :] = h
    y_lru = b_s[...] * _silu(proj(C_LRU_G, C_S5))

    u = proj(C_S5, C_S5_G)
    u_b = u.astype(bf16)
    half = S5_DIM // 2
    for k in range(2):
        uk = u_b[:, half * k:half * (k + 1)]
        bur_s[:, S5_HALF * k:S5_HALF * (k + 1)] = _dot(uk, bb_ref[k])
        bui_s[:, S5_HALF * k:S5_HALF * (k + 1)] = _dot(uk, bb_ref[2 + k])

    def tab(r0):
        return tab_ref[0, r0:r0 + SUBLANES, :], tab_ref[1, r0:r0 + SUBLANES, :]

    def cmul_add(pr, pi, xr, xi, yr, yi):
        return pr * xr - pi * xi + yr, pr * xi + pi * xr + yi

    ar, ai = tab(TAB_A)
    if prompt:
        sub = sub_iota(S5_FLAT)
        c_r, c_i = s5_cr[...], s5_ci[...]
        for r0 in range(0, T, CHUNK):
            hr, hi = vrow(bur_s, r0, 0), vrow(bui_s, r0, 0)
            for k in range(1, SEG):
                hr, hi = cmul_add(ar, ai, hr, hi, vrow(bur_s, r0, k), vrow(bui_s, r0, k))
                set_vrow(bur_s, r0, k, hr)
                set_vrow(bui_s, r0, k, hi)
            er = jnp.where(sub == 0, jnp.broadcast_to(c_r, (SUBLANES, S5_FLAT)), pltpu.roll(hr, 1, 0))
            ei = jnp.where(sub == 0, jnp.broadcast_to(c_i, (SUBLANES, S5_FLAT)), pltpu.roll(hi, 1, 0))
            for t, d in enumerate((1, 2, 4)):
                qr, qi = tab(TAB_Q + t * SUBLANES)
                er, ei = cmul_add(qr, qi, pltpu.roll(er, d, 0), pltpu.roll(ei, d, 0), er, ei)
            sr, si = tab(TAB_ASEG)
            nr, ni = cmul_add(sr, si, er, ei, hr, hi)
            c_r, c_i = nr[SUBLANES - 1:SUBLANES, :], ni[SUBLANES - 1:SUBLANES, :]
            for k in range(SEG):
                pr, pi = tab(TAB_PW + k * SUBLANES)
                vr, vi = cmul_add(pr, pi, er, ei, vrow(bur_s, r0, k), vrow(bui_s, r0, k))
                set_vrow(bur_s, r0, k, vr)
                set_vrow(bui_s, r0, k, vi)
        s5_cr[...] = c_r
        s5_ci[...] = c_i
        o_s5r_ref[...] = c_r
        o_s5i_ref[...] = c_i
    else:
        for s in range(T // Q):
            rows = slice(SUBLANES * s, SUBLANES * (s + 1))
            hr, hi = h0_s5r_ref[rows, :], h0_s5i_ref[rows, :]
            for k in range(Q // SUBLANES):
                hr, hi = cmul_add(ar, ai, hr, hi, vrow(bur_s, Q * s, k), vrow(bui_s, Q * s, k))
                set_vrow(bur_s, Q * s, k, hr)
                set_vrow(bui_s, Q * s, k, hi)
            o_s5r_ref[rows, :] = hr
            o_s5i_ref[rows, :] = hi
    ys = []
    for k in range(2):
        hk = jnp.concatenate([bur_s[:, S5_HALF * k:S5_HALF * (k + 1)].astype(bf16),
                              bui_s[:, S5_HALF * k:S5_HALF * (k + 1)].astype(bf16)], axis=1)
        ys.append(_dot(hk, cc_ref[k]))
    ys5 = jnp.concatenate(ys, axis=1) + s5d_ref[...] * u
    ys5 = jax.nn.gelu(ys5)
    ys5 = ys5 * jax.nn.sigmoid(_dot(ys5.astype(bf16), glu_w_ref[...]) + glu_b_ref[...])
    y_s5 = ys5 * _silu(proj(C_S5_G, C_DT))

    ycat = jnp.concatenate([y_ssd.astype(bf16), y_lru.astype(bf16), y_s5.astype(bf16)], axis=1)
    return x + _dot(ycat, w_out_ref[...])


def _prompt_body(final, *refs):
    x_ref = refs[0]
    w = refs[1:1 + N_WEIGHTS]
    y_ref = refs[1 + N_WEIGHTS]
    o = refs[2 + N_WEIGHTS:8 + N_WEIGHTS]
    scr = refs[8 + N_WEIGHTS:]
    h_ssd, prev_ssd, prev_lru, _, _, _, _, lru_c, s5_cr, s5_ci = scr

    @pl.when(pl.program_id(1) == 0)
    def _():
        h_ssd[...] = jnp.zeros_like(h_ssd)
        prev_ssd[...] = jnp.zeros_like(prev_ssd)
        prev_lru[...] = jnp.zeros_like(prev_lru)
        lru_c[...] = jnp.zeros_like(lru_c)
        s5_cr[...] = jnp.zeros_like(s5_cr)
        s5_ci[...] = jnp.zeros_like(s5_ci)

    out = _layer_math(True, TILE_P, x_ref[...], w, None, o, scr)
    if final:
        out = _rms(out, w[-1][...])
    y_ref[...] = out


def _sample_body(*refs):
    x_ref = refs[0]
    st = refs[1:7]
    w = refs[7:7 + N_WEIGHTS]
    y_ref = refs[7 + N_WEIGHTS]
    o = refs[8 + N_WEIGHTS:14 + N_WEIGHTS]
    x_all = refs[14 + N_WEIGHTS]
    scr = refs[15 + N_WEIGHTS:]
    layer = pl.program_id(0)
    last_layer = layer == pl.num_programs(0) - 1
    r0 = pl.multiple_of(pl.program_id(1) * TILE_S, TILE_S)

    @pl.when(layer == 0)
    def _():
        x_all[pl.ds(r0, TILE_S), :] = x_ref[...]

    out = _layer_math(False, TILE_S, x_all[pl.ds(r0, TILE_S), :], w, st, o, scr)
    x_all[pl.ds(r0, TILE_S), :] = out

    @pl.when(last_layer)
    def _():
        y_ref[...] = _rms(out, w[-1][...])

    @pl.when(jnp.logical_not(last_layer))
    def _():
        y_ref[...] = out


def _prompt_call(layer, final, x, weights):
    T = TILE_P
    nb, seq, _ = x.shape

    def wspec(a):
        nd = a.ndim - 1
        return pl.BlockSpec((None,) + a.shape[1:], lambda b, c: (layer,) + (0,) * nd, pipeline_mode=pl.Buffered(1))

    def st(shape):
        nd = len(shape)
        return pl.BlockSpec((None,) + shape, lambda b, c: (b,) + (0,) * nd)

    x_spec = pl.BlockSpec((None, T, D_MODEL), lambda b, c: (b, c, 0))
    out_specs = [x_spec, st((SSD_DIM, SSD_STATE)), st((CONV_WIDTH - 1, SSD_CONV_DIM)), st((1, LRU_DIM)),
                 st((CONV_WIDTH - 1, LRU_DIM)), st((1, S5_FLAT)), st((1, S5_FLAT))]
    out_shape = [jax.ShapeDtypeStruct(x.shape, f32),
                 jax.ShapeDtypeStruct((nb, SSD_DIM, SSD_STATE), f32),
                 jax.ShapeDtypeStruct((nb, CONV_WIDTH - 1, SSD_CONV_DIM), f32),
                 jax.ShapeDtypeStruct((nb, 1, LRU_DIM), f32),
                 jax.ShapeDtypeStruct((nb, CONV_WIDTH - 1, LRU_DIM), f32),
                 jax.ShapeDtypeStruct((nb, 1, S5_FLAT), f32),
                 jax.ShapeDtypeStruct((nb, 1, S5_FLAT), f32)]
    scratch = [pltpu.VMEM((SSD_DIM, SSD_STATE), f32),
               pltpu.VMEM((HALO, SSD_CONV_DIM), f32),
               pltpu.VMEM((HALO, LRU_DIM), f32),
               pltpu.VMEM((T, LRU_DIM), f32), pltpu.VMEM((T, LRU_DIM), f32),
               pltpu.VMEM((T, S5_FLAT), f32), pltpu.VMEM((T, S5_FLAT), f32),
               pltpu.VMEM((1, LRU_DIM), f32),
               pltpu.VMEM((1, S5_FLAT), f32), pltpu.VMEM((1, S5_FLAT), f32)]
    return pl.pallas_call(
        functools.partial(_prompt_body, final),
        grid=(nb, seq // T), in_specs=[x_spec] + [wspec(a) for a in weights],
        out_specs=out_specs, out_shape=out_shape, scratch_shapes=scratch,
        compiler_params=pltpu.CompilerParams(dimension_semantics=("arbitrary", "arbitrary"),
                                             vmem_limit_bytes=VMEM_LIMIT_BYTES),
        name="layer_prompt",
    )(x, *weights)


def _sample_call(x, states, weights):
    T = TILE_S
    rows = x.shape[0]
    depth = weights[0].shape[0]
    nseq = T // SEQ_S

    def wspec(a):
        nd = a.ndim - 1
        return pl.BlockSpec((None,) + a.shape[1:], lambda l, i: (l,) + (0,) * nd, pipeline_mode=pl.Buffered(1))

    def st(a):
        nd = a.ndim - 2
        if a.ndim == 4:
            return pl.BlockSpec(memory_space=pl.ANY)
        if a.ndim == 6:
            return pl.BlockSpec((None, None) + a.shape[2:], lambda l, i: (l, i) + (0,) * nd)
        return pl.BlockSpec((None, nseq) + a.shape[2:], lambda l, i: (l, i) + (0,) * nd)

    x_spec = pl.BlockSpec((T, D_MODEL), lambda l, i: (i, 0))
    st_specs = [st(a) for a in states]
    scratch = [pltpu.VMEM((rows, D_MODEL), f32),
               pltpu.VMEM((T, LRU_DIM), f32), pltpu.VMEM((T, LRU_DIM), f32),
               pltpu.VMEM((T, S5_FLAT), f32), pltpu.VMEM((T, S5_FLAT), f32),
               pltpu.VMEM((T, SSD_BC), f32), pltpu.VMEM((T, SSD_BC), f32),
               pltpu.VMEM((T, SSD_DIM), f32), pltpu.VMEM((T, SSD_DIM), f32),
               pltpu.VMEM((T, LANES), f32),
               pltpu.VMEM((2, SSD_DIM, SSD_STATE), f32), pltpu.VMEM((2, SSD_DIM, SSD_STATE), f32),
               pltpu.SemaphoreType.DMA((2,)), pltpu.SemaphoreType.DMA((2,))]
    return pl.pallas_call(
        _sample_body,
        grid=(depth, rows // T), in_specs=[x_spec] + st_specs + [wspec(a) for a in weights],
        out_specs=[pl.BlockSpec((None, T, D_MODEL), lambda l, i: (l, i, 0))] + st_specs,
        out_shape=[jax.ShapeDtypeStruct((depth,) + x.shape, f32)] + [jax.ShapeDtypeStruct(a.shape, f32) for a in states],
        scratch_shapes=scratch,
        compiler_params=pltpu.CompilerParams(dimension_semantics=("arbitrary", "arbitrary"),
                                             vmem_limit_bytes=VMEM_LIMIT_BYTES),
        name="layers_sample",
    )(x, *states, *weights)


def _block_diag(blocks):
    *lead, n, r, c = blocks.shape
    eye = jnp.eye(n, dtype=blocks.dtype)
    return (blocks[..., :, :, None, :] * eye[:, None, :, None]).reshape(*lead, n * r, n * c)


def _pad_lanes(v):
    return jnp.pad(v, [(0, 0)] * (v.ndim - 1) + [(0, LANES - v.shape[-1])])


def kernel(x_prompt, x_sample, state_ssd, state_ssd_conv, state_lru, state_lru_conv, state_s5_re, state_s5_im, norm_g, w_in, ssd_conv_w, ssd_conv_b, ssd_dt_bias, ssd_a_log, ssd_d, ssd_norm_g, lru_conv_w, lru_conv_b, lru_wa, lru_ba, lru_wx, lru_bx, lru_lambda, s5_lambda_re, s5_lambda_im, s5_log_dt, s5_b_re, s5_b_im, s5_c_re, s5_c_im, s5_d, s5_glu_w, s5_glu_b, w_out, final_norm_g):
    depth = w_in.shape[0]
    nbp = x_prompt.shape[0]
    nbs, ls, _ = x_sample.shape
    assert ls == SEQ_S and x_prompt.shape[1] % TILE_P == 0 and (nbs * ls) % TILE_S == 0

    tab, bbar_re, bbar_im = _s5_prep(s5_lambda_re.astype(f32), s5_lambda_im.astype(f32), s5_log_dt.astype(f32),
                                     s5_b_re.astype(f32), s5_b_im.astype(f32))

    def row(v, n):
        return v.astype(f32).reshape(depth, 1, n)

    wi = w_in.astype(f32)
    w_in_r = jnp.concatenate(
        [wi[..., 0:1024], wi[..., 1024:3072], wi[..., 3088:3600], wi[..., 3600:4112], wi[..., 4112:4624],
         wi[..., 4624:5136], _pad_lanes(wi[..., 3072:3088])], axis=-1).astype(bf16)

    def halves(v):
        return _block_diag(v.reshape(depth, 2, S5_NGROUPS // 2, S5_GROUP, S5_STATE))

    bb = jnp.concatenate([halves(bbar_re), halves(bbar_im)], axis=1).astype(bf16)

    def chalves(v):
        return _block_diag(jnp.transpose(v.astype(f32), (0, 1, 3, 2)).reshape(depth, 2, S5_NGROUPS // 2, S5_STATE, S5_GROUP))

    cc = jnp.concatenate([chalves(s5_c_re), -chalves(s5_c_im)], axis=2).astype(bf16)
    weights = (
        row(norm_g, D_MODEL), w_in_r,
        ssd_conv_w.astype(f32), row(ssd_conv_b, SSD_CONV_DIM),
        _pad_lanes(row(ssd_dt_bias, SSD_HEADS)), _pad_lanes(row(ssd_a_log, SSD_HEADS)),
        jnp.repeat(ssd_d.astype(f32), SSD_HEADDIM, axis=-1).reshape(depth, 1, SSD_DIM),
        row(ssd_norm_g, SSD_DIM),
        lru_conv_w.astype(f32), row(lru_conv_b, LRU_DIM),
        jnp.concatenate([_block_diag(v.astype(f32).reshape(depth, 2, v.shape[1] // 2, *v.shape[2:]))
                         for v in (lru_wa, lru_wx)], axis=1).astype(bf16),
        jnp.concatenate([lru_ba, lru_bx], axis=-1).astype(f32).reshape(depth, 1, 2 * LRU_DIM),
        row(lru_lambda, LRU_DIM),
        tab, bb, cc, row(s5_d, S5_DIM),
        s5_glu_w.astype(bf16), row(s5_glu_b, S5_DIM),
        w_out.astype(bf16),
        jnp.broadcast_to(final_norm_g.astype(f32).reshape(1, 1, D_MODEL), (depth, 1, D_MODEL)),
    )

    seq_p = x_prompt.shape[1]
    xp = jnp.swapaxes(x_prompt.astype(f32).reshape(nbp, seq_p // CHUNK, SUBLANES, SEG, D_MODEL), 2, 3)
    xp = xp.reshape(nbp, seq_p, D_MODEL)
    outs_p = [[] for _ in range(6)]
    for i in range(depth):
        res = _prompt_call(i, i == depth - 1, xp, weights)
        xp = res[0]
        for j in range(6):
            outs_p[j].append(res[1 + j])
    y_prompt = jnp.swapaxes(xp.reshape(nbp, seq_p // CHUNK, SEG, SUBLANES, D_MODEL), 2, 3).reshape(nbp, seq_p, D_MODEL)

    ntile = nbs // SUBLANES
    nsub = TILE_S // SUBTILE_S

    def conv_in(v):
        v = jnp.swapaxes(v.astype(f32).reshape(depth, ntile, SUBLANES, CONV_WIDTH - 1, v.shape[-1]), 2, 3)
        return v.reshape(depth, ntile // nsub, nsub, CONV_WIDTH - 1, SUBLANES, v.shape[-1])

    def conv_out(v, dtype):
        v = v.reshape(depth, ntile, CONV_WIDTH - 1, SUBLANES, v.shape[-1])
        return jnp.swapaxes(v, 2, 3).reshape(depth, nbs, CONV_WIDTH - 1, v.shape[-1]).astype(dtype)

    xs = jnp.swapaxes(x_sample.astype(f32).reshape(ntile, SUBLANES, ls, D_MODEL), 1, 2).reshape(nbs * ls, D_MODEL)
    states_s = (state_ssd.astype(f32).reshape(depth, nbs, SSD_DIM, SSD_STATE), conv_in(state_ssd_conv),
                state_lru.astype(f32), conv_in(state_lru_conv),
                state_s5_re.astype(f32).reshape(depth, nbs, S5_FLAT), state_s5_im.astype(f32).reshape(depth, nbs, S5_FLAT))
    weights_s = tuple(w[:, :, TAB_A:TAB_A + SUBLANES] if w is tab else w for w in weights)
    res_s = _sample_call(xs, states_s, weights_s)
    y_sample = jnp.swapaxes(res_s[0][depth - 1].reshape(ntile, ls, SUBLANES, D_MODEL), 1, 2).reshape(nbs, ls, D_MODEL)

    def stack(lst, shape, dtype):
        return jnp.stack(lst).reshape((depth,) + shape).astype(dtype)

    ssd_shape = (SSD_HEADS, SSD_HEADDIM, SSD_STATE)
    s5_shape = (S5_NGROUPS, S5_STATE)
    return (
        y_prompt.astype(x_prompt.dtype), y_sample.astype(x_sample.dtype),
        stack(outs_p[0], (nbp,) + ssd_shape, state_ssd.dtype),
        res_s[1].reshape((depth, nbs) + ssd_shape).astype(state_ssd.dtype),
        stack(outs_p[1], (nbp, CONV_WIDTH - 1, SSD_CONV_DIM), state_ssd_conv.dtype),
        conv_out(res_s[2], state_ssd_conv.dtype),
        stack(outs_p[2], (nbp, LRU_DIM), state_lru.dtype), res_s[3].astype(state_lru.dtype),
        stack(outs_p[3], (nbp, CONV_WIDTH - 1, LRU_DIM), state_lru_conv.dtype),
        conv_out(res_s[4], state_lru_conv.dtype),
        stack(outs_p[4], (nbp,) + s5_shape, state_s5_re.dtype),
        res_s[5].reshape((depth, nbs) + s5_shape).astype(state_s5_re.dtype),
        stack(outs_p[5], (nbp,) + s5_shape, state_s5_im.dtype),
        res_s[6].reshape((depth, nbs) + s5_shape).astype(state_s5_im.dtype),
    )
```

```python
import functools

import jax
import jax.numpy as jnp
from jax import lax
from jax.experimental import pallas as pl
from jax.experimental.pallas import tpu as pltpu

f32 = jnp.float32
bf16 = jnp.bfloat16

D_MODEL = 1024
CONV_WIDTH = 4
SSD_DIM = 1024
SSD_HEADDIM = 64
SSD_HEADS = 16
SSD_GROUPS = 4
SSD_HPG = 4
SSD_STATE = 128
SSD_BC = SSD_GROUPS * SSD_STATE
SSD_CONV_DIM = SSD_DIM + 2 * SSD_BC
LRU_DIM = 512
LRU_C = 8.0
S5_DIM = 512
S5_GROUP = 16
S5_NGROUPS = 32
S5_STATE = 64
S5_FLAT = S5_NGROUPS * S5_STATE
S5_HALF = S5_FLAT // 2
EPS = 1e-6

LANES = 128
SUBLANES = 8
CHUNK = 128
TILE_P = 256
TILE_S = 256
SUBTILE_S = 64
SUBTILE_SHIFT = 6
SEQ_S = 8
STATE_BUFS = 4
NEG = -1e30

SEG = CHUNK // SUBLANES
SEG_SHIFT = 4
SUB_SHIFT = 3
HALO = (CONV_WIDTH - 1) * SUBLANES

TAB_A = 0
TAB_Q = TAB_A + SUBLANES
TAB_ASEG = TAB_Q + 3 * SUBLANES
TAB_PW = TAB_ASEG + SUBLANES
TAB_ROWS = TAB_PW + SEG * SUBLANES

C_Z = 0
C_XBC = C_Z + SSD_DIM
C_LRU = C_XBC + SSD_CONV_DIM
C_LRU_G = C_LRU + LRU_DIM
C_S5 = C_LRU_G + LRU_DIM
C_S5_G = C_S5 + S5_DIM
C_DT = C_S5_G + S5_DIM
IN_COLS = C_DT + LANES

VMEM_LIMIT_BYTES = 56 * 1024 * 1024

N_WEIGHTS = 21


def _rms(x, g):
    return x * lax.rsqrt(jnp.mean(x * x, axis=-1, keepdims=True) + EPS) * g


def _silu(x):
    return x * jax.nn.sigmoid(x)


def _dot(a, b):
    return jnp.dot(a, b, preferred_element_type=f32)


def _dot_nt(a, b):
    return lax.dot_general(a, b, (((1,), (1,)), ((), ())), preferred_element_type=f32)


def _dot_tn(a, b):
    return lax.dot_general(a, b, (((0,), (0,)), ((), ())), preferred_element_type=f32)


def _dot_exact(a, b):
    return jnp.dot(a, b, preferred_element_type=f32, precision=lax.Precision.HIGHEST)


def _pair_expand(v, j, lane_lo):
    q = v.shape[0]
    lo = jnp.broadcast_to(v[:, 2 * j:2 * j + 1], (q, LANES))
    hi = jnp.broadcast_to(v[:, 2 * j + 1:2 * j + 2], (q, LANES))
    return jnp.where(lane_lo, lo, hi)


def _s5_prep_body(lre_ref, lim_ref, ldt_ref, lre_rep_ref, lim_rep_ref, ldt_rep_ref,
                  bre_ref, bim_ref, tre_ref, tim_ref, bbre_ref, bbim_ref):
    def abar(lre, lim, ldt):
        delta = jnp.exp(ldt)
        mag = jnp.exp(lre * delta)
        return mag * jnp.cos(lim * delta), mag * jnp.sin(lim * delta)

    ar, ai = abar(lre_ref[...], lim_ref[...], ldt_ref[...])

    def cmul(xr, xi, yr, yi):
        return xr * yr - xi * yi, xr * yi + xi * yr

    pw = [(ar, ai)]
    for _ in range(SEG - 1):
        pw.append(cmul(*pw[-1], ar, ai))
    seg = [pw[SEG - 1]]
    for _ in range(2):
        seg.append(cmul(*seg[-1], *seg[-1]))
    zero = jnp.zeros_like(ar)

    def put(i, v):
        tre_ref[i] = v[0]
        tim_ref[i] = v[1]

    for r in range(SUBLANES):
        put(TAB_A + r, pw[0])
        put(TAB_ASEG + r, seg[0])
        for t, d in enumerate((1, 2, 4)):
            put(TAB_Q + t * SUBLANES + r, seg[t] if r >= d else (zero, zero))
        for k in range(SEG):
            put(TAB_PW + k * SUBLANES + r, pw[k])

    lre, lim = lre_rep_ref[...], lim_rep_ref[...]
    ar, ai = abar(lre, lim, ldt_rep_ref[...])
    denom = lre * lre + lim * lim
    nr = ar - 1.0
    ni = ai
    coef_re = (nr * lre + ni * lim) / denom
    coef_im = (ni * lre - nr * lim) / denom
    bre, bim = bre_ref[...], bim_ref[...]
    bbre_ref[...] = coef_re * bre - coef_im * bim
    bbim_ref[...] = coef_re * bim + coef_im * bre


def _s5_prep(lam_re, lam_im, log_dt, b_re, b_im):
    depth = lam_re.shape[0]
    rows_c = S5_FLAT // LANES
    rows_r = S5_DIM * S5_STATE // LANES
    ldt = jnp.broadcast_to(log_dt[:, :, None], (depth, S5_NGROUPS, S5_STATE))

    def rep(v):
        return jnp.broadcast_to(v[:, :, None, :], (depth, S5_NGROUPS, S5_GROUP, S5_STATE)).reshape(depth, rows_r, LANES)

    def bt(v):
        return jnp.transpose(v, (0, 1, 3, 2)).reshape(depth, rows_r, LANES)

    cspec = pl.BlockSpec((None, rows_c, LANES), lambda i: (i, 0, 0))
    rspec = pl.BlockSpec((None, rows_r, LANES), lambda i: (i, 0, 0))
    tspec = pl.BlockSpec((None, TAB_ROWS, rows_c, LANES), lambda i: (i, 0, 0, 0))
    tre, tim, bbre, bbim = pl.pallas_call(
        _s5_prep_body,
        grid=(depth,),
        in_specs=[cspec, cspec, cspec, rspec, rspec, rspec, rspec, rspec],
        out_specs=[tspec, tspec, rspec, rspec],
        out_shape=[jax.ShapeDtypeStruct((depth, TAB_ROWS, rows_c, LANES), f32)] * 2
        + [jax.ShapeDtypeStruct((depth, rows_r, LANES), f32)] * 2,
        name="s5_prep",
    )(lam_re.reshape(depth, rows_c, LANES), lam_im.reshape(depth, rows_c, LANES),
      ldt.reshape(depth, rows_c, LANES), rep(lam_re), rep(lam_im), rep(ldt), bt(b_re), bt(b_im))
    tab = jnp.stack([tre, tim], axis=1).reshape(depth, 2, TAB_ROWS, S5_FLAT)
    return tab, bbre.reshape(depth, S5_DIM, S5_STATE), bbim.reshape(depth, S5_DIM, S5_STATE)


def _layer_math(prompt, T, x, w, st, o, scr):
    (ng_ref, w_in_ref, cw_ssd_ref, cb_ssd_ref, dtb_ref, alog_ref, dfull_ref, sng_ref,
     cw_lru_ref, cb_lru_ref, lru_w_ref, lru_b_ref, lam_ref,
     tab_ref, bb_ref, cc_ref, s5d_ref, glu_w_ref, glu_b_ref, w_out_ref, _) = w
    o_ssd_ref, o_cssd_ref, o_lru_ref, o_clru_ref, o_s5r_ref, o_s5i_ref = o
    if prompt:
        h_ssd, prev_ssd, prev_lru, a_s, b_s, bur_s, bui_s, lru_c, s5_cr, s5_ci = scr
    else:
        h0_ssd_hbm, c0_ssd_ref, h0_lru_ref, c0_lru_ref, h0_s5r_ref, h0_s5i_ref = st
        a_s, b_s, bur_s, bui_s, c_s, bm_s, xw_s, yoff_s, eac_s, h_in, h_out, sem_in, sem_out = scr
    nseq = T // SEQ_S
    if not prompt:
        layer = pl.program_id(0)
        seq0 = pl.program_id(1) * nseq

        def in_copy(i, slot):
            return pltpu.make_async_copy(h0_ssd_hbm.at[layer, seq0 + i], h_in.at[slot], sem_in.at[slot])

        def out_copy(i, slot):
            return pltpu.make_async_copy(h_out.at[slot], o_ssd_ref.at[layer, seq0 + i], sem_out.at[slot])

        for j in range(STATE_BUFS - 1):
            in_copy(j, j).start()
    Q = CHUNK if prompt else SUBTILE_S

    hn = _rms(x, ng_ref[...]).astype(bf16)

    def proj(lo, hi):
        return _dot(hn, w_in_ref[:, lo:hi])

    def sub_iota(n):
        return lax.broadcasted_iota(jnp.int32, (SUBLANES, n), 0)

    def conv_taps(halo, rs, cw_ref, cb_ref):
        ext = jnp.concatenate([halo, rs], axis=0)
        n = rs.shape[0]
        acc = cb_ref[...] + cw_ref[3:4, :] * rs
        for j in range(1, CONV_WIDTH):
            acc = acc + cw_ref[3 - j:4 - j, :] * ext[HALO - SUBLANES * j:HALO - SUBLANES * j + n, :]
        return acc

    def conv(raw, prev_ref, c0_ref, cw_ref, cb_ref, o_ref):
        cdim = raw.shape[1]
        if not prompt:
            outs = []
            for s in range(T // Q):
                rs = raw[Q * s:Q * (s + 1), :]
                o_ref[s] = rs[Q - HALO:, :].reshape(CONV_WIDTH - 1, SUBLANES, cdim)
                outs.append(conv_taps(c0_ref[s].reshape(HALO, cdim), rs, cw_ref, cb_ref))
            return jnp.concatenate(outs, axis=0)
        first = sub_iota(cdim) == 0
        tail = prev_ref[...]
        outs = []
        for r0 in range(0, T, CHUNK):
            rs = raw[r0:r0 + CHUNK, :]
            cur = rs[CHUNK - HALO:, :]
            halo = jnp.concatenate(
                [jnp.where(first, pltpu.roll(tail[SUBLANES * k:SUBLANES * (k + 1), :], 1, 0),
                           pltpu.roll(cur[SUBLANES * k:SUBLANES * (k + 1), :], 1, 0))
                 for k in range(CONV_WIDTH - 1)], axis=0)
            outs.append(conv_taps(halo, rs, cw_ref, cb_ref))
            tail = cur
        prev_ref[...] = tail
        for k in range(CONV_WIDTH - 1):
            o_ref[k:k + 1, :] = tail[SUBLANES * k + SUBLANES - 1:SUBLANES * (k + 1), :]
        return jnp.concatenate(outs, axis=0)

    row = lax.broadcasted_iota(jnp.int32, (Q, Q), 0)
    col = lax.broadcasted_iota(jnp.int32, (Q, Q), 1)
    if prompt:
        def local_time(i):
            return jnp.bitwise_or(jnp.left_shift(jnp.bitwise_and(i, SUBLANES - 1), SEG_SHIFT),
                                  jnp.right_shift(i, SUB_SHIFT))
        causal = local_time(row) >= local_time(col)
    else:
        same_seq = jnp.bitwise_and(row, SUBLANES - 1) == jnp.bitwise_and(col, SUBLANES - 1)
        causal = jnp.logical_and(same_seq, jnp.right_shift(row, SUB_SHIFT) >= jnp.right_shift(col, SUB_SHIFT))
    tril = jnp.where(causal, 1.0, 0.0)
    lane_lo = lax.broadcasted_iota(jnp.int32, (Q, LANES), 1) < SSD_HEADDIM
    gsz = SSD_HPG * SSD_HEADDIM

    xbc = _silu(conv(proj(C_XBC, C_LRU), prev_ssd if prompt else None, None if prompt else c0_ssd_ref,
                     cw_ssd_ref, cb_ssd_ref, o_cssd_ref))
    dt_all = jax.nn.softplus(proj(C_DT, IN_COLS) + dtb_ref[...])
    a_neg = -jnp.exp(alog_ref[...])

    def ssd_state_io(rows_c, rows_xw, rows_b, e_last, h_get, h_set):
        outs = []
        for g in range(SSD_GROUPS):
            hp = h_get(g)
            outs.append(_dot_nt(rows_c[:, LANES * g:LANES * (g + 1)].astype(bf16), hp.astype(bf16)))
            sg = _dot_tn(rows_xw[:, gsz * g:gsz * (g + 1)].astype(bf16),
                         rows_b[:, LANES * g:LANES * (g + 1)].astype(bf16))
            dec = jnp.concatenate(
                [jnp.broadcast_to(e_last[:, SSD_HPG * g + k:SSD_HPG * g + k + 1], (SSD_HEADDIM, SSD_STATE))
                 for k in range(SSD_HPG)], axis=0)
            h_set(g, dec * hp + sg)
        return jnp.concatenate(outs, axis=1)

    def ssd_chunk(r0):
        xs = xbc[r0:r0 + Q, :SSD_DIM]
        bm = xbc[r0:r0 + Q, SSD_DIM:SSD_DIM + SSD_BC]
        cm = xbc[r0:r0 + Q, SSD_DIM + SSD_BC:]
        dt = dt_all[r0:r0 + Q, :]
        bm_b = bm.astype(bf16)
        cm_b = cm.astype(bf16)
        acum = _dot_exact(tril, dt * a_neg)
        acum_row = acum.T
        dt_row = dt.T
        eac = jnp.exp(acum)
        if prompt:
            acum_end = acum[Q - 1:Q, :]
        else:
            sel = jnp.where(col == jnp.bitwise_and(row, SUBLANES - 1) + (Q - SUBLANES), 1.0, 0.0)
            acum_end = _dot_exact(sel, acum)
        wgt = jnp.exp(acum_end - acum) * dt

        scores = [_dot_nt(cm_b[:, LANES * g:LANES * (g + 1)], bm_b[:, LANES * g:LANES * (g + 1)])
                  for g in range(SSD_GROUPS)]
        y_pairs, xw_pairs, ecol_pairs = [], [], []
        for j in range(SSD_HEADS // 2):
            g = (2 * j) // SSD_HPG
            ms = []
            for h in (2 * j, 2 * j + 1):
                diff = acum[:, h:h + 1] - acum_row[h:h + 1, :]
                decay = jnp.exp(jnp.where(causal, diff, NEG))
                ms.append((scores[g] * decay * dt_row[h:h + 1, :]).astype(bf16))
            xp = xs[:, LANES * j:LANES * (j + 1)]
            xbd = jnp.concatenate([jnp.where(lane_lo, xp, 0.0), jnp.where(lane_lo, 0.0, xp)], axis=0).astype(bf16)
            y_pairs.append(_dot(jnp.concatenate(ms, axis=1), xbd))
            xw_pairs.append(xp * _pair_expand(wgt, j, lane_lo))
            ecol_pairs.append(_pair_expand(eac, j, lane_lo))
        y_diag = jnp.concatenate(y_pairs, axis=1)
        xw = jnp.concatenate(xw_pairs, axis=1)
        ecol = jnp.concatenate(ecol_pairs, axis=1)

        if prompt:
            def h_get(g):
                return h_ssd[gsz * g:gsz * (g + 1), :]

            def h_set(g, v):
                h_ssd[gsz * g:gsz * (g + 1), :] = v

            y_off = ssd_state_io(cm, xw, bm, eac[Q - 1:Q, :], h_get, h_set)
            return y_diag + y_off * ecol + dfull_ref[...] * xs
        c_s[r0:r0 + Q, :] = _dot(perm_b, cm_b)
        bm_s[r0:r0 + Q, :] = _dot(perm_b, bm_b)
        xw_s[r0:r0 + Q, :] = _dot(perm_b, xw.astype(bf16))
        eac_s[r0:r0 + Q, :] = eac
        return y_diag + dfull_ref[...] * xs, ecol

    if prompt:
        y = jnp.concatenate([ssd_chunk(r0) for r0 in range(0, T, Q)], axis=0)
        o_ssd_ref[...] = h_ssd[...]
    else:
        to_seq = jnp.bitwise_or(jnp.left_shift(jnp.bitwise_and(row, SUBLANES - 1), SUB_SHIFT),
                                jnp.right_shift(row, SUB_SHIFT)) == col
        perm_b = jnp.where(to_seq, 1.0, 0.0).astype(bf16)
        parts = [ssd_chunk(r0) for r0 in range(0, T, Q)]

        def seq_step(i, carry):
            slot = jnp.bitwise_and(i, STATE_BUFS - 1)
            s0 = pl.multiple_of(i * SEQ_S, SEQ_S)
            ahead = i + (STATE_BUFS - 1)

            @pl.when(ahead < nseq)
            def _():
                in_copy(ahead, jnp.bitwise_and(ahead, STATE_BUFS - 1)).start()

            in_copy(i, slot).wait()

            @pl.when(i >= STATE_BUFS)
            def _():
                out_copy(i - STATE_BUFS, slot).wait()

            def h_get(g):
                return h_in[slot, pl.ds(gsz * g, gsz), :]

            def h_set(g, v):
                h_out[slot, pl.ds(gsz * g, gsz), :] = v

            e_row = jnp.left_shift(jnp.right_shift(i, SUB_SHIFT), SUBTILE_SHIFT) + (Q - SUBLANES) \
                + jnp.bitwise_and(i, SUBLANES - 1)
            yoff_s[pl.ds(s0, SEQ_S), :] = ssd_state_io(
                c_s[pl.ds(s0, SEQ_S), :], xw_s[pl.ds(s0, SEQ_S), :], bm_s[pl.ds(s0, SEQ_S), :],
                eac_s[pl.ds(e_row, 1), :], h_get, h_set)
            out_copy(i, slot).start()
            return carry

        lax.fori_loop(0, nseq, seq_step, 0)
        for j in range(STATE_BUFS):
            out_copy(nseq - STATE_BUFS + j, j).wait()
        perm_f = jnp.where(to_seq, 1.0, 0.0)
        y = jnp.concatenate(
            [part + _dot_exact(perm_f, yoff_s[r0:r0 + Q, :]) * ecol
             for r0, (part, ecol) in zip(range(0, T, Q), parts)], axis=0)
    y_ssd = _rms(y * _silu(proj(C_Z, C_XBC)), sng_ref[...])

    xr = conv(proj(C_LRU, C_LRU_G), prev_lru if prompt else None, None if prompt else c0_lru_ref,
              cw_lru_ref, cb_lru_ref, o_clru_ref)
    xr_b = xr.astype(bf16)
    hl = LRU_DIM // 2
    gates = jnp.concatenate([_dot(xr_b[:, hl * (k % 2):hl * (k % 2 + 1)], lru_w_ref[k]) for k in range(4)],
                            axis=1) + lru_b_ref[...]
    r_gate = jax.nn.sigmoid(gates[:, :LRU_DIM])
    i_gate = jax.nn.sigmoid(gates[:, LRU_DIM:])
    log_a = -LRU_C * r_gate * jax.nn.softplus(-lam_ref[...])
    a_t = jnp.exp(log_a)
    gain = jnp.sqrt(jnp.maximum(-jnp.tanh(log_a) * (a_t * a_t + 1.0), 0.0))
    a_s[...] = a_t
    b_s[...] = gain * i_gate * xr

    def vrow(ref, r0, k):
        return ref[r0 + SUBLANES * k:r0 + SUBLANES * (k + 1), :]

    def set_vrow(ref, r0, k, v):
        ref[r0 + SUBLANES * k:r0 + SUBLANES * (k + 1), :] = v

    if prompt:
        sub = sub_iota(LRU_DIM)
        carry = lru_c[...]
        for r0 in range(0, T, CHUNK):
            acc_a, acc_h = vrow(a_s, r0, 0), vrow(b_s, r0, 0)
            for k in range(1, SEG):
                a_k = vrow(a_s, r0, k)
                acc_h = a_k * acc_h + vrow(b_s, r0, k)
                acc_a = a_k * acc_a
                set_vrow(a_s, r0, k, acc_a)
                set_vrow(b_s, r0, k, acc_h)
            alpha = jnp.where(sub == 0, 0.0, pltpu.roll(acc_a, 1, 0))
            beta = jnp.where(sub == 0, jnp.broadcast_to(carry, (SUBLANES, LRU_DIM)), pltpu.roll(acc_h, 1, 0))
            for d in (1, 2, 4):
                a_sh = jnp.where(sub >= d, pltpu.roll(alpha, d, 0), 1.0)
                b_sh = jnp.where(sub >= d, pltpu.roll(beta, d, 0), 0.0)
                beta = alpha * b_sh + beta
                alpha = alpha * a_sh
            carry = (acc_a * beta + acc_h)[SUBLANES - 1:SUBLANES, :]
            for k in range(SEG):
                set_vrow(b_s, r0, k, vrow(b_s, r0, k) + vrow(a_s, r0, k) * beta)
        lru_c[...] = carry
        o_lru_ref[...] = carry
    else:
        for s in range(T // Q):
            h = h0_lru_ref[SUBLANES * s:SUBLANES * (s + 1), :]
            for k in range(Q // SUBLANES):
                h = vrow(a_s, Q * s, k) * h + vrow(b_s, Q * s, k)
                set_vrow(b_s, Q * s, k, h)
            o_lru_ref[SUBLANES * s:SUBLANES * (s + 1), :] = h
    y_lru = b_s[...] * _silu(proj(C_LRU_G, C_S5))

    u = proj(C_S5, C_S5_G)
    u_b = u.astype(bf16)
    half = S5_DIM // 2
    for k in range(2):
        uk = u_b[:, half * k:half * (k + 1)]
        bur_s[:, S5_HALF * k:S5_HALF * (k + 1)] = _dot(uk, bb_ref[k])
        bui_s[:, S5_HALF * k:S5_HALF * (k + 1)] = _dot(uk, bb_ref[2 + k])

    def tab(r0):
        return tab_ref[0, r0:r0 + SUBLANES, :], tab_ref[1, r0:r0 + SUBLANES, :]

    def cmul_add(pr, pi, xr, xi, yr, yi):
        return pr * xr - pi * xi + yr, pr * xi + pi * xr + yi

    ar, ai = tab(TAB_A)
    if prompt:
        sub = sub_iota(S5_FLAT)
        c_r, c_i = s5_cr[...], s5_ci[...]
        for r0 in range(0, T, CHUNK):
            hr, hi = vrow(bur_s, r0, 0), vrow(bui_s, r0, 0)
            for k in range(1, SEG):
                hr, hi = cmul_add(ar, ai, hr, hi, vrow(bur_s, r0, k), vrow(bui_s, r0, k))
                set_vrow(bur_s, r0, k, hr)
                set_vrow(bui_s, r0, k, hi)
            er = jnp.where(sub == 0, jnp.broadcast_to(c_r, (SUBLANES, S5_FLAT)), pltpu.roll(hr, 1, 0))
            ei = jnp.where(sub == 0, jnp.broadcast_to(c_i, (SUBLANES, S5_FLAT)), pltpu.roll(hi, 1, 0))
            for t, d in enumerate((1, 2, 4)):
                qr, qi = tab(TAB_Q + t * SUBLANES)
                er, ei = cmul_add(qr, qi, pltpu.roll(er, d, 0), pltpu.roll(ei, d, 0), er, ei)
            sr, si = tab(TAB_ASEG)
            nr, ni = cmul_add(sr, si, er, ei, hr, hi)
            c_r, c_i = nr[SUBLANES - 1:SUBLANES, :], ni[SUBLANES - 1:SUBLANES, :]
            for k in range(SEG):
                pr, pi = tab(TAB_PW + k * SUBLANES)
                vr, vi = cmul_add(pr, pi, er, ei, vrow(bur_s, r0, k), vrow(bui_s, r0, k))
                set_vrow(bur_s, r0, k, vr)
                set_vrow(bui_s, r0, k, vi)
        s5_cr[...] = c_r
        s5_ci[...] = c_i
        o_s5r_ref[...] = c_r
        o_s5i_ref[...] = c_i
    else:
        for s in range(T // Q):
            rows = slice(SUBLANES * s, SUBLANES * (s + 1))
            hr, hi = h0_s5r_ref[rows, :], h0_s5i_ref[rows, :]
            for k in range(Q // SUBLANES):
                hr, hi = cmul_add(ar, ai, hr, hi, vrow(bur_s, Q * s, k), vrow(bui_s, Q * s, k))
                set_vrow(bur_s, Q * s, k, hr)
                set_vrow(bui_s, Q * s, k, hi)
            o_s5r_ref[rows, :] = hr
            o_s5i_ref[rows, :] = hi
    ys = []
    for k in range(2):
        hk = jnp.concatenate([bur_s[:, S5_HALF * k:S5_HALF * (k + 1)].astype(bf16),
                              bui_s[:, S5_HALF * k:S5_HALF * (k + 1)].astype(bf16)], axis=1)
        ys.append(_dot(hk, cc_ref[k]))
    ys5 = jnp.concatenate(ys, axis=1) + s5d_ref[...] * u
    ys5 = jax.nn.gelu(ys5)
    ys5 = ys5 * jax.nn.sigmoid(_dot(ys5.astype(bf16), glu_w_ref[...]) + glu_b_ref[...])
    y_s5 = ys5 * _silu(proj(C_S5_G, C_DT))

    ycat = jnp.concatenate([y_ssd.astype(bf16), y_lru.astype(bf16), y_s5.astype(bf16)], axis=1)
    return x + _dot(ycat, w_out_ref[...])


def _prompt_body(final, *refs):
    x_ref = refs[0]
    w = refs[1:1 + N_WEIGHTS]
    y_ref = refs[1 + N_WEIGHTS]
    o = refs[2 + N_WEIGHTS:8 + N_WEIGHTS]
    scr = refs[8 + N_WEIGHTS:]
    h_ssd, prev_ssd, prev_lru, _, _, _, _, lru_c, s5_cr, s5_ci = scr

    @pl.when(pl.program_id(1) == 0)
    def _():
        h_ssd[...] = jnp.zeros_like(h_ssd)
        prev_ssd[...] = jnp.zeros_like(prev_ssd)
        prev_lru[...] = jnp.zeros_like(prev_lru)
        lru_c[...] = jnp.zeros_like(lru_c)
        s5_cr[...] = jnp.zeros_like(s5_cr)
        s5_ci[...] = jnp.zeros_like(s5_ci)

    out = _layer_math(True, TILE_P, x_ref[...], w, None, o, scr)
    if final:
        out = _rms(out, w[-1][...])
    y_ref[...] = out


def _sample_body(*refs):
    x_ref = refs[0]
    st = refs[1:7]
    w = refs[7:7 + N_WEIGHTS]
    y_ref = refs[7 + N_WEIGHTS]
    o = refs[8 + N_WEIGHTS:14 + N_WEIGHTS]
    x_all = refs[14 + N_WEIGHTS]
    scr = refs[15 + N_WEIGHTS:]
    layer = pl.program_id(0)
    last_layer = layer == pl.num_programs(0) - 1
    r0 = pl.multiple_of(pl.program_id(1) * TILE_S, TILE_S)

    @pl.when(layer == 0)
    def _():
        x_all[pl.ds(r0, TILE_S), :] = x_ref[...]

    out = _layer_math(False, TILE_S, x_all[pl.ds(r0, TILE_S), :], w, st, o, scr)
    x_all[pl.ds(r0, TILE_S), :] = out

    @pl.when(last_layer)
    def _():
        y_ref[...] = _rms(out, w[-1][...])

    @pl.when(jnp.logical_not(last_layer))
    def _():
        y_ref[...] = out


def _prompt_call(layer, final, x, weights):
    T = TILE_P
    nb, seq, _ = x.shape

    def wspec(a):
        nd = a.ndim - 1
        return pl.BlockSpec((None,) + a.shape[1:], lambda b, c: (layer,) + (0,) * nd, pipeline_mode=pl.Buffered(1))

    def st(shape):
        nd = len(shape)
        return pl.BlockSpec((None,) + shape, lambda b, c: (b,) + (0,) * nd)

    x_spec = pl.BlockSpec((None, T, D_MODEL), lambda b, c: (b, c, 0))
    out_specs = [x_spec, st((SSD_DIM, SSD_STATE)), st((CONV_WIDTH - 1, SSD_CONV_DIM)), st((1, LRU_DIM)),
                 st((CONV_WIDTH - 1, LRU_DIM)), st((1, S5_FLAT)), st((1, S5_FLAT))]
    out_shape = [jax.ShapeDtypeStruct(x.shape, f32),
                 jax.ShapeDtypeStruct((nb, SSD_DIM, SSD_STATE), f32),
                 jax.ShapeDtypeStruct((nb, CONV_WIDTH - 1, SSD_CONV_DIM), f32),
                 jax.ShapeDtypeStruct((nb, 1, LRU_DIM), f32),
                 jax.ShapeDtypeStruct((nb, CONV_WIDTH - 1, LRU_DIM), f32),
                 jax.ShapeDtypeStruct((nb, 1, S5_FLAT), f32),
                 jax.ShapeDtypeStruct((nb, 1, S5_FLAT), f32)]
    scratch = [pltpu.VMEM((SSD_DIM, SSD_STATE), f32),
               pltpu.VMEM((HALO, SSD_CONV_DIM), f32),
               pltpu.VMEM((HALO, LRU_DIM), f32),
               pltpu.VMEM((T, LRU_DIM), f32), pltpu.VMEM((T, LRU_DIM), f32),
               pltpu.VMEM((T, S5_FLAT), f32), pltpu.VMEM((T, S5_FLAT), f32),
               pltpu.VMEM((1, LRU_DIM), f32),
               pltpu.VMEM((1, S5_FLAT), f32), pltpu.VMEM((1, S5_FLAT), f32)]
    return pl.pallas_call(
        functools.partial(_prompt_body, final),
        grid=(nb, seq // T), in_specs=[x_spec] + [wspec(a) for a in weights],
        out_specs=out_specs, out_shape=out_shape, scratch_shapes=scratch,
        compiler_params=pltpu.CompilerParams(dimension_semantics=("arbitrary", "arbitrary"),
                                             vmem_limit_bytes=VMEM_LIMIT_BYTES),
        name="layer_prompt",
    )(x, *weights)


def _sample_call(x, states, weights):
    T = TILE_S
    rows = x.shape[0]
    depth = weights[0].shape[0]
    nseq = T // SEQ_S

    def wspec(a):
        nd = a.ndim - 1
        return pl.BlockSpec((None,) + a.shape[1:], lambda l, i: (l,) + (0,) * nd, pipeline_mode=pl.Buffered(1))

    def st(a):
        nd = a.ndim - 2
        if a.ndim == 4:
            return pl.BlockSpec(memory_space=pl.ANY)
        if a.ndim == 6:
            return pl.BlockSpec((None, None) + a.shape[2:], lambda l, i: (l, i) + (0,) * nd)
        return pl.BlockSpec((None, nseq) + a.shape[2:], lambda l, i: (l, i) + (0,) * nd)

    x_spec = pl.BlockSpec((T, D_MODEL), lambda l, i: (i, 0))
    st_specs = [st(a) for a in states]
    scratch = [pltpu.VMEM((rows, D_MODEL), f32),
               pltpu.VMEM((T, LRU_DIM), f32), pltpu.VMEM((T, LRU_DIM), f32),
               pltpu.VMEM((T, S5_FLAT), f32), pltpu.VMEM((T, S5_FLAT), f32),
               pltpu.VMEM((T, SSD_BC), f32), pltpu.VMEM((T, SSD_BC), f32),
               pltpu.VMEM((T, SSD_DIM), f32), pltpu.VMEM((T, SSD_DIM), f32),
               pltpu.VMEM((T, LANES), f32),
               pltpu.VMEM((STATE_BUFS, SSD_DIM, SSD_STATE), f32), pltpu.VMEM((STATE_BUFS, SSD_DIM, SSD_STATE), f32),
               pltpu.SemaphoreType.DMA((STATE_BUFS,)), pltpu.SemaphoreType.DMA((STATE_BUFS,))]
    return pl.pallas_call(
        _sample_body,
        grid=(depth, rows // T), in_specs=[x_spec] + st_specs + [wspec(a) for a in weights],
        out_specs=[pl.BlockSpec((None, T, D_MODEL), lambda l, i: (l, i, 0))] + st_specs,
        out_shape=[jax.ShapeDtypeStruct((depth,) + x.shape, f32)] + [jax.ShapeDtypeStruct(a.shape, f32) for a in states],
        scratch_shapes=scratch,
        compiler_params=pltpu.CompilerParams(dimension_semantics=("arbitrary", "arbitrary"),
                                             vmem_limit_bytes=VMEM_LIMIT_BYTES),
        name="layers_sample",
    )(x, *states, *weights)


def _block_diag(blocks):
    *lead, n, r, c = blocks.shape
    eye = jnp.eye(n, dtype=blocks.dtype)
    return (blocks[..., :, :, None, :] * eye[:, None, :, None]).reshape(*lead, n * r, n * c)


def _pad_lanes(v):
    return jnp.pad(v, [(0, 0)] * (v.ndim - 1) + [(0, LANES - v.shape[-1])])


def kernel(x_prompt, x_sample, state_ssd, state_ssd_conv, state_lru, state_lru_conv, state_s5_re, state_s5_im, norm_g, w_in, ssd_conv_w, ssd_conv_b, ssd_dt_bias, ssd_a_log, ssd_d, ssd_norm_g, lru_conv_w, lru_conv_b, lru_wa, lru_ba, lru_wx, lru_bx, lru_lambda, s5_lambda_re, s5_lambda_im, s5_log_dt, s5_b_re, s5_b_im, s5_c_re, s5_c_im, s5_d, s5_glu_w, s5_glu_b, w_out, final_norm_g):
    depth = w_in.shape[0]
    nbp = x_prompt.shape[0]
    nbs, ls, _ = x_sample.shape
    assert ls == SEQ_S and x_prompt.shape[1] % TILE_P == 0 and (nbs * ls) % TILE_S == 0

    tab, bbar_re, bbar_im = _s5_prep(s5_lambda_re.astype(f32), s5_lambda_im.astype(f32), s5_log_dt.astype(f32),
                                     s5_b_re.astype(f32), s5_b_im.astype(f32))

    def row(v, n):
        return v.astype(f32).reshape(depth, 1, n)

    wi = w_in.astype(f32)
    w_in_r = jnp.concatenate(
        [wi[..., 0:1024], wi[..., 1024:3072], wi[..., 3088:3600], wi[..., 3600:4112], wi[..., 4112:4624],
         wi[..., 4624:5136], _pad_lanes(wi[..., 3072:3088])], axis=-1).astype(bf16)

    def halves(v):
        return _block_diag(v.reshape(depth, 2, S5_NGROUPS // 2, S5_GROUP, S5_STATE))

    bb = jnp.concatenate([halves(bbar_re), halves(bbar_im)], axis=1).astype(bf16)

    def chalves(v):
        return _block_diag(jnp.transpose(v.astype(f32), (0, 1, 3, 2)).reshape(depth, 2, S5_NGROUPS // 2, S5_STATE, S5_GROUP))

    cc = jnp.concatenate([chalves(s5_c_re), -chalves(s5_c_im)], axis=2).astype(bf16)
    weights = (
        row(norm_g, D_MODEL), w_in_r,
        ssd_conv_w.astype(f32), row(ssd_conv_b, SSD_CONV_DIM),
        _pad_lanes(row(ssd_dt_bias, SSD_HEADS)), _pad_lanes(row(ssd_a_log, SSD_HEADS)),
        jnp.repeat(ssd_d.astype(f32), SSD_HEADDIM, axis=-1).reshape(depth, 1, SSD_DIM),
        row(ssd_norm_g, SSD_DIM),
        lru_conv_w.astype(f32), row(lru_conv_b, LRU_DIM),
        jnp.concatenate([_block_diag(v.astype(f32).reshape(depth, 2, v.shape[1] // 2, *v.shape[2:]))
                         for v in (lru_wa, lru_wx)], axis=1).astype(bf16),
        jnp.concatenate([lru_ba, lru_bx], axis=-1).astype(f32).reshape(depth, 1, 2 * LRU_DIM),
        row(lru_lambda, LRU_DIM),
        tab, bb, cc, row(s5_d, S5_DIM),
        s5_glu_w.astype(bf16), row(s5_glu_b, S5_DIM),
        w_out.astype(bf16),
        jnp.broadcast_to(final_norm_g.astype(f32).reshape(1, 1, D_MODEL), (depth, 1, D_MODEL)),
    )

    seq_p = x_prompt.shape[1]
    xp = jnp.swapaxes(x_prompt.astype(f32).reshape(nbp, seq_p // CHUNK, SUBLANES, SEG, D_MODEL), 2, 3)
    xp = xp.reshape(nbp, seq_p, D_MODEL)
    outs_p = [[] for _ in range(6)]
    for i in range(depth):
        res = _prompt_call(i, i == depth - 1, xp, weights)
        xp = res[0]
        for j in range(6):
            outs_p[j].append(res[1 + j])
    y_prompt = jnp.swapaxes(xp.reshape(nbp, seq_p // CHUNK, SEG, SUBLANES, D_MODEL), 2, 3).reshape(nbp, seq_p, D_MODEL)

    ntile = nbs // SUBLANES
    nsub = TILE_S // SUBTILE_S

    def conv_in(v):
        v = jnp.swapaxes(v.astype(f32).reshape(depth, ntile, SUBLANES, CONV_WIDTH - 1, v.shape[-1]), 2, 3)
        return v.reshape(depth, ntile // nsub, nsub, CONV_WIDTH - 1, SUBLANES, v.shape[-1])

    def conv_out(v, dtype):
        v = v.reshape(depth, ntile, CONV_WIDTH - 1, SUBLANES, v.shape[-1])
        return jnp.swapaxes(v, 2, 3).reshape(depth, nbs, CONV_WIDTH - 1, v.shape[-1]).astype(dtype)

    xs = jnp.swapaxes(x_sample.astype(f32).reshape(ntile, SUBLANES, ls, D_MODEL), 1, 2).reshape(nbs * ls, D_MODEL)
    states_s = (state_ssd.astype(f32).reshape(depth, nbs, SSD_DIM, SSD_STATE), conv_in(state_ssd_conv),
                state_lru.astype(f32), conv_in(state_lru_conv),
                state_s5_re.astype(f32).reshape(depth, nbs, S5_FLAT), state_s5_im.astype(f32).reshape(depth, nbs, S5_FLAT))
    weights_s = tuple(w[:, :, TAB_A:TAB_A + SUBLANES] if w is tab else w for w in weights)
    res_s = _sample_call(xs, states_s, weights_s)
    y_sample = jnp.swapaxes(res_s[0][depth - 1].reshape(ntile, ls, SUBLANES, D_MODEL), 1, 2).reshape(nbs, ls, D_MODEL)

    def stack(lst, shape, dtype):
        return jnp.stack(lst).reshape((depth,) + shape).astype(dtype)

    ssd_shape = (SSD_HEADS, SSD_HEADDIM, SSD_STATE)
    s5_shape = (S5_NGROUPS, S5_STATE)
    return (
        y_prompt.astype(x_prompt.dtype), y_sample.astype(x_sample.dtype),
        stack(outs_p[0], (nbp,) + ssd_shape, state_ssd.dtype),
        res_s[1].reshape((depth, nbs) + ssd_shape).astype(state_ssd.dtype),
        stack(outs_p[1], (nbp, CONV_WIDTH - 1, SSD_CONV_DIM), state_ssd_conv.dtype),
        conv_out(res_s[2], state_ssd_conv.dtype),
        stack(outs_p[2], (nbp, LRU_DIM), state_lru.dtype), res_s[3].astype(state_lru.dtype),
        stack(outs_p[3], (nbp, CONV_WIDTH - 1, LRU_DIM), state_lru_conv.dtype),
        conv_out(res_s[4], state_lru_conv.dtype),
        stack(outs_p[4], (nbp,) + s5_shape, state_s5_re.dtype),
        res_s[5].reshape((depth, nbs) + s5_shape).astype(state_s5_re.dtype),
        stack(outs_p[5], (nbp,) + s5_shape, state_s5_im.dtype),
        res_s[6].reshape((depth, nbs) + s5_shape).astype(state_s5_im.dtype),
    )
```

```python
import functools

import jax
import jax.numpy as jnp
from jax import lax
from jax.experimental import pallas as pl
from jax.experimental.pallas import tpu as pltpu

f32 = jnp.float32
bf16 = jnp.bfloat16

D_MODEL = 1024
CONV_WIDTH = 4
SSD_DIM = 1024
SSD_HEADDIM = 64
SSD_HEADS = 16
SSD_GROUPS = 4
SSD_HPG = 4
SSD_STATE = 128
SSD_BC = SSD_GROUPS * SSD_STATE
SSD_CONV_DIM = SSD_DIM + 2 * SSD_BC
LRU_DIM = 512
LRU_C = 8.0
S5_DIM = 512
S5_GROUP = 16
S5_NGROUPS = 32
S5_STATE = 64
S5_FLAT = S5_NGROUPS * S5_STATE
S5_HALF = S5_FLAT // 2
EPS = 1e-6

LANES = 128
SUBLANES = 8
CHUNK = 128
TILE_P = 256
TILE_S = 256
SUBTILE_S = 64
SUBTILE_SHIFT = 6
SEQ_S = 8
STATE_BUFS = 4
STATE_SEQS = 2
NEG = -1e30

SEG = CHUNK // SUBLANES
SEG_SHIFT = 4
SUB_SHIFT = 3
HALO = (CONV_WIDTH - 1) * SUBLANES

TAB_A = 0
TAB_Q = TAB_A + SUBLANES
TAB_ASEG = TAB_Q + 3 * SUBLANES
TAB_PW = TAB_ASEG + SUBLANES
TAB_ROWS = TAB_PW + SEG * SUBLANES

C_Z = 0
C_XBC = C_Z + SSD_DIM
C_LRU = C_XBC + SSD_CONV_DIM
C_LRU_G = C_LRU + LRU_DIM
C_S5 = C_LRU_G + LRU_DIM
C_S5_G = C_S5 + S5_DIM
C_DT = C_S5_G + S5_DIM
IN_COLS = C_DT + LANES

VMEM_LIMIT_BYTES = 56 * 1024 * 1024

N_WEIGHTS = 21


def _rms(x, g):
    return x * lax.rsqrt(jnp.mean(x * x, axis=-1, keepdims=True) + EPS) * g


def _silu(x):
    return x * jax.nn.sigmoid(x)


def _dot(a, b):
    return jnp.dot(a, b, preferred_element_type=f32)


def _dot_nt(a, b):
    return lax.dot_general(a, b, (((1,), (1,)), ((), ())), preferred_element_type=f32)


def _dot_tn(a, b):
    return lax.dot_general(a, b, (((0,), (0,)), ((), ())), preferred_element_type=f32)


def _dot_exact(a, b):
    return jnp.dot(a, b, preferred_element_type=f32, precision=lax.Precision.HIGHEST)


def _pair_expand(v, j, lane_lo):
    q = v.shape[0]
    lo = jnp.broadcast_to(v[:, 2 * j:2 * j + 1], (q, LANES))
    hi = jnp.broadcast_to(v[:, 2 * j + 1:2 * j + 2], (q, LANES))
    return jnp.where(lane_lo, lo, hi)


def _s5_prep_body(lre_ref, lim_ref, ldt_ref, lre_rep_ref, lim_rep_ref, ldt_rep_ref,
                  bre_ref, bim_ref, tre_ref, tim_ref, bbre_ref, bbim_ref):
    def abar(lre, lim, ldt):
        delta = jnp.exp(ldt)
        mag = jnp.exp(lre * delta)
        return mag * jnp.cos(lim * delta), mag * jnp.sin(lim * delta)

    ar, ai = abar(lre_ref[...], lim_ref[...], ldt_ref[...])

    def cmul(xr, xi, yr, yi):
        return xr * yr - xi * yi, xr * yi + xi * yr

    pw = [(ar, ai)]
    for _ in range(SEG - 1):
        pw.append(cmul(*pw[-1], ar, ai))
    seg = [pw[SEG - 1]]
    for _ in range(2):
        seg.append(cmul(*seg[-1], *seg[-1]))
    zero = jnp.zeros_like(ar)

    def put(i, v):
        tre_ref[i] = v[0]
        tim_ref[i] = v[1]

    for r in range(SUBLANES):
        put(TAB_A + r, pw[0])
        put(TAB_ASEG + r, seg[0])
        for t, d in enumerate((1, 2, 4)):
            put(TAB_Q + t * SUBLANES + r, seg[t] if r >= d else (zero, zero))
        for k in range(SEG):
            put(TAB_PW + k * SUBLANES + r, pw[k])

    lre, lim = lre_rep_ref[...], lim_rep_ref[...]
    ar, ai = abar(lre, lim, ldt_rep_ref[...])
    denom = lre * lre + lim * lim
    nr = ar - 1.0
    ni = ai
    coef_re = (nr * lre + ni * lim) / denom
    coef_im = (ni * lre - nr * lim) / denom
    bre, bim = bre_ref[...], bim_ref[...]
    bbre_ref[...] = coef_re * bre - coef_im * bim
    bbim_ref[...] = coef_re * bim + coef_im * bre


def _s5_prep(lam_re, lam_im, log_dt, b_re, b_im):
    depth = lam_re.shape[0]
    rows_c = S5_FLAT // LANES
    rows_r = S5_DIM * S5_STATE // LANES
    ldt = jnp.broadcast_to(log_dt[:, :, None], (depth, S5_NGROUPS, S5_STATE))

    def rep(v):
        return jnp.broadcast_to(v[:, :, None, :], (depth, S5_NGROUPS, S5_GROUP, S5_STATE)).reshape(depth, rows_r, LANES)

    def bt(v):
        return jnp.transpose(v, (0, 1, 3, 2)).reshape(depth, rows_r, LANES)

    cspec = pl.BlockSpec((None, rows_c, LANES), lambda i: (i, 0, 0))
    rspec = pl.BlockSpec((None, rows_r, LANES), lambda i: (i, 0, 0))
    tspec = pl.BlockSpec((None, TAB_ROWS, rows_c, LANES), lambda i: (i, 0, 0, 0))
    tre, tim, bbre, bbim = pl.pallas_call(
        _s5_prep_body,
        grid=(depth,),
        in_specs=[cspec, cspec, cspec, rspec, rspec, rspec, rspec, rspec],
        out_specs=[tspec, tspec, rspec, rspec],
        out_shape=[jax.ShapeDtypeStruct((depth, TAB_ROWS, rows_c, LANES), f32)] * 2
        + [jax.ShapeDtypeStruct((depth, rows_r, LANES), f32)] * 2,
        name="s5_prep",
    )(lam_re.reshape(depth, rows_c, LANES), lam_im.reshape(depth, rows_c, LANES),
      ldt.reshape(depth, rows_c, LANES), rep(lam_re), rep(lam_im), rep(ldt), bt(b_re), bt(b_im))
    tab = jnp.stack([tre, tim], axis=1).reshape(depth, 2, TAB_ROWS, S5_FLAT)
    return tab, bbre.reshape(depth, S5_DIM, S5_STATE), bbim.reshape(depth, S5_DIM, S5_STATE)


def _layer_math(prompt, T, x, w, st, o, scr):
    (ng_ref, w_in_ref, cw_ssd_ref, cb_ssd_ref, dtb_ref, alog_ref, dfull_ref, sng_ref,
     cw_lru_ref, cb_lru_ref, lru_w_ref, lru_b_ref, lam_ref,
     tab_ref, bb_ref, cc_ref, s5d_ref, glu_w_ref, glu_b_ref, w_out_ref, _) = w
    o_ssd_ref, o_cssd_ref, o_lru_ref, o_clru_ref, o_s5r_ref, o_s5i_ref = o
    if prompt:
        h_ssd, prev_ssd, prev_lru, a_s, b_s, bur_s, bui_s, lru_c, s5_cr, s5_ci = scr
    else:
        h0_ssd_hbm, c0_ssd_ref, h0_lru_ref, c0_lru_ref, h0_s5r_ref, h0_s5i_ref = st
        a_s, b_s, bur_s, bui_s, c_s, bm_s, xw_s, yoff_s, eac_s, h_in, h_out, sem_in, sem_out = scr
    nseq = T // SEQ_S
    if not prompt:
        layer = pl.program_id(0)
        seq0 = pl.program_id(1) * nseq

        def in_copy(i, slot):
            return pltpu.make_async_copy(h0_ssd_hbm.at[layer, pl.ds(seq0 + i * STATE_SEQS, STATE_SEQS)],
                                         h_in.at[slot], sem_in.at[slot])

        def out_copy(i, slot):
            return pltpu.make_async_copy(h_out.at[slot],
                                         o_ssd_ref.at[layer, pl.ds(seq0 + i * STATE_SEQS, STATE_SEQS)],
                                         sem_out.at[slot])

        for j in range(STATE_BUFS - 1):
            in_copy(j, j).start()
    Q = CHUNK if prompt else SUBTILE_S

    hn = _rms(x, ng_ref[...]).astype(bf16)

    def proj(lo, hi):
        return _dot(hn, w_in_ref[:, lo:hi])

    def sub_iota(n):
        return lax.broadcasted_iota(jnp.int32, (SUBLANES, n), 0)

    def conv_taps(halo, rs, cw_ref, cb_ref):
        ext = jnp.concatenate([halo, rs], axis=0)
        n = rs.shape[0]
        acc = cb_ref[...] + cw_ref[3:4, :] * rs
        for j in range(1, CONV_WIDTH):
            acc = acc + cw_ref[3 - j:4 - j, :] * ext[HALO - SUBLANES * j:HALO - SUBLANES * j + n, :]
        return acc

    def conv(raw, prev_ref, c0_ref, cw_ref, cb_ref, o_ref):
        cdim = raw.shape[1]
        if not prompt:
            outs = []
            for s in range(T // Q):
                rs = raw[Q * s:Q * (s + 1), :]
                o_ref[s] = rs[Q - HALO:, :].reshape(CONV_WIDTH - 1, SUBLANES, cdim)
                outs.append(conv_taps(c0_ref[s].reshape(HALO, cdim), rs, cw_ref, cb_ref))
            return jnp.concatenate(outs, axis=0)
        first = sub_iota(cdim) == 0
        tail = prev_ref[...]
        outs = []
        for r0 in range(0, T, CHUNK):
            rs = raw[r0:r0 + CHUNK, :]
            cur = rs[CHUNK - HALO:, :]
            halo = jnp.concatenate(
                [jnp.where(first, pltpu.roll(tail[SUBLANES * k:SUBLANES * (k + 1), :], 1, 0),
                           pltpu.roll(cur[SUBLANES * k:SUBLANES * (k + 1), :], 1, 0))
                 for k in range(CONV_WIDTH - 1)], axis=0)
            outs.append(conv_taps(halo, rs, cw_ref, cb_ref))
            tail = cur
        prev_ref[...] = tail
        for k in range(CONV_WIDTH - 1):
            o_ref[k:k + 1, :] = tail[SUBLANES * k + SUBLANES - 1:SUBLANES * (k + 1), :]
        return jnp.concatenate(outs, axis=0)

    row = lax.broadcasted_iota(jnp.int32, (Q, Q), 0)
    col = lax.broadcasted_iota(jnp.int32, (Q, Q), 1)
    if prompt:
        def local_time(i):
            return jnp.bitwise_or(jnp.left_shift(jnp.bitwise_and(i, SUBLANES - 1), SEG_SHIFT),
                                  jnp.right_shift(i, SUB_SHIFT))
        causal = local_time(row) >= local_time(col)
    else:
        same_seq = jnp.bitwise_and(row, SUBLANES - 1) == jnp.bitwise_and(col, SUBLANES - 1)
        causal = jnp.logical_and(same_seq, jnp.right_shift(row, SUB_SHIFT) >= jnp.right_shift(col, SUB_SHIFT))
    tril = jnp.where(causal, 1.0, 0.0)
    lane_lo = lax.broadcasted_iota(jnp.int32, (Q, LANES), 1) < SSD_HEADDIM
    gsz = SSD_HPG * SSD_HEADDIM

    xbc = _silu(conv(proj(C_XBC, C_LRU), prev_ssd if prompt else None, None if prompt else c0_ssd_ref,
                     cw_ssd_ref, cb_ssd_ref, o_cssd_ref))
    dt_all = jax.nn.softplus(proj(C_DT, IN_COLS) + dtb_ref[...])
    a_neg = -jnp.exp(alog_ref[...])

    def ssd_state_io(rows_c, rows_xw, rows_b, e_last, h_get, h_set):
        outs = []
        for g in range(SSD_GROUPS):
            hp = h_get(g)
            outs.append(_dot_nt(rows_c[:, LANES * g:LANES * (g + 1)].astype(bf16), hp.astype(bf16)))
            sg = _dot_tn(rows_xw[:, gsz * g:gsz * (g + 1)].astype(bf16),
                         rows_b[:, LANES * g:LANES * (g + 1)].astype(bf16))
            dec = jnp.concatenate(
                [jnp.broadcast_to(e_last[:, SSD_HPG * g + k:SSD_HPG * g + k + 1], (SSD_HEADDIM, SSD_STATE))
                 for k in range(SSD_HPG)], axis=0)
            h_set(g, dec * hp + sg)
        return jnp.concatenate(outs, axis=1)

    def ssd_chunk(r0):
        xs = xbc[r0:r0 + Q, :SSD_DIM]
        bm = xbc[r0:r0 + Q, SSD_DIM:SSD_DIM + SSD_BC]
        cm = xbc[r0:r0 + Q, SSD_DIM + SSD_BC:]
        dt = dt_all[r0:r0 + Q, :]
        bm_b = bm.astype(bf16)
        cm_b = cm.astype(bf16)
        acum = _dot_exact(tril, dt * a_neg)
        acum_row = acum.T
        dt_row = dt.T
        eac = jnp.exp(acum)
        if prompt:
            acum_end = acum[Q - 1:Q, :]
        else:
            sel = jnp.where(col == jnp.bitwise_and(row, SUBLANES - 1) + (Q - SUBLANES), 1.0, 0.0)
            acum_end = _dot_exact(sel, acum)
        wgt = jnp.exp(acum_end - acum) * dt

        scores = [_dot_nt(cm_b[:, LANES * g:LANES * (g + 1)], bm_b[:, LANES * g:LANES * (g + 1)])
                  for g in range(SSD_GROUPS)]
        y_pairs, xw_pairs, ecol_pairs = [], [], []
        for j in range(SSD_HEADS // 2):
            g = (2 * j) // SSD_HPG
            ms = []
            for h in (2 * j, 2 * j + 1):
                diff = acum[:, h:h + 1] - acum_row[h:h + 1, :]
                decay = jnp.exp(jnp.where(causal, diff, NEG))
                ms.append((scores[g] * decay * dt_row[h:h + 1, :]).astype(bf16))
            xp = xs[:, LANES * j:LANES * (j + 1)]
            xbd = jnp.concatenate([jnp.where(lane_lo, xp, 0.0), jnp.where(lane_lo, 0.0, xp)], axis=0).astype(bf16)
            y_pairs.append(_dot(jnp.concatenate(ms, axis=1), xbd))
            xw_pairs.append(xp * _pair_expand(wgt, j, lane_lo))
            ecol_pairs.append(_pair_expand(eac, j, lane_lo))
        y_diag = jnp.concatenate(y_pairs, axis=1)
        xw = jnp.concatenate(xw_pairs, axis=1)
        ecol = jnp.concatenate(ecol_pairs, axis=1)

        if prompt:
            def h_get(g):
                return h_ssd[gsz * g:gsz * (g + 1), :]

            def h_set(g, v):
                h_ssd[gsz * g:gsz * (g + 1), :] = v

            y_off = ssd_state_io(cm, xw, bm, eac[Q - 1:Q, :], h_get, h_set)
            return y_diag + y_off * ecol + dfull_ref[...] * xs
        c_s[r0:r0 + Q, :] = _dot(perm_b, cm_b)
        bm_s[r0:r0 + Q, :] = _dot(perm_b, bm_b)
        xw_s[r0:r0 + Q, :] = _dot(perm_b, xw.astype(bf16))
        eac_s[r0:r0 + Q, :] = eac
        return y_diag + dfull_ref[...] * xs, ecol

    if prompt:
        y = jnp.concatenate([ssd_chunk(r0) for r0 in range(0, T, Q)], axis=0)
        o_ssd_ref[...] = h_ssd[...]
    else:
        to_seq = jnp.bitwise_or(jnp.left_shift(jnp.bitwise_and(row, SUBLANES - 1), SUB_SHIFT),
                                jnp.right_shift(row, SUB_SHIFT)) == col
        perm_b = jnp.where(to_seq, 1.0, 0.0).astype(bf16)
        parts = [ssd_chunk(r0) for r0 in range(0, T, Q)]

        ngrp = nseq // STATE_SEQS

        def seq_step(i, carry):
            slot = jnp.bitwise_and(i, STATE_BUFS - 1)
            ahead = i + (STATE_BUFS - 1)

            @pl.when(ahead < ngrp)
            def _():
                in_copy(ahead, jnp.bitwise_and(ahead, STATE_BUFS - 1)).start()

            in_copy(i, slot).wait()

            @pl.when(i >= STATE_BUFS)
            def _():
                out_copy(i - STATE_BUFS, slot).wait()

            for q in range(STATE_SEQS):
                sq = i * STATE_SEQS + q
                s0 = pl.multiple_of(sq * SEQ_S, SEQ_S)

                def h_get(g):
                    return h_in[slot, q, pl.ds(gsz * g, gsz), :]

                def h_set(g, v):
                    h_out[slot, q, pl.ds(gsz * g, gsz), :] = v

                e_row = jnp.left_shift(jnp.right_shift(sq, SUB_SHIFT), SUBTILE_SHIFT) + (Q - SUBLANES) \
                    + jnp.bitwise_and(sq, SUBLANES - 1)
                yoff_s[pl.ds(s0, SEQ_S), :] = ssd_state_io(
                    c_s[pl.ds(s0, SEQ_S), :], xw_s[pl.ds(s0, SEQ_S), :], bm_s[pl.ds(s0, SEQ_S), :],
                    eac_s[pl.ds(e_row, 1), :], h_get, h_set)
            out_copy(i, slot).start()
            return carry

        lax.fori_loop(0, ngrp, seq_step, 0)
        for j in range(STATE_BUFS):
            out_copy(ngrp - STATE_BUFS + j, j).wait()
        perm_f = jnp.where(to_seq, 1.0, 0.0)
        y = jnp.concatenate(
            [part + _dot_exact(perm_f, yoff_s[r0:r0 + Q, :]) * ecol
             for r0, (part, ecol) in zip(range(0, T, Q), parts)], axis=0)
    y_ssd = _rms(y * _silu(proj(C_Z, C_XBC)), sng_ref[...])

    xr = conv(proj(C_LRU, C_LRU_G), prev_lru if prompt else None, None if prompt else c0_lru_ref,
              cw_lru_ref, cb_lru_ref, o_clru_ref)
    xr_b = xr.astype(bf16)
    hl = LRU_DIM // 2
    gates = jnp.concatenate([_dot(xr_b[:, hl * (k % 2):hl * (k % 2 + 1)], lru_w_ref[k]) for k in range(4)],
                            axis=1) + lru_b_ref[...]
    r_gate = jax.nn.sigmoid(gates[:, :LRU_DIM])
    i_gate = jax.nn.sigmoid(gates[:, LRU_DIM:])
    log_a = -LRU_C * r_gate * jax.nn.softplus(-lam_ref[...])
    a_t = jnp.exp(log_a)
    gain = jnp.sqrt(jnp.maximum(-jnp.tanh(log_a) * (a_t * a_t + 1.0), 0.0))
    a_s[...] = a_t
    b_s[...] = gain * i_gate * xr

    def vrow(ref, r0, k):
        return ref[r0 + SUBLANES * k:r0 + SUBLANES * (k + 1), :]

    def set_vrow(ref, r0, k, v):
        ref[r0 + SUBLANES * k:r0 + SUBLANES * (k + 1), :] = v

    if prompt:
        sub = sub_iota(LRU_DIM)
        carry = lru_c[...]
        for r0 in range(0, T, CHUNK):
            acc_a, acc_h = vrow(a_s, r0, 0), vrow(b_s, r0, 0)
            for k in range(1, SEG):
                a_k = vrow(a_s, r0, k)
                acc_h = a_k * acc_h + vrow(b_s, r0, k)
                acc_a = a_k * acc_a
                set_vrow(a_s, r0, k, acc_a)
                set_vrow(b_s, r0, k, acc_h)
            alpha = jnp.where(sub == 0, 0.0, pltpu.roll(acc_a, 1, 0))
            beta = jnp.where(sub == 0, jnp.broadcast_to(carry, (SUBLANES, LRU_DIM)), pltpu.roll(acc_h, 1, 0))
            for d in (1, 2, 4):
                a_sh = jnp.where(sub >= d, pltpu.roll(alpha, d, 0), 1.0)
                b_sh = jnp.where(sub >= d, pltpu.roll(beta, d, 0), 0.0)
                beta = alpha * b_sh + beta
                alpha = alpha * a_sh
            carry = (acc_a * beta + acc_h)[SUBLANES - 1:SUBLANES, :]
            for k in range(SEG):
                set_vrow(b_s, r0, k, vrow(b_s, r0, k) + vrow(a_s, r0, k) * beta)
        lru_c[...] = carry
        o_lru_ref[...] = carry
    else:
        for s in range(T // Q):
            h = h0_lru_ref[SUBLANES * s:SUBLANES * (s + 1), :]
            for k in range(Q // SUBLANES):
                h = vrow(a_s, Q * s, k) * h + vrow(b_s, Q * s, k)
                set_vrow(b_s, Q * s, k, h)
            o_lru_ref[SUBLANES * s:SUBLANES * (s + 1), :] = h
    y_lru = b_s[...] * _silu(proj(C_LRU_G, C_S5))

    u = proj(C_S5, C_S5_G)
    u_b = u.astype(bf16)
    half = S5_DIM // 2
    for k in range(2):
        uk = u_b[:, half * k:half * (k + 1)]
        bur_s[:, S5_HALF * k:S5_HALF * (k + 1)] = _dot(uk, bb_ref[k])
        bui_s[:, S5_HALF * k:S5_HALF * (k + 1)] = _dot(uk, bb_ref[2 + k])

    def tab(r0):
        return tab_ref[0, r0:r0 + SUBLANES, :], tab_ref[1, r0:r0 + SUBLANES, :]

    def cmul_add(pr, pi, xr, xi, yr, yi):
        return pr * xr - pi * xi + yr, pr * xi + pi * xr + yi

    ar, ai = tab(TAB_A)
    if prompt:
        sub = sub_iota(S5_FLAT)
        c_r, c_i = s5_cr[...], s5_ci[...]
        for r0 in range(0, T, CHUNK):
            hr, hi = vrow(bur_s, r0, 0), vrow(bui_s, r0, 0)
            for k in range(1, SEG):
                hr, hi = cmul_add(ar, ai, hr, hi, vrow(bur_s, r0, k), vrow(bui_s, r0, k))
                set_vrow(bur_s, r0, k, hr)
                set_vrow(bui_s, r0, k, hi)
            er = jnp.where(sub == 0, jnp.broadcast_to(c_r, (SUBLANES, S5_FLAT)), pltpu.roll(hr, 1, 0))
            ei = jnp.where(sub == 0, jnp.broadcast_to(c_i, (SUBLANES, S5_FLAT)), pltpu.roll(hi, 1, 0))
            for t, d in enumerate((1, 2, 4)):
                qr, qi = tab(TAB_Q + t * SUBLANES)
                er, ei = cmul_add(qr, qi, pltpu.roll(er, d, 0), pltpu.roll(ei, d, 0), er, ei)
            sr, si = tab(TAB_ASEG)
            nr, ni = cmul_add(sr, si, er, ei, hr, hi)
            c_r, c_i = nr[SUBLANES - 1:SUBLANES, :], ni[SUBLANES - 1:SUBLANES, :]
            for k in range(SEG):
                pr, pi = tab(TAB_PW + k * SUBLANES)
                vr, vi = cmul_add(pr, pi, er, ei, vrow(bur_s, r0, k), vrow(bui_s, r0, k))
                set_vrow(bur_s, r0, k, vr)
                set_vrow(bui_s, r0, k, vi)
        s5_cr[...] = c_r
        s5_ci[...] = c_i
        o_s5r_ref[...] = c_r
        o_s5i_ref[...] = c_i
    else:
        for s in range(T // Q):
            rows = slice(SUBLANES * s, SUBLANES * (s + 1))
            hr, hi = h0_s5r_ref[rows, :], h0_s5i_ref[rows, :]
            for k in range(Q // SUBLANES):
                hr, hi = cmul_add(ar, ai, hr, hi, vrow(bur_s, Q * s, k), vrow(bui_s, Q * s, k))
                set_vrow(bur_s, Q * s, k, hr)
                set_vrow(bui_s, Q * s, k, hi)
            o_s5r_ref[rows, :] = hr
            o_s5i_ref[rows, :] = hi
    ys = []
    for k in range(2):
        hk = jnp.concatenate([bur_s[:, S5_HALF * k:S5_HALF * (k + 1)].astype(bf16),
                              bui_s[:, S5_HALF * k:S5_HALF * (k + 1)].astype(bf16)], axis=1)
        ys.append(_dot(hk, cc_ref[k]))
    ys5 = jnp.concatenate(ys, axis=1) + s5d_ref[...] * u
    ys5 = jax.nn.gelu(ys5)
    ys5 = ys5 * jax.nn.sigmoid(_dot(ys5.astype(bf16), glu_w_ref[...]) + glu_b_ref[...])
    y_s5 = ys5 * _silu(proj(C_S5_G, C_DT))

    ycat = jnp.concatenate([y_ssd.astype(bf16), y_lru.astype(bf16), y_s5.astype(bf16)], axis=1)
    return x + _dot(ycat, w_out_ref[...])


def _prompt_body(final, *refs):
    x_ref = refs[0]
    w = refs[1:1 + N_WEIGHTS]
    y_ref = refs[1 + N_WEIGHTS]
    o = refs[2 + N_WEIGHTS:8 + N_WEIGHTS]
    scr = refs[8 + N_WEIGHTS:]
    h_ssd, prev_ssd, prev_lru, _, _, _, _, lru_c, s5_cr, s5_ci = scr

    @pl.when(pl.program_id(1) == 0)
    def _():
        h_ssd[...] = jnp.zeros_like(h_ssd)
        prev_ssd[...] = jnp.zeros_like(prev_ssd)
        prev_lru[...] = jnp.zeros_like(prev_lru)
        lru_c[...] = jnp.zeros_like(lru_c)
        s5_cr[...] = jnp.zeros_like(s5_cr)
        s5_ci[...] = jnp.zeros_like(s5_ci)

    out = _layer_math(True, TILE_P, x_ref[...], w, None, o, scr)
    if final:
        out = _rms(out, w[-1][...])
    y_ref[...] = out


def _sample_body(*refs):
    x_ref = refs[0]
    st = refs[1:7]
    w = refs[7:7 + N_WEIGHTS]
    y_ref = refs[7 + N_WEIGHTS]
    o = refs[8 + N_WEIGHTS:14 + N_WEIGHTS]
    x_all = refs[14 + N_WEIGHTS]
    scr = refs[15 + N_WEIGHTS:]
    layer = pl.program_id(0)
    last_layer = layer == pl.num_programs(0) - 1
    r0 = pl.multiple_of(pl.program_id(1) * TILE_S, TILE_S)

    @pl.when(layer == 0)
    def _():
        x_all[pl.ds(r0, TILE_S), :] = x_ref[...]

    out = _layer_math(False, TILE_S, x_all[pl.ds(r0, TILE_S), :], w, st, o, scr)
    x_all[pl.ds(r0, TILE_S), :] = out

    @pl.when(last_layer)
    def _():
        y_ref[...] = _rms(out, w[-1][...])

    @pl.when(jnp.logical_not(last_layer))
    def _():
        y_ref[...] = out


def _prompt_call(layer, final, x, weights):
    T = TILE_P
    nb, seq, _ = x.shape

    def wspec(a):
        nd = a.ndim - 1
        return pl.BlockSpec((None,) + a.shape[1:], lambda b, c: (layer,) + (0,) * nd, pipeline_mode=pl.Buffered(1))

    def st(shape):
        nd = len(shape)
        return pl.BlockSpec((None,) + shape, lambda b, c: (b,) + (0,) * nd)

    x_spec = pl.BlockSpec((None, T, D_MODEL), lambda b, c: (b, c, 0))
    out_specs = [x_spec, st((SSD_DIM, SSD_STATE)), st((CONV_WIDTH - 1, SSD_CONV_DIM)), st((1, LRU_DIM)),
                 st((CONV_WIDTH - 1, LRU_DIM)), st((1, S5_FLAT)), st((1, S5_FLAT))]
    out_shape = [jax.ShapeDtypeStruct(x.shape, f32),
                 jax.ShapeDtypeStruct((nb, SSD_DIM, SSD_STATE), f32),
                 jax.ShapeDtypeStruct((nb, CONV_WIDTH - 1, SSD_CONV_DIM), f32),
                 jax.ShapeDtypeStruct((nb, 1, LRU_DIM), f32),
                 jax.ShapeDtypeStruct((nb, CONV_WIDTH - 1, LRU_DIM), f32),
                 jax.ShapeDtypeStruct((nb, 1, S5_FLAT), f32),
                 jax.ShapeDtypeStruct((nb, 1, S5_FLAT), f32)]
    scratch = [pltpu.VMEM((SSD_DIM, SSD_STATE), f32),
               pltpu.VMEM((HALO, SSD_CONV_DIM), f32),
               pltpu.VMEM((HALO, LRU_DIM), f32),
               pltpu.VMEM((T, LRU_DIM), f32), pltpu.VMEM((T, LRU_DIM), f32),
               pltpu.VMEM((T, S5_FLAT), f32), pltpu.VMEM((T, S5_FLAT), f32),
               pltpu.VMEM((1, LRU_DIM), f32),
               pltpu.VMEM((1, S5_FLAT), f32), pltpu.VMEM((1, S5_FLAT), f32)]
    return pl.pallas_call(
        functools.partial(_prompt_body, final),
        grid=(nb, seq // T), in_specs=[x_spec] + [wspec(a) for a in weights],
        out_specs=out_specs, out_shape=out_shape, scratch_shapes=scratch,
        compiler_params=pltpu.CompilerParams(dimension_semantics=("arbitrary", "arbitrary"),
                                             vmem_limit_bytes=VMEM_LIMIT_BYTES),
        name="layer_prompt",
    )(x, *weights)


def _sample_call(x, states, weights):
    T = TILE_S
    rows = x.shape[0]
    depth = weights[0].shape[0]
    nseq = T // SEQ_S

    def wspec(a):
        nd = a.ndim - 1
        return pl.BlockSpec((None,) + a.shape[1:], lambda l, i: (l,) + (0,) * nd, pipeline_mode=pl.Buffered(1))

    def st(a):
        nd = a.ndim - 2
        if a.ndim == 4:
            return pl.BlockSpec(memory_space=pl.ANY)
        if a.ndim == 6:
            return pl.BlockSpec((None, None) + a.shape[2:], lambda l, i: (l, i) + (0,) * nd)
        return pl.BlockSpec((None, nseq) + a.shape[2:], lambda l, i: (l, i) + (0,) * nd)

    x_spec = pl.BlockSpec((T, D_MODEL), lambda l, i: (i, 0))
    st_specs = [st(a) for a in states]
    scratch = [pltpu.VMEM((rows, D_MODEL), f32),
               pltpu.VMEM((T, LRU_DIM), f32), pltpu.VMEM((T, LRU_DIM), f32),
               pltpu.VMEM((T, S5_FLAT), f32), pltpu.VMEM((T, S5_FLAT), f32),
               pltpu.VMEM((T, SSD_BC), f32), pltpu.VMEM((T, SSD_BC), f32),
               pltpu.VMEM((T, SSD_DIM), f32), pltpu.VMEM((T, SSD_DIM), f32),
               pltpu.VMEM((T, LANES), f32),
               pltpu.VMEM((STATE_BUFS, STATE_SEQS, SSD_DIM, SSD_STATE), f32),
               pltpu.VMEM((STATE_BUFS, STATE_SEQS, SSD_DIM, SSD_STATE), f32),
               pltpu.SemaphoreType.DMA((STATE_BUFS,)), pltpu.SemaphoreType.DMA((STATE_BUFS,))]
    return pl.pallas_call(
        _sample_body,
        grid=(depth, rows // T), in_specs=[x_spec] + st_specs + [wspec(a) for a in weights],
        out_specs=[pl.BlockSpec((None, T, D_MODEL), lambda l, i: (l, i, 0))] + st_specs,
        out_shape=[jax.ShapeDtypeStruct((depth,) + x.shape, f32)] + [jax.ShapeDtypeStruct(a.shape, f32) for a in states],
        scratch_shapes=scratch,
        compiler_params=pltpu.CompilerParams(dimension_semantics=("arbitrary", "arbitrary"),
                                             vmem_limit_bytes=VMEM_LIMIT_BYTES),
        name="layers_sample",
    )(x, *states, *weights)


def _block_diag(blocks):
    *lead, n, r, c = blocks.shape
    eye = jnp.eye(n, dtype=blocks.dtype)
    return (blocks[..., :, :, None, :] * eye[:, None, :, None]).reshape(*lead, n * r, n * c)


def _pad_lanes(v):
    return jnp.pad(v, [(0, 0)] * (v.ndim - 1) + [(0, LANES - v.shape[-1])])


def kernel(x_prompt, x_sample, state_ssd, state_ssd_conv, state_lru, state_lru_conv, state_s5_re, state_s5_im, norm_g, w_in, ssd_conv_w, ssd_conv_b, ssd_dt_bias, ssd_a_log, ssd_d, ssd_norm_g, lru_conv_w, lru_conv_b, lru_wa, lru_ba, lru_wx, lru_bx, lru_lambda, s5_lambda_re, s5_lambda_im, s5_log_dt, s5_b_re, s5_b_im, s5_c_re, s5_c_im, s5_d, s5_glu_w, s5_glu_b, w_out, final_norm_g):
    depth = w_in.shape[0]
    nbp = x_prompt.shape[0]
    nbs, ls, _ = x_sample.shape
    assert ls == SEQ_S and x_prompt.shape[1] % TILE_P == 0 and (nbs * ls) % TILE_S == 0

    tab, bbar_re, bbar_im = _s5_prep(s5_lambda_re.astype(f32), s5_lambda_im.astype(f32), s5_log_dt.astype(f32),
                                     s5_b_re.astype(f32), s5_b_im.astype(f32))

    def row(v, n):
        return v.astype(f32).reshape(depth, 1, n)

    wi = w_in.astype(bf16)
    w_in_r = jnp.concatenate([wi[..., 0:3072], wi[..., 3088:5136], _pad_lanes(wi[..., 3072:3088])], axis=-1)

    def halves(v):
        return _block_diag(v.reshape(depth, 2, S5_NGROUPS // 2, S5_GROUP, S5_STATE))

    bb = jnp.concatenate([halves(bbar_re), halves(bbar_im)], axis=1).astype(bf16)

    def chalves(v):
        return _block_diag(jnp.transpose(v.astype(f32), (0, 1, 3, 2)).reshape(depth, 2, S5_NGROUPS // 2, S5_STATE, S5_GROUP))

    cc = jnp.concatenate([chalves(s5_c_re), -chalves(s5_c_im)], axis=2).astype(bf16)
    weights = (
        row(norm_g, D_MODEL), w_in_r,
        ssd_conv_w.astype(f32), row(ssd_conv_b, SSD_CONV_DIM),
        _pad_lanes(row(ssd_dt_bias, SSD_HEADS)), _pad_lanes(row(ssd_a_log, SSD_HEADS)),
        jnp.repeat(ssd_d.astype(f32), SSD_HEADDIM, axis=-1).reshape(depth, 1, SSD_DIM),
        row(ssd_norm_g, SSD_DIM),
        lru_conv_w.astype(f32), row(lru_conv_b, LRU_DIM),
        jnp.concatenate([_block_diag(v.astype(f32).reshape(depth, 2, v.shape[1] // 2, *v.shape[2:]))
                         for v in (lru_wa, lru_wx)], axis=1).astype(bf16),
        jnp.concatenate([lru_ba, lru_bx], axis=-1).astype(f32).reshape(depth, 1, 2 * LRU_DIM),
        row(lru_lambda, LRU_DIM),
        tab, bb, cc, row(s5_d, S5_DIM),
        s5_glu_w.astype(bf16), row(s5_glu_b, S5_DIM),
        w_out.astype(bf16),
        jnp.broadcast_to(final_norm_g.astype(f32).reshape(1, 1, D_MODEL), (depth, 1, D_MODEL)),
    )

    seq_p = x_prompt.shape[1]
    xp = jnp.swapaxes(x_prompt.astype(f32).reshape(nbp, seq_p // CHUNK, SUBLANES, SEG, D_MODEL), 2, 3)
    xp = xp.reshape(nbp, seq_p, D_MODEL)
    outs_p = [[] for _ in range(6)]
    for i in range(depth):
        res = _prompt_call(i, i == depth - 1, xp, weights)
        xp = res[0]
        for j in range(6):
            outs_p[j].append(res[1 + j])
    y_prompt = jnp.swapaxes(xp.reshape(nbp, seq_p // CHUNK, SEG, SUBLANES, D_MODEL), 2, 3).reshape(nbp, seq_p, D_MODEL)

    ntile = nbs // SUBLANES
    nsub = TILE_S // SUBTILE_S

    def conv_in(v):
        v = jnp.swapaxes(v.astype(f32).reshape(depth, ntile, SUBLANES, CONV_WIDTH - 1, v.shape[-1]), 2, 3)
        return v.reshape(depth, ntile // nsub, nsub, CONV_WIDTH - 1, SUBLANES, v.shape[-1])

    def conv_out(v, dtype):
        v = v.reshape(depth, ntile, CONV_WIDTH - 1, SUBLANES, v.shape[-1])
        return jnp.swapaxes(v, 2, 3).reshape(depth, nbs, CONV_WIDTH - 1, v.shape[-1]).astype(dtype)

    xs = jnp.swapaxes(x_sample.astype(f32).reshape(ntile, SUBLANES, ls, D_MODEL), 1, 2).reshape(nbs * ls, D_MODEL)
    states_s = (state_ssd.astype(f32).reshape(depth, nbs, SSD_DIM, SSD_STATE), conv_in(state_ssd_conv),
                state_lru.astype(f32), conv_in(state_lru_conv),
                state_s5_re.astype(f32).reshape(depth, nbs, S5_FLAT), state_s5_im.astype(f32).reshape(depth, nbs, S5_FLAT))
    weights_s = tuple(w[:, :, TAB_A:TAB_A + SUBLANES] if w is tab else w for w in weights)
    res_s = _sample_call(xs, states_s, weights_s)
    y_sample = jnp.swapaxes(res_s[0][depth - 1].reshape(ntile, ls, SUBLANES, D_MODEL), 1, 2).reshape(nbs, ls, D_MODEL)

    def stack(lst, shape, dtype):
        return jnp.stack(lst).reshape((depth,) + shape).astype(dtype)

    ssd_shape = (SSD_HEADS, SSD_HEADDIM, SSD_STATE)
    s5_shape = (S5_NGROUPS, S5_STATE)
    return (
        y_prompt.astype(x_prompt.dtype), y_sample.astype(x_sample.dtype),
        stack(outs_p[0], (nbp,) + ssd_shape, state_ssd.dtype),
        res_s[1].reshape((depth, nbs) + ssd_shape).astype(state_ssd.dtype),
        stack(outs_p[1], (nbp, CONV_WIDTH - 1, SSD_CONV_DIM), state_ssd_conv.dtype),
        conv_out(res_s[2], state_ssd_conv.dtype),
        stack(outs_p[2], (nbp, LRU_DIM), state_lru.dtype), res_s[3].astype(state_lru.dtype),
        stack(outs_p[3], (nbp, CONV_WIDTH - 1, LRU_DIM), state_lru_conv.dtype),
        conv_out(res_s[4], state_lru_conv.dtype),
        stack(outs_p[4], (nbp,) + s5_shape, state_s5_re.dtype),
        res_s[5].reshape((depth, nbs) + s5_shape).astype(state_s5_re.dtype),
        stack(outs_p[5], (nbp,) + s5_shape, state_s5_im.dtype),
        res_s[6].reshape((depth, nbs) + s5_shape).astype(state_s5_im.dtype),
    )
```

```python
import functools

import jax
import jax.numpy as jnp
from jax import lax
from jax.experimental import pallas as pl
from jax.experimental.pallas import tpu as pltpu

f32 = jnp.float32
bf16 = jnp.bfloat16

D_MODEL = 1024
CONV_WIDTH = 4
SSD_DIM = 1024
SSD_HEADDIM = 64
SSD_HEADS = 16
SSD_GROUPS = 4
SSD_HPG = 4
SSD_STATE = 128
SSD_BC = SSD_GROUPS * SSD_STATE
SSD_CONV_DIM = SSD_DIM + 2 * SSD_BC
LRU_DIM = 512
LRU_C = 8.0
S5_DIM = 512
S5_GROUP = 16
S5_NGROUPS = 32
S5_STATE = 64
S5_FLAT = S5_NGROUPS * S5_STATE
S5_HALF = S5_FLAT // 2
EPS = 1e-6

LANES = 128
SUBLANES = 8
CHUNK = 128
TILE_P = 256
TILE_S = 256
SUBTILE_S = 64
SUBTILE_SHIFT = 6
SEQ_S = 8
STATE_BUFS = 4
STATE_SEQS = 2
NEG = -1e30

SEG = CHUNK // SUBLANES
SEG_SHIFT = 4
SUB_SHIFT = 3
HALO = (CONV_WIDTH - 1) * SUBLANES

TAB_A = 0
TAB_Q = TAB_A + SUBLANES
TAB_ASEG = TAB_Q + 3 * SUBLANES
TAB_PW = TAB_ASEG + SUBLANES
TAB_ROWS = TAB_PW + SEG * SUBLANES

C_Z = 0
C_XBC = C_Z + SSD_DIM
C_LRU = C_XBC + SSD_CONV_DIM
C_LRU_G = C_LRU + LRU_DIM
C_S5 = C_LRU_G + LRU_DIM
C_S5_G = C_S5 + S5_DIM
C_DT = C_S5_G + S5_DIM
IN_COLS = C_DT + LANES

VMEM_LIMIT_BYTES = 56 * 1024 * 1024

N_WEIGHTS = 21


def _rms(x, g):
    return x * lax.rsqrt(jnp.mean(x * x, axis=-1, keepdims=True) + EPS) * g


def _silu(x):
    return x * jax.nn.sigmoid(x)


def _dot(a, b):
    return jnp.dot(a, b, preferred_element_type=f32)


def _dot_nt(a, b):
    return lax.dot_general(a, b, (((1,), (1,)), ((), ())), preferred_element_type=f32)


def _dot_tn(a, b):
    return lax.dot_general(a, b, (((0,), (0,)), ((), ())), preferred_element_type=f32)


def _dot_exact(a, b):
    return jnp.dot(a, b, preferred_element_type=f32, precision=lax.Precision.HIGHEST)


def _pair_expand(v, j, lane_lo):
    q = v.shape[0]
    lo = jnp.broadcast_to(v[:, 2 * j:2 * j + 1], (q, LANES))
    hi = jnp.broadcast_to(v[:, 2 * j + 1:2 * j + 2], (q, LANES))
    return jnp.where(lane_lo, lo, hi)


def _s5_prep_body(lre_ref, lim_ref, ldt_ref, lre_rep_ref, lim_rep_ref, ldt_rep_ref,
                  bre_ref, bim_ref, tre_ref, tim_ref, bbre_ref, bbim_ref):
    def abar(lre, lim, ldt):
        delta = jnp.exp(ldt)
        mag = jnp.exp(lre * delta)
        return mag * jnp.cos(lim * delta), mag * jnp.sin(lim * delta)

    ar, ai = abar(lre_ref[...], lim_ref[...], ldt_ref[...])

    def cmul(xr, xi, yr, yi):
        return xr * yr - xi * yi, xr * yi + xi * yr

    pw = [(ar, ai)]
    for _ in range(SEG - 1):
        pw.append(cmul(*pw[-1], ar, ai))
    seg = [pw[SEG - 1]]
    for _ in range(2):
        seg.append(cmul(*seg[-1], *seg[-1]))
    zero = jnp.zeros_like(ar)

    def put(i, v):
        tre_ref[i] = v[0]
        tim_ref[i] = v[1]

    for r in range(SUBLANES):
        put(TAB_A + r, pw[0])
        put(TAB_ASEG + r, seg[0])
        for t, d in enumerate((1, 2, 4)):
            put(TAB_Q + t * SUBLANES + r, seg[t] if r >= d else (zero, zero))
        for k in range(SEG):
            put(TAB_PW + k * SUBLANES + r, pw[k])

    lre, lim = lre_rep_ref[...], lim_rep_ref[...]
    ar, ai = abar(lre, lim, ldt_rep_ref[...])
    denom = lre * lre + lim * lim
    nr = ar - 1.0
    ni = ai
    coef_re = (nr * lre + ni * lim) / denom
    coef_im = (ni * lre - nr * lim) / denom
    bre, bim = bre_ref[...], bim_ref[...]
    bbre_ref[...] = coef_re * bre - coef_im * bim
    bbim_ref[...] = coef_re * bim + coef_im * bre


def _s5_prep(lam_re, lam_im, log_dt, b_re, b_im):
    depth = lam_re.shape[0]
    rows_c = S5_FLAT // LANES
    rows_r = S5_DIM * S5_STATE // LANES
    ldt = jnp.broadcast_to(log_dt[:, :, None], (depth, S5_NGROUPS, S5_STATE))

    def rep(v):
        return jnp.broadcast_to(v[:, :, None, :], (depth, S5_NGROUPS, S5_GROUP, S5_STATE)).reshape(depth, rows_r, LANES)

    def bt(v):
        return jnp.transpose(v, (0, 1, 3, 2)).reshape(depth, rows_r, LANES)

    cspec = pl.BlockSpec((None, rows_c, LANES), lambda i: (i, 0, 0))
    rspec = pl.BlockSpec((None, rows_r, LANES), lambda i: (i, 0, 0))
    tspec = pl.BlockSpec((None, TAB_ROWS, rows_c, LANES), lambda i: (i, 0, 0, 0))
    tre, tim, bbre, bbim = pl.pallas_call(
        _s5_prep_body,
        grid=(depth,),
        in_specs=[cspec, cspec, cspec, rspec, rspec, rspec, rspec, rspec],
        out_specs=[tspec, tspec, rspec, rspec],
        out_shape=[jax.ShapeDtypeStruct((depth, TAB_ROWS, rows_c, LANES), f32)] * 2
        + [jax.ShapeDtypeStruct((depth, rows_r, LANES), f32)] * 2,
        name="s5_prep",
    )(lam_re.reshape(depth, rows_c, LANES), lam_im.reshape(depth, rows_c, LANES),
      ldt.reshape(depth, rows_c, LANES), rep(lam_re), rep(lam_im), rep(ldt), bt(b_re), bt(b_im))
    tab = jnp.stack([tre, tim], axis=1).reshape(depth, 2, TAB_ROWS, S5_FLAT)
    return tab, bbre.reshape(depth, S5_DIM, S5_STATE), bbim.reshape(depth, S5_DIM, S5_STATE)


def _layer_math(prompt, T, x, w, st, o, scr):
    (ng_ref, w_in_ref, cw_ssd_ref, cb_ssd_ref, dtb_ref, alog_ref, dfull_ref, sng_ref,
     cw_lru_ref, cb_lru_ref, lru_w_ref, lru_b_ref, lam_ref,
     tab_ref, bb_ref, cc_ref, s5d_ref, glu_w_ref, glu_b_ref, w_out_ref, _) = w
    o_ssd_ref, o_cssd_ref, o_lru_ref, o_clru_ref, o_s5r_ref, o_s5i_ref = o
    if prompt:
        h_ssd, prev_ssd, prev_lru, a_s, b_s, bur_s, bui_s, lru_c, s5_cr, s5_ci = scr
    else:
        h0_ssd_hbm, c0_ssd_ref, h0_lru_ref, c0_lru_ref, h0_s5r_ref, h0_s5i_ref = st
        a_s, b_s, bur_s, bui_s, c_s, bm_s, xw_s, yoff_s, eac_s, h_in, h_out, sem_in, sem_out = scr
    nseq = T // SEQ_S
    if not prompt:
        layer = pl.program_id(0)
        seq0 = pl.program_id(1) * nseq

        def in_copy(i, slot):
            return pltpu.make_async_copy(h0_ssd_hbm.at[layer, pl.ds(seq0 + i * STATE_SEQS, STATE_SEQS)],
                                         h_in.at[slot], sem_in.at[slot])

        def out_copy(i, slot):
            return pltpu.make_async_copy(h_out.at[slot],
                                         o_ssd_ref.at[layer, pl.ds(seq0 + i * STATE_SEQS, STATE_SEQS)],
                                         sem_out.at[slot])

        for j in range(STATE_BUFS - 1):
            in_copy(j, j).start()
    Q = CHUNK if prompt else SUBTILE_S

    hn = _rms(x, ng_ref[...]).astype(bf16)

    def proj(lo, hi):
        return _dot(hn, w_in_ref[:, lo:hi])

    def sub_iota(n):
        return lax.broadcasted_iota(jnp.int32, (SUBLANES, n), 0)

    def conv_taps(halo, rs, cw_ref, cb_ref):
        ext = jnp.concatenate([halo, rs], axis=0)
        n = rs.shape[0]
        acc = cb_ref[...] + cw_ref[3:4, :] * rs
        for j in range(1, CONV_WIDTH):
            acc = acc + cw_ref[3 - j:4 - j, :] * ext[HALO - SUBLANES * j:HALO - SUBLANES * j + n, :]
        return acc

    def conv(raw, prev_ref, c0_ref, cw_ref, cb_ref, o_ref):
        cdim = raw.shape[1]
        if not prompt:
            outs = []
            for s in range(T // Q):
                rs = raw[Q * s:Q * (s + 1), :]
                o_ref[s] = rs[Q - HALO:, :].reshape(CONV_WIDTH - 1, SUBLANES, cdim)
                outs.append(conv_taps(c0_ref[s].reshape(HALO, cdim), rs, cw_ref, cb_ref))
            return jnp.concatenate(outs, axis=0)
        first = sub_iota(cdim) == 0
        tail = prev_ref[...]
        outs = []
        for r0 in range(0, T, CHUNK):
            rs = raw[r0:r0 + CHUNK, :]
            cur = rs[CHUNK - HALO:, :]
            halo = jnp.concatenate(
                [jnp.where(first, pltpu.roll(tail[SUBLANES * k:SUBLANES * (k + 1), :], 1, 0),
                           pltpu.roll(cur[SUBLANES * k:SUBLANES * (k + 1), :], 1, 0))
                 for k in range(CONV_WIDTH - 1)], axis=0)
            outs.append(conv_taps(halo, rs, cw_ref, cb_ref))
            tail = cur
        prev_ref[...] = tail
        for k in range(CONV_WIDTH - 1):
            o_ref[k:k + 1, :] = tail[SUBLANES * k + SUBLANES - 1:SUBLANES * (k + 1), :]
        return jnp.concatenate(outs, axis=0)

    row = lax.broadcasted_iota(jnp.int32, (Q, Q), 0)
    col = lax.broadcasted_iota(jnp.int32, (Q, Q), 1)
    if prompt:
        def local_time(i):
            return jnp.bitwise_or(jnp.left_shift(jnp.bitwise_and(i, SUBLANES - 1), SEG_SHIFT),
                                  jnp.right_shift(i, SUB_SHIFT))
        causal = local_time(row) >= local_time(col)
    else:
        same_seq = jnp.bitwise_and(row, SUBLANES - 1) == jnp.bitwise_and(col, SUBLANES - 1)
        causal = jnp.logical_and(same_seq, jnp.right_shift(row, SUB_SHIFT) >= jnp.right_shift(col, SUB_SHIFT))
    tril = jnp.where(causal, 1.0, 0.0)
    lane_lo = lax.broadcasted_iota(jnp.int32, (Q, LANES), 1) < SSD_HEADDIM
    gsz = SSD_HPG * SSD_HEADDIM

    xbc = _silu(conv(proj(C_XBC, C_LRU), prev_ssd if prompt else None, None if prompt else c0_ssd_ref,
                     cw_ssd_ref, cb_ssd_ref, o_cssd_ref))
    dt_all = jax.nn.softplus(proj(C_DT, IN_COLS) + dtb_ref[...])
    a_neg = -jnp.exp(alog_ref[...])

    def ssd_state_io(rows_c, rows_xw, rows_b, e_last, h_get, h_set):
        outs = []
        for g in range(SSD_GROUPS):
            hp = h_get(g)
            outs.append(_dot_nt(rows_c[:, LANES * g:LANES * (g + 1)].astype(bf16), hp.astype(bf16)))
            sg = _dot_tn(rows_xw[:, gsz * g:gsz * (g + 1)].astype(bf16),
                         rows_b[:, LANES * g:LANES * (g + 1)].astype(bf16))
            dec = jnp.concatenate(
                [jnp.broadcast_to(e_last[:, SSD_HPG * g + k:SSD_HPG * g + k + 1], (SSD_HEADDIM, SSD_STATE))
                 for k in range(SSD_HPG)], axis=0)
            h_set(g, dec * hp + sg)
        return jnp.concatenate(outs, axis=1)

    def ssd_chunk(r0):
        xs = xbc[r0:r0 + Q, :SSD_DIM]
        bm = xbc[r0:r0 + Q, SSD_DIM:SSD_DIM + SSD_BC]
        cm = xbc[r0:r0 + Q, SSD_DIM + SSD_BC:]
        dt = dt_all[r0:r0 + Q, :]
        bm_b = bm.astype(bf16)
        cm_b = cm.astype(bf16)
        acum = _dot_exact(tril, dt * a_neg)
        acum_row = acum.T
        dt_row = dt.T
        scores = [_dot_nt(cm_b[:, LANES * g:LANES * (g + 1)], bm_b[:, LANES * g:LANES * (g + 1)])
                  for g in range(SSD_GROUPS)]
        if prompt:
            arow = acum_row[0:SSD_HEADS, :]
            w_row = jnp.exp(arow[:, Q - 1:Q] - arow) * dt_row[0:SSD_HEADS, :]
            e_end = jnp.exp(acum[Q - 1:Q, :])
            lane1 = lane_lo[0:1, :]
            bts = [bm[:, LANES * g:LANES * (g + 1)].T for g in range(SSD_GROUPS)]
            y_pairs = []
            for j in range(SSD_HEADS // 2):
                g = (2 * j) // SSD_HPG
                cm_g = cm[:, LANES * g:LANES * (g + 1)]
                bt_g = bts[g]
                lhs_y, lhs_s = [], []
                for h in (2 * j, 2 * j + 1):
                    colb = jnp.broadcast_to(acum[:, h:h + 1], (Q, LANES))
                    decay = jnp.exp(jnp.where(causal, colb - acum_row[h:h + 1, :], NEG))
                    lhs_y.append((scores[g] * decay * dt_row[h:h + 1, :]).astype(bf16))
                    lhs_s.append((bt_g * w_row[h:h + 1, :]).astype(bf16))
                for h in (2 * j, 2 * j + 1):
                    colb = jnp.broadcast_to(acum[:, h:h + 1], (Q, LANES))
                    lhs_y.append((jnp.exp(colb) * cm_g).astype(bf16))
                xp = xs[:, LANES * j:LANES * (j + 1)]
                hp = h_ssd[:, LANES * j:LANES * (j + 1)]
                xbd = jnp.concatenate([jnp.where(lane_lo, xp, 0.0), jnp.where(lane_lo, 0.0, xp)],
                                      axis=0).astype(bf16)
                hbd = jnp.concatenate([jnp.where(lane_lo, hp, 0.0), jnp.where(lane_lo, 0.0, hp)],
                                      axis=0).astype(bf16)
                y_pairs.append(_dot(jnp.concatenate(lhs_y, axis=1), jnp.concatenate([xbd, hbd], axis=0)))
                dec = jnp.where(lane1, jnp.broadcast_to(e_end[:, 2 * j:2 * j + 1], (1, LANES)),
                                jnp.broadcast_to(e_end[:, 2 * j + 1:2 * j + 2], (1, LANES)))
                h_ssd[:, LANES * j:LANES * (j + 1)] = dec * hp + _dot(jnp.concatenate(lhs_s, axis=1), xbd)
            return jnp.concatenate(y_pairs, axis=1) + dfull_ref[...] * xs

        eac = jnp.exp(acum)
        sel = jnp.where(col == jnp.bitwise_and(row, SUBLANES - 1) + (Q - SUBLANES), 1.0, 0.0)
        acum_end = _dot_exact(sel, acum)
        wgt = jnp.exp(acum_end - acum) * dt
        y_pairs, xw_pairs, ecol_pairs = [], [], []
        for j in range(SSD_HEADS // 2):
            g = (2 * j) // SSD_HPG
            ms = []
            for h in (2 * j, 2 * j + 1):
                diff = acum[:, h:h + 1] - acum_row[h:h + 1, :]
                decay = jnp.exp(jnp.where(causal, diff, NEG))
                ms.append((scores[g] * decay * dt_row[h:h + 1, :]).astype(bf16))
            xp = xs[:, LANES * j:LANES * (j + 1)]
            xbd = jnp.concatenate([jnp.where(lane_lo, xp, 0.0), jnp.where(lane_lo, 0.0, xp)], axis=0).astype(bf16)
            y_pairs.append(_dot(jnp.concatenate(ms, axis=1), xbd))
            xw_pairs.append(xp * _pair_expand(wgt, j, lane_lo))
            ecol_pairs.append(_pair_expand(eac, j, lane_lo))
        y_diag = jnp.concatenate(y_pairs, axis=1)
        xw = jnp.concatenate(xw_pairs, axis=1)
        ecol = jnp.concatenate(ecol_pairs, axis=1)

        c_s[r0:r0 + Q, :] = _dot(perm_b, cm_b)
        bm_s[r0:r0 + Q, :] = _dot(perm_b, bm_b)
        xw_s[r0:r0 + Q, :] = _dot(perm_b, xw.astype(bf16))
        eac_s[r0:r0 + Q, :] = eac
        return y_diag + dfull_ref[...] * xs, ecol

    if prompt:
        y = jnp.concatenate([ssd_chunk(r0) for r0 in range(0, T, Q)], axis=0)
    else:
        to_seq = jnp.bitwise_or(jnp.left_shift(jnp.bitwise_and(row, SUBLANES - 1), SUB_SHIFT),
                                jnp.right_shift(row, SUB_SHIFT)) == col
        perm_b = jnp.where(to_seq, 1.0, 0.0).astype(bf16)
        parts = [ssd_chunk(r0) for r0 in range(0, T, Q)]

        ngrp = nseq // STATE_SEQS

        def seq_step(i, carry):
            slot = jnp.bitwise_and(i, STATE_BUFS - 1)
            ahead = i + (STATE_BUFS - 1)

            @pl.when(ahead < ngrp)
            def _():
                in_copy(ahead, jnp.bitwise_and(ahead, STATE_BUFS - 1)).start()

            in_copy(i, slot).wait()

            @pl.when(i >= STATE_BUFS)
            def _():
                out_copy(i - STATE_BUFS, slot).wait()

            for q in range(STATE_SEQS):
                sq = i * STATE_SEQS + q
                s0 = pl.multiple_of(sq * SEQ_S, SEQ_S)

                def h_get(g):
                    return h_in[slot, q, pl.ds(gsz * g, gsz), :]

                def h_set(g, v):
                    h_out[slot, q, pl.ds(gsz * g, gsz), :] = v

                e_row = jnp.left_shift(jnp.right_shift(sq, SUB_SHIFT), SUBTILE_SHIFT) + (Q - SUBLANES) \
                    + jnp.bitwise_and(sq, SUBLANES - 1)
                yoff_s[pl.ds(s0, SEQ_S), :] = ssd_state_io(
                    c_s[pl.ds(s0, SEQ_S), :], xw_s[pl.ds(s0, SEQ_S), :], bm_s[pl.ds(s0, SEQ_S), :],
                    eac_s[pl.ds(e_row, 1), :], h_get, h_set)
            out_copy(i, slot).start()
            return carry

        lax.fori_loop(0, ngrp, seq_step, 0)
        for j in range(STATE_BUFS):
            out_copy(ngrp - STATE_BUFS + j, j).wait()
        perm_f = jnp.where(to_seq, 1.0, 0.0)
        y = jnp.concatenate(
            [part + _dot_exact(perm_f, yoff_s[r0:r0 + Q, :]) * ecol
             for r0, (part, ecol) in zip(range(0, T, Q), parts)], axis=0)
    y_ssd = _rms(y * _silu(proj(C_Z, C_XBC)), sng_ref[...])

    xr = conv(proj(C_LRU, C_LRU_G), prev_lru if prompt else None, None if prompt else c0_lru_ref,
              cw_lru_ref, cb_lru_ref, o_clru_ref)
    xr_b = xr.astype(bf16)
    hl = LRU_DIM // 2
    gates = jnp.concatenate([_dot(xr_b[:, hl * (k % 2):hl * (k % 2 + 1)], lru_w_ref[k]) for k in range(4)],
                            axis=1) + lru_b_ref[...]
    r_gate = jax.nn.sigmoid(gates[:, :LRU_DIM])
    i_gate = jax.nn.sigmoid(gates[:, LRU_DIM:])
    log_a = -LRU_C * r_gate * jax.nn.softplus(-lam_ref[...])
    a_t = jnp.exp(log_a)
    gain = jnp.sqrt(jnp.maximum(-jnp.tanh(log_a) * (a_t * a_t + 1.0), 0.0))
    a_s[...] = a_t
    b_s[...] = gain * i_gate * xr

    def vrow(ref, r0, k):
        return ref[r0 + SUBLANES * k:r0 + SUBLANES * (k + 1), :]

    def set_vrow(ref, r0, k, v):
        ref[r0 + SUBLANES * k:r0 + SUBLANES * (k + 1), :] = v

    if prompt:
        sub = sub_iota(LRU_DIM)
        carry = lru_c[...]
        for r0 in range(0, T, CHUNK):
            acc_a, acc_h = vrow(a_s, r0, 0), vrow(b_s, r0, 0)
            for k in range(1, SEG):
                a_k = vrow(a_s, r0, k)
                acc_h = a_k * acc_h + vrow(b_s, r0, k)
                acc_a = a_k * acc_a
                set_vrow(a_s, r0, k, acc_a)
                set_vrow(b_s, r0, k, acc_h)
            alpha = jnp.where(sub == 0, 0.0, pltpu.roll(acc_a, 1, 0))
            beta = jnp.where(sub == 0, jnp.broadcast_to(carry, (SUBLANES, LRU_DIM)), pltpu.roll(acc_h, 1, 0))
            for d in (1, 2, 4):
                a_sh = jnp.where(sub >= d, pltpu.roll(alpha, d, 0), 1.0)
                b_sh = jnp.where(sub >= d, pltpu.roll(beta, d, 0), 0.0)
                beta = alpha * b_sh + beta
                alpha = alpha * a_sh
            carry = (acc_a * beta + acc_h)[SUBLANES - 1:SUBLANES, :]
            for k in range(SEG):
                set_vrow(b_s, r0, k, vrow(b_s, r0, k) + vrow(a_s, r0, k) * beta)
        lru_c[...] = carry
        o_lru_ref[...] = carry
    else:
        for s in range(T // Q):
            h = h0_lru_ref[SUBLANES * s:SUBLANES * (s + 1), :]
            for k in range(Q // SUBLANES):
                h = vrow(a_s, Q * s, k) * h + vrow(b_s, Q * s, k)
                set_vrow(b_s, Q * s, k, h)
            o_lru_ref[SUBLANES * s:SUBLANES * (s + 1), :] = h
    y_lru = b_s[...] * _silu(proj(C_LRU_G, C_S5))

    u = proj(C_S5, C_S5_G)
    u_b = u.astype(bf16)
    half = S5_DIM // 2
    for k in range(2):
        uk = u_b[:, half * k:half * (k + 1)]
        bur_s[:, S5_HALF * k:S5_HALF * (k + 1)] = _dot(uk, bb_ref[k])
        bui_s[:, S5_HALF * k:S5_HALF * (k + 1)] = _dot(uk, bb_ref[2 + k])

    def tab(r0):
        return tab_ref[0, r0:r0 + SUBLANES, :], tab_ref[1, r0:r0 + SUBLANES, :]

    def cmul_add(pr, pi, xr, xi, yr, yi):
        return pr * xr - pi * xi + yr, pr * xi + pi * xr + yi

    ar, ai = tab(TAB_A)
    if prompt:
        sub = sub_iota(S5_FLAT)
        c_r, c_i = s5_cr[...], s5_ci[...]
        for r0 in range(0, T, CHUNK):
            hr, hi = vrow(bur_s, r0, 0), vrow(bui_s, r0, 0)
            for k in range(1, SEG):
                hr, hi = cmul_add(ar, ai, hr, hi, vrow(bur_s, r0, k), vrow(bui_s, r0, k))
                set_vrow(bur_s, r0, k, hr)
                set_vrow(bui_s, r0, k, hi)
            er = jnp.where(sub == 0, jnp.broadcast_to(c_r, (SUBLANES, S5_FLAT)), pltpu.roll(hr, 1, 0))
            ei = jnp.where(sub == 0, jnp.broadcast_to(c_i, (SUBLANES, S5_FLAT)), pltpu.roll(hi, 1, 0))
            for t, d in enumerate((1, 2, 4)):
                qr, qi = tab(TAB_Q + t * SUBLANES)
                er, ei = cmul_add(qr, qi, pltpu.roll(er, d, 0), pltpu.roll(ei, d, 0), er, ei)
            sr, si = tab(TAB_ASEG)
            nr, ni = cmul_add(sr, si, er, ei, hr, hi)
            c_r, c_i = nr[SUBLANES - 1:SUBLANES, :], ni[SUBLANES - 1:SUBLANES, :]
            for k in range(SEG):
                pr, pi = tab(TAB_PW + k * SUBLANES)
                vr, vi = cmul_add(pr, pi, er, ei, vrow(bur_s, r0, k), vrow(bui_s, r0, k))
                set_vrow(bur_s, r0, k, vr)
                set_vrow(bui_s, r0, k, vi)
        s5_cr[...] = c_r
        s5_ci[...] = c_i
        o_s5r_ref[...] = c_r
        o_s5i_ref[...] = c_i
    else:
        for s in range(T // Q):
            rows = slice(SUBLANES * s, SUBLANES * (s + 1))
            hr, hi = h0_s5r_ref[rows, :], h0_s5i_ref[rows, :]
            for k in range(Q // SUBLANES):
                hr, hi = cmul_add(ar, ai, hr, hi, vrow(bur_s, Q * s, k), vrow(bui_s, Q * s, k))
                set_vrow(bur_s, Q * s, k, hr)
                set_vrow(bui_s, Q * s, k, hi)
            o_s5r_ref[rows, :] = hr
            o_s5i_ref[rows, :] = hi
    ys = []
    for k in range(2):
        hk = jnp.concatenate([bur_s[:, S5_HALF * k:S5_HALF * (k + 1)].astype(bf16),
                              bui_s[:, S5_HALF * k:S5_HALF * (k + 1)].astype(bf16)], axis=1)
        ys.append(_dot(hk, cc_ref[k]))
    ys5 = jnp.concatenate(ys, axis=1) + s5d_ref[...] * u
    ys5 = jax.nn.gelu(ys5)
    ys5 = ys5 * jax.nn.sigmoid(_dot(ys5.astype(bf16), glu_w_ref[...]) + glu_b_ref[...])
    y_s5 = ys5 * _silu(proj(C_S5_G, C_DT))

    ycat = jnp.concatenate([y_ssd.astype(bf16), y_lru.astype(bf16), y_s5.astype(bf16)], axis=1)
    return x + _dot(ycat, w_out_ref[...])


def _prompt_body(final, *refs):
    x_ref = refs[0]
    w = refs[1:1 + N_WEIGHTS]
    y_ref = refs[1 + N_WEIGHTS]
    o = refs[2 + N_WEIGHTS:8 + N_WEIGHTS]
    scr = refs[8 + N_WEIGHTS:]
    h_ssd, prev_ssd, prev_lru, _, _, _, _, lru_c, s5_cr, s5_ci = scr

    @pl.when(pl.program_id(1) == 0)
    def _():
        h_ssd[...] = jnp.zeros_like(h_ssd)
        prev_ssd[...] = jnp.zeros_like(prev_ssd)
        prev_lru[...] = jnp.zeros_like(prev_lru)
        lru_c[...] = jnp.zeros_like(lru_c)
        s5_cr[...] = jnp.zeros_like(s5_cr)
        s5_ci[...] = jnp.zeros_like(s5_ci)

    out = _layer_math(True, TILE_P, x_ref[...], w, None, o, scr)
    if final:
        out = _rms(out, w[-1][...])
    y_ref[...] = out

    @pl.when(pl.program_id(1) == pl.num_programs(1) - 1)
    def _():
        o[0][...] = h_ssd[...].T


def _sample_body(*refs):
    x_ref = refs[0]
    st = refs[1:7]
    w = refs[7:7 + N_WEIGHTS]
    y_ref = refs[7 + N_WEIGHTS]
    o = refs[8 + N_WEIGHTS:14 + N_WEIGHTS]
    x_all = refs[14 + N_WEIGHTS]
    scr = refs[15 + N_WEIGHTS:]
    layer = pl.program_id(0)
    last_layer = layer == pl.num_programs(0) - 1
    r0 = pl.multiple_of(pl.program_id(1) * TILE_S, TILE_S)

    @pl.when(layer == 0)
    def _():
        x_all[pl.ds(r0, TILE_S), :] = x_ref[...]

    out = _layer_math(False, TILE_S, x_all[pl.ds(r0, TILE_S), :], w, st, o, scr)
    x_all[pl.ds(r0, TILE_S), :] = out

    @pl.when(last_layer)
    def _():
        y_ref[...] = _rms(out, w[-1][...])

    @pl.when(jnp.logical_not(last_layer))
    def _():
        y_ref[...] = out


def _prompt_call(layer, final, x, weights):
    T = TILE_P
    nb, seq, _ = x.shape

    def wspec(a):
        nd = a.ndim - 1
        return pl.BlockSpec((None,) + a.shape[1:], lambda b, c: (layer,) + (0,) * nd, pipeline_mode=pl.Buffered(1))

    def st(shape):
        nd = len(shape)
        return pl.BlockSpec((None,) + shape, lambda b, c: (b,) + (0,) * nd)

    x_spec = pl.BlockSpec((None, T, D_MODEL), lambda b, c: (b, c, 0))
    out_specs = [x_spec, st((SSD_DIM, SSD_STATE)), st((CONV_WIDTH - 1, SSD_CONV_DIM)), st((1, LRU_DIM)),
                 st((CONV_WIDTH - 1, LRU_DIM)), st((1, S5_FLAT)), st((1, S5_FLAT))]
    out_shape = [jax.ShapeDtypeStruct(x.shape, f32),
                 jax.ShapeDtypeStruct((nb, SSD_DIM, SSD_STATE), f32),
                 jax.ShapeDtypeStruct((nb, CONV_WIDTH - 1, SSD_CONV_DIM), f32),
                 jax.ShapeDtypeStruct((nb, 1, LRU_DIM), f32),
                 jax.ShapeDtypeStruct((nb, CONV_WIDTH - 1, LRU_DIM), f32),
                 jax.ShapeDtypeStruct((nb, 1, S5_FLAT), f32),
                 jax.ShapeDtypeStruct((nb, 1, S5_FLAT), f32)]
    scratch = [pltpu.VMEM((SSD_STATE, SSD_DIM), f32),
               pltpu.VMEM((HALO, SSD_CONV_DIM), f32),
               pltpu.VMEM((HALO, LRU_DIM), f32),
               pltpu.VMEM((T, LRU_DIM), f32), pltpu.VMEM((T, LRU_DIM), f32),
               pltpu.VMEM((T, S5_FLAT), f32), pltpu.VMEM((T, S5_FLAT), f32),
               pltpu.VMEM((1, LRU_DIM), f32),
               pltpu.VMEM((1, S5_FLAT), f32), pltpu.VMEM((1, S5_FLAT), f32)]
    return pl.pallas_call(
        functools.partial(_prompt_body, final),
        grid=(nb, seq // T), in_specs=[x_spec] + [wspec(a) for a in weights],
        out_specs=out_specs, out_shape=out_shape, scratch_shapes=scratch,
        compiler_params=pltpu.CompilerParams(dimension_semantics=("arbitrary", "arbitrary"),
                                             vmem_limit_bytes=VMEM_LIMIT_BYTES),
        name="layer_prompt",
    )(x, *weights)


def _sample_call(x, states, weights):
    T = TILE_S
    rows = x.shape[0]
    depth = weights[0].shape[0]
    nseq = T // SEQ_S

    def wspec(a):
        nd = a.ndim - 1
        return pl.BlockSpec((None,) + a.shape[1:], lambda l, i: (l,) + (0,) * nd, pipeline_mode=pl.Buffered(1))

    def st(a):
        nd = a.ndim - 2
        if a.ndim == 4:
            return pl.BlockSpec(memory_space=pl.ANY)
        if a.ndim == 6:
            return pl.BlockSpec((None, None) + a.shape[2:], lambda l, i: (l, i) + (0,) * nd)
        return pl.BlockSpec((None, nseq) + a.shape[2:], lambda l, i: (l, i) + (0,) * nd)

    x_spec = pl.BlockSpec((T, D_MODEL), lambda l, i: (i, 0))
    st_specs = [st(a) for a in states]
    scratch = [pltpu.VMEM((rows, D_MODEL), f32),
               pltpu.VMEM((T, LRU_DIM), f32), pltpu.VMEM((T, LRU_DIM), f32),
               pltpu.VMEM((T, S5_FLAT), f32), pltpu.VMEM((T, S5_FLAT), f32),
               pltpu.VMEM((T, SSD_BC), f32), pltpu.VMEM((T, SSD_BC), f32),
               pltpu.VMEM((T, SSD_DIM), f32), pltpu.VMEM((T, SSD_DIM), f32),
               pltpu.VMEM((T, LANES), f32),
               pltpu.VMEM((STATE_BUFS, STATE_SEQS, SSD_DIM, SSD_STATE), f32),
               pltpu.VMEM((STATE_BUFS, STATE_SEQS, SSD_DIM, SSD_STATE), f32),
               pltpu.SemaphoreType.DMA((STATE_BUFS,)), pltpu.SemaphoreType.DMA((STATE_BUFS,))]
    return pl.pallas_call(
        _sample_body,
        grid=(depth, rows // T), in_specs=[x_spec] + st_specs + [wspec(a) for a in weights],
        out_specs=[pl.BlockSpec((None, T, D_MODEL), lambda l, i: (l, i, 0))] + st_specs,
        out_shape=[jax.ShapeDtypeStruct((depth,) + x.shape, f32)] + [jax.ShapeDtypeStruct(a.shape, f32) for a in states],
        scratch_shapes=scratch,
        compiler_params=pltpu.CompilerParams(dimension_semantics=("arbitrary", "arbitrary"),
                                             vmem_limit_bytes=VMEM_LIMIT_BYTES),
        name="layers_sample",
    )(x, *states, *weights)


def _block_diag(blocks):
    *lead, n, r, c = blocks.shape
    eye = jnp.eye(n, dtype=blocks.dtype)
    return (blocks[..., :, :, None, :] * eye[:, None, :, None]).reshape(*lead, n * r, n * c)


def _pad_lanes(v):
    return jnp.pad(v, [(0, 0)] * (v.ndim - 1) + [(0, LANES - v.shape[-1])])


def kernel(x_prompt, x_sample, state_ssd, state_ssd_conv, state_lru, state_lru_conv, state_s5_re, state_s5_im, norm_g, w_in, ssd_conv_w, ssd_conv_b, ssd_dt_bias, ssd_a_log, ssd_d, ssd_norm_g, lru_conv_w, lru_conv_b, lru_wa, lru_ba, lru_wx, lru_bx, lru_lambda, s5_lambda_re, s5_lambda_im, s5_log_dt, s5_b_re, s5_b_im, s5_c_re, s5_c_im, s5_d, s5_glu_w, s5_glu_b, w_out, final_norm_g):
    depth = w_in.shape[0]
    nbp = x_prompt.shape[0]
    nbs, ls, _ = x_sample.shape
    assert ls == SEQ_S and x_prompt.shape[1] % TILE_P == 0 and (nbs * ls) % TILE_S == 0

    tab, bbar_re, bbar_im = _s5_prep(s5_lambda_re.astype(f32), s5_lambda_im.astype(f32), s5_log_dt.astype(f32),
                                     s5_b_re.astype(f32), s5_b_im.astype(f32))

    def row(v, n):
        return v.astype(f32).reshape(depth, 1, n)

    wi = w_in.astype(bf16)
    w_in_r = jnp.concatenate([wi[..., 0:3072], wi[..., 3088:5136], _pad_lanes(wi[..., 3072:3088])], axis=-1)

    def halves(v):
        return _block_diag(v.reshape(depth, 2, S5_NGROUPS // 2, S5_GROUP, S5_STATE))

    bb = jnp.concatenate([halves(bbar_re), halves(bbar_im)], axis=1).astype(bf16)

    def chalves(v):
        return _block_diag(jnp.transpose(v.astype(f32), (0, 1, 3, 2)).reshape(depth, 2, S5_NGROUPS // 2, S5_STATE, S5_GROUP))

    cc = jnp.concatenate([chalves(s5_c_re), -chalves(s5_c_im)], axis=2).astype(bf16)
    weights = (
        row(norm_g, D_MODEL), w_in_r,
        ssd_conv_w.astype(f32), row(ssd_conv_b, SSD_CONV_DIM),
        _pad_lanes(row(ssd_dt_bias, SSD_HEADS)), _pad_lanes(row(ssd_a_log, SSD_HEADS)),
        jnp.repeat(ssd_d.astype(f32), SSD_HEADDIM, axis=-1).reshape(depth, 1, SSD_DIM),
        row(ssd_norm_g, SSD_DIM),
        lru_conv_w.astype(f32), row(lru_conv_b, LRU_DIM),
        jnp.concatenate([_block_diag(v.astype(f32).reshape(depth, 2, v.shape[1] // 2, *v.shape[2:]))
                         for v in (lru_wa, lru_wx)], axis=1).astype(bf16),
        jnp.concatenate([lru_ba, lru_bx], axis=-1).astype(f32).reshape(depth, 1, 2 * LRU_DIM),
        row(lru_lambda, LRU_DIM),
        tab, bb, cc, row(s5_d, S5_DIM),
        s5_glu_w.astype(bf16), row(s5_glu_b, S5_DIM),
        w_out.astype(bf16),
        jnp.broadcast_to(final_norm_g.astype(f32).reshape(1, 1, D_MODEL), (depth, 1, D_MODEL)),
    )

    seq_p = x_prompt.shape[1]
    xp = jnp.swapaxes(x_prompt.astype(f32).reshape(nbp, seq_p // CHUNK, SUBLANES, SEG, D_MODEL), 2, 3)
    xp = xp.reshape(nbp, seq_p, D_MODEL)
    outs_p = [[] for _ in range(6)]
    for i in range(depth):
        res = _prompt_call(i, i == depth - 1, xp, weights)
        xp = res[0]
        for j in range(6):
            outs_p[j].append(res[1 + j])
    y_prompt = jnp.swapaxes(xp.reshape(nbp, seq_p // CHUNK, SEG, SUBLANES, D_MODEL), 2, 3).reshape(nbp, seq_p, D_MODEL)

    ntile = nbs // SUBLANES
    nsub = TILE_S // SUBTILE_S

    def conv_in(v):
        v = jnp.swapaxes(v.astype(f32).reshape(depth, ntile, SUBLANES, CONV_WIDTH - 1, v.shape[-1]), 2, 3)
        return v.reshape(depth, ntile // nsub, nsub, CONV_WIDTH - 1, SUBLANES, v.shape[-1])

    def conv_out(v, dtype):
        v = v.reshape(depth, ntile, CONV_WIDTH - 1, SUBLANES, v.shape[-1])
        return jnp.swapaxes(v, 2, 3).reshape(depth, nbs, CONV_WIDTH - 1, v.shape[-1]).astype(dtype)

    xs = jnp.swapaxes(x_sample.astype(f32).reshape(ntile, SUBLANES, ls, D_MODEL), 1, 2).reshape(nbs * ls, D_MODEL)
    states_s = (state_ssd.astype(f32).reshape(depth, nbs, SSD_DIM, SSD_STATE), conv_in(state_ssd_conv),
                state_lru.astype(f32), conv_in(state_lru_conv),
                state_s5_re.astype(f32).reshape(depth, nbs, S5_FLAT), state_s5_im.astype(f32).reshape(depth, nbs, S5_FLAT))
    weights_s = tuple(w[:, :, TAB_A:TAB_A + SUBLANES] if w is tab else w for w in weights)
    res_s = _sample_call(xs, states_s, weights_s)
    y_sample = jnp.swapaxes(res_s[0][depth - 1].reshape(ntile, ls, SUBLANES, D_MODEL), 1, 2).reshape(nbs, ls, D_MODEL)

    def stack(lst, shape, dtype):
        return jnp.stack(lst).reshape((depth,) + shape).astype(dtype)

    ssd_shape = (SSD_HEADS, SSD_HEADDIM, SSD_STATE)
    s5_shape = (S5_NGROUPS, S5_STATE)
    return (
        y_prompt.astype(x_prompt.dtype), y_sample.astype(x_sample.dtype),
        stack(outs_p[0], (nbp,) + ssd_shape, state_ssd.dtype),
        res_s[1].reshape((depth, nbs) + ssd_shape).astype(state_ssd.dtype),
        stack(outs_p[1], (nbp, CONV_WIDTH - 1, SSD_CONV_DIM), state_ssd_conv.dtype),
        conv_out(res_s[2], state_ssd_conv.dtype),
        stack(outs_p[2], (nbp, LRU_DIM), state_lru.dtype), res_s[3].astype(state_lru.dtype),
        stack(outs_p[3], (nbp, CONV_WIDTH - 1, LRU_DIM), state_lru_conv.dtype),
        conv_out(res_s[4], state_lru_conv.dtype),
        stack(outs_p[4], (nbp,) + s5_shape, state_s5_re.dtype),
        res_s[5].reshape((depth, nbs) + s5_shape).astype(state_s5_re.dtype),
        stack(outs_p[5], (nbp,) + s5_shape, state_s5_im.dtype),
        res_s[6].reshape((depth, nbs) + s5_shape).astype(state_s5_im.dtype),
    )
```

```python
import functools

import jax
import jax.numpy as jnp
from jax import lax
from jax.experimental import pallas as pl
from jax.experimental.pallas import tpu as pltpu

f32 = jnp.float32
bf16 = jnp.bfloat16

D_MODEL = 1024
CONV_WIDTH = 4
SSD_DIM = 1024
SSD_HEADDIM = 64
SSD_HEADS = 16
SSD_GROUPS = 4
SSD_HPG = 4
SSD_STATE = 128
SSD_BC = SSD_GROUPS * SSD_STATE
SSD_CONV_DIM = SSD_DIM + 2 * SSD_BC
LRU_DIM = 512
LRU_C = 8.0
S5_DIM = 512
S5_GROUP = 16
S5_NGROUPS = 32
S5_STATE = 64
S5_FLAT = S5_NGROUPS * S5_STATE
S5_HALF = S5_FLAT // 2
EPS = 1e-6

LANES = 128
SUBLANES = 8
CHUNK = 128
TILE_P = 256
TILES_PER_STEP = 2
TILE_S = 256
SUBTILE_S = 64
SUBTILE_SHIFT = 6
SEQ_S = 8
STATE_BUFS = 4
STATE_SEQS = 2
NEG = -1e30

SEG = CHUNK // SUBLANES
SEG_SHIFT = 4
SUB_SHIFT = 3
HALO = (CONV_WIDTH - 1) * SUBLANES

TAB_A = 0
TAB_Q = TAB_A + SUBLANES
TAB_ASEG = TAB_Q + 3 * SUBLANES
TAB_PW = TAB_ASEG + SUBLANES
TAB_ROWS = TAB_PW + SEG * SUBLANES

C_Z = 0
C_XBC = C_Z + SSD_DIM
C_LRU = C_XBC + SSD_CONV_DIM
C_LRU_G = C_LRU + LRU_DIM
C_S5 = C_LRU_G + LRU_DIM
C_S5_G = C_S5 + S5_DIM
C_DT = C_S5_G + S5_DIM
IN_COLS = C_DT + LANES

VMEM_LIMIT_BYTES = 56 * 1024 * 1024

N_WEIGHTS = 21


def _rms(x, g):
    return x * lax.rsqrt(jnp.mean(x * x, axis=-1, keepdims=True) + EPS) * g


def _silu(x):
    return x * jax.nn.sigmoid(x)


def _dot(a, b):
    return jnp.dot(a, b, preferred_element_type=f32)


def _dot_nt(a, b):
    return lax.dot_general(a, b, (((1,), (1,)), ((), ())), preferred_element_type=f32)


def _dot_tn(a, b):
    return lax.dot_general(a, b, (((0,), (0,)), ((), ())), preferred_element_type=f32)


def _dot_exact(a, b):
    return jnp.dot(a, b, preferred_element_type=f32, precision=lax.Precision.HIGHEST)


def _pair_expand(v, j, lane_lo):
    q = v.shape[0]
    lo = jnp.broadcast_to(v[:, 2 * j:2 * j + 1], (q, LANES))
    hi = jnp.broadcast_to(v[:, 2 * j + 1:2 * j + 2], (q, LANES))
    return jnp.where(lane_lo, lo, hi)


def _s5_prep_body(lre_ref, lim_ref, ldt_ref, lre_rep_ref, lim_rep_ref, ldt_rep_ref,
                  bre_ref, bim_ref, tre_ref, tim_ref, bbre_ref, bbim_ref):
    def abar(lre, lim, ldt):
        delta = jnp.exp(ldt)
        mag = jnp.exp(lre * delta)
        return mag * jnp.cos(lim * delta), mag * jnp.sin(lim * delta)

    ar, ai = abar(lre_ref[...], lim_ref[...], ldt_ref[...])

    def cmul(xr, xi, yr, yi):
        return xr * yr - xi * yi, xr * yi + xi * yr

    pw = [(ar, ai)]
    for _ in range(SEG - 1):
        pw.append(cmul(*pw[-1], ar, ai))
    seg = [pw[SEG - 1]]
    for _ in range(2):
        seg.append(cmul(*seg[-1], *seg[-1]))
    zero = jnp.zeros_like(ar)

    def put(i, v):
        tre_ref[i] = v[0]
        tim_ref[i] = v[1]

    for r in range(SUBLANES):
        put(TAB_A + r, pw[0])
        put(TAB_ASEG + r, seg[0])
        for t, d in enumerate((1, 2, 4)):
            put(TAB_Q + t * SUBLANES + r, seg[t] if r >= d else (zero, zero))
        for k in range(SEG):
            put(TAB_PW + k * SUBLANES + r, pw[k])

    lre, lim = lre_rep_ref[...], lim_rep_ref[...]
    ar, ai = abar(lre, lim, ldt_rep_ref[...])
    denom = lre * lre + lim * lim
    nr = ar - 1.0
    ni = ai
    coef_re = (nr * lre + ni * lim) / denom
    coef_im = (ni * lre - nr * lim) / denom
    bre, bim = bre_ref[...], bim_ref[...]
    bbre_ref[...] = coef_re * bre - coef_im * bim
    bbim_ref[...] = coef_re * bim + coef_im * bre


def _s5_prep(lam_re, lam_im, log_dt, b_re, b_im):
    depth = lam_re.shape[0]
    rows_c = S5_FLAT // LANES
    rows_r = S5_DIM * S5_STATE // LANES
    ldt = jnp.broadcast_to(log_dt[:, :, None], (depth, S5_NGROUPS, S5_STATE))

    def rep(v):
        return jnp.broadcast_to(v[:, :, None, :], (depth, S5_NGROUPS, S5_GROUP, S5_STATE)).reshape(depth, rows_r, LANES)

    def bt(v):
        return jnp.transpose(v, (0, 1, 3, 2)).reshape(depth, rows_r, LANES)

    cspec = pl.BlockSpec((None, rows_c, LANES), lambda i: (i, 0, 0))
    rspec = pl.BlockSpec((None, rows_r, LANES), lambda i: (i, 0, 0))
    tspec = pl.BlockSpec((None, TAB_ROWS, rows_c, LANES), lambda i: (i, 0, 0, 0))
    tre, tim, bbre, bbim = pl.pallas_call(
        _s5_prep_body,
        grid=(depth,),
        in_specs=[cspec, cspec, cspec, rspec, rspec, rspec, rspec, rspec],
        out_specs=[tspec, tspec, rspec, rspec],
        out_shape=[jax.ShapeDtypeStruct((depth, TAB_ROWS, rows_c, LANES), f32)] * 2
        + [jax.ShapeDtypeStruct((depth, rows_r, LANES), f32)] * 2,
        name="s5_prep",
    )(lam_re.reshape(depth, rows_c, LANES), lam_im.reshape(depth, rows_c, LANES),
      ldt.reshape(depth, rows_c, LANES), rep(lam_re), rep(lam_im), rep(ldt), bt(b_re), bt(b_im))
    tab = jnp.stack([tre, tim], axis=1).reshape(depth, 2, TAB_ROWS, S5_FLAT)
    return tab, bbre.reshape(depth, S5_DIM, S5_STATE), bbim.reshape(depth, S5_DIM, S5_STATE)


def _layer_math(prompt, T, x, w, st, o, scr):
    (ng_ref, w_in_ref, cw_ssd_ref, cb_ssd_ref, dtb_ref, alog_ref, dfull_ref, sng_ref,
     cw_lru_ref, cb_lru_ref, lru_w_ref, lru_b_ref, lam_ref,
     tab_ref, bb_ref, cc_ref, s5d_ref, glu_w_ref, glu_b_ref, w_out_ref, _) = w
    o_ssd_ref, o_cssd_ref, o_lru_ref, o_clru_ref, o_s5r_ref, o_s5i_ref = o
    if prompt:
        h_ssd, prev_ssd, prev_lru, a_s, b_s, bur_s, bui_s, lru_c, s5_cr, s5_ci = scr
    else:
        h0_ssd_hbm, c0_ssd_ref, h0_lru_ref, c0_lru_ref, h0_s5r_ref, h0_s5i_ref = st
        a_s, b_s, bur_s, bui_s, c_s, bm_s, xw_s, yoff_s, eac_s, h_in, h_out, sem_in, sem_out = scr
    nseq = T // SEQ_S
    if not prompt:
        layer = pl.program_id(0)
        seq0 = pl.program_id(1) * nseq

        def in_copy(i, slot):
            return pltpu.make_async_copy(h0_ssd_hbm.at[layer, pl.ds(seq0 + i * STATE_SEQS, STATE_SEQS)],
                                         h_in.at[slot], sem_in.at[slot])

        def out_copy(i, slot):
            return pltpu.make_async_copy(h_out.at[slot],
                                         o_ssd_ref.at[layer, pl.ds(seq0 + i * STATE_SEQS, STATE_SEQS)],
                                         sem_out.at[slot])

        for j in range(STATE_BUFS - 1):
            in_copy(j, j).start()
    Q = CHUNK if prompt else SUBTILE_S

    hn = _rms(x, ng_ref[...]).astype(bf16)

    def proj(lo, hi):
        return _dot(hn, w_in_ref[:, lo:hi])

    def sub_iota(n):
        return lax.broadcasted_iota(jnp.int32, (SUBLANES, n), 0)

    def conv_taps(halo, rs, cw_ref, cb_ref):
        ext = jnp.concatenate([halo, rs], axis=0)
        n = rs.shape[0]
        acc = cb_ref[...] + cw_ref[3:4, :] * rs
        for j in range(1, CONV_WIDTH):
            acc = acc + cw_ref[3 - j:4 - j, :] * ext[HALO - SUBLANES * j:HALO - SUBLANES * j + n, :]
        return acc

    def conv(raw, prev_ref, c0_ref, cw_ref, cb_ref, o_ref):
        cdim = raw.shape[1]
        if not prompt:
            outs = []
            for s in range(T // Q):
                rs = raw[Q * s:Q * (s + 1), :]
                o_ref[s] = rs[Q - HALO:, :].reshape(CONV_WIDTH - 1, SUBLANES, cdim)
                outs.append(conv_taps(c0_ref[s].reshape(HALO, cdim), rs, cw_ref, cb_ref))
            return jnp.concatenate(outs, axis=0)
        first = sub_iota(cdim) == 0
        tail = prev_ref[...]
        outs = []
        for r0 in range(0, T, CHUNK):
            rs = raw[r0:r0 + CHUNK, :]
            cur = rs[CHUNK - HALO:, :]
            halo = jnp.concatenate(
                [jnp.where(first, pltpu.roll(tail[SUBLANES * k:SUBLANES * (k + 1), :], 1, 0),
                           pltpu.roll(cur[SUBLANES * k:SUBLANES * (k + 1), :], 1, 0))
                 for k in range(CONV_WIDTH - 1)], axis=0)
            outs.append(conv_taps(halo, rs, cw_ref, cb_ref))
            tail = cur
        prev_ref[...] = tail
        for k in range(CONV_WIDTH - 1):
            o_ref[k:k + 1, :] = tail[SUBLANES * k + SUBLANES - 1:SUBLANES * (k + 1), :]
        return jnp.concatenate(outs, axis=0)

    row = lax.broadcasted_iota(jnp.int32, (Q, Q), 0)
    col = lax.broadcasted_iota(jnp.int32, (Q, Q), 1)
    if prompt:
        def local_time(i):
            return jnp.bitwise_or(jnp.left_shift(jnp.bitwise_and(i, SUBLANES - 1), SEG_SHIFT),
                                  jnp.right_shift(i, SUB_SHIFT))
        causal = local_time(row) >= local_time(col)
    else:
        same_seq = jnp.bitwise_and(row, SUBLANES - 1) == jnp.bitwise_and(col, SUBLANES - 1)
        causal = jnp.logical_and(same_seq, jnp.right_shift(row, SUB_SHIFT) >= jnp.right_shift(col, SUB_SHIFT))
    tril = jnp.where(causal, 1.0, 0.0)
    lane_lo = lax.broadcasted_iota(jnp.int32, (Q, LANES), 1) < SSD_HEADDIM
    gsz = SSD_HPG * SSD_HEADDIM

    xbc = _silu(conv(proj(C_XBC, C_LRU), prev_ssd if prompt else None, None if prompt else c0_ssd_ref,
                     cw_ssd_ref, cb_ssd_ref, o_cssd_ref))
    dt_all = jax.nn.softplus(proj(C_DT, IN_COLS) + dtb_ref[...])
    a_neg = -jnp.exp(alog_ref[...])

    def ssd_state_io(rows_c, rows_xw, rows_b, e_last, h_get, h_set):
        outs = []
        for g in range(SSD_GROUPS):
            hp = h_get(g)
            outs.append(_dot_nt(rows_c[:, LANES * g:LANES * (g + 1)].astype(bf16), hp.astype(bf16)))
            sg = _dot_tn(rows_xw[:, gsz * g:gsz * (g + 1)].astype(bf16),
                         rows_b[:, LANES * g:LANES * (g + 1)].astype(bf16))
            dec = jnp.concatenate(
                [jnp.broadcast_to(e_last[:, SSD_HPG * g + k:SSD_HPG * g + k + 1], (SSD_HEADDIM, SSD_STATE))
                 for k in range(SSD_HPG)], axis=0)
            h_set(g, dec * hp + sg)
        return jnp.concatenate(outs, axis=1)

    def ssd_chunk(r0):
        xs = xbc[r0:r0 + Q, :SSD_DIM]
        bm = xbc[r0:r0 + Q, SSD_DIM:SSD_DIM + SSD_BC]
        cm = xbc[r0:r0 + Q, SSD_DIM + SSD_BC:]
        dt = dt_all[r0:r0 + Q, :]
        bm_b = bm.astype(bf16)
        cm_b = cm.astype(bf16)
        acum = _dot_exact(tril, dt * a_neg)
        acum_row = acum.T
        dt_row = dt.T
        scores = [_dot_nt(cm_b[:, LANES * g:LANES * (g + 1)], bm_b[:, LANES * g:LANES * (g + 1)])
                  for g in range(SSD_GROUPS)]
        if prompt:
            arow = acum_row[0:SSD_HEADS, :]
            w_row = jnp.exp(arow[:, Q - 1:Q] - arow) * dt_row[0:SSD_HEADS, :]
            e_end = jnp.exp(acum[Q - 1:Q, :])
            lane1 = lane_lo[0:1, :]
            bts = [bm[:, LANES * g:LANES * (g + 1)].T for g in range(SSD_GROUPS)]
            y_pairs = []
            for j in range(SSD_HEADS // 2):
                g = (2 * j) // SSD_HPG
                cm_g = cm[:, LANES * g:LANES * (g + 1)]
                bt_g = bts[g]
                lhs_y, lhs_s = [], []
                for h in (2 * j, 2 * j + 1):
                    colb = jnp.broadcast_to(acum[:, h:h + 1], (Q, LANES))
                    decay = jnp.exp(jnp.where(causal, colb - acum_row[h:h + 1, :], NEG))
                    lhs_y.append((scores[g] * decay * dt_row[h:h + 1, :]).astype(bf16))
                    lhs_s.append((bt_g * w_row[h:h + 1, :]).astype(bf16))
                for h in (2 * j, 2 * j + 1):
                    colb = jnp.broadcast_to(acum[:, h:h + 1], (Q, LANES))
                    lhs_y.append((jnp.exp(colb) * cm_g).astype(bf16))
                xp = xs[:, LANES * j:LANES * (j + 1)]
                hp = h_ssd[:, LANES * j:LANES * (j + 1)]
                xbd = jnp.concatenate([jnp.where(lane_lo, xp, 0.0), jnp.where(lane_lo, 0.0, xp)],
                                      axis=0).astype(bf16)
                hbd = jnp.concatenate([jnp.where(lane_lo, hp, 0.0), jnp.where(lane_lo, 0.0, hp)],
                                      axis=0).astype(bf16)
                y_pairs.append(_dot(jnp.concatenate(lhs_y, axis=1), jnp.concatenate([xbd, hbd], axis=0)))
                dec = jnp.where(lane1, jnp.broadcast_to(e_end[:, 2 * j:2 * j + 1], (1, LANES)),
                                jnp.broadcast_to(e_end[:, 2 * j + 1:2 * j + 2], (1, LANES)))
                h_ssd[:, LANES * j:LANES * (j + 1)] = dec * hp + _dot(jnp.concatenate(lhs_s, axis=1), xbd)
            return jnp.concatenate(y_pairs, axis=1) + dfull_ref[...] * xs

        eac = jnp.exp(acum)
        sel = jnp.where(col == jnp.bitwise_and(row, SUBLANES - 1) + (Q - SUBLANES), 1.0, 0.0)
        acum_end = _dot_exact(sel, acum)
        wgt = jnp.exp(acum_end - acum) * dt
        y_pairs, xw_pairs, ecol_pairs = [], [], []
        for j in range(SSD_HEADS // 2):
            g = (2 * j) // SSD_HPG
            ms = []
            for h in (2 * j, 2 * j + 1):
                diff = acum[:, h:h + 1] - acum_row[h:h + 1, :]
                decay = jnp.exp(jnp.where(causal, diff, NEG))
                ms.append((scores[g] * decay * dt_row[h:h + 1, :]).astype(bf16))
            xp = xs[:, LANES * j:LANES * (j + 1)]
            xbd = jnp.concatenate([jnp.where(lane_lo, xp, 0.0), jnp.where(lane_lo, 0.0, xp)], axis=0).astype(bf16)
            y_pairs.append(_dot(jnp.concatenate(ms, axis=1), xbd))
            xw_pairs.append(xp * _pair_expand(wgt, j, lane_lo))
            ecol_pairs.append(_pair_expand(eac, j, lane_lo))
        y_diag = jnp.concatenate(y_pairs, axis=1)
        xw = jnp.concatenate(xw_pairs, axis=1)
        ecol = jnp.concatenate(ecol_pairs, axis=1)

        c_s[r0:r0 + Q, :] = _dot(perm_b, cm_b)
        bm_s[r0:r0 + Q, :] = _dot(perm_b, bm_b)
        xw_s[r0:r0 + Q, :] = _dot(perm_b, xw.astype(bf16))
        eac_s[r0:r0 + Q, :] = eac
        return y_diag + dfull_ref[...] * xs, ecol

    if prompt:
        y = jnp.concatenate([ssd_chunk(r0) for r0 in range(0, T, Q)], axis=0)
    else:
        to_seq = jnp.bitwise_or(jnp.left_shift(jnp.bitwise_and(row, SUBLANES - 1), SUB_SHIFT),
                                jnp.right_shift(row, SUB_SHIFT)) == col
        perm_b = jnp.where(to_seq, 1.0, 0.0).astype(bf16)
        parts = [ssd_chunk(r0) for r0 in range(0, T, Q)]

        ngrp = nseq // STATE_SEQS

        def seq_step(i, carry):
            slot = jnp.bitwise_and(i, STATE_BUFS - 1)
            ahead = i + (STATE_BUFS - 1)

            @pl.when(ahead < ngrp)
            def _():
                in_copy(ahead, jnp.bitwise_and(ahead, STATE_BUFS - 1)).start()

            in_copy(i, slot).wait()

            @pl.when(i >= STATE_BUFS)
            def _():
                out_copy(i - STATE_BUFS, slot).wait()

            for q in range(STATE_SEQS):
                sq = i * STATE_SEQS + q
                s0 = pl.multiple_of(sq * SEQ_S, SEQ_S)

                def h_get(g):
                    return h_in[slot, q, pl.ds(gsz * g, gsz), :]

                def h_set(g, v):
                    h_out[slot, q, pl.ds(gsz * g, gsz), :] = v

                e_row = jnp.left_shift(jnp.right_shift(sq, SUB_SHIFT), SUBTILE_SHIFT) + (Q - SUBLANES) \
                    + jnp.bitwise_and(sq, SUBLANES - 1)
                yoff_s[pl.ds(s0, SEQ_S), :] = ssd_state_io(
                    c_s[pl.ds(s0, SEQ_S), :], xw_s[pl.ds(s0, SEQ_S), :], bm_s[pl.ds(s0, SEQ_S), :],
                    eac_s[pl.ds(e_row, 1), :], h_get, h_set)
            out_copy(i, slot).start()
            return carry

        lax.fori_loop(0, ngrp, seq_step, 0)
        for j in range(STATE_BUFS):
            out_copy(ngrp - STATE_BUFS + j, j).wait()
        perm_f = jnp.where(to_seq, 1.0, 0.0)
        y = jnp.concatenate(
            [part + _dot_exact(perm_f, yoff_s[r0:r0 + Q, :]) * ecol
             for r0, (part, ecol) in zip(range(0, T, Q), parts)], axis=0)
    y_ssd = _rms(y * _silu(proj(C_Z, C_XBC)), sng_ref[...])

    xr = conv(proj(C_LRU, C_LRU_G), prev_lru if prompt else None, None if prompt else c0_lru_ref,
              cw_lru_ref, cb_lru_ref, o_clru_ref)
    xr_b = xr.astype(bf16)
    hl = LRU_DIM // 2
    gates = jnp.concatenate([_dot(xr_b[:, hl * (k % 2):hl * (k % 2 + 1)], lru_w_ref[k]) for k in range(4)],
                            axis=1) + lru_b_ref[...]
    r_gate = jax.nn.sigmoid(gates[:, :LRU_DIM])
    i_gate = jax.nn.sigmoid(gates[:, LRU_DIM:])
    log_a = -LRU_C * r_gate * jax.nn.softplus(-lam_ref[...])
    a_t = jnp.exp(log_a)
    gain = jnp.sqrt(jnp.maximum(-jnp.tanh(log_a) * (a_t * a_t + 1.0), 0.0))
    a_s[...] = a_t
    b_s[...] = gain * i_gate * xr

    def vrow(ref, r0, k):
        return ref[r0 + SUBLANES * k:r0 + SUBLANES * (k + 1), :]

    def set_vrow(ref, r0, k, v):
        ref[r0 + SUBLANES * k:r0 + SUBLANES * (k + 1), :] = v

    if prompt:
        sub = sub_iota(LRU_DIM)
        carry = lru_c[...]
        for r0 in range(0, T, CHUNK):
            acc_a, acc_h = vrow(a_s, r0, 0), vrow(b_s, r0, 0)
            for k in range(1, SEG):
                a_k = vrow(a_s, r0, k)
                acc_h = a_k * acc_h + vrow(b_s, r0, k)
                acc_a = a_k * acc_a
                set_vrow(a_s, r0, k, acc_a)
                set_vrow(b_s, r0, k, acc_h)
            alpha = jnp.where(sub == 0, 0.0, pltpu.roll(acc_a, 1, 0))
            beta = jnp.where(sub == 0, jnp.broadcast_to(carry, (SUBLANES, LRU_DIM)), pltpu.roll(acc_h, 1, 0))
            for d in (1, 2, 4):
                a_sh = jnp.where(sub >= d, pltpu.roll(alpha, d, 0), 1.0)
                b_sh = jnp.where(sub >= d, pltpu.roll(beta, d, 0), 0.0)
                beta = alpha * b_sh + beta
                alpha = alpha * a_sh
            carry = (acc_a * beta + acc_h)[SUBLANES - 1:SUBLANES, :]
            for k in range(SEG):
                set_vrow(b_s, r0, k, vrow(b_s, r0, k) + vrow(a_s, r0, k) * beta)
        lru_c[...] = carry
        o_lru_ref[...] = carry
    else:
        for s in range(T // Q):
            h = h0_lru_ref[SUBLANES * s:SUBLANES * (s + 1), :]
            for k in range(Q // SUBLANES):
                h = vrow(a_s, Q * s, k) * h + vrow(b_s, Q * s, k)
                set_vrow(b_s, Q * s, k, h)
            o_lru_ref[SUBLANES * s:SUBLANES * (s + 1), :] = h
    y_lru = b_s[...] * _silu(proj(C_LRU_G, C_S5))

    u = proj(C_S5, C_S5_G)
    u_b = u.astype(bf16)
    half = S5_DIM // 2
    for k in range(2):
        uk = u_b[:, half * k:half * (k + 1)]
        bur_s[:, S5_HALF * k:S5_HALF * (k + 1)] = _dot(uk, bb_ref[k])
        bui_s[:, S5_HALF * k:S5_HALF * (k + 1)] = _dot(uk, bb_ref[2 + k])

    def tab(r0):
        return tab_ref[0, r0:r0 + SUBLANES, :], tab_ref[1, r0:r0 + SUBLANES, :]

    def cmul_add(pr, pi, xr, xi, yr, yi):
        return pr * xr - pi * xi + yr, pr * xi + pi * xr + yi

    ar, ai = tab(TAB_A)
    if prompt:
        sub = sub_iota(S5_FLAT)
        c_r, c_i = s5_cr[...], s5_ci[...]
        for r0 in range(0, T, CHUNK):
            hr, hi = vrow(bur_s, r0, 0), vrow(bui_s, r0, 0)
            for k in range(1, SEG):
                hr, hi = cmul_add(ar, ai, hr, hi, vrow(bur_s, r0, k), vrow(bui_s, r0, k))
                set_vrow(bur_s, r0, k, hr)
                set_vrow(bui_s, r0, k, hi)
            er = jnp.where(sub == 0, jnp.broadcast_to(c_r, (SUBLANES, S5_FLAT)), pltpu.roll(hr, 1, 0))
            ei = jnp.where(sub == 0, jnp.broadcast_to(c_i, (SUBLANES, S5_FLAT)), pltpu.roll(hi, 1, 0))
            for t, d in enumerate((1, 2, 4)):
                qr, qi = tab(TAB_Q + t * SUBLANES)
                er, ei = cmul_add(qr, qi, pltpu.roll(er, d, 0), pltpu.roll(ei, d, 0), er, ei)
            sr, si = tab(TAB_ASEG)
            nr, ni = cmul_add(sr, si, er, ei, hr, hi)
            c_r, c_i = nr[SUBLANES - 1:SUBLANES, :], ni[SUBLANES - 1:SUBLANES, :]
            for k in range(SEG):
                pr, pi = tab(TAB_PW + k * SUBLANES)
                vr, vi = cmul_add(pr, pi, er, ei, vrow(bur_s, r0, k), vrow(bui_s, r0, k))
                set_vrow(bur_s, r0, k, vr)
                set_vrow(bui_s, r0, k, vi)
        s5_cr[...] = c_r
        s5_ci[...] = c_i
        o_s5r_ref[...] = c_r
        o_s5i_ref[...] = c_i
    else:
        for s in range(T // Q):
            rows = slice(SUBLANES * s, SUBLANES * (s + 1))
            hr, hi = h0_s5r_ref[rows, :], h0_s5i_ref[rows, :]
            for k in range(Q // SUBLANES):
                hr, hi = cmul_add(ar, ai, hr, hi, vrow(bur_s, Q * s, k), vrow(bui_s, Q * s, k))
                set_vrow(bur_s, Q * s, k, hr)
                set_vrow(bui_s, Q * s, k, hi)
            o_s5r_ref[rows, :] = hr
            o_s5i_ref[rows, :] = hi
    ys = []
    for k in range(2):
        hk = jnp.concatenate([bur_s[:, S5_HALF * k:S5_HALF * (k + 1)].astype(bf16),
                              bui_s[:, S5_HALF * k:S5_HALF * (k + 1)].astype(bf16)], axis=1)
        ys.append(_dot(hk, cc_ref[k]))
    ys5 = jnp.concatenate(ys, axis=1) + s5d_ref[...] * u
    ys5 = jax.nn.gelu(ys5)
    ys5 = ys5 * jax.nn.sigmoid(_dot(ys5.astype(bf16), glu_w_ref[...]) + glu_b_ref[...])
    y_s5 = ys5 * _silu(proj(C_S5_G, C_DT))

    ycat = jnp.concatenate([y_ssd.astype(bf16), y_lru.astype(bf16), y_s5.astype(bf16)], axis=1)
    return x + _dot(ycat, w_out_ref[...])


def _prompt_body(final, *refs):
    x_ref = refs[0]
    w = refs[1:1 + N_WEIGHTS]
    y_ref = refs[1 + N_WEIGHTS]
    o = refs[2 + N_WEIGHTS:8 + N_WEIGHTS]
    scr = refs[8 + N_WEIGHTS:]
    h_ssd, prev_ssd, prev_lru = scr[:3]
    lru_c, s5_cr, s5_ci = scr[-3:]

    @pl.when(pl.program_id(1) == 0)
    def _():
        h_ssd[...] = jnp.zeros_like(h_ssd)
        prev_ssd[...] = jnp.zeros_like(prev_ssd)
        prev_lru[...] = jnp.zeros_like(prev_lru)
        lru_c[...] = jnp.zeros_like(lru_c)
        s5_cr[...] = jnp.zeros_like(s5_cr)
        s5_ci[...] = jnp.zeros_like(s5_ci)

    for k in range(TILES_PER_STEP):
        rows = slice(TILE_P * k, TILE_P * (k + 1))
        scr_k = scr[:3] + scr[3 + 4 * k:7 + 4 * k] + scr[3 + 4 * TILES_PER_STEP:]
        out = _layer_math(True, TILE_P, x_ref[rows, :], w, None, o, scr_k)
        if final:
            out = _rms(out, w[-1][...])
        y_ref[rows, :] = out

    @pl.when(pl.program_id(1) == pl.num_programs(1) - 1)
    def _():
        o[0][...] = h_ssd[...].T


def _sample_body(*refs):
    x_ref = refs[0]
    st = refs[1:7]
    w = refs[7:7 + N_WEIGHTS]
    y_ref = refs[7 + N_WEIGHTS]
    o = refs[8 + N_WEIGHTS:14 + N_WEIGHTS]
    x_all = refs[14 + N_WEIGHTS]
    scr = refs[15 + N_WEIGHTS:]
    layer = pl.program_id(0)
    last_layer = layer == pl.num_programs(0) - 1
    r0 = pl.multiple_of(pl.program_id(1) * TILE_S, TILE_S)

    @pl.when(layer == 0)
    def _():
        x_all[pl.ds(r0, TILE_S), :] = x_ref[...]

    out = _layer_math(False, TILE_S, x_all[pl.ds(r0, TILE_S), :], w, st, o, scr)
    x_all[pl.ds(r0, TILE_S), :] = out

    @pl.when(last_layer)
    def _():
        y_ref[...] = _rms(out, w[-1][...])

    @pl.when(jnp.logical_not(last_layer))
    def _():
        y_ref[...] = out


def _prompt_call(layer, final, x, weights):
    T = TILE_P
    nb, seq, _ = x.shape

    def wspec(a):
        nd = a.ndim - 1
        return pl.BlockSpec((None,) + a.shape[1:], lambda b, c: (layer,) + (0,) * nd, pipeline_mode=pl.Buffered(1))

    def st(shape):
        nd = len(shape)
        return pl.BlockSpec((None,) + shape, lambda b, c: (b,) + (0,) * nd)

    step_rows = T * TILES_PER_STEP
    x_spec = pl.BlockSpec((None, step_rows, D_MODEL), lambda b, c: (b, c, 0))
    out_specs = [x_spec, st((SSD_DIM, SSD_STATE)), st((CONV_WIDTH - 1, SSD_CONV_DIM)), st((1, LRU_DIM)),
                 st((CONV_WIDTH - 1, LRU_DIM)), st((1, S5_FLAT)), st((1, S5_FLAT))]
    out_shape = [jax.ShapeDtypeStruct(x.shape, f32),
                 jax.ShapeDtypeStruct((nb, SSD_DIM, SSD_STATE), f32),
                 jax.ShapeDtypeStruct((nb, CONV_WIDTH - 1, SSD_CONV_DIM), f32),
                 jax.ShapeDtypeStruct((nb, 1, LRU_DIM), f32),
                 jax.ShapeDtypeStruct((nb, CONV_WIDTH - 1, LRU_DIM), f32),
                 jax.ShapeDtypeStruct((nb, 1, S5_FLAT), f32),
                 jax.ShapeDtypeStruct((nb, 1, S5_FLAT), f32)]
    scratch = [pltpu.VMEM((SSD_STATE, SSD_DIM), f32),
               pltpu.VMEM((HALO, SSD_CONV_DIM), f32),
               pltpu.VMEM((HALO, LRU_DIM), f32)]
    scratch += [pltpu.VMEM((T, LRU_DIM), f32), pltpu.VMEM((T, LRU_DIM), f32),
                pltpu.VMEM((T, S5_FLAT), f32), pltpu.VMEM((T, S5_FLAT), f32)] * TILES_PER_STEP
    scratch += [pltpu.VMEM((1, LRU_DIM), f32), pltpu.VMEM((1, S5_FLAT), f32), pltpu.VMEM((1, S5_FLAT), f32)]
    return pl.pallas_call(
        functools.partial(_prompt_body, final),
        grid=(nb, seq // step_rows), in_specs=[x_spec] + [wspec(a) for a in weights],
        out_specs=out_specs, out_shape=out_shape, scratch_shapes=scratch,
        compiler_params=pltpu.CompilerParams(dimension_semantics=("arbitrary", "arbitrary"),
                                             vmem_limit_bytes=VMEM_LIMIT_BYTES),
        name="layer_prompt",
    )(x, *weights)


def _sample_call(x, states, weights):
    T = TILE_S
    rows = x.shape[0]
    depth = weights[0].shape[0]
    nseq = T // SEQ_S

    def wspec(a):
        nd = a.ndim - 1
        return pl.BlockSpec((None,) + a.shape[1:], lambda l, i: (l,) + (0,) * nd, pipeline_mode=pl.Buffered(1))

    def st(a):
        nd = a.ndim - 2
        if a.ndim == 4:
            return pl.BlockSpec(memory_space=pl.ANY)
        if a.ndim == 6:
            return pl.BlockSpec((None, None) + a.shape[2:], lambda l, i: (l, i) + (0,) * nd)
        return pl.BlockSpec((None, nseq) + a.shape[2:], lambda l, i: (l, i) + (0,) * nd)

    x_spec = pl.BlockSpec((T, D_MODEL), lambda l, i: (i, 0))
    st_specs = [st(a) for a in states]
    scratch = [pltpu.VMEM((rows, D_MODEL), f32),
               pltpu.VMEM((T, LRU_DIM), f32), pltpu.VMEM((T, LRU_DIM), f32),
               pltpu.VMEM((T, S5_FLAT), f32), pltpu.VMEM((T, S5_FLAT), f32),
               pltpu.VMEM((T, SSD_BC), f32), pltpu.VMEM((T, SSD_BC), f32),
               pltpu.VMEM((T, SSD_DIM), f32), pltpu.VMEM((T, SSD_DIM), f32),
               pltpu.VMEM((T, LANES), f32),
               pltpu.VMEM((STATE_BUFS, STATE_SEQS, SSD_DIM, SSD_STATE), f32),
               pltpu.VMEM((STATE_BUFS, STATE_SEQS, SSD_DIM, SSD_STATE), f32),
               pltpu.SemaphoreType.DMA((STATE_BUFS,)), pltpu.SemaphoreType.DMA((STATE_BUFS,))]
    return pl.pallas_call(
        _sample_body,
        grid=(depth, rows // T), in_specs=[x_spec] + st_specs + [wspec(a) for a in weights],
        out_specs=[pl.BlockSpec((None, T, D_MODEL), lambda l, i: (l, i, 0))] + st_specs,
        out_shape=[jax.ShapeDtypeStruct((depth,) + x.shape, f32)] + [jax.ShapeDtypeStruct(a.shape, f32) for a in states],
        scratch_shapes=scratch,
        compiler_params=pltpu.CompilerParams(dimension_semantics=("arbitrary", "arbitrary"),
                                             vmem_limit_bytes=VMEM_LIMIT_BYTES),
        name="layers_sample",
    )(x, *states, *weights)


def _block_diag(blocks):
    *lead, n, r, c = blocks.shape
    eye = jnp.eye(n, dtype=blocks.dtype)
    return (blocks[..., :, :, None, :] * eye[:, None, :, None]).reshape(*lead, n * r, n * c)


def _pad_lanes(v):
    return jnp.pad(v, [(0, 0)] * (v.ndim - 1) + [(0, LANES - v.shape[-1])])


def kernel(x_prompt, x_sample, state_ssd, state_ssd_conv, state_lru, state_lru_conv, state_s5_re, state_s5_im, norm_g, w_in, ssd_conv_w, ssd_conv_b, ssd_dt_bias, ssd_a_log, ssd_d, ssd_norm_g, lru_conv_w, lru_conv_b, lru_wa, lru_ba, lru_wx, lru_bx, lru_lambda, s5_lambda_re, s5_lambda_im, s5_log_dt, s5_b_re, s5_b_im, s5_c_re, s5_c_im, s5_d, s5_glu_w, s5_glu_b, w_out, final_norm_g):
    depth = w_in.shape[0]
    nbp = x_prompt.shape[0]
    nbs, ls, _ = x_sample.shape
    assert ls == SEQ_S and x_prompt.shape[1] % (TILE_P * TILES_PER_STEP) == 0 and (nbs * ls) % TILE_S == 0

    tab, bbar_re, bbar_im = _s5_prep(s5_lambda_re.astype(f32), s5_lambda_im.astype(f32), s5_log_dt.astype(f32),
                                     s5_b_re.astype(f32), s5_b_im.astype(f32))

    def row(v, n):
        return v.astype(f32).reshape(depth, 1, n)

    wi = w_in.astype(bf16)
    w_in_r = jnp.concatenate([wi[..., 0:3072], wi[..., 3088:5136], _pad_lanes(wi[..., 3072:3088])], axis=-1)

    def halves(v):
        return _block_diag(v.reshape(depth, 2, S5_NGROUPS // 2, S5_GROUP, S5_STATE))

    bb = jnp.concatenate([halves(bbar_re), halves(bbar_im)], axis=1).astype(bf16)

    def chalves(v):
        return _block_diag(jnp.transpose(v.astype(f32), (0, 1, 3, 2)).reshape(depth, 2, S5_NGROUPS // 2, S5_STATE, S5_GROUP))

    cc = jnp.concatenate([chalves(s5_c_re), -chalves(s5_c_im)], axis=2).astype(bf16)
    weights = (
        row(norm_g, D_MODEL), w_in_r,
        ssd_conv_w.astype(f32), row(ssd_conv_b, SSD_CONV_DIM),
        _pad_lanes(row(ssd_dt_bias, SSD_HEADS)), _pad_lanes(row(ssd_a_log, SSD_HEADS)),
        jnp.repeat(ssd_d.astype(f32), SSD_HEADDIM, axis=-1).reshape(depth, 1, SSD_DIM),
        row(ssd_norm_g, SSD_DIM),
        lru_conv_w.astype(f32), row(lru_conv_b, LRU_DIM),
        jnp.concatenate([_block_diag(v.astype(f32).reshape(depth, 2, v.shape[1] // 2, *v.shape[2:]))
                         for v in (lru_wa, lru_wx)], axis=1).astype(bf16),
        jnp.concatenate([lru_ba, lru_bx], axis=-1).astype(f32).reshape(depth, 1, 2 * LRU_DIM),
        row(lru_lambda, LRU_DIM),
        tab, bb, cc, row(s5_d, S5_DIM),
        s5_glu_w.astype(bf16), row(s5_glu_b, S5_DIM),
        w_out.astype(bf16),
        jnp.broadcast_to(final_norm_g.astype(f32).reshape(1, 1, D_MODEL), (depth, 1, D_MODEL)),
    )

    seq_p = x_prompt.shape[1]
    xp = jnp.swapaxes(x_prompt.astype(f32).reshape(nbp, seq_p // CHUNK, SUBLANES, SEG, D_MODEL), 2, 3)
    xp = xp.reshape(nbp, seq_p, D_MODEL)
    outs_p = [[] for _ in range(6)]
    for i in range(depth):
        res = _prompt_call(i, i == depth - 1, xp, weights)
        xp = res[0]
        for j in range(6):
            outs_p[j].append(res[1 + j])
    y_prompt = jnp.swapaxes(xp.reshape(nbp, seq_p // CHUNK, SEG, SUBLANES, D_MODEL), 2, 3).reshape(nbp, seq_p, D_MODEL)

    ntile = nbs // SUBLANES
    nsub = TILE_S // SUBTILE_S

    def conv_in(v):
        v = jnp.swapaxes(v.astype(f32).reshape(depth, ntile, SUBLANES, CONV_WIDTH - 1, v.shape[-1]), 2, 3)
        return v.reshape(depth, ntile // nsub, nsub, CONV_WIDTH - 1, SUBLANES, v.shape[-1])

    def conv_out(v, dtype):
        v = v.reshape(depth, ntile, CONV_WIDTH - 1, SUBLANES, v.shape[-1])
        return jnp.swapaxes(v, 2, 3).reshape(depth, nbs, CONV_WIDTH - 1, v.shape[-1]).astype(dtype)

    xs = jnp.swapaxes(x_sample.astype(f32).reshape(ntile, SUBLANES, ls, D_MODEL), 1, 2).reshape(nbs * ls, D_MODEL)
    states_s = (state_ssd.astype(f32).reshape(depth, nbs, SSD_DIM, SSD_STATE), conv_in(state_ssd_conv),
                state_lru.astype(f32), conv_in(state_lru_conv),
                state_s5_re.astype(f32).reshape(depth, nbs, S5_FLAT), state_s5_im.astype(f32).reshape(depth, nbs, S5_FLAT))
    weights_s = tuple(w[:, :, TAB_A:TAB_A + SUBLANES] if w is tab else w for w in weights)
    res_s = _sample_call(xs, states_s, weights_s)
    y_sample = jnp.swapaxes(res_s[0][depth - 1].reshape(ntile, ls, SUBLANES, D_MODEL), 1, 2).reshape(nbs, ls, D_MODEL)

    def stack(lst, shape, dtype):
        return jnp.stack(lst).reshape((depth,) + shape).astype(dtype)

    ssd_shape = (SSD_HEADS, SSD_HEADDIM, SSD_STATE)
    s5_shape = (S5_NGROUPS, S5_STATE)
    return (
        y_prompt.astype(x_prompt.dtype), y_sample.astype(x_sample.dtype),
        stack(outs_p[0], (nbp,) + ssd_shape, state_ssd.dtype),
        res_s[1].reshape((depth, nbs) + ssd_shape).astype(state_ssd.dtype),
        stack(outs_p[1], (nbp, CONV_WIDTH - 1, SSD_CONV_DIM), state_ssd_conv.dtype),
        conv_out(res_s[2], state_ssd_conv.dtype),
        stack(outs_p[2], (nbp, LRU_DIM), state_lru.dtype), res_s[3].astype(state_lru.dtype),
        stack(outs_p[3], (nbp, CONV_WIDTH - 1, LRU_DIM), state_lru_conv.dtype),
        conv_out(res_s[4], state_lru_conv.dtype),
        stack(outs_p[4], (nbp,) + s5_shape, state_s5_re.dtype),
        res_s[5].reshape((depth, nbs) + s5_shape).astype(state_s5_re.dtype),
        stack(outs_p[5], (nbp,) + s5_shape, state_s5_im.dtype),
        res_s[6].reshape((depth, nbs) + s5_shape).astype(state_s5_im.dtype),
    )
```

```python
import functools

import jax
import jax.numpy as jnp
from jax import lax
from jax.experimental import pallas as pl
from jax.experimental.pallas import tpu as pltpu

f32 = jnp.float32
bf16 = jnp.bfloat16

D_MODEL = 1024
CONV_WIDTH = 4
SSD_DIM = 1024
SSD_HEADDIM = 64
SSD_HEADS = 16
SSD_GROUPS = 4
SSD_HPG = 4
SSD_STATE = 128
SSD_BC = SSD_GROUPS * SSD_STATE
SSD_CONV_DIM = SSD_DIM + 2 * SSD_BC
LRU_DIM = 512
LRU_C = 8.0
S5_DIM = 512
S5_GROUP = 16
S5_NGROUPS = 32
S5_STATE = 64
S5_FLAT = S5_NGROUPS * S5_STATE
S5_HALF = S5_FLAT // 2
EPS = 1e-6

LANES = 128
SUBLANES = 8
CHUNK = 128
TILE_P = 256
TILES_PER_STEP = 1
TILE_S = 256
SUBTILE_S = 64
SUBTILE_SHIFT = 6
SEQ_S = 8
STATE_BUFS = 4
STATE_SEQS = 2
NEG = -1e30

SEG = CHUNK // SUBLANES
SEG_SHIFT = 4
SUB_SHIFT = 3
HALO = (CONV_WIDTH - 1) * SUBLANES

TAB_A = 0
TAB_Q = TAB_A + SUBLANES
TAB_ASEG = TAB_Q + 3 * SUBLANES
TAB_PW = TAB_ASEG + SUBLANES
TAB_ROWS = TAB_PW + SEG * SUBLANES

C_Z = 0
C_XBC = C_Z + SSD_DIM
C_LRU = C_XBC + SSD_CONV_DIM
C_LRU_G = C_LRU + LRU_DIM
C_S5 = C_LRU_G + LRU_DIM
C_S5_G = C_S5 + S5_DIM
C_DT = C_S5_G + S5_DIM
IN_COLS = C_DT + LANES

VMEM_LIMIT_BYTES = 56 * 1024 * 1024

N_WEIGHTS = 21


def _rms(x, g):
    return x * lax.rsqrt(jnp.mean(x * x, axis=-1, keepdims=True) + EPS) * g


def _silu(x):
    return x * jax.nn.sigmoid(x)


def _dot(a, b):
    return jnp.dot(a, b, preferred_element_type=f32)


def _dot_nt(a, b):
    return lax.dot_general(a, b, (((1,), (1,)), ((), ())), preferred_element_type=f32)


def _dot_tn(a, b):
    return lax.dot_general(a, b, (((0,), (0,)), ((), ())), preferred_element_type=f32)


def _dot_exact(a, b):
    return jnp.dot(a, b, preferred_element_type=f32, precision=lax.Precision.HIGHEST)


def _pair_expand(v, j, lane_lo):
    q = v.shape[0]
    lo = jnp.broadcast_to(v[:, 2 * j:2 * j + 1], (q, LANES))
    hi = jnp.broadcast_to(v[:, 2 * j + 1:2 * j + 2], (q, LANES))
    return jnp.where(lane_lo, lo, hi)


def _s5_prep_body(lre_ref, lim_ref, ldt_ref, lre_g_ref, lim_g_ref, ldt_g_ref,
                  bre_ref, bim_ref, tre_ref, tim_ref, bbre_ref, bbim_ref):
    def abar(lre, lim, ldt):
        delta = jnp.exp(ldt)
        mag = jnp.exp(lre * delta)
        return mag * jnp.cos(lim * delta), mag * jnp.sin(lim * delta)

    ar, ai = abar(lre_ref[...], lim_ref[...], ldt_ref[...])

    def cmul(xr, xi, yr, yi):
        return xr * yr - xi * yi, xr * yi + xi * yr

    pw = [(ar, ai)]
    for _ in range(SEG - 1):
        pw.append(cmul(*pw[-1], ar, ai))
    seg = [pw[SEG - 1]]
    for _ in range(2):
        seg.append(cmul(*seg[-1], *seg[-1]))
    zero = jnp.zeros_like(ar)

    def put(i, v):
        tre_ref[i] = v[0]
        tim_ref[i] = v[1]

    for r in range(SUBLANES):
        put(TAB_A + r, pw[0])
        put(TAB_ASEG + r, seg[0])
        for t, d in enumerate((1, 2, 4)):
            put(TAB_Q + t * SUBLANES + r, seg[t] if r >= d else (zero, zero))
        for k in range(SEG):
            put(TAB_PW + k * SUBLANES + r, pw[k])

    lre, lim = lre_g_ref[...], lim_g_ref[...]
    ar, ai = abar(lre, lim, ldt_g_ref[...])
    denom = lre * lre + lim * lim
    nr = ar - 1.0
    ni = ai

    def per_channel(v):
        return jnp.broadcast_to(v[:, None, :], (S5_NGROUPS, S5_GROUP, S5_STATE)).reshape(S5_DIM, S5_STATE)

    coef_re = per_channel((nr * lre + ni * lim) / denom)
    coef_im = per_channel((ni * lre - nr * lim) / denom)
    bre, bim = bre_ref[...], bim_ref[...]
    bbre_ref[...] = coef_re * bre - coef_im * bim
    bbim_ref[...] = coef_re * bim + coef_im * bre


def _s5_prep(lam_re, lam_im, log_dt, b_re, b_im):
    depth = lam_re.shape[0]
    rows_c = S5_FLAT // LANES
    ldt = jnp.broadcast_to(log_dt[:, :, None], (depth, S5_NGROUPS, S5_STATE))

    def bt(v):
        return jnp.transpose(v, (0, 1, 3, 2)).reshape(depth, S5_DIM, S5_STATE)

    cspec = pl.BlockSpec((None, rows_c, LANES), lambda i: (i, 0, 0))
    gspec = pl.BlockSpec((None, S5_NGROUPS, S5_STATE), lambda i: (i, 0, 0))
    rspec = pl.BlockSpec((None, S5_DIM, S5_STATE), lambda i: (i, 0, 0))
    tspec = pl.BlockSpec((None, TAB_ROWS, rows_c, LANES), lambda i: (i, 0, 0, 0))
    tre, tim, bbre, bbim = pl.pallas_call(
        _s5_prep_body,
        grid=(depth,),
        in_specs=[cspec, cspec, cspec, gspec, gspec, gspec, rspec, rspec],
        out_specs=[tspec, tspec, rspec, rspec],
        out_shape=[jax.ShapeDtypeStruct((depth, TAB_ROWS, rows_c, LANES), f32)] * 2
        + [jax.ShapeDtypeStruct((depth, S5_DIM, S5_STATE), f32)] * 2,
        name="s5_prep",
    )(lam_re.reshape(depth, rows_c, LANES), lam_im.reshape(depth, rows_c, LANES),
      ldt.reshape(depth, rows_c, LANES), lam_re, lam_im, ldt, bt(b_re), bt(b_im))
    tab = jnp.stack([tre, tim], axis=1).reshape(depth, 2, TAB_ROWS, S5_FLAT)
    return tab, bbre, bbim


W_IN_ROWS = 256


def _w_in_body(w_ref, o_ref):
    n_dt = SSD_HEADS
    src_dt = C_LRU
    o_ref[:, 0:C_LRU] = w_ref[:, 0:C_LRU].astype(bf16)
    o_ref[:, C_LRU:C_DT] = w_ref[:, src_dt + n_dt:src_dt + n_dt + (C_DT - C_LRU)].astype(bf16)
    o_ref[:, C_DT:IN_COLS] = jnp.concatenate(
        [w_ref[:, src_dt:src_dt + n_dt], jnp.zeros((W_IN_ROWS, LANES - n_dt), f32)], axis=1).astype(bf16)


def _regroup_w_in(w_in):
    depth, rows, cols = w_in.shape
    return pl.pallas_call(
        _w_in_body,
        grid=(depth, rows // W_IN_ROWS),
        in_specs=[pl.BlockSpec((None, W_IN_ROWS, cols), lambda l, i: (l, i, 0))],
        out_specs=pl.BlockSpec((None, W_IN_ROWS, IN_COLS), lambda l, i: (l, i, 0)),
        out_shape=jax.ShapeDtypeStruct((depth, rows, IN_COLS), bf16),
        compiler_params=pltpu.CompilerParams(dimension_semantics=("arbitrary", "arbitrary"),
                                             vmem_limit_bytes=VMEM_LIMIT_BYTES),
        name="w_in_regroup",
    )(w_in)


def _layer_math(prompt, T, x, w, st, o, scr):
    (ng_ref, w_in_ref, cw_ssd_ref, cb_ssd_ref, dtb_ref, alog_ref, dfull_ref, sng_ref,
     cw_lru_ref, cb_lru_ref, lru_w_ref, lru_b_ref, lam_ref,
     tab_ref, bb_ref, cc_ref, s5d_ref, glu_w_ref, glu_b_ref, w_out_ref, _) = w
    o_ssd_ref, o_cssd_ref, o_lru_ref, o_clru_ref, o_s5r_ref, o_s5i_ref = o
    if prompt:
        h_ssd, prev_ssd, prev_lru, a_s, b_s, bur_s, bui_s, lru_c, s5_cr, s5_ci = scr
    else:
        h0_ssd_hbm, c0_ssd_ref, h0_lru_ref, c0_lru_ref, h0_s5r_ref, h0_s5i_ref = st
        a_s, b_s, bur_s, bui_s, c_s, bm_s, xw_s, yoff_s, eac_s, h_in, h_out, sem_in, sem_out = scr
    nseq = T // SEQ_S
    if not prompt:
        layer = pl.program_id(0)
        seq0 = pl.program_id(1) * nseq

        def in_copy(i, slot):
            return pltpu.make_async_copy(h0_ssd_hbm.at[layer, pl.ds(seq0 + i * STATE_SEQS, STATE_SEQS)],
                                         h_in.at[slot], sem_in.at[slot])

        def out_copy(i, slot):
            return pltpu.make_async_copy(h_out.at[slot],
                                         o_ssd_ref.at[layer, pl.ds(seq0 + i * STATE_SEQS, STATE_SEQS)],
                                         sem_out.at[slot])

        for j in range(STATE_BUFS - 1):
            in_copy(j, j).start()
    Q = CHUNK if prompt else SUBTILE_S

    hn = _rms(x, ng_ref[...]).astype(bf16)

    def proj(lo, hi):
        return _dot(hn, w_in_ref[:, lo:hi])

    def sub_iota(n):
        return lax.broadcasted_iota(jnp.int32, (SUBLANES, n), 0)

    def conv_taps(halo, rs, cw_ref, cb_ref):
        ext = jnp.concatenate([halo, rs], axis=0)
        n = rs.shape[0]
        acc = cb_ref[...] + cw_ref[3:4, :] * rs
        for j in range(1, CONV_WIDTH):
            acc = acc + cw_ref[3 - j:4 - j, :] * ext[HALO - SUBLANES * j:HALO - SUBLANES * j + n, :]
        return acc

    def conv(raw, prev_ref, c0_ref, cw_ref, cb_ref, o_ref):
        cdim = raw.shape[1]
        if not prompt:
            outs = []
            for s in range(T // Q):
                rs = raw[Q * s:Q * (s + 1), :]
                o_ref[s] = rs[Q - HALO:, :].reshape(CONV_WIDTH - 1, SUBLANES, cdim)
                outs.append(conv_taps(c0_ref[s].reshape(HALO, cdim), rs, cw_ref, cb_ref))
            return jnp.concatenate(outs, axis=0)
        first = sub_iota(cdim) == 0
        tail = prev_ref[...]
        outs = []
        for r0 in range(0, T, CHUNK):
            rs = raw[r0:r0 + CHUNK, :]
            cur = rs[CHUNK - HALO:, :]
            halo = jnp.concatenate(
                [jnp.where(first, pltpu.roll(tail[SUBLANES * k:SUBLANES * (k + 1), :], 1, 0),
                           pltpu.roll(cur[SUBLANES * k:SUBLANES * (k + 1), :], 1, 0))
                 for k in range(CONV_WIDTH - 1)], axis=0)
            outs.append(conv_taps(halo, rs, cw_ref, cb_ref))
            tail = cur
        prev_ref[...] = tail
        for k in range(CONV_WIDTH - 1):
            o_ref[k:k + 1, :] = tail[SUBLANES * k + SUBLANES - 1:SUBLANES * (k + 1), :]
        return jnp.concatenate(outs, axis=0)

    row = lax.broadcasted_iota(jnp.int32, (Q, Q), 0)
    col = lax.broadcasted_iota(jnp.int32, (Q, Q), 1)
    if prompt:
        def local_time(i):
            return jnp.bitwise_or(jnp.left_shift(jnp.bitwise_and(i, SUBLANES - 1), SEG_SHIFT),
                                  jnp.right_shift(i, SUB_SHIFT))
        causal = local_time(row) >= local_time(col)
    else:
        same_seq = jnp.bitwise_and(row, SUBLANES - 1) == jnp.bitwise_and(col, SUBLANES - 1)
        causal = jnp.logical_and(same_seq, jnp.right_shift(row, SUB_SHIFT) >= jnp.right_shift(col, SUB_SHIFT))
    tril = jnp.where(causal, 1.0, 0.0)
    lane_lo = lax.broadcasted_iota(jnp.int32, (Q, LANES), 1) < SSD_HEADDIM
    gsz = SSD_HPG * SSD_HEADDIM

    xbc = _silu(conv(proj(C_XBC, C_LRU), prev_ssd if prompt else None, None if prompt else c0_ssd_ref,
                     cw_ssd_ref, cb_ssd_ref, o_cssd_ref))
    dt_all = jax.nn.softplus(proj(C_DT, IN_COLS) + dtb_ref[...])
    a_neg = -jnp.exp(alog_ref[...])

    def ssd_state_io(rows_c, rows_xw, rows_b, e_last, h_get, h_set):
        outs = []
        for g in range(SSD_GROUPS):
            hp = h_get(g)
            outs.append(_dot_nt(rows_c[:, LANES * g:LANES * (g + 1)].astype(bf16), hp.astype(bf16)))
            sg = _dot_tn(rows_xw[:, gsz * g:gsz * (g + 1)].astype(bf16),
                         rows_b[:, LANES * g:LANES * (g + 1)].astype(bf16))
            dec = jnp.concatenate(
                [jnp.broadcast_to(e_last[:, SSD_HPG * g + k:SSD_HPG * g + k + 1], (SSD_HEADDIM, SSD_STATE))
                 for k in range(SSD_HPG)], axis=0)
            h_set(g, dec * hp + sg)
        return jnp.concatenate(outs, axis=1)

    def ssd_chunk(r0):
        xs = xbc[r0:r0 + Q, :SSD_DIM]
        bm = xbc[r0:r0 + Q, SSD_DIM:SSD_DIM + SSD_BC]
        cm = xbc[r0:r0 + Q, SSD_DIM + SSD_BC:]
        dt = dt_all[r0:r0 + Q, :]
        bm_b = bm.astype(bf16)
        cm_b = cm.astype(bf16)
        acum = _dot_exact(tril, dt * a_neg)
        acum_row = acum.T
        dt_row = dt.T
        scores = [_dot_nt(cm_b[:, LANES * g:LANES * (g + 1)], bm_b[:, LANES * g:LANES * (g + 1)])
                  for g in range(SSD_GROUPS)]
        if prompt:
            arow = acum_row[0:SSD_HEADS, :]
            w_row = jnp.exp(arow[:, Q - 1:Q] - arow) * dt_row[0:SSD_HEADS, :]
            e_end = jnp.exp(acum[Q - 1:Q, :])
            lane1 = lane_lo[0:1, :]
            bts = [bm[:, LANES * g:LANES * (g + 1)].T for g in range(SSD_GROUPS)]
            y_pairs = []
            for j in range(SSD_HEADS // 2):
                g = (2 * j) // SSD_HPG
                cm_g = cm[:, LANES * g:LANES * (g + 1)]
                bt_g = bts[g]
                lhs_y, lhs_s = [], []
                for h in (2 * j, 2 * j + 1):
                    colb = jnp.broadcast_to(acum[:, h:h + 1], (Q, LANES))
                    decay = jnp.exp(jnp.where(causal, colb - acum_row[h:h + 1, :], NEG))
                    lhs_y.append((scores[g] * decay * dt_row[h:h + 1, :]).astype(bf16))
                    lhs_s.append((bt_g * w_row[h:h + 1, :]).astype(bf16))
                for h in (2 * j, 2 * j + 1):
                    colb = jnp.broadcast_to(acum[:, h:h + 1], (Q, LANES))
                    lhs_y.append((jnp.exp(colb) * cm_g).astype(bf16))
                xp = xs[:, LANES * j:LANES * (j + 1)]
                hp = h_ssd[:, LANES * j:LANES * (j + 1)]
                xbd = jnp.concatenate([jnp.where(lane_lo, xp, 0.0), jnp.where(lane_lo, 0.0, xp)],
                                      axis=0).astype(bf16)
                hbd = jnp.concatenate([jnp.where(lane_lo, hp, 0.0), jnp.where(lane_lo, 0.0, hp)],
                                      axis=0).astype(bf16)
                y_pairs.append(_dot(jnp.concatenate(lhs_y, axis=1), jnp.concatenate([xbd, hbd], axis=0)))
                dec = jnp.where(lane1, jnp.broadcast_to(e_end[:, 2 * j:2 * j + 1], (1, LANES)),
                                jnp.broadcast_to(e_end[:, 2 * j + 1:2 * j + 2], (1, LANES)))
                h_ssd[:, LANES * j:LANES * (j + 1)] = dec * hp + _dot(jnp.concatenate(lhs_s, axis=1), xbd)
            return jnp.concatenate(y_pairs, axis=1) + dfull_ref[...] * xs

        eac = jnp.exp(acum)
        sel = jnp.where(col == jnp.bitwise_and(row, SUBLANES - 1) + (Q - SUBLANES), 1.0, 0.0)
        acum_end = _dot_exact(sel, acum)
        wgt = jnp.exp(acum_end - acum) * dt
        y_pairs, xw_pairs, ecol_pairs = [], [], []
        for j in range(SSD_HEADS // 2):
            g = (2 * j) // SSD_HPG
            ms = []
            for h in (2 * j, 2 * j + 1):
                diff = acum[:, h:h + 1] - acum_row[h:h + 1, :]
                decay = jnp.exp(jnp.where(causal, diff, NEG))
                ms.append((scores[g] * decay * dt_row[h:h + 1, :]).astype(bf16))
            xp = xs[:, LANES * j:LANES * (j + 1)]
            xbd = jnp.concatenate([jnp.where(lane_lo, xp, 0.0), jnp.where(lane_lo, 0.0, xp)], axis=0).astype(bf16)
            y_pairs.append(_dot(jnp.concatenate(ms, axis=1), xbd))
            xw_pairs.append(xp * _pair_expand(wgt, j, lane_lo))
            ecol_pairs.append(_pair_expand(eac, j, lane_lo))
        y_diag = jnp.concatenate(y_pairs, axis=1)
        xw = jnp.concatenate(xw_pairs, axis=1)
        ecol = jnp.concatenate(ecol_pairs, axis=1)

        c_s[r0:r0 + Q, :] = _dot(perm_b, cm_b)
        bm_s[r0:r0 + Q, :] = _dot(perm_b, bm_b)
        xw_s[r0:r0 + Q, :] = _dot(perm_b, xw.astype(bf16))
        eac_s[r0:r0 + Q, :] = eac
        return y_diag + dfull_ref[...] * xs, ecol

    if prompt:
        y = jnp.concatenate([ssd_chunk(r0) for r0 in range(0, T, Q)], axis=0)
    else:
        to_seq = jnp.bitwise_or(jnp.left_shift(jnp.bitwise_and(row, SUBLANES - 1), SUB_SHIFT),
                                jnp.right_shift(row, SUB_SHIFT)) == col
        perm_b = jnp.where(to_seq, 1.0, 0.0).astype(bf16)
        parts = [ssd_chunk(r0) for r0 in range(0, T, Q)]

        ngrp = nseq // STATE_SEQS

        def seq_step(i, carry):
            slot = jnp.bitwise_and(i, STATE_BUFS - 1)
            ahead = i + (STATE_BUFS - 1)

            @pl.when(ahead < ngrp)
            def _():
                in_copy(ahead, jnp.bitwise_and(ahead, STATE_BUFS - 1)).start()

            in_copy(i, slot).wait()

            @pl.when(i >= STATE_BUFS)
            def _():
                out_copy(i - STATE_BUFS, slot).wait()

            for q in range(STATE_SEQS):
                sq = i * STATE_SEQS + q
                s0 = pl.multiple_of(sq * SEQ_S, SEQ_S)

                def h_get(g):
                    return h_in[slot, q, pl.ds(gsz * g, gsz), :]

                def h_set(g, v):
                    h_out[slot, q, pl.ds(gsz * g, gsz), :] = v

                e_row = jnp.left_shift(jnp.right_shift(sq, SUB_SHIFT), SUBTILE_SHIFT) + (Q - SUBLANES) \
                    + jnp.bitwise_and(sq, SUBLANES - 1)
                yoff_s[pl.ds(s0, SEQ_S), :] = ssd_state_io(
                    c_s[pl.ds(s0, SEQ_S), :], xw_s[pl.ds(s0, SEQ_S), :], bm_s[pl.ds(s0, SEQ_S), :],
                    eac_s[pl.ds(e_row, 1), :], h_get, h_set)
            out_copy(i, slot).start()
            return carry

        lax.fori_loop(0, ngrp, seq_step, 0)
        for j in range(STATE_BUFS):
            out_copy(ngrp - STATE_BUFS + j, j).wait()
        perm_f = jnp.where(to_seq, 1.0, 0.0)
        y = jnp.concatenate(
            [part + _dot_exact(perm_f, yoff_s[r0:r0 + Q, :]) * ecol
             for r0, (part, ecol) in zip(range(0, T, Q), parts)], axis=0)
    y_ssd = _rms(y * _silu(proj(C_Z, C_XBC)), sng_ref[...])

    xr = conv(proj(C_LRU, C_LRU_G), prev_lru if prompt else None, None if prompt else c0_lru_ref,
              cw_lru_ref, cb_lru_ref, o_clru_ref)
    xr_b = xr.astype(bf16)
    hl = LRU_DIM // 2
    gates = jnp.concatenate([_dot(xr_b[:, hl * (k % 2):hl * (k % 2 + 1)], lru_w_ref[k]) for k in range(4)],
                            axis=1) + lru_b_ref[...]
    r_gate = jax.nn.sigmoid(gates[:, :LRU_DIM])
    i_gate = jax.nn.sigmoid(gates[:, LRU_DIM:])
    log_a = -LRU_C * r_gate * jax.nn.softplus(-lam_ref[...])
    a_t = jnp.exp(log_a)
    gain = jnp.sqrt(jnp.maximum(-jnp.tanh(log_a) * (a_t * a_t + 1.0), 0.0))
    a_s[...] = a_t
    b_s[...] = gain * i_gate * xr

    def vrow(ref, r0, k):
        return ref[r0 + SUBLANES * k:r0 + SUBLANES * (k + 1), :]

    def set_vrow(ref, r0, k, v):
        ref[r0 + SUBLANES * k:r0 + SUBLANES * (k + 1), :] = v

    if prompt:
        sub = sub_iota(LRU_DIM)
        carry = lru_c[...]
        for r0 in range(0, T, CHUNK):
            acc_a, acc_h = vrow(a_s, r0, 0), vrow(b_s, r0, 0)
            for k in range(1, SEG):
                a_k = vrow(a_s, r0, k)
                acc_h = a_k * acc_h + vrow(b_s, r0, k)
                acc_a = a_k * acc_a
                set_vrow(a_s, r0, k, acc_a)
                set_vrow(b_s, r0, k, acc_h)
            alpha = jnp.where(sub == 0, 0.0, pltpu.roll(acc_a, 1, 0))
            beta = jnp.where(sub == 0, jnp.broadcast_to(carry, (SUBLANES, LRU_DIM)), pltpu.roll(acc_h, 1, 0))
            for d in (1, 2, 4):
                a_sh = jnp.where(sub >= d, pltpu.roll(alpha, d, 0), 1.0)
                b_sh = jnp.where(sub >= d, pltpu.roll(beta, d, 0), 0.0)
                beta = alpha * b_sh + beta
                alpha = alpha * a_sh
            carry = (acc_a * beta + acc_h)[SUBLANES - 1:SUBLANES, :]
            for k in range(SEG):
                set_vrow(b_s, r0, k, vrow(b_s, r0, k) + vrow(a_s, r0, k) * beta)
        lru_c[...] = carry
        o_lru_ref[...] = carry
    else:
        for s in range(T // Q):
            h = h0_lru_ref[SUBLANES * s:SUBLANES * (s + 1), :]
            for k in range(Q // SUBLANES):
                h = vrow(a_s, Q * s, k) * h + vrow(b_s, Q * s, k)
                set_vrow(b_s, Q * s, k, h)
            o_lru_ref[SUBLANES * s:SUBLANES * (s + 1), :] = h
    y_lru = b_s[...] * _silu(proj(C_LRU_G, C_S5))

    u = proj(C_S5, C_S5_G)
    u_b = u.astype(bf16)
    half = S5_DIM // 2
    for k in range(2):
        uk = u_b[:, half * k:half * (k + 1)]
        bur_s[:, S5_HALF * k:S5_HALF * (k + 1)] = _dot(uk, bb_ref[k])
        bui_s[:, S5_HALF * k:S5_HALF * (k + 1)] = _dot(uk, bb_ref[2 + k])

    def tab(r0):
        return tab_ref[0, r0:r0 + SUBLANES, :], tab_ref[1, r0:r0 + SUBLANES, :]

    def cmul_add(pr, pi, xr, xi, yr, yi):
        return pr * xr - pi * xi + yr, pr * xi + pi * xr + yi

    ar, ai = tab(TAB_A)
    if prompt:
        sub = sub_iota(S5_FLAT)
        c_r, c_i = s5_cr[...], s5_ci[...]
        for r0 in range(0, T, CHUNK):
            hr, hi = vrow(bur_s, r0, 0), vrow(bui_s, r0, 0)
            for k in range(1, SEG):
                hr, hi = cmul_add(ar, ai, hr, hi, vrow(bur_s, r0, k), vrow(bui_s, r0, k))
                set_vrow(bur_s, r0, k, hr)
                set_vrow(bui_s, r0, k, hi)
            er = jnp.where(sub == 0, jnp.broadcast_to(c_r, (SUBLANES, S5_FLAT)), pltpu.roll(hr, 1, 0))
            ei = jnp.where(sub == 0, jnp.broadcast_to(c_i, (SUBLANES, S5_FLAT)), pltpu.roll(hi, 1, 0))
            for t, d in enumerate((1, 2, 4)):
                qr, qi = tab(TAB_Q + t * SUBLANES)
                er, ei = cmul_add(qr, qi, pltpu.roll(er, d, 0), pltpu.roll(ei, d, 0), er, ei)
            sr, si = tab(TAB_ASEG)
            nr, ni = cmul_add(sr, si, er, ei, hr, hi)
            c_r, c_i = nr[SUBLANES - 1:SUBLANES, :], ni[SUBLANES - 1:SUBLANES, :]
            for k in range(SEG):
                pr, pi = tab(TAB_PW + k * SUBLANES)
                vr, vi = cmul_add(pr, pi, er, ei, vrow(bur_s, r0, k), vrow(bui_s, r0, k))
                set_vrow(bur_s, r0, k, vr)
                set_vrow(bui_s, r0, k, vi)
        s5_cr[...] = c_r
        s5_ci[...] = c_i
        o_s5r_ref[...] = c_r
        o_s5i_ref[...] = c_i
    else:
        for s in range(T // Q):
            rows = slice(SUBLANES * s, SUBLANES * (s + 1))
            hr, hi = h0_s5r_ref[rows, :], h0_s5i_ref[rows, :]
            for k in range(Q // SUBLANES):
                hr, hi = cmul_add(ar, ai, hr, hi, vrow(bur_s, Q * s, k), vrow(bui_s, Q * s, k))
                set_vrow(bur_s, Q * s, k, hr)
                set_vrow(bui_s, Q * s, k, hi)
            o_s5r_ref[rows, :] = hr
            o_s5i_ref[rows, :] = hi
    ys = []
    for k in range(2):
        hk = jnp.concatenate([bur_s[:, S5_HALF * k:S5_HALF * (k + 1)].astype(bf16),
                              bui_s[:, S5_HALF * k:S5_HALF * (k + 1)].astype(bf16)], axis=1)
        ys.append(_dot(hk, cc_ref[k]))
    ys5 = jnp.concatenate(ys, axis=1) + s5d_ref[...] * u
    ys5 = jax.nn.gelu(ys5)
    ys5 = ys5 * jax.nn.sigmoid(_dot(ys5.astype(bf16), glu_w_ref[...]) + glu_b_ref[...])
    y_s5 = ys5 * _silu(proj(C_S5_G, C_DT))

    ycat = jnp.concatenate([y_ssd.astype(bf16), y_lru.astype(bf16), y_s5.astype(bf16)], axis=1)
    return x + _dot(ycat, w_out_ref[...])


def _prompt_body(final, *refs):
    x_ref = refs[0]
    w = refs[1:1 + N_WEIGHTS]
    y_ref = refs[1 + N_WEIGHTS]
    o = refs[2 + N_WEIGHTS:8 + N_WEIGHTS]
    scr = refs[8 + N_WEIGHTS:]
    h_ssd, prev_ssd, prev_lru = scr[:3]
    lru_c, s5_cr, s5_ci = scr[-3:]

    @pl.when(pl.program_id(1) == 0)
    def _():
        h_ssd[...] = jnp.zeros_like(h_ssd)
        prev_ssd[...] = jnp.zeros_like(prev_ssd)
        prev_lru[...] = jnp.zeros_like(prev_lru)
        lru_c[...] = jnp.zeros_like(lru_c)
        s5_cr[...] = jnp.zeros_like(s5_cr)
        s5_ci[...] = jnp.zeros_like(s5_ci)

    for k in range(TILES_PER_STEP):
        rows = slice(TILE_P * k, TILE_P * (k + 1))
        scr_k = scr[:3] + scr[3 + 4 * k:7 + 4 * k] + scr[3 + 4 * TILES_PER_STEP:]
        out = _layer_math(True, TILE_P, x_ref[rows, :], w, None, o, scr_k)
        if final:
            out = _rms(out, w[-1][...])
        y_ref[rows, :] = out

    @pl.when(pl.program_id(1) == pl.num_programs(1) - 1)
    def _():
        o[0][...] = h_ssd[...].T


def _sample_body(*refs):
    x_ref = refs[0]
    st = refs[1:7]
    w = refs[7:7 + N_WEIGHTS]
    y_ref = refs[7 + N_WEIGHTS]
    o = refs[8 + N_WEIGHTS:14 + N_WEIGHTS]
    x_all = refs[14 + N_WEIGHTS]
    scr = refs[15 + N_WEIGHTS:]
    layer = pl.program_id(0)
    last_layer = layer == pl.num_programs(0) - 1
    r0 = pl.multiple_of(pl.program_id(1) * TILE_S, TILE_S)

    @pl.when(layer == 0)
    def _():
        x_all[pl.ds(r0, TILE_S), :] = x_ref[...]

    out = _layer_math(False, TILE_S, x_all[pl.ds(r0, TILE_S), :], w, st, o, scr)
    x_all[pl.ds(r0, TILE_S), :] = out

    @pl.when(last_layer)
    def _():
        y_ref[...] = _rms(out, w[-1][...])

    @pl.when(jnp.logical_not(last_layer))
    def _():
        y_ref[...] = out


def _prompt_call(layer, final, x, weights):
    T = TILE_P
    nb, seq, _ = x.shape

    def wspec(a):
        nd = a.ndim - 1
        return pl.BlockSpec((None,) + a.shape[1:], lambda b, c: (layer,) + (0,) * nd, pipeline_mode=pl.Buffered(1))

    def st(shape):
        nd = len(shape)
        return pl.BlockSpec((None,) + shape, lambda b, c: (b,) + (0,) * nd)

    step_rows = T * TILES_PER_STEP
    x_spec = pl.BlockSpec((None, step_rows, D_MODEL), lambda b, c: (b, c, 0))
    out_specs = [x_spec, st((SSD_DIM, SSD_STATE)), st((CONV_WIDTH - 1, SSD_CONV_DIM)), st((1, LRU_DIM)),
                 st((CONV_WIDTH - 1, LRU_DIM)), st((1, S5_FLAT)), st((1, S5_FLAT))]
    out_shape = [jax.ShapeDtypeStruct(x.shape, f32),
                 jax.ShapeDtypeStruct((nb, SSD_DIM, SSD_STATE), f32),
                 jax.ShapeDtypeStruct((nb, CONV_WIDTH - 1, SSD_CONV_DIM), f32),
                 jax.ShapeDtypeStruct((nb, 1, LRU_DIM), f32),
                 jax.ShapeDtypeStruct((nb, CONV_WIDTH - 1, LRU_DIM), f32),
                 jax.ShapeDtypeStruct((nb, 1, S5_FLAT), f32),
                 jax.ShapeDtypeStruct((nb, 1, S5_FLAT), f32)]
    scratch = [pltpu.VMEM((SSD_STATE, SSD_DIM), f32),
               pltpu.VMEM((HALO, SSD_CONV_DIM), f32),
               pltpu.VMEM((HALO, LRU_DIM), f32)]
    scratch += [pltpu.VMEM((T, LRU_DIM), f32), pltpu.VMEM((T, LRU_DIM), f32),
                pltpu.VMEM((T, S5_FLAT), f32), pltpu.VMEM((T, S5_FLAT), f32)] * TILES_PER_STEP
    scratch += [pltpu.VMEM((1, LRU_DIM), f32), pltpu.VMEM((1, S5_FLAT), f32), pltpu.VMEM((1, S5_FLAT), f32)]
    return pl.pallas_call(
        functools.partial(_prompt_body, final),
        grid=(nb, seq // step_rows), in_specs=[x_spec] + [wspec(a) for a in weights],
        out_specs=out_specs, out_shape=out_shape, scratch_shapes=scratch,
        compiler_params=pltpu.CompilerParams(dimension_semantics=("arbitrary", "arbitrary"),
                                             vmem_limit_bytes=VMEM_LIMIT_BYTES),
        name="layer_prompt",
    )(x, *weights)


def _sample_call(x, states, weights):
    T = TILE_S
    rows = x.shape[0]
    depth = weights[0].shape[0]
    nseq = T // SEQ_S

    def wspec(a):
        nd = a.ndim - 1
        return pl.BlockSpec((None,) + a.shape[1:], lambda l, i: (l,) + (0,) * nd, pipeline_mode=pl.Buffered(1))

    def st(a):
        nd = a.ndim - 2
        if a.ndim == 4:
            return pl.BlockSpec(memory_space=pl.ANY)
        if a.ndim == 6:
            return pl.BlockSpec((None, None) + a.shape[2:], lambda l, i: (l, i) + (0,) * nd)
        return pl.BlockSpec((None, nseq) + a.shape[2:], lambda l, i: (l, i) + (0,) * nd)

    x_spec = pl.BlockSpec((T, D_MODEL), lambda l, i: (i, 0))
    st_specs = [st(a) for a in states]
    scratch = [pltpu.VMEM((rows, D_MODEL), f32),
               pltpu.VMEM((T, LRU_DIM), f32), pltpu.VMEM((T, LRU_DIM), f32),
               pltpu.VMEM((T, S5_FLAT), f32), pltpu.VMEM((T, S5_FLAT), f32),
               pltpu.VMEM((T, SSD_BC), f32), pltpu.VMEM((T, SSD_BC), f32),
               pltpu.VMEM((T, SSD_DIM), f32), pltpu.VMEM((T, SSD_DIM), f32),
               pltpu.VMEM((T, LANES), f32),
               pltpu.VMEM((STATE_BUFS, STATE_SEQS, SSD_DIM, SSD_STATE), f32),
               pltpu.VMEM((STATE_BUFS, STATE_SEQS, SSD_DIM, SSD_STATE), f32),
               pltpu.SemaphoreType.DMA((STATE_BUFS,)), pltpu.SemaphoreType.DMA((STATE_BUFS,))]
    return pl.pallas_call(
        _sample_body,
        grid=(depth, rows // T), in_specs=[x_spec] + st_specs + [wspec(a) for a in weights],
        out_specs=[pl.BlockSpec((None, T, D_MODEL), lambda l, i: (l, i, 0))] + st_specs,
        out_shape=[jax.ShapeDtypeStruct((depth,) + x.shape, f32)] + [jax.ShapeDtypeStruct(a.shape, f32) for a in states],
        scratch_shapes=scratch,
        compiler_params=pltpu.CompilerParams(dimension_semantics=("arbitrary", "arbitrary"),
                                             vmem_limit_bytes=VMEM_LIMIT_BYTES),
        name="layers_sample",
    )(x, *states, *weights)


def _block_diag(blocks):
    *lead, n, r, c = blocks.shape
    eye = jnp.eye(n, dtype=blocks.dtype)
    return (blocks[..., :, :, None, :] * eye[:, None, :, None]).reshape(*lead, n * r, n * c)


def _pad_lanes(v):
    return jnp.pad(v, [(0, 0)] * (v.ndim - 1) + [(0, LANES - v.shape[-1])])


def kernel(x_prompt, x_sample, state_ssd, state_ssd_conv, state_lru, state_lru_conv, state_s5_re, state_s5_im, norm_g, w_in, ssd_conv_w, ssd_conv_b, ssd_dt_bias, ssd_a_log, ssd_d, ssd_norm_g, lru_conv_w, lru_conv_b, lru_wa, lru_ba, lru_wx, lru_bx, lru_lambda, s5_lambda_re, s5_lambda_im, s5_log_dt, s5_b_re, s5_b_im, s5_c_re, s5_c_im, s5_d, s5_glu_w, s5_glu_b, w_out, final_norm_g):
    depth = w_in.shape[0]
    nbp = x_prompt.shape[0]
    nbs, ls, _ = x_sample.shape
    assert ls == SEQ_S and x_prompt.shape[1] % (TILE_P * TILES_PER_STEP) == 0 and (nbs * ls) % TILE_S == 0

    tab, bbar_re, bbar_im = _s5_prep(s5_lambda_re.astype(f32), s5_lambda_im.astype(f32), s5_log_dt.astype(f32),
                                     s5_b_re.astype(f32), s5_b_im.astype(f32))

    def row(v, n):
        return v.astype(f32).reshape(depth, 1, n)

    assert w_in.shape[1:] == (D_MODEL, C_DT + SSD_HEADS)
    w_in_r = _regroup_w_in(w_in.astype(f32))

    def halves(v):
        return _block_diag(v.reshape(depth, 2, S5_NGROUPS // 2, S5_GROUP, S5_STATE))

    bb = jnp.concatenate([halves(bbar_re), halves(bbar_im)], axis=1).astype(bf16)

    def chalves(v):
        return _block_diag(jnp.transpose(v.astype(f32), (0, 1, 3, 2)).reshape(depth, 2, S5_NGROUPS // 2, S5_STATE, S5_GROUP))

    cc = jnp.concatenate([chalves(s5_c_re), -chalves(s5_c_im)], axis=2).astype(bf16)
    weights = (
        row(norm_g, D_MODEL), w_in_r,
        ssd_conv_w.astype(f32), row(ssd_conv_b, SSD_CONV_DIM),
        _pad_lanes(row(ssd_dt_bias, SSD_HEADS)), _pad_lanes(row(ssd_a_log, SSD_HEADS)),
        jnp.repeat(ssd_d.astype(f32), SSD_HEADDIM, axis=-1).reshape(depth, 1, SSD_DIM),
        row(ssd_norm_g, SSD_DIM),
        lru_conv_w.astype(f32), row(lru_conv_b, LRU_DIM),
        jnp.concatenate([_block_diag(v.astype(f32).reshape(depth, 2, v.shape[1] // 2, *v.shape[2:]))
                         for v in (lru_wa, lru_wx)], axis=1).astype(bf16),
        jnp.concatenate([lru_ba, lru_bx], axis=-1).astype(f32).reshape(depth, 1, 2 * LRU_DIM),
        row(lru_lambda, LRU_DIM),
        tab, bb, cc, row(s5_d, S5_DIM),
        s5_glu_w.astype(bf16), row(s5_glu_b, S5_DIM),
        w_out.astype(bf16),
        jnp.broadcast_to(final_norm_g.astype(f32).reshape(1, 1, D_MODEL), (depth, 1, D_MODEL)),
    )

    seq_p = x_prompt.shape[1]
    xp = jnp.swapaxes(x_prompt.astype(f32).reshape(nbp, seq_p // CHUNK, SUBLANES, SEG, D_MODEL), 2, 3)
    xp = xp.reshape(nbp, seq_p, D_MODEL)
    outs_p = [[] for _ in range(6)]
    for i in range(depth):
        res = _prompt_call(i, i == depth - 1, xp, weights)
        xp = res[0]
        for j in range(6):
            outs_p[j].append(res[1 + j])
    y_prompt = jnp.swapaxes(xp.reshape(nbp, seq_p // CHUNK, SEG, SUBLANES, D_MODEL), 2, 3).reshape(nbp, seq_p, D_MODEL)

    ntile = nbs // SUBLANES
    nsub = TILE_S // SUBTILE_S

    def conv_in(v):
        v = jnp.swapaxes(v.astype(f32).reshape(depth, ntile, SUBLANES, CONV_WIDTH - 1, v.shape[-1]), 2, 3)
        return v.reshape(depth, ntile // nsub, nsub, CONV_WIDTH - 1, SUBLANES, v.shape[-1])

    def conv_out(v, dtype):
        v = v.reshape(depth, ntile, CONV_WIDTH - 1, SUBLANES, v.shape[-1])
        return jnp.swapaxes(v, 2, 3).reshape(depth, nbs, CONV_WIDTH - 1, v.shape[-1]).astype(dtype)

    xs = jnp.swapaxes(x_sample.astype(f32).reshape(ntile, SUBLANES, ls, D_MODEL), 1, 2).reshape(nbs * ls, D_MODEL)
    states_s = (state_ssd.astype(f32).reshape(depth, nbs, SSD_DIM, SSD_STATE), conv_in(state_ssd_conv),
                state_lru.astype(f32), conv_in(state_lru_conv),
                state_s5_re.astype(f32).reshape(depth, nbs, S5_FLAT), state_s5_im.astype(f32).reshape(depth, nbs, S5_FLAT))
    weights_s = tuple(w[:, :, TAB_A:TAB_A + SUBLANES] if w is tab else w for w in weights)
    res_s = _sample_call(xs, states_s, weights_s)
    y_sample = jnp.swapaxes(res_s[0][depth - 1].reshape(ntile, ls, SUBLANES, D_MODEL), 1, 2).reshape(nbs, ls, D_MODEL)

    def stack(lst, shape, dtype):
        return jnp.stack(lst).reshape((depth,) + shape).astype(dtype)

    ssd_shape = (SSD_HEADS, SSD_HEADDIM, SSD_STATE)
    s5_shape = (S5_NGROUPS, S5_STATE)
    return (
        y_prompt.astype(x_prompt.dtype), y_sample.astype(x_sample.dtype),
        stack(outs_p[0], (nbp,) + ssd_shape, state_ssd.dtype),
        res_s[1].reshape((depth, nbs) + ssd_shape).astype(state_ssd.dtype),
        stack(outs_p[1], (nbp, CONV_WIDTH - 1, SSD_CONV_DIM), state_ssd_conv.dtype),
        conv_out(res_s[2], state_ssd_conv.dtype),
        stack(outs_p[2], (nbp, LRU_DIM), state_lru.dtype), res_s[3].astype(state_lru.dtype),
        stack(outs_p[3], (nbp, CONV_WIDTH - 1, LRU_DIM), state_lru_conv.dtype),
        conv_out(res_s[4], state_lru_conv.dtype),
        stack(outs_p[4], (nbp,) + s5_shape, state_s5_re.dtype),
        res_s[5].reshape((depth, nbs) + s5_shape).astype(state_s5_re.dtype),
        stack(outs_p[5], (nbp,) + s5_shape, state_s5_im.dtype),
        res_s[6].reshape((depth, nbs) + s5_shape).astype(state_s5_im.dtype),
    )
```

```python
import functools

import jax
import jax.numpy as jnp
from jax import lax
from jax.experimental import pallas as pl
from jax.experimental.pallas import tpu as pltpu

f32 = jnp.float32
bf16 = jnp.bfloat16

D_MODEL = 1024
CONV_WIDTH = 4
SSD_DIM = 1024
SSD_HEADDIM = 64
SSD_HEADS = 16
SSD_GROUPS = 4
SSD_HPG = 4
SSD_STATE = 128
SSD_BC = SSD_GROUPS * SSD_STATE
SSD_CONV_DIM = SSD_DIM + 2 * SSD_BC
LRU_DIM = 512
LRU_C = 8.0
S5_DIM = 512
S5_GROUP = 16
S5_NGROUPS = 32
S5_STATE = 64
S5_FLAT = S5_NGROUPS * S5_STATE
S5_HALF = S5_FLAT // 2
EPS = 1e-6

LANES = 128
SUBLANES = 8
CHUNK = 128
TILE_P = 256
TILES_PER_STEP = 1
TILE_S = 256
SUBTILE_S = 64
SUBTILE_SHIFT = 6
SEQ_S = 8
STATE_BUFS = 4
STATE_SEQS = 2
NEG = -1e30

SEG = CHUNK // SUBLANES
SEG_SHIFT = 4
SUB_SHIFT = 3
HALO = (CONV_WIDTH - 1) * SUBLANES

TAB_A = 0
TAB_Q = TAB_A + SUBLANES
TAB_ASEG = TAB_Q + 3 * SUBLANES
TAB_PW = TAB_ASEG + SUBLANES
TAB_ROWS = TAB_PW + SEG * SUBLANES

C_Z = 0
C_XBC = C_Z + SSD_DIM
C_LRU = C_XBC + SSD_CONV_DIM
C_LRU_G = C_LRU + LRU_DIM
C_S5 = C_LRU_G + LRU_DIM
C_S5_G = C_S5 + S5_DIM
C_DT = C_S5_G + S5_DIM
IN_COLS = C_DT + LANES

VMEM_LIMIT_BYTES = 56 * 1024 * 1024

N_WEIGHTS = 21


def _rms(x, g):
    return x * lax.rsqrt(jnp.mean(x * x, axis=-1, keepdims=True) + EPS) * g


def _silu(x):
    return x * jax.nn.sigmoid(x)


def _dot(a, b):
    return jnp.dot(a, b, preferred_element_type=f32)


def _dot_nt(a, b):
    return lax.dot_general(a, b, (((1,), (1,)), ((), ())), preferred_element_type=f32)


def _dot_tn(a, b):
    return lax.dot_general(a, b, (((0,), (0,)), ((), ())), preferred_element_type=f32)


def _dot_exact(a, b):
    return jnp.dot(a, b, preferred_element_type=f32, precision=lax.Precision.HIGHEST)


def _pair_expand(v, j, lane_lo):
    q = v.shape[0]
    lo = jnp.broadcast_to(v[:, 2 * j:2 * j + 1], (q, LANES))
    hi = jnp.broadcast_to(v[:, 2 * j + 1:2 * j + 2], (q, LANES))
    return jnp.where(lane_lo, lo, hi)


def _s5_prep_body(lre_ref, lim_ref, ldt_ref, lre_g_ref, lim_g_ref, ldt_g_ref,
                  bre_ref, bim_ref, tre_ref, tim_ref, bbre_ref, bbim_ref):
    def abar(lre, lim, ldt):
        delta = jnp.exp(ldt)
        mag = jnp.exp(lre * delta)
        return mag * jnp.cos(lim * delta), mag * jnp.sin(lim * delta)

    ar, ai = abar(lre_ref[...], lim_ref[...], ldt_ref[...])

    def cmul(xr, xi, yr, yi):
        return xr * yr - xi * yi, xr * yi + xi * yr

    pw = [(ar, ai)]
    for _ in range(SEG - 1):
        pw.append(cmul(*pw[-1], ar, ai))
    seg = [pw[SEG - 1]]
    for _ in range(2):
        seg.append(cmul(*seg[-1], *seg[-1]))
    zero = jnp.zeros_like(ar)

    def put(i, v):
        tre_ref[i] = v[0]
        tim_ref[i] = v[1]

    for r in range(SUBLANES):
        put(TAB_A + r, pw[0])
        put(TAB_ASEG + r, seg[0])
        for t, d in enumerate((1, 2, 4)):
            put(TAB_Q + t * SUBLANES + r, seg[t] if r >= d else (zero, zero))
        for k in range(SEG):
            put(TAB_PW + k * SUBLANES + r, pw[k])

    lre, lim = lre_g_ref[...], lim_g_ref[...]
    ar, ai = abar(lre, lim, ldt_g_ref[...])
    denom = lre * lre + lim * lim
    nr = ar - 1.0
    ni = ai

    def per_channel(v):
        return jnp.broadcast_to(v[:, None, :], (S5_NGROUPS, S5_GROUP, S5_STATE)).reshape(S5_DIM, S5_STATE)

    coef_re = per_channel((nr * lre + ni * lim) / denom)
    coef_im = per_channel((ni * lre - nr * lim) / denom)
    bre, bim = bre_ref[...], bim_ref[...]
    bbre_ref[...] = coef_re * bre - coef_im * bim
    bbim_ref[...] = coef_re * bim + coef_im * bre


def _s5_prep(lam_re, lam_im, log_dt, b_re, b_im):
    depth = lam_re.shape[0]
    rows_c = S5_FLAT // LANES
    ldt = jnp.broadcast_to(log_dt[:, :, None], (depth, S5_NGROUPS, S5_STATE))

    def bt(v):
        return jnp.transpose(v, (0, 1, 3, 2)).reshape(depth, S5_DIM, S5_STATE)

    cspec = pl.BlockSpec((None, rows_c, LANES), lambda i: (i, 0, 0))
    gspec = pl.BlockSpec((None, S5_NGROUPS, S5_STATE), lambda i: (i, 0, 0))
    rspec = pl.BlockSpec((None, S5_DIM, S5_STATE), lambda i: (i, 0, 0))
    tspec = pl.BlockSpec((None, TAB_ROWS, rows_c, LANES), lambda i: (i, 0, 0, 0))
    tre, tim, bbre, bbim = pl.pallas_call(
        _s5_prep_body,
        grid=(depth,),
        in_specs=[cspec, cspec, cspec, gspec, gspec, gspec, rspec, rspec],
        out_specs=[tspec, tspec, rspec, rspec],
        out_shape=[jax.ShapeDtypeStruct((depth, TAB_ROWS, rows_c, LANES), f32)] * 2
        + [jax.ShapeDtypeStruct((depth, S5_DIM, S5_STATE), f32)] * 2,
        name="s5_prep",
    )(lam_re.reshape(depth, rows_c, LANES), lam_im.reshape(depth, rows_c, LANES),
      ldt.reshape(depth, rows_c, LANES), lam_re, lam_im, ldt, bt(b_re), bt(b_im))
    tab = jnp.stack([tre, tim], axis=1).reshape(depth, 2, TAB_ROWS, S5_FLAT)
    return tab, bbre, bbim


def _layer_math(prompt, T, x, w, st, o, scr):
    (ng_ref, w_in_ref, cw_ssd_ref, cb_ssd_ref, dtb_ref, alog_ref, dfull_ref, sng_ref,
     cw_lru_ref, cb_lru_ref, lru_w_ref, lru_b_ref, lam_ref,
     tab_ref, bb_ref, cc_ref, s5d_ref, glu_w_ref, glu_b_ref, w_out_ref, _) = w
    o_ssd_ref, o_cssd_ref, o_lru_ref, o_clru_ref, o_s5r_ref, o_s5i_ref = o
    if prompt:
        h_ssd, prev_ssd, prev_lru, a_s, b_s, bur_s, bui_s, lru_c, s5_cr, s5_ci = scr
    else:
        h0_ssd_hbm, c0_ssd_ref, h0_lru_ref, c0_lru_ref, h0_s5r_ref, h0_s5i_ref = st
        a_s, b_s, bur_s, bui_s, c_s, bm_s, xw_s, yoff_s, eac_s, h_in, h_out, sem_in, sem_out = scr
    nseq = T // SEQ_S
    if not prompt:
        layer = pl.program_id(0)
        seq0 = pl.program_id(1) * nseq

        def in_copy(i, slot):
            return pltpu.make_async_copy(h0_ssd_hbm.at[layer, pl.ds(seq0 + i * STATE_SEQS, STATE_SEQS)],
                                         h_in.at[slot], sem_in.at[slot])

        def out_copy(i, slot):
            return pltpu.make_async_copy(h_out.at[slot],
                                         o_ssd_ref.at[layer, pl.ds(seq0 + i * STATE_SEQS, STATE_SEQS)],
                                         sem_out.at[slot])

        for j in range(STATE_BUFS - 1):
            in_copy(j, j).start()
    Q = CHUNK if prompt else SUBTILE_S

    hn = _rms(x, ng_ref[...]).astype(bf16)

    def proj(lo, hi):
        return _dot(hn, w_in_ref[:, lo:hi])

    def sub_iota(n):
        return lax.broadcasted_iota(jnp.int32, (SUBLANES, n), 0)

    def conv_taps(halo, rs, cw_ref, cb_ref):
        ext = jnp.concatenate([halo, rs], axis=0)
        n = rs.shape[0]
        acc = cb_ref[...] + cw_ref[3:4, :] * rs
        for j in range(1, CONV_WIDTH):
            acc = acc + cw_ref[3 - j:4 - j, :] * ext[HALO - SUBLANES * j:HALO - SUBLANES * j + n, :]
        return acc

    def conv(raw, prev_ref, c0_ref, cw_ref, cb_ref, o_ref):
        cdim = raw.shape[1]
        if not prompt:
            outs = []
            for s in range(T // Q):
                rs = raw[Q * s:Q * (s + 1), :]
                o_ref[s] = rs[Q - HALO:, :].reshape(CONV_WIDTH - 1, SUBLANES, cdim)
                outs.append(conv_taps(c0_ref[s].reshape(HALO, cdim), rs, cw_ref, cb_ref))
            return jnp.concatenate(outs, axis=0)
        first = sub_iota(cdim) == 0
        tail = prev_ref[...]
        outs = []
        for r0 in range(0, T, CHUNK):
            rs = raw[r0:r0 + CHUNK, :]
            cur = rs[CHUNK - HALO:, :]
            halo = jnp.concatenate(
                [jnp.where(first, pltpu.roll(tail[SUBLANES * k:SUBLANES * (k + 1), :], 1, 0),
                           pltpu.roll(cur[SUBLANES * k:SUBLANES * (k + 1), :], 1, 0))
                 for k in range(CONV_WIDTH - 1)], axis=0)
            outs.append(conv_taps(halo, rs, cw_ref, cb_ref))
            tail = cur
        prev_ref[...] = tail
        for k in range(CONV_WIDTH - 1):
            o_ref[k:k + 1, :] = tail[SUBLANES * k + SUBLANES - 1:SUBLANES * (k + 1), :]
        return jnp.concatenate(outs, axis=0)

    row = lax.broadcasted_iota(jnp.int32, (Q, Q), 0)
    col = lax.broadcasted_iota(jnp.int32, (Q, Q), 1)
    if prompt:
        def local_time(i):
            return jnp.bitwise_or(jnp.left_shift(jnp.bitwise_and(i, SUBLANES - 1), SEG_SHIFT),
                                  jnp.right_shift(i, SUB_SHIFT))
        causal = local_time(row) >= local_time(col)
    else:
        same_seq = jnp.bitwise_and(row, SUBLANES - 1) == jnp.bitwise_and(col, SUBLANES - 1)
        causal = jnp.logical_and(same_seq, jnp.right_shift(row, SUB_SHIFT) >= jnp.right_shift(col, SUB_SHIFT))
    tril = jnp.where(causal, 1.0, 0.0)
    lane_lo = lax.broadcasted_iota(jnp.int32, (Q, LANES), 1) < SSD_HEADDIM
    gsz = SSD_HPG * SSD_HEADDIM

    xbc = _silu(conv(proj(C_XBC, C_LRU), prev_ssd if prompt else None, None if prompt else c0_ssd_ref,
                     cw_ssd_ref, cb_ssd_ref, o_cssd_ref))
    dt_all = jax.nn.softplus(proj(C_DT, IN_COLS) + dtb_ref[...])
    a_neg = -jnp.exp(alog_ref[...])

    def ssd_state_io(rows_c, rows_xw, rows_b, e_last, h_get, h_set):
        outs = []
        for g in range(SSD_GROUPS):
            hp = h_get(g)
            outs.append(_dot_nt(rows_c[:, LANES * g:LANES * (g + 1)].astype(bf16), hp.astype(bf16)))
            sg = _dot_tn(rows_xw[:, gsz * g:gsz * (g + 1)].astype(bf16),
                         rows_b[:, LANES * g:LANES * (g + 1)].astype(bf16))
            dec = jnp.concatenate(
                [jnp.broadcast_to(e_last[:, SSD_HPG * g + k:SSD_HPG * g + k + 1], (SSD_HEADDIM, SSD_STATE))
                 for k in range(SSD_HPG)], axis=0)
            h_set(g, dec * hp + sg)
        return jnp.concatenate(outs, axis=1)

    def ssd_chunk(r0):
        xs = xbc[r0:r0 + Q, :SSD_DIM]
        bm = xbc[r0:r0 + Q, SSD_DIM:SSD_DIM + SSD_BC]
        cm = xbc[r0:r0 + Q, SSD_DIM + SSD_BC:]
        dt = dt_all[r0:r0 + Q, :]
        bm_b = bm.astype(bf16)
        cm_b = cm.astype(bf16)
        acum = _dot_exact(tril, dt * a_neg)
        acum_row = acum.T
        dt_row = dt.T
        scores = [_dot_nt(cm_b[:, LANES * g:LANES * (g + 1)], bm_b[:, LANES * g:LANES * (g + 1)])
                  for g in range(SSD_GROUPS)]
        if prompt:
            arow = acum_row[0:SSD_HEADS, :]
            w_row = jnp.exp(arow[:, Q - 1:Q] - arow) * dt_row[0:SSD_HEADS, :]
            e_end = jnp.exp(acum[Q - 1:Q, :])
            lane1 = lane_lo[0:1, :]
            bts = [bm[:, LANES * g:LANES * (g + 1)].T for g in range(SSD_GROUPS)]
            y_pairs = []
            for j in range(SSD_HEADS // 2):
                g = (2 * j) // SSD_HPG
                cm_g = cm[:, LANES * g:LANES * (g + 1)]
                bt_g = bts[g]
                lhs_y, lhs_s = [], []
                for h in (2 * j, 2 * j + 1):
                    colb = jnp.broadcast_to(acum[:, h:h + 1], (Q, LANES))
                    decay = jnp.exp(jnp.where(causal, colb - acum_row[h:h + 1, :], NEG))
                    lhs_y.append((scores[g] * decay * dt_row[h:h + 1, :]).astype(bf16))
                    lhs_s.append((bt_g * w_row[h:h + 1, :]).astype(bf16))
                for h in (2 * j, 2 * j + 1):
                    colb = jnp.broadcast_to(acum[:, h:h + 1], (Q, LANES))
                    lhs_y.append((jnp.exp(colb) * cm_g).astype(bf16))
                xp = xs[:, LANES * j:LANES * (j + 1)]
                hp = h_ssd[:, LANES * j:LANES * (j + 1)]
                xbd = jnp.concatenate([jnp.where(lane_lo, xp, 0.0), jnp.where(lane_lo, 0.0, xp)],
                                      axis=0).astype(bf16)
                hbd = jnp.concatenate([jnp.where(lane_lo, hp, 0.0), jnp.where(lane_lo, 0.0, hp)],
                                      axis=0).astype(bf16)
                y_pairs.append(_dot(jnp.concatenate(lhs_y, axis=1), jnp.concatenate([xbd, hbd], axis=0)))
                dec = jnp.where(lane1, jnp.broadcast_to(e_end[:, 2 * j:2 * j + 1], (1, LANES)),
                                jnp.broadcast_to(e_end[:, 2 * j + 1:2 * j + 2], (1, LANES)))
                h_ssd[:, LANES * j:LANES * (j + 1)] = dec * hp + _dot(jnp.concatenate(lhs_s, axis=1), xbd)
            return jnp.concatenate(y_pairs, axis=1) + dfull_ref[...] * xs

        eac = jnp.exp(acum)
        sel = jnp.where(col == jnp.bitwise_and(row, SUBLANES - 1) + (Q - SUBLANES), 1.0, 0.0)
        acum_end = _dot_exact(sel, acum)
        wgt = jnp.exp(acum_end - acum) * dt
        y_pairs, xw_pairs, ecol_pairs = [], [], []
        for j in range(SSD_HEADS // 2):
            g = (2 * j) // SSD_HPG
            ms = []
            for h in (2 * j, 2 * j + 1):
                diff = acum[:, h:h + 1] - acum_row[h:h + 1, :]
                decay = jnp.exp(jnp.where(causal, diff, NEG))
                ms.append((scores[g] * decay * dt_row[h:h + 1, :]).astype(bf16))
            xp = xs[:, LANES * j:LANES * (j + 1)]
            xbd = jnp.concatenate([jnp.where(lane_lo, xp, 0.0), jnp.where(lane_lo, 0.0, xp)], axis=0).astype(bf16)
            y_pairs.append(_dot(jnp.concatenate(ms, axis=1), xbd))
            xw_pairs.append(xp * _pair_expand(wgt, j, lane_lo))
            ecol_pairs.append(_pair_expand(eac, j, lane_lo))
        y_diag = jnp.concatenate(y_pairs, axis=1)
        xw = jnp.concatenate(xw_pairs, axis=1)
        ecol = jnp.concatenate(ecol_pairs, axis=1)

        c_s[r0:r0 + Q, :] = _dot(perm_b, cm_b)
        bm_s[r0:r0 + Q, :] = _dot(perm_b, bm_b)
        xw_s[r0:r0 + Q, :] = _dot(perm_b, xw.astype(bf16))
        eac_s[r0:r0 + Q, :] = eac
        return y_diag + dfull_ref[...] * xs, ecol

    if prompt:
        y = jnp.concatenate([ssd_chunk(r0) for r0 in range(0, T, Q)], axis=0)
    else:
        to_seq = jnp.bitwise_or(jnp.left_shift(jnp.bitwise_and(row, SUBLANES - 1), SUB_SHIFT),
                                jnp.right_shift(row, SUB_SHIFT)) == col
        perm_b = jnp.where(to_seq, 1.0, 0.0).astype(bf16)
        parts = [ssd_chunk(r0) for r0 in range(0, T, Q)]

        ngrp = nseq // STATE_SEQS

        def seq_step(i, carry):
            slot = jnp.bitwise_and(i, STATE_BUFS - 1)
            ahead = i + (STATE_BUFS - 1)

            @pl.when(ahead < ngrp)
            def _():
                in_copy(ahead, jnp.bitwise_and(ahead, STATE_BUFS - 1)).start()

            in_copy(i, slot).wait()

            @pl.when(i >= STATE_BUFS)
            def _():
                out_copy(i - STATE_BUFS, slot).wait()

            for q in range(STATE_SEQS):
                sq = i * STATE_SEQS + q
                s0 = pl.multiple_of(sq * SEQ_S, SEQ_S)

                def h_get(g):
                    return h_in[slot, q, pl.ds(gsz * g, gsz), :]

                def h_set(g, v):
                    h_out[slot, q, pl.ds(gsz * g, gsz), :] = v

                e_row = jnp.left_shift(jnp.right_shift(sq, SUB_SHIFT), SUBTILE_SHIFT) + (Q - SUBLANES) \
                    + jnp.bitwise_and(sq, SUBLANES - 1)
                yoff_s[pl.ds(s0, SEQ_S), :] = ssd_state_io(
                    c_s[pl.ds(s0, SEQ_S), :], xw_s[pl.ds(s0, SEQ_S), :], bm_s[pl.ds(s0, SEQ_S), :],
                    eac_s[pl.ds(e_row, 1), :], h_get, h_set)
            out_copy(i, slot).start()
            return carry

        lax.fori_loop(0, ngrp, seq_step, 0)
        for j in range(STATE_BUFS):
            out_copy(ngrp - STATE_BUFS + j, j).wait()
        perm_f = jnp.where(to_seq, 1.0, 0.0)
        y = jnp.concatenate(
            [part + _dot_exact(perm_f, yoff_s[r0:r0 + Q, :]) * ecol
             for r0, (part, ecol) in zip(range(0, T, Q), parts)], axis=0)
    y_ssd = _rms(y * _silu(proj(C_Z, C_XBC)), sng_ref[...])

    xr = conv(proj(C_LRU, C_LRU_G), prev_lru if prompt else None, None if prompt else c0_lru_ref,
              cw_lru_ref, cb_lru_ref, o_clru_ref)
    xr_b = xr.astype(bf16)
    hl = LRU_DIM // 2
    gates = jnp.concatenate([_dot(xr_b[:, hl * (k % 2):hl * (k % 2 + 1)], lru_w_ref[k]) for k in range(4)],
                            axis=1) + lru_b_ref[...]
    r_gate = jax.nn.sigmoid(gates[:, :LRU_DIM])
    i_gate = jax.nn.sigmoid(gates[:, LRU_DIM:])
    log_a = -LRU_C * r_gate * jax.nn.softplus(-lam_ref[...])
    a_t = jnp.exp(log_a)
    gain = jnp.sqrt(jnp.maximum(-jnp.tanh(log_a) * (a_t * a_t + 1.0), 0.0))
    a_s[...] = a_t
    b_s[...] = gain * i_gate * xr

    def vrow(ref, r0, k):
        return ref[r0 + SUBLANES * k:r0 + SUBLANES * (k + 1), :]

    def set_vrow(ref, r0, k, v):
        ref[r0 + SUBLANES * k:r0 + SUBLANES * (k + 1), :] = v

    if prompt:
        sub = sub_iota(LRU_DIM)
        carry = lru_c[...]
        for r0 in range(0, T, CHUNK):
            acc_a, acc_h = vrow(a_s, r0, 0), vrow(b_s, r0, 0)
            for k in range(1, SEG):
                a_k = vrow(a_s, r0, k)
                acc_h = a_k * acc_h + vrow(b_s, r0, k)
                acc_a = a_k * acc_a
                set_vrow(a_s, r0, k, acc_a)
                set_vrow(b_s, r0, k, acc_h)
            alpha = jnp.where(sub == 0, 0.0, pltpu.roll(acc_a, 1, 0))
            beta = jnp.where(sub == 0, jnp.broadcast_to(carry, (SUBLANES, LRU_DIM)), pltpu.roll(acc_h, 1, 0))
            for d in (1, 2, 4):
                a_sh = jnp.where(sub >= d, pltpu.roll(alpha, d, 0), 1.0)
                b_sh = jnp.where(sub >= d, pltpu.roll(beta, d, 0), 0.0)
                beta = alpha * b_sh + beta
                alpha = alpha * a_sh
            carry = (acc_a * beta + acc_h)[SUBLANES - 1:SUBLANES, :]
            for k in range(SEG):
                set_vrow(b_s, r0, k, vrow(b_s, r0, k) + vrow(a_s, r0, k) * beta)
        lru_c[...] = carry
        o_lru_ref[...] = carry
    else:
        for s in range(T // Q):
            h = h0_lru_ref[SUBLANES * s:SUBLANES * (s + 1), :]
            for k in range(Q // SUBLANES):
                h = vrow(a_s, Q * s, k) * h + vrow(b_s, Q * s, k)
                set_vrow(b_s, Q * s, k, h)
            o_lru_ref[SUBLANES * s:SUBLANES * (s + 1), :] = h
    y_lru = b_s[...] * _silu(proj(C_LRU_G, C_S5))

    u = proj(C_S5, C_S5_G)
    u_b = u.astype(bf16)
    half = S5_DIM // 2
    for k in range(2):
        uk = u_b[:, half * k:half * (k + 1)]
        bur_s[:, S5_HALF * k:S5_HALF * (k + 1)] = _dot(uk, bb_ref[k])
        bui_s[:, S5_HALF * k:S5_HALF * (k + 1)] = _dot(uk, bb_ref[2 + k])

    def tab(r0):
        return tab_ref[0, r0:r0 + SUBLANES, :], tab_ref[1, r0:r0 + SUBLANES, :]

    def cmul_add(pr, pi, xr, xi, yr, yi):
        return pr * xr - pi * xi + yr, pr * xi + pi * xr + yi

    ar, ai = tab(TAB_A)
    if prompt:
        sub = sub_iota(S5_FLAT)
        c_r, c_i = s5_cr[...], s5_ci[...]
        for r0 in range(0, T, CHUNK):
            hr, hi = vrow(bur_s, r0, 0), vrow(bui_s, r0, 0)
            for k in range(1, SEG):
                hr, hi = cmul_add(ar, ai, hr, hi, vrow(bur_s, r0, k), vrow(bui_s, r0, k))
                set_vrow(bur_s, r0, k, hr)
                set_vrow(bui_s, r0, k, hi)
            er = jnp.where(sub == 0, jnp.broadcast_to(c_r, (SUBLANES, S5_FLAT)), pltpu.roll(hr, 1, 0))
            ei = jnp.where(sub == 0, jnp.broadcast_to(c_i, (SUBLANES, S5_FLAT)), pltpu.roll(hi, 1, 0))
            for t, d in enumerate((1, 2, 4)):
                qr, qi = tab(TAB_Q + t * SUBLANES)
                er, ei = cmul_add(qr, qi, pltpu.roll(er, d, 0), pltpu.roll(ei, d, 0), er, ei)
            sr, si = tab(TAB_ASEG)
            nr, ni = cmul_add(sr, si, er, ei, hr, hi)
            c_r, c_i = nr[SUBLANES - 1:SUBLANES, :], ni[SUBLANES - 1:SUBLANES, :]
            for k in range(SEG):
                pr, pi = tab(TAB_PW + k * SUBLANES)
                vr, vi = cmul_add(pr, pi, er, ei, vrow(bur_s, r0, k), vrow(bui_s, r0, k))
                set_vrow(bur_s, r0, k, vr)
                set_vrow(bui_s, r0, k, vi)
        s5_cr[...] = c_r
        s5_ci[...] = c_i
        o_s5r_ref[...] = c_r
        o_s5i_ref[...] = c_i
    else:
        h0r = h0_s5r_ref[...].reshape(nseq, S5_FLAT)
        h0i = h0_s5i_ref[...].reshape(nseq, S5_FLAT)
        ends_r, ends_i = [], []
        for s in range(T // Q):
            rows = slice(SUBLANES * s, SUBLANES * (s + 1))
            hr, hi = h0r[rows, :], h0i[rows, :]
            for k in range(Q // SUBLANES):
                hr, hi = cmul_add(ar, ai, hr, hi, vrow(bur_s, Q * s, k), vrow(bui_s, Q * s, k))
                set_vrow(bur_s, Q * s, k, hr)
                set_vrow(bui_s, Q * s, k, hi)
            ends_r.append(hr)
            ends_i.append(hi)
        o_s5r_ref[...] = jnp.concatenate(ends_r, axis=0).reshape(nseq, S5_NGROUPS, S5_STATE)
        o_s5i_ref[...] = jnp.concatenate(ends_i, axis=0).reshape(nseq, S5_NGROUPS, S5_STATE)
    ys = []
    for k in range(2):
        hk = jnp.concatenate([bur_s[:, S5_HALF * k:S5_HALF * (k + 1)].astype(bf16),
                              bui_s[:, S5_HALF * k:S5_HALF * (k + 1)].astype(bf16)], axis=1)
        ys.append(_dot(hk, cc_ref[k]))
    ys5 = jnp.concatenate(ys, axis=1) + s5d_ref[...] * u
    ys5 = jax.nn.gelu(ys5)
    ys5 = ys5 * jax.nn.sigmoid(_dot(ys5.astype(bf16), glu_w_ref[...]) + glu_b_ref[...])
    y_s5 = ys5 * _silu(proj(C_S5_G, C_DT))

    ycat = jnp.concatenate([y_ssd.astype(bf16), y_lru.astype(bf16), y_s5.astype(bf16)], axis=1)
    return x + _dot(ycat, w_out_ref[...])


def _prompt_body(final, *refs):
    x_ref = refs[0]
    w = refs[1:1 + N_WEIGHTS]
    y_ref = refs[1 + N_WEIGHTS]
    o = refs[2 + N_WEIGHTS:8 + N_WEIGHTS]
    scr = refs[8 + N_WEIGHTS:]
    h_ssd, prev_ssd, prev_lru = scr[:3]
    lru_c, s5_cr, s5_ci = scr[-3:]

    @pl.when(pl.program_id(1) == 0)
    def _():
        h_ssd[...] = jnp.zeros_like(h_ssd)
        prev_ssd[...] = jnp.zeros_like(prev_ssd)
        prev_lru[...] = jnp.zeros_like(prev_lru)
        lru_c[...] = jnp.zeros_like(lru_c)
        s5_cr[...] = jnp.zeros_like(s5_cr)
        s5_ci[...] = jnp.zeros_like(s5_ci)

    for k in range(TILES_PER_STEP):
        rows = slice(TILE_P * k, TILE_P * (k + 1))
        scr_k = scr[:3] + scr[3 + 4 * k:7 + 4 * k] + scr[3 + 4 * TILES_PER_STEP:]
        out = _layer_math(True, TILE_P, x_ref[rows, :], w, None, o, scr_k)
        if final:
            out = _rms(out, w[-1][...])
        y_ref[rows, :] = out

    @pl.when(pl.program_id(1) == pl.num_programs(1) - 1)
    def _():
        o[0][...] = h_ssd[...].T


def _sample_body(*refs):
    x_ref = refs[0]
    st = refs[1:7]
    w = refs[7:7 + N_WEIGHTS]
    y_ref = refs[7 + N_WEIGHTS]
    o = refs[8 + N_WEIGHTS:14 + N_WEIGHTS]
    x_all = refs[14 + N_WEIGHTS]
    scr = refs[15 + N_WEIGHTS:]
    layer = pl.program_id(0)
    last_layer = layer == pl.num_programs(0) - 1
    r0 = pl.multiple_of(pl.program_id(1) * TILE_S, TILE_S)

    @pl.when(layer == 0)
    def _():
        x_all[pl.ds(r0, TILE_S), :] = x_ref[...]

    out = _layer_math(False, TILE_S, x_all[pl.ds(r0, TILE_S), :], w, st, o, scr)
    x_all[pl.ds(r0, TILE_S), :] = out

    @pl.when(last_layer)
    def _():
        y_ref[...] = _rms(out, w[-1][...])

    @pl.when(jnp.logical_not(last_layer))
    def _():
        y_ref[...] = out


def _prompt_call(layer, final, x, weights):
    T = TILE_P
    nb, seq, _ = x.shape

    def wspec(a):
        nd = a.ndim - 1
        return pl.BlockSpec((None,) + a.shape[1:], lambda b, c: (layer,) + (0,) * nd, pipeline_mode=pl.Buffered(1))

    def st(shape):
        nd = len(shape)
        return pl.BlockSpec((None,) + shape, lambda b, c: (b,) + (0,) * nd)

    step_rows = T * TILES_PER_STEP
    x_spec = pl.BlockSpec((None, step_rows, D_MODEL), lambda b, c: (b, c, 0))
    out_specs = [x_spec, st((SSD_DIM, SSD_STATE)), st((CONV_WIDTH - 1, SSD_CONV_DIM)), st((1, LRU_DIM)),
                 st((CONV_WIDTH - 1, LRU_DIM)), st((1, S5_FLAT)), st((1, S5_FLAT))]
    out_shape = [jax.ShapeDtypeStruct(x.shape, f32),
                 jax.ShapeDtypeStruct((nb, SSD_DIM, SSD_STATE), f32),
                 jax.ShapeDtypeStruct((nb, CONV_WIDTH - 1, SSD_CONV_DIM), f32),
                 jax.ShapeDtypeStruct((nb, 1, LRU_DIM), f32),
                 jax.ShapeDtypeStruct((nb, CONV_WIDTH - 1, LRU_DIM), f32),
                 jax.ShapeDtypeStruct((nb, 1, S5_FLAT), f32),
                 jax.ShapeDtypeStruct((nb, 1, S5_FLAT), f32)]
    scratch = [pltpu.VMEM((SSD_STATE, SSD_DIM), f32),
               pltpu.VMEM((HALO, SSD_CONV_DIM), f32),
               pltpu.VMEM((HALO, LRU_DIM), f32)]
    scratch += [pltpu.VMEM((T, LRU_DIM), f32), pltpu.VMEM((T, LRU_DIM), f32),
                pltpu.VMEM((T, S5_FLAT), f32), pltpu.VMEM((T, S5_FLAT), f32)] * TILES_PER_STEP
    scratch += [pltpu.VMEM((1, LRU_DIM), f32), pltpu.VMEM((1, S5_FLAT), f32), pltpu.VMEM((1, S5_FLAT), f32)]
    return pl.pallas_call(
        functools.partial(_prompt_body, final),
        grid=(nb, seq // step_rows), in_specs=[x_spec] + [wspec(a) for a in weights],
        out_specs=out_specs, out_shape=out_shape, scratch_shapes=scratch,
        compiler_params=pltpu.CompilerParams(dimension_semantics=("arbitrary", "arbitrary"),
                                             vmem_limit_bytes=VMEM_LIMIT_BYTES),
        name="layer_prompt",
    )(x, *weights)


def _sample_call(x, states, weights):
    T = TILE_S
    rows = x.shape[0]
    depth = weights[0].shape[0]
    nseq = T // SEQ_S

    def wspec(a):
        nd = a.ndim - 1
        return pl.BlockSpec((None,) + a.shape[1:], lambda l, i: (l,) + (0,) * nd, pipeline_mode=pl.Buffered(1))

    def st(a):
        nd = a.ndim - 2
        if a.ndim == 4 and a.shape[2:] == (SSD_DIM, SSD_STATE):
            return pl.BlockSpec(memory_space=pl.ANY)
        if a.ndim == 6:
            return pl.BlockSpec((None, None) + a.shape[2:], lambda l, i: (l, i) + (0,) * nd)
        return pl.BlockSpec((None, nseq) + a.shape[2:], lambda l, i: (l, i) + (0,) * nd)

    x_spec = pl.BlockSpec((T, D_MODEL), lambda l, i: (i, 0))
    st_specs = [st(a) for a in states]
    scratch = [pltpu.VMEM((rows, D_MODEL), f32),
               pltpu.VMEM((T, LRU_DIM), f32), pltpu.VMEM((T, LRU_DIM), f32),
               pltpu.VMEM((T, S5_FLAT), f32), pltpu.VMEM((T, S5_FLAT), f32),
               pltpu.VMEM((T, SSD_BC), f32), pltpu.VMEM((T, SSD_BC), f32),
               pltpu.VMEM((T, SSD_DIM), f32), pltpu.VMEM((T, SSD_DIM), f32),
               pltpu.VMEM((T, LANES), f32),
               pltpu.VMEM((STATE_BUFS, STATE_SEQS, SSD_DIM, SSD_STATE), f32),
               pltpu.VMEM((STATE_BUFS, STATE_SEQS, SSD_DIM, SSD_STATE), f32),
               pltpu.SemaphoreType.DMA((STATE_BUFS,)), pltpu.SemaphoreType.DMA((STATE_BUFS,))]
    return pl.pallas_call(
        _sample_body,
        grid=(depth, rows // T), in_specs=[x_spec] + st_specs + [wspec(a) for a in weights],
        out_specs=[pl.BlockSpec((None, T, D_MODEL), lambda l, i: (l, i, 0))] + st_specs,
        out_shape=[jax.ShapeDtypeStruct((depth,) + x.shape, f32)] + [jax.ShapeDtypeStruct(a.shape, f32) for a in states],
        scratch_shapes=scratch,
        compiler_params=pltpu.CompilerParams(dimension_semantics=("arbitrary", "arbitrary"),
                                             vmem_limit_bytes=VMEM_LIMIT_BYTES),
        name="layers_sample",
    )(x, *states, *weights)


def _block_diag(blocks):
    *lead, n, r, c = blocks.shape
    eye = jnp.eye(n, dtype=blocks.dtype)
    return (blocks[..., :, :, None, :] * eye[:, None, :, None]).reshape(*lead, n * r, n * c)


def _pad_lanes(v):
    return jnp.pad(v, [(0, 0)] * (v.ndim - 1) + [(0, LANES - v.shape[-1])])


def kernel(x_prompt, x_sample, state_ssd, state_ssd_conv, state_lru, state_lru_conv, state_s5_re, state_s5_im, norm_g, w_in, ssd_conv_w, ssd_conv_b, ssd_dt_bias, ssd_a_log, ssd_d, ssd_norm_g, lru_conv_w, lru_conv_b, lru_wa, lru_ba, lru_wx, lru_bx, lru_lambda, s5_lambda_re, s5_lambda_im, s5_log_dt, s5_b_re, s5_b_im, s5_c_re, s5_c_im, s5_d, s5_glu_w, s5_glu_b, w_out, final_norm_g):
    depth = w_in.shape[0]
    nbp = x_prompt.shape[0]
    nbs, ls, _ = x_sample.shape
    assert ls == SEQ_S and x_prompt.shape[1] % (TILE_P * TILES_PER_STEP) == 0 and (nbs * ls) % TILE_S == 0

    tab, bbar_re, bbar_im = _s5_prep(s5_lambda_re.astype(f32), s5_lambda_im.astype(f32), s5_log_dt.astype(f32),
                                     s5_b_re.astype(f32), s5_b_im.astype(f32))

    def row(v, n):
        return v.astype(f32).reshape(depth, 1, n)

    wi = w_in.astype(bf16)
    w_in_r = jnp.concatenate([wi[..., 0:3072], wi[..., 3088:5136], _pad_lanes(wi[..., 3072:3088])], axis=-1)

    def halves(v):
        return _block_diag(v.reshape(depth, 2, S5_NGROUPS // 2, S5_GROUP, S5_STATE))

    bb = jnp.concatenate([halves(bbar_re), halves(bbar_im)], axis=1).astype(bf16)

    def chalves(v):
        return _block_diag(jnp.transpose(v.astype(f32), (0, 1, 3, 2)).reshape(depth, 2, S5_NGROUPS // 2, S5_STATE, S5_GROUP))

    cc = jnp.concatenate([chalves(s5_c_re), -chalves(s5_c_im)], axis=2).astype(bf16)
    weights = (
        row(norm_g, D_MODEL), w_in_r,
        ssd_conv_w.astype(f32), row(ssd_conv_b, SSD_CONV_DIM),
        _pad_lanes(row(ssd_dt_bias, SSD_HEADS)), _pad_lanes(row(ssd_a_log, SSD_HEADS)),
        jnp.repeat(ssd_d.astype(f32), SSD_HEADDIM, axis=-1).reshape(depth, 1, SSD_DIM),
        row(ssd_norm_g, SSD_DIM),
        lru_conv_w.astype(f32), row(lru_conv_b, LRU_DIM),
        jnp.concatenate([_block_diag(v.astype(f32).reshape(depth, 2, v.shape[1] // 2, *v.shape[2:]))
                         for v in (lru_wa, lru_wx)], axis=1).astype(bf16),
        jnp.concatenate([lru_ba, lru_bx], axis=-1).astype(f32).reshape(depth, 1, 2 * LRU_DIM),
        row(lru_lambda, LRU_DIM),
        tab, bb, cc, row(s5_d, S5_DIM),
        s5_glu_w.astype(bf16), row(s5_glu_b, S5_DIM),
        w_out.astype(bf16),
        jnp.broadcast_to(final_norm_g.astype(f32).reshape(1, 1, D_MODEL), (depth, 1, D_MODEL)),
    )

    seq_p = x_prompt.shape[1]
    xp = jnp.swapaxes(x_prompt.astype(f32).reshape(nbp, seq_p // CHUNK, SUBLANES, SEG, D_MODEL), 2, 3)
    xp = xp.reshape(nbp, seq_p, D_MODEL)
    outs_p = [[] for _ in range(6)]
    for i in range(depth):
        res = _prompt_call(i, i == depth - 1, xp, weights)
        xp = res[0]
        for j in range(6):
            outs_p[j].append(res[1 + j])
    y_prompt = jnp.swapaxes(xp.reshape(nbp, seq_p // CHUNK, SEG, SUBLANES, D_MODEL), 2, 3).reshape(nbp, seq_p, D_MODEL)

    ntile = nbs // SUBLANES
    nsub = TILE_S // SUBTILE_S

    def conv_in(v):
        v = jnp.swapaxes(v.astype(f32).reshape(depth, ntile, SUBLANES, CONV_WIDTH - 1, v.shape[-1]), 2, 3)
        return v.reshape(depth, ntile // nsub, nsub, CONV_WIDTH - 1, SUBLANES, v.shape[-1])

    def conv_out(v, dtype):
        v = v.reshape(depth, ntile, CONV_WIDTH - 1, SUBLANES, v.shape[-1])
        return jnp.swapaxes(v, 2, 3).reshape(depth, nbs, CONV_WIDTH - 1, v.shape[-1]).astype(dtype)

    xs = jnp.swapaxes(x_sample.astype(f32).reshape(ntile, SUBLANES, ls, D_MODEL), 1, 2).reshape(nbs * ls, D_MODEL)
    states_s = (state_ssd.astype(f32).reshape(depth, nbs, SSD_DIM, SSD_STATE), conv_in(state_ssd_conv),
                state_lru.astype(f32), conv_in(state_lru_conv),
                state_s5_re.astype(f32), state_s5_im.astype(f32))
    weights_s = tuple(w[:, :, TAB_A:TAB_A + SUBLANES] if w is tab else w for w in weights)
    res_s = _sample_call(xs, states_s, weights_s)
    y_sample = jnp.swapaxes(res_s[0][depth - 1].reshape(ntile, ls, SUBLANES, D_MODEL), 1, 2).reshape(nbs, ls, D_MODEL)

    def stack(lst, shape, dtype):
        return jnp.stack(lst).reshape((depth,) + shape).astype(dtype)

    ssd_shape = (SSD_HEADS, SSD_HEADDIM, SSD_STATE)
    s5_shape = (S5_NGROUPS, S5_STATE)
    return (
        y_prompt.astype(x_prompt.dtype), y_sample.astype(x_sample.dtype),
        stack(outs_p[0], (nbp,) + ssd_shape, state_ssd.dtype),
        res_s[1].reshape((depth, nbs) + ssd_shape).astype(state_ssd.dtype),
        stack(outs_p[1], (nbp, CONV_WIDTH - 1, SSD_CONV_DIM), state_ssd_conv.dtype),
        conv_out(res_s[2], state_ssd_conv.dtype),
        stack(outs_p[2], (nbp, LRU_DIM), state_lru.dtype), res_s[3].astype(state_lru.dtype),
        stack(outs_p[3], (nbp, CONV_WIDTH - 1, LRU_DIM), state_lru_conv.dtype),
        conv_out(res_s[4], state_lru_conv.dtype),
        stack(outs_p[4], (nbp,) + s5_shape, state_s5_re.dtype),
        res_s[5].reshape((depth, nbs) + s5_shape).astype(state_s5_re.dtype),
        stack(outs_p[5], (nbp,) + s5_shape, state_s5_im.dtype),
        res_s[6].reshape((depth, nbs) + s5_shape).astype(state_s5_im.dtype),
    )
```

```python
import functools

import jax
import jax.numpy as jnp
from jax import lax
from jax.experimental import pallas as pl
from jax.experimental.pallas import tpu as pltpu

f32 = jnp.float32
bf16 = jnp.bfloat16

D_MODEL = 1024
CONV_WIDTH = 4
SSD_DIM = 1024
SSD_HEADDIM = 64
SSD_HEADS = 16
SSD_GROUPS = 4
SSD_HPG = 4
SSD_STATE = 128
SSD_BC = SSD_GROUPS * SSD_STATE
SSD_CONV_DIM = SSD_DIM + 2 * SSD_BC
LRU_DIM = 512
LRU_BLOCKS = 8
LRU_BLOCK = LRU_DIM // LRU_BLOCKS
LRU_C = 8.0
S5_DIM = 512
S5_GROUP = 16
S5_NGROUPS = 32
S5_STATE = 64
S5_FLAT = S5_NGROUPS * S5_STATE
S5_HALF = S5_FLAT // 2
EPS = 1e-6

LANES = 128
SUBLANES = 8
CHUNK = 128
TILE_P = 256
TILES_PER_STEP = 1
TILE_S = 256
SUBTILE_S = 64
SUBTILE_SHIFT = 6
SEQ_S = 8
STATE_BUFS = 4
STATE_SEQS = 2
NEG = -1e30

SEG = CHUNK // SUBLANES
SEG_SHIFT = 4
SUB_SHIFT = 3
HALO = (CONV_WIDTH - 1) * SUBLANES

TAB_A = 0
TAB_Q = TAB_A + SUBLANES
TAB_ASEG = TAB_Q + 3 * SUBLANES
TAB_PW = TAB_ASEG + SUBLANES
TAB_ROWS = TAB_PW + SEG * SUBLANES

C_Z = 0
C_XBC = C_Z + SSD_DIM
C_LRU = C_XBC + SSD_CONV_DIM
C_LRU_G = C_LRU + LRU_DIM
C_S5 = C_LRU_G + LRU_DIM
C_S5_G = C_S5 + S5_DIM
C_DT = C_S5_G + S5_DIM
IN_COLS = C_DT + LANES

VMEM_LIMIT_BYTES = 56 * 1024 * 1024

N_WEIGHTS = 21


def _rms(x, g):
    return x * lax.rsqrt(jnp.mean(x * x, axis=-1, keepdims=True) + EPS) * g


def _silu(x):
    return x * jax.nn.sigmoid(x)


def _dot(a, b):
    return jnp.dot(a, b, preferred_element_type=f32)


def _dot_nt(a, b):
    return lax.dot_general(a, b, (((1,), (1,)), ((), ())), preferred_element_type=f32)


def _dot_tn(a, b):
    return lax.dot_general(a, b, (((0,), (0,)), ((), ())), preferred_element_type=f32)


def _dot_exact(a, b):
    return jnp.dot(a, b, preferred_element_type=f32, precision=lax.Precision.HIGHEST)


def _pair_expand(v, j, lane_lo):
    q = v.shape[0]
    lo = jnp.broadcast_to(v[:, 2 * j:2 * j + 1], (q, LANES))
    hi = jnp.broadcast_to(v[:, 2 * j + 1:2 * j + 2], (q, LANES))
    return jnp.where(lane_lo, lo, hi)


def _s5_prep_body(lre_ref, lim_ref, ldt_ref, lre_g_ref, lim_g_ref, ldt_g_ref,
                  bre_ref, bim_ref, cre_ref, cim_ref, wa_ref, wx_ref, tre_ref, tim_ref, bb_ref, cc_ref, lw_ref):
    def abar(lre, lim, ldt):
        delta = jnp.exp(ldt)
        mag = jnp.exp(lre * delta)
        return mag * jnp.cos(lim * delta), mag * jnp.sin(lim * delta)

    ar, ai = abar(lre_ref[...], lim_ref[...], ldt_ref[...])

    def cmul(xr, xi, yr, yi):
        return xr * yr - xi * yi, xr * yi + xi * yr

    pw = [(ar, ai)]
    for _ in range(SEG - 1):
        pw.append(cmul(*pw[-1], ar, ai))
    seg = [pw[SEG - 1]]
    for _ in range(2):
        seg.append(cmul(*seg[-1], *seg[-1]))
    zero = jnp.zeros_like(ar)

    def put(i, v):
        tre_ref[i] = v[0]
        tim_ref[i] = v[1]

    for r in range(SUBLANES):
        put(TAB_A + r, pw[0])
        put(TAB_ASEG + r, seg[0])
        for t, d in enumerate((1, 2, 4)):
            put(TAB_Q + t * SUBLANES + r, seg[t] if r >= d else (zero, zero))
        for k in range(SEG):
            put(TAB_PW + k * SUBLANES + r, pw[k])

    lre, lim = lre_g_ref[...], lim_g_ref[...]
    ar, ai = abar(lre, lim, ldt_g_ref[...])
    denom = lre * lre + lim * lim
    nr = ar - 1.0
    ni = ai

    def per_channel(v):
        return jnp.broadcast_to(v[:, None, :], (S5_NGROUPS, S5_GROUP, S5_STATE)).reshape(S5_DIM, S5_STATE)

    coef_re = per_channel((nr * lre + ni * lim) / denom)
    coef_im = per_channel((ni * lre - nr * lim) / denom)
    bre, bim = bre_ref[...], bim_ref[...]
    bbar = (coef_re * bre - coef_im * bim, coef_re * bim + coef_im * bre)
    bb_ref[...] = jnp.zeros(bb_ref.shape, bf16)
    gh = S5_NGROUPS // 2
    for part in range(2):
        for half in range(2):
            for g in range(gh):
                r0 = (half * gh + g) * S5_GROUP
                bb_ref[2 * part + half, S5_GROUP * g:S5_GROUP * (g + 1), S5_STATE * g:S5_STATE * (g + 1)] = \
                    bbar[part][r0:r0 + S5_GROUP, :].astype(bf16)
    cc_ref[...] = jnp.zeros(cc_ref.shape, bf16)
    for part, (c_ref, sign) in enumerate(((cre_ref, 1.0), (cim_ref, -1.0))):
        for half in range(2):
            for g in range(gh):
                row0 = part * S5_HALF + S5_STATE * g
                cc_ref[half, row0:row0 + S5_STATE, S5_GROUP * g:S5_GROUP * (g + 1)] = \
                    (sign * c_ref[half * gh + g]).astype(bf16)
    lw_ref[...] = jnp.zeros(lw_ref.shape, bf16)
    nb = LRU_BLOCKS // 2
    for m, w_ref in enumerate((wa_ref, wx_ref)):
        for half in range(2):
            for b in range(nb):
                lw_ref[2 * m + half, LRU_BLOCK * b:LRU_BLOCK * (b + 1), LRU_BLOCK * b:LRU_BLOCK * (b + 1)] = \
                    w_ref[half * nb + b].astype(bf16)


def _s5_prep(lam_re, lam_im, log_dt, b_re, b_im, c_re, c_im, lru_wa, lru_wx):
    depth = lam_re.shape[0]
    rows_c = S5_FLAT // LANES
    ldt = jnp.broadcast_to(log_dt[:, :, None], (depth, S5_NGROUPS, S5_STATE))

    def bt(v):
        return jnp.transpose(v, (0, 1, 3, 2)).reshape(depth, S5_DIM, S5_STATE)

    cspec = pl.BlockSpec((None, rows_c, LANES), lambda i: (i, 0, 0))
    gspec = pl.BlockSpec((None, S5_NGROUPS, S5_STATE), lambda i: (i, 0, 0))
    rspec = pl.BlockSpec((None, S5_DIM, S5_STATE), lambda i: (i, 0, 0))
    tspec = pl.BlockSpec((None, TAB_ROWS, rows_c, LANES), lambda i: (i, 0, 0, 0))
    def whole(shape):
        nd = len(shape)
        return pl.BlockSpec((None,) + shape, lambda i: (i,) + (0,) * nd)

    bb_shape = (4, S5_DIM // 2, S5_HALF)
    cc_shape = (2, 2 * S5_HALF, S5_DIM // 2)
    lw_shape = (4, LRU_DIM // 2, LRU_DIM // 2)
    c_shape = (S5_NGROUPS, S5_STATE, S5_GROUP)
    w_shape = (LRU_BLOCKS, LRU_BLOCK, LRU_BLOCK)
    tre, tim, bb, cc, lw = pl.pallas_call(
        _s5_prep_body,
        grid=(depth,),
        in_specs=[cspec, cspec, cspec, gspec, gspec, gspec, rspec, rspec,
                  whole(c_shape), whole(c_shape), whole(w_shape), whole(w_shape)],
        out_specs=[tspec, tspec, whole(bb_shape), whole(cc_shape), whole(lw_shape)],
        out_shape=[jax.ShapeDtypeStruct((depth, TAB_ROWS, rows_c, LANES), f32)] * 2
        + [jax.ShapeDtypeStruct((depth,) + s, bf16) for s in (bb_shape, cc_shape, lw_shape)],
        name="s5_prep",
    )(lam_re.reshape(depth, rows_c, LANES), lam_im.reshape(depth, rows_c, LANES),
      ldt.reshape(depth, rows_c, LANES), lam_re, lam_im, ldt, bt(b_re), bt(b_im),
      jnp.swapaxes(c_re, 2, 3), jnp.swapaxes(c_im, 2, 3), lru_wa, lru_wx)
    tab = jnp.stack([tre, tim], axis=1).reshape(depth, 2, TAB_ROWS, S5_FLAT)
    return tab, bb, cc, lw


def _layer_math(prompt, T, x, w, st, o, scr):
    (ng_ref, w_in_ref, cw_ssd_ref, cb_ssd_ref, dtb_ref, alog_ref, dfull_ref, sng_ref,
     cw_lru_ref, cb_lru_ref, lru_w_ref, lru_b_ref, lam_ref,
     tab_ref, bb_ref, cc_ref, s5d_ref, glu_w_ref, glu_b_ref, w_out_ref, _) = w
    o_ssd_ref, o_cssd_ref, o_lru_ref, o_clru_ref, o_s5r_ref, o_s5i_ref = o
    if prompt:
        h_ssd, prev_ssd, prev_lru, a_s, b_s, bur_s, bui_s, lru_c, s5_cr, s5_ci = scr
    else:
        h0_ssd_hbm, c0_ssd_ref, h0_lru_ref, c0_lru_ref, h0_s5r_ref, h0_s5i_ref = st
        a_s, b_s, bur_s, bui_s, c_s, bm_s, xw_s, yoff_s, eac_s, h_in, h_out, sem_in, sem_out = scr
    nseq = T // SEQ_S
    if not prompt:
        layer = pl.program_id(0)
        seq0 = pl.program_id(1) * nseq

        def in_copy(i, slot):
            return pltpu.make_async_copy(h0_ssd_hbm.at[layer, pl.ds(seq0 + i * STATE_SEQS, STATE_SEQS)],
                                         h_in.at[slot], sem_in.at[slot])

        def out_copy(i, slot):
            return pltpu.make_async_copy(h_out.at[slot],
                                         o_ssd_ref.at[layer, pl.ds(seq0 + i * STATE_SEQS, STATE_SEQS)],
                                         sem_out.at[slot])

        for j in range(STATE_BUFS - 1):
            in_copy(j, j).start()
    Q = CHUNK if prompt else SUBTILE_S

    hn = _rms(x, ng_ref[...]).astype(bf16)

    def proj(lo, hi):
        return _dot(hn, w_in_ref[:, lo:hi])

    def sub_iota(n):
        return lax.broadcasted_iota(jnp.int32, (SUBLANES, n), 0)

    def conv_taps(halo, rs, cw_ref, cb_ref):
        ext = jnp.concatenate([halo, rs], axis=0)
        n = rs.shape[0]
        acc = cb_ref[...] + cw_ref[3:4, :] * rs
        for j in range(1, CONV_WIDTH):
            acc = acc + cw_ref[3 - j:4 - j, :] * ext[HALO - SUBLANES * j:HALO - SUBLANES * j + n, :]
        return acc

    def conv(raw, prev_ref, c0_ref, cw_ref, cb_ref, o_ref):
        cdim = raw.shape[1]
        if not prompt:
            outs = []
            for s in range(T // Q):
                rs = raw[Q * s:Q * (s + 1), :]
                o_ref[s] = rs[Q - HALO:, :].reshape(CONV_WIDTH - 1, SUBLANES, cdim)
                outs.append(conv_taps(c0_ref[s].reshape(HALO, cdim), rs, cw_ref, cb_ref))
            return jnp.concatenate(outs, axis=0)
        first = sub_iota(cdim) == 0
        tail = prev_ref[...]
        outs = []
        for r0 in range(0, T, CHUNK):
            rs = raw[r0:r0 + CHUNK, :]
            cur = rs[CHUNK - HALO:, :]
            halo = jnp.concatenate(
                [jnp.where(first, pltpu.roll(tail[SUBLANES * k:SUBLANES * (k + 1), :], 1, 0),
                           pltpu.roll(cur[SUBLANES * k:SUBLANES * (k + 1), :], 1, 0))
                 for k in range(CONV_WIDTH - 1)], axis=0)
            outs.append(conv_taps(halo, rs, cw_ref, cb_ref))
            tail = cur
        prev_ref[...] = tail
        for k in range(CONV_WIDTH - 1):
            o_ref[k:k + 1, :] = tail[SUBLANES * k + SUBLANES - 1:SUBLANES * (k + 1), :]
        return jnp.concatenate(outs, axis=0)

    row = lax.broadcasted_iota(jnp.int32, (Q, Q), 0)
    col = lax.broadcasted_iota(jnp.int32, (Q, Q), 1)
    if prompt:
        def local_time(i):
            return jnp.bitwise_or(jnp.left_shift(jnp.bitwise_and(i, SUBLANES - 1), SEG_SHIFT),
                                  jnp.right_shift(i, SUB_SHIFT))
        causal = local_time(row) >= local_time(col)
    else:
        same_seq = jnp.bitwise_and(row, SUBLANES - 1) == jnp.bitwise_and(col, SUBLANES - 1)
        causal = jnp.logical_and(same_seq, jnp.right_shift(row, SUB_SHIFT) >= jnp.right_shift(col, SUB_SHIFT))
    tril = jnp.where(causal, 1.0, 0.0)
    lane_lo = lax.broadcasted_iota(jnp.int32, (Q, LANES), 1) < SSD_HEADDIM
    gsz = SSD_HPG * SSD_HEADDIM

    xbc = _silu(conv(proj(C_XBC, C_LRU), prev_ssd if prompt else None, None if prompt else c0_ssd_ref,
                     cw_ssd_ref, cb_ssd_ref, o_cssd_ref))
    dt_all = jax.nn.softplus(proj(C_DT, IN_COLS) + dtb_ref[...])
    a_neg = -jnp.exp(alog_ref[...])

    def ssd_state_io(rows_c, rows_xw, rows_b, e_last, h_get, h_set):
        outs = []
        for g in range(SSD_GROUPS):
            hp = h_get(g)
            outs.append(_dot_nt(rows_c[:, LANES * g:LANES * (g + 1)].astype(bf16), hp.astype(bf16)))
            sg = _dot_tn(rows_xw[:, gsz * g:gsz * (g + 1)].astype(bf16),
                         rows_b[:, LANES * g:LANES * (g + 1)].astype(bf16))
            dec = jnp.concatenate(
                [jnp.broadcast_to(e_last[:, SSD_HPG * g + k:SSD_HPG * g + k + 1], (SSD_HEADDIM, SSD_STATE))
                 for k in range(SSD_HPG)], axis=0)
            h_set(g, dec * hp + sg)
        return jnp.concatenate(outs, axis=1)

    def ssd_chunk(r0):
        xs = xbc[r0:r0 + Q, :SSD_DIM]
        bm = xbc[r0:r0 + Q, SSD_DIM:SSD_DIM + SSD_BC]
        cm = xbc[r0:r0 + Q, SSD_DIM + SSD_BC:]
        dt = dt_all[r0:r0 + Q, :]
        bm_b = bm.astype(bf16)
        cm_b = cm.astype(bf16)
        acum = _dot_exact(tril, dt * a_neg)
        acum_row = acum.T
        dt_row = dt.T
        scores = [_dot_nt(cm_b[:, LANES * g:LANES * (g + 1)], bm_b[:, LANES * g:LANES * (g + 1)])
                  for g in range(SSD_GROUPS)]
        if prompt:
            arow = acum_row[0:SSD_HEADS, :]
            w_row = jnp.exp(arow[:, Q - 1:Q] - arow) * dt_row[0:SSD_HEADS, :]
            e_end = jnp.exp(acum[Q - 1:Q, :])
            lane1 = lane_lo[0:1, :]
            bts = [bm[:, LANES * g:LANES * (g + 1)].T for g in range(SSD_GROUPS)]
            y_pairs = []
            for j in range(SSD_HEADS // 2):
                g = (2 * j) // SSD_HPG
                cm_g = cm[:, LANES * g:LANES * (g + 1)]
                bt_g = bts[g]
                lhs_y, lhs_s = [], []
                for h in (2 * j, 2 * j + 1):
                    colb = jnp.broadcast_to(acum[:, h:h + 1], (Q, LANES))
                    decay = jnp.exp(jnp.where(causal, colb - acum_row[h:h + 1, :], NEG))
                    lhs_y.append((scores[g] * decay * dt_row[h:h + 1, :]).astype(bf16))
                    lhs_s.append((bt_g * w_row[h:h + 1, :]).astype(bf16))
                for h in (2 * j, 2 * j + 1):
                    colb = jnp.broadcast_to(acum[:, h:h + 1], (Q, LANES))
                    lhs_y.append((jnp.exp(colb) * cm_g).astype(bf16))
                xp = xs[:, LANES * j:LANES * (j + 1)]
                hp = h_ssd[:, LANES * j:LANES * (j + 1)]
                xbd = jnp.concatenate([jnp.where(lane_lo, xp, 0.0), jnp.where(lane_lo, 0.0, xp)],
                                      axis=0).astype(bf16)
                hbd = jnp.concatenate([jnp.where(lane_lo, hp, 0.0), jnp.where(lane_lo, 0.0, hp)],
                                      axis=0).astype(bf16)
                y_pairs.append(_dot(jnp.concatenate(lhs_y, axis=1), jnp.concatenate([xbd, hbd], axis=0)))
                dec = jnp.where(lane1, jnp.broadcast_to(e_end[:, 2 * j:2 * j + 1], (1, LANES)),
                                jnp.broadcast_to(e_end[:, 2 * j + 1:2 * j + 2], (1, LANES)))
                h_ssd[:, LANES * j:LANES * (j + 1)] = dec * hp + _dot(jnp.concatenate(lhs_s, axis=1), xbd)
            return jnp.concatenate(y_pairs, axis=1) + dfull_ref[...] * xs

        eac = jnp.exp(acum)
        sel = jnp.where(col == jnp.bitwise_and(row, SUBLANES - 1) + (Q - SUBLANES), 1.0, 0.0)
        acum_end = _dot_exact(sel, acum)
        wgt = jnp.exp(acum_end - acum) * dt
        y_pairs, xw_pairs, ecol_pairs = [], [], []
        for j in range(SSD_HEADS // 2):
            g = (2 * j) // SSD_HPG
            ms = []
            for h in (2 * j, 2 * j + 1):
                diff = acum[:, h:h + 1] - acum_row[h:h + 1, :]
                decay = jnp.exp(jnp.where(causal, diff, NEG))
                ms.append((scores[g] * decay * dt_row[h:h + 1, :]).astype(bf16))
            xp = xs[:, LANES * j:LANES * (j + 1)]
            xbd = jnp.concatenate([jnp.where(lane_lo, xp, 0.0), jnp.where(lane_lo, 0.0, xp)], axis=0).astype(bf16)
            y_pairs.append(_dot(jnp.concatenate(ms, axis=1), xbd))
            xw_pairs.append(xp * _pair_expand(wgt, j, lane_lo))
            ecol_pairs.append(_pair_expand(eac, j, lane_lo))
        y_diag = jnp.concatenate(y_pairs, axis=1)
        xw = jnp.concatenate(xw_pairs, axis=1)
        ecol = jnp.concatenate(ecol_pairs, axis=1)

        c_s[r0:r0 + Q, :] = _dot(perm_b, cm_b)
        bm_s[r0:r0 + Q, :] = _dot(perm_b, bm_b)
        xw_s[r0:r0 + Q, :] = _dot(perm_b, xw.astype(bf16))
        eac_s[r0:r0 + Q, :] = eac
        return y_diag + dfull_ref[...] * xs, ecol

    if prompt:
        y = jnp.concatenate([ssd_chunk(r0) for r0 in range(0, T, Q)], axis=0)
    else:
        to_seq = jnp.bitwise_or(jnp.left_shift(jnp.bitwise_and(row, SUBLANES - 1), SUB_SHIFT),
                                jnp.right_shift(row, SUB_SHIFT)) == col
        perm_b = jnp.where(to_seq, 1.0, 0.0).astype(bf16)
        parts = [ssd_chunk(r0) for r0 in range(0, T, Q)]

        ngrp = nseq // STATE_SEQS

        def seq_step(i, carry):
            slot = jnp.bitwise_and(i, STATE_BUFS - 1)
            ahead = i + (STATE_BUFS - 1)

            @pl.when(ahead < ngrp)
            def _():
                in_copy(ahead, jnp.bitwise_and(ahead, STATE_BUFS - 1)).start()

            in_copy(i, slot).wait()

            @pl.when(i >= STATE_BUFS)
            def _():
                out_copy(i - STATE_BUFS, slot).wait()

            for q in range(STATE_SEQS):
                sq = i * STATE_SEQS + q
                s0 = pl.multiple_of(sq * SEQ_S, SEQ_S)

                def h_get(g):
                    return h_in[slot, q, pl.ds(gsz * g, gsz), :]

                def h_set(g, v):
                    h_out[slot, q, pl.ds(gsz * g, gsz), :] = v

                e_row = jnp.left_shift(jnp.right_shift(sq, SUB_SHIFT), SUBTILE_SHIFT) + (Q - SUBLANES) \
                    + jnp.bitwise_and(sq, SUBLANES - 1)
                yoff_s[pl.ds(s0, SEQ_S), :] = ssd_state_io(
                    c_s[pl.ds(s0, SEQ_S), :], xw_s[pl.ds(s0, SEQ_S), :], bm_s[pl.ds(s0, SEQ_S), :],
                    eac_s[pl.ds(e_row, 1), :], h_get, h_set)
            out_copy(i, slot).start()
            return carry

        lax.fori_loop(0, ngrp, seq_step, 0)
        for j in range(STATE_BUFS):
            out_copy(ngrp - STATE_BUFS + j, j).wait()
        perm_f = jnp.where(to_seq, 1.0, 0.0)
        y = jnp.concatenate(
            [part + _dot_exact(perm_f, yoff_s[r0:r0 + Q, :]) * ecol
             for r0, (part, ecol) in zip(range(0, T, Q), parts)], axis=0)
    y_ssd = _rms(y * _silu(proj(C_Z, C_XBC)), sng_ref[...])

    xr = conv(proj(C_LRU, C_LRU_G), prev_lru if prompt else None, None if prompt else c0_lru_ref,
              cw_lru_ref, cb_lru_ref, o_clru_ref)
    xr_b = xr.astype(bf16)
    hl = LRU_DIM // 2
    gates = jnp.concatenate([_dot(xr_b[:, hl * (k % 2):hl * (k % 2 + 1)], lru_w_ref[k]) for k in range(4)],
                            axis=1) + lru_b_ref[...]
    r_gate = jax.nn.sigmoid(gates[:, :LRU_DIM])
    i_gate = jax.nn.sigmoid(gates[:, LRU_DIM:])
    log_a = -LRU_C * r_gate * jax.nn.softplus(-lam_ref[...])
    a_t = jnp.exp(log_a)
    gain = jnp.sqrt(jnp.maximum(-jnp.tanh(log_a) * (a_t * a_t + 1.0), 0.0))
    a_s[...] = a_t
    b_s[...] = gain * i_gate * xr

    def vrow(ref, r0, k):
        return ref[r0 + SUBLANES * k:r0 + SUBLANES * (k + 1), :]

    def set_vrow(ref, r0, k, v):
        ref[r0 + SUBLANES * k:r0 + SUBLANES * (k + 1), :] = v

    if prompt:
        sub = sub_iota(LRU_DIM)
        carry = lru_c[...]
        for r0 in range(0, T, CHUNK):
            acc_a, acc_h = vrow(a_s, r0, 0), vrow(b_s, r0, 0)
            for k in range(1, SEG):
                a_k = vrow(a_s, r0, k)
                acc_h = a_k * acc_h + vrow(b_s, r0, k)
                acc_a = a_k * acc_a
                set_vrow(a_s, r0, k, acc_a)
                set_vrow(b_s, r0, k, acc_h)
            alpha = jnp.where(sub == 0, 0.0, pltpu.roll(acc_a, 1, 0))
            beta = jnp.where(sub == 0, jnp.broadcast_to(carry, (SUBLANES, LRU_DIM)), pltpu.roll(acc_h, 1, 0))
            for d in (1, 2, 4):
                a_sh = jnp.where(sub >= d, pltpu.roll(alpha, d, 0), 1.0)
                b_sh = jnp.where(sub >= d, pltpu.roll(beta, d, 0), 0.0)
                beta = alpha * b_sh + beta
                alpha = alpha * a_sh
            carry = (acc_a * beta + acc_h)[SUBLANES - 1:SUBLANES, :]
            for k in range(SEG):
                set_vrow(b_s, r0, k, vrow(b_s, r0, k) + vrow(a_s, r0, k) * beta)
        lru_c[...] = carry
        o_lru_ref[...] = carry
    else:
        for s in range(T // Q):
            h = h0_lru_ref[SUBLANES * s:SUBLANES * (s + 1), :]
            for k in range(Q // SUBLANES):
                h = vrow(a_s, Q * s, k) * h + vrow(b_s, Q * s, k)
                set_vrow(b_s, Q * s, k, h)
            o_lru_ref[SUBLANES * s:SUBLANES * (s + 1), :] = h
    y_lru = b_s[...] * _silu(proj(C_LRU_G, C_S5))

    u = proj(C_S5, C_S5_G)
    u_b = u.astype(bf16)
    half = S5_DIM // 2
    for k in range(2):
        uk = u_b[:, half * k:half * (k + 1)]
        bur_s[:, S5_HALF * k:S5_HALF * (k + 1)] = _dot(uk, bb_ref[k])
        bui_s[:, S5_HALF * k:S5_HALF * (k + 1)] = _dot(uk, bb_ref[2 + k])

    def tab(r0):
        return tab_ref[0, r0:r0 + SUBLANES, :], tab_ref[1, r0:r0 + SUBLANES, :]

    def cmul_add(pr, pi, xr, xi, yr, yi):
        return pr * xr - pi * xi + yr, pr * xi + pi * xr + yi

    ar, ai = tab(TAB_A)
    if prompt:
        sub = sub_iota(S5_FLAT)
        c_r, c_i = s5_cr[...], s5_ci[...]
        for r0 in range(0, T, CHUNK):
            hr, hi = vrow(bur_s, r0, 0), vrow(bui_s, r0, 0)
            for k in range(1, SEG):
                hr, hi = cmul_add(ar, ai, hr, hi, vrow(bur_s, r0, k), vrow(bui_s, r0, k))
                set_vrow(bur_s, r0, k, hr)
                set_vrow(bui_s, r0, k, hi)
            er = jnp.where(sub == 0, jnp.broadcast_to(c_r, (SUBLANES, S5_FLAT)), pltpu.roll(hr, 1, 0))
            ei = jnp.where(sub == 0, jnp.broadcast_to(c_i, (SUBLANES, S5_FLAT)), pltpu.roll(hi, 1, 0))
            for t, d in enumerate((1, 2, 4)):
                qr, qi = tab(TAB_Q + t * SUBLANES)
                er, ei = cmul_add(qr, qi, pltpu.roll(er, d, 0), pltpu.roll(ei, d, 0), er, ei)
            sr, si = tab(TAB_ASEG)
            nr, ni = cmul_add(sr, si, er, ei, hr, hi)
            c_r, c_i = nr[SUBLANES - 1:SUBLANES, :], ni[SUBLANES - 1:SUBLANES, :]
            for k in range(SEG):
                pr, pi = tab(TAB_PW + k * SUBLANES)
                vr, vi = cmul_add(pr, pi, er, ei, vrow(bur_s, r0, k), vrow(bui_s, r0, k))
                set_vrow(bur_s, r0, k, vr)
                set_vrow(bui_s, r0, k, vi)
        s5_cr[...] = c_r
        s5_ci[...] = c_i
        o_s5r_ref[...] = c_r
        o_s5i_ref[...] = c_i
    else:
        h0r = h0_s5r_ref[...].reshape(nseq, S5_FLAT)
        h0i = h0_s5i_ref[...].reshape(nseq, S5_FLAT)
        ends_r, ends_i = [], []
        for s in range(T // Q):
            rows = slice(SUBLANES * s, SUBLANES * (s + 1))
            hr, hi = h0r[rows, :], h0i[rows, :]
            for k in range(Q // SUBLANES):
                hr, hi = cmul_add(ar, ai, hr, hi, vrow(bur_s, Q * s, k), vrow(bui_s, Q * s, k))
                set_vrow(bur_s, Q * s, k, hr)
                set_vrow(bui_s, Q * s, k, hi)
            ends_r.append(hr)
            ends_i.append(hi)
        o_s5r_ref[...] = jnp.concatenate(ends_r, axis=0).reshape(nseq, S5_NGROUPS, S5_STATE)
        o_s5i_ref[...] = jnp.concatenate(ends_i, axis=0).reshape(nseq, S5_NGROUPS, S5_STATE)
    ys = []
    for k in range(2):
        hk = jnp.concatenate([bur_s[:, S5_HALF * k:S5_HALF * (k + 1)].astype(bf16),
                              bui_s[:, S5_HALF * k:S5_HALF * (k + 1)].astype(bf16)], axis=1)
        ys.append(_dot(hk, cc_ref[k]))
    ys5 = jnp.concatenate(ys, axis=1) + s5d_ref[...] * u
    ys5 = jax.nn.gelu(ys5)
    ys5 = ys5 * jax.nn.sigmoid(_dot(ys5.astype(bf16), glu_w_ref[...]) + glu_b_ref[...])
    y_s5 = ys5 * _silu(proj(C_S5_G, C_DT))

    ycat = jnp.concatenate([y_ssd.astype(bf16), y_lru.astype(bf16), y_s5.astype(bf16)], axis=1)
    return x + _dot(ycat, w_out_ref[...])


def _prompt_body(final, *refs):
    x_ref = refs[0]
    w = refs[1:1 + N_WEIGHTS]
    y_ref = refs[1 + N_WEIGHTS]
    o = refs[2 + N_WEIGHTS:8 + N_WEIGHTS]
    scr = refs[8 + N_WEIGHTS:]
    h_ssd, prev_ssd, prev_lru = scr[:3]
    lru_c, s5_cr, s5_ci = scr[-3:]

    @pl.when(pl.program_id(1) == 0)
    def _():
        h_ssd[...] = jnp.zeros_like(h_ssd)
        prev_ssd[...] = jnp.zeros_like(prev_ssd)
        prev_lru[...] = jnp.zeros_like(prev_lru)
        lru_c[...] = jnp.zeros_like(lru_c)
        s5_cr[...] = jnp.zeros_like(s5_cr)
        s5_ci[...] = jnp.zeros_like(s5_ci)

    for k in range(TILES_PER_STEP):
        rows = slice(TILE_P * k, TILE_P * (k + 1))
        scr_k = scr[:3] + scr[3 + 4 * k:7 + 4 * k] + scr[3 + 4 * TILES_PER_STEP:]
        out = _layer_math(True, TILE_P, x_ref[rows, :], w, None, o, scr_k)
        if final:
            out = _rms(out, w[-1][...])
        y_ref[rows, :] = out

    @pl.when(pl.program_id(1) == pl.num_programs(1) - 1)
    def _():
        o[0][...] = h_ssd[...].T


def _sample_body(*refs):
    x_ref = refs[0]
    st = refs[1:7]
    w = refs[7:7 + N_WEIGHTS]
    y_ref = refs[7 + N_WEIGHTS]
    o = refs[8 + N_WEIGHTS:14 + N_WEIGHTS]
    x_all = refs[14 + N_WEIGHTS]
    scr = refs[15 + N_WEIGHTS:]
    layer = pl.program_id(0)
    last_layer = layer == pl.num_programs(0) - 1
    r0 = pl.multiple_of(pl.program_id(1) * TILE_S, TILE_S)

    @pl.when(layer == 0)
    def _():
        x_all[pl.ds(r0, TILE_S), :] = x_ref[...]

    out = _layer_math(False, TILE_S, x_all[pl.ds(r0, TILE_S), :], w, st, o, scr)
    x_all[pl.ds(r0, TILE_S), :] = out

    @pl.when(last_layer)
    def _():
        y_ref[...] = _rms(out, w[-1][...])

    @pl.when(jnp.logical_not(last_layer))
    def _():
        y_ref[...] = out


def _prompt_call(layer, final, x, weights):
    T = TILE_P
    nb, seq, _ = x.shape

    def wspec(a):
        nd = a.ndim - 1
        return pl.BlockSpec((None,) + a.shape[1:], lambda b, c: (layer,) + (0,) * nd, pipeline_mode=pl.Buffered(1))

    def st(shape):
        nd = len(shape)
        return pl.BlockSpec((None,) + shape, lambda b, c: (b,) + (0,) * nd)

    step_rows = T * TILES_PER_STEP
    x_spec = pl.BlockSpec((None, step_rows, D_MODEL), lambda b, c: (b, c, 0))
    out_specs = [x_spec, st((SSD_DIM, SSD_STATE)), st((CONV_WIDTH - 1, SSD_CONV_DIM)), st((1, LRU_DIM)),
                 st((CONV_WIDTH - 1, LRU_DIM)), st((1, S5_FLAT)), st((1, S5_FLAT))]
    out_shape = [jax.ShapeDtypeStruct(x.shape, f32),
                 jax.ShapeDtypeStruct((nb, SSD_DIM, SSD_STATE), f32),
                 jax.ShapeDtypeStruct((nb, CONV_WIDTH - 1, SSD_CONV_DIM), f32),
                 jax.ShapeDtypeStruct((nb, 1, LRU_DIM), f32),
                 jax.ShapeDtypeStruct((nb, CONV_WIDTH - 1, LRU_DIM), f32),
                 jax.ShapeDtypeStruct((nb, 1, S5_FLAT), f32),
                 jax.ShapeDtypeStruct((nb, 1, S5_FLAT), f32)]
    scratch = [pltpu.VMEM((SSD_STATE, SSD_DIM), f32),
               pltpu.VMEM((HALO, SSD_CONV_DIM), f32),
               pltpu.VMEM((HALO, LRU_DIM), f32)]
    scratch += [pltpu.VMEM((T, LRU_DIM), f32), pltpu.VMEM((T, LRU_DIM), f32),
                pltpu.VMEM((T, S5_FLAT), f32), pltpu.VMEM((T, S5_FLAT), f32)] * TILES_PER_STEP
    scratch += [pltpu.VMEM((1, LRU_DIM), f32), pltpu.VMEM((1, S5_FLAT), f32), pltpu.VMEM((1, S5_FLAT), f32)]
    return pl.pallas_call(
        functools.partial(_prompt_body, final),
        grid=(nb, seq // step_rows), in_specs=[x_spec] + [wspec(a) for a in weights],
        out_specs=out_specs, out_shape=out_shape, scratch_shapes=scratch,
        compiler_params=pltpu.CompilerParams(dimension_semantics=("arbitrary", "arbitrary"),
                                             vmem_limit_bytes=VMEM_LIMIT_BYTES),
        name="layer_prompt",
    )(x, *weights)


def _sample_call(x, states, weights):
    T = TILE_S
    rows = x.shape[0]
    depth = weights[0].shape[0]
    nseq = T // SEQ_S

    def wspec(a):
        nd = a.ndim - 1
        return pl.BlockSpec((None,) + a.shape[1:], lambda l, i: (l,) + (0,) * nd, pipeline_mode=pl.Buffered(1))

    def st(a):
        nd = a.ndim - 2
        if a.ndim == 4 and a.shape[2:] == (SSD_DIM, SSD_STATE):
            return pl.BlockSpec(memory_space=pl.ANY)
        if a.ndim == 6:
            return pl.BlockSpec((None, None) + a.shape[2:], lambda l, i: (l, i) + (0,) * nd)
        return pl.BlockSpec((None, nseq) + a.shape[2:], lambda l, i: (l, i) + (0,) * nd)

    x_spec = pl.BlockSpec((T, D_MODEL), lambda l, i: (i, 0))
    st_specs = [st(a) for a in states]
    scratch = [pltpu.VMEM((rows, D_MODEL), f32),
               pltpu.VMEM((T, LRU_DIM), f32), pltpu.VMEM((T, LRU_DIM), f32),
               pltpu.VMEM((T, S5_FLAT), f32), pltpu.VMEM((T, S5_FLAT), f32),
               pltpu.VMEM((T, SSD_BC), f32), pltpu.VMEM((T, SSD_BC), f32),
               pltpu.VMEM((T, SSD_DIM), f32), pltpu.VMEM((T, SSD_DIM), f32),
               pltpu.VMEM((T, LANES), f32),
               pltpu.VMEM((STATE_BUFS, STATE_SEQS, SSD_DIM, SSD_STATE), f32),
               pltpu.VMEM((STATE_BUFS, STATE_SEQS, SSD_DIM, SSD_STATE), f32),
               pltpu.SemaphoreType.DMA((STATE_BUFS,)), pltpu.SemaphoreType.DMA((STATE_BUFS,))]
    return pl.pallas_call(
        _sample_body,
        grid=(depth, rows // T), in_specs=[x_spec] + st_specs + [wspec(a) for a in weights],
        out_specs=[pl.BlockSpec((None, T, D_MODEL), lambda l, i: (l, i, 0))] + st_specs,
        out_shape=[jax.ShapeDtypeStruct((depth,) + x.shape, f32)] + [jax.ShapeDtypeStruct(a.shape, f32) for a in states],
        scratch_shapes=scratch,
        compiler_params=pltpu.CompilerParams(dimension_semantics=("arbitrary", "arbitrary"),
                                             vmem_limit_bytes=VMEM_LIMIT_BYTES),
        name="layers_sample",
    )(x, *states, *weights)


def _pad_lanes(v):
    return jnp.pad(v, [(0, 0)] * (v.ndim - 1) + [(0, LANES - v.shape[-1])])


def kernel(x_prompt, x_sample, state_ssd, state_ssd_conv, state_lru, state_lru_conv, state_s5_re, state_s5_im, norm_g, w_in, ssd_conv_w, ssd_conv_b, ssd_dt_bias, ssd_a_log, ssd_d, ssd_norm_g, lru_conv_w, lru_conv_b, lru_wa, lru_ba, lru_wx, lru_bx, lru_lambda, s5_lambda_re, s5_lambda_im, s5_log_dt, s5_b_re, s5_b_im, s5_c_re, s5_c_im, s5_d, s5_glu_w, s5_glu_b, w_out, final_norm_g):
    depth = w_in.shape[0]
    nbp = x_prompt.shape[0]
    nbs, ls, _ = x_sample.shape
    assert ls == SEQ_S and x_prompt.shape[1] % (TILE_P * TILES_PER_STEP) == 0 and (nbs * ls) % TILE_S == 0

    tab, bb, cc, lru_w = _s5_prep(s5_lambda_re.astype(f32), s5_lambda_im.astype(f32), s5_log_dt.astype(f32),
                                  s5_b_re.astype(f32), s5_b_im.astype(f32), s5_c_re.astype(f32),
                                  s5_c_im.astype(f32), lru_wa.astype(f32), lru_wx.astype(f32))

    def row(v, n):
        return v.astype(f32).reshape(depth, 1, n)

    wi = w_in.astype(bf16)
    w_in_r = jnp.concatenate([wi[..., 0:3072], wi[..., 3088:5136], _pad_lanes(wi[..., 3072:3088])], axis=-1)

    weights = (
        row(norm_g, D_MODEL), w_in_r,
        ssd_conv_w.astype(f32), row(ssd_conv_b, SSD_CONV_DIM),
        _pad_lanes(row(ssd_dt_bias, SSD_HEADS)), _pad_lanes(row(ssd_a_log, SSD_HEADS)),
        jnp.repeat(ssd_d.astype(f32), SSD_HEADDIM, axis=-1).reshape(depth, 1, SSD_DIM),
        row(ssd_norm_g, SSD_DIM),
        lru_conv_w.astype(f32), row(lru_conv_b, LRU_DIM),
        lru_w,
        jnp.concatenate([lru_ba, lru_bx], axis=-1).astype(f32).reshape(depth, 1, 2 * LRU_DIM),
        row(lru_lambda, LRU_DIM),
        tab, bb, cc, row(s5_d, S5_DIM),
        s5_glu_w.astype(bf16), row(s5_glu_b, S5_DIM),
        w_out.astype(bf16),
        jnp.broadcast_to(final_norm_g.astype(f32).reshape(1, 1, D_MODEL), (depth, 1, D_MODEL)),
    )

    seq_p = x_prompt.shape[1]
    xp = jnp.swapaxes(x_prompt.astype(f32).reshape(nbp, seq_p // CHUNK, SUBLANES, SEG, D_MODEL), 2, 3)
    xp = xp.reshape(nbp, seq_p, D_MODEL)
    outs_p = [[] for _ in range(6)]
    for i in range(depth):
        res = _prompt_call(i, i == depth - 1, xp, weights)
        xp = res[0]
        for j in range(6):
            outs_p[j].append(res[1 + j])
    y_prompt = jnp.swapaxes(xp.reshape(nbp, seq_p // CHUNK, SEG, SUBLANES, D_MODEL), 2, 3).reshape(nbp, seq_p, D_MODEL)

    ntile = nbs // SUBLANES
    nsub = TILE_S // SUBTILE_S

    def conv_in(v):
        v = jnp.swapaxes(v.astype(f32).reshape(depth, ntile, SUBLANES, CONV_WIDTH - 1, v.shape[-1]), 2, 3)
        return v.reshape(depth, ntile // nsub, nsub, CONV_WIDTH - 1, SUBLANES, v.shape[-1])

    def conv_out(v, dtype):
        v = v.reshape(depth, ntile, CONV_WIDTH - 1, SUBLANES, v.shape[-1])
        return jnp.swapaxes(v, 2, 3).reshape(depth, nbs, CONV_WIDTH - 1, v.shape[-1]).astype(dtype)

    xs = jnp.swapaxes(x_sample.astype(f32).reshape(ntile, SUBLANES, ls, D_MODEL), 1, 2).reshape(nbs * ls, D_MODEL)
    states_s = (state_ssd.astype(f32).reshape(depth, nbs, SSD_DIM, SSD_STATE), conv_in(state_ssd_conv),
                state_lru.astype(f32), conv_in(state_lru_conv),
                state_s5_re.astype(f32), state_s5_im.astype(f32))
    weights_s = tuple(w[:, :, TAB_A:TAB_A + SUBLANES] if w is tab else w for w in weights)
    res_s = _sample_call(xs, states_s, weights_s)
    y_sample = jnp.swapaxes(res_s[0][depth - 1].reshape(ntile, ls, SUBLANES, D_MODEL), 1, 2).reshape(nbs, ls, D_MODEL)

    def stack(lst, shape, dtype):
        return jnp.stack(lst).reshape((depth,) + shape).astype(dtype)

    ssd_shape = (SSD_HEADS, SSD_HEADDIM, SSD_STATE)
    s5_shape = (S5_NGROUPS, S5_STATE)
    return (
        y_prompt.astype(x_prompt.dtype), y_sample.astype(x_sample.dtype),
        stack(outs_p[0], (nbp,) + ssd_shape, state_ssd.dtype),
        res_s[1].reshape((depth, nbs) + ssd_shape).astype(state_ssd.dtype),
        stack(outs_p[1], (nbp, CONV_WIDTH - 1, SSD_CONV_DIM), state_ssd_conv.dtype),
        conv_out(res_s[2], state_ssd_conv.dtype),
        stack(outs_p[2], (nbp, LRU_DIM), state_lru.dtype), res_s[3].astype(state_lru.dtype),
        stack(outs_p[3], (nbp, CONV_WIDTH - 1, LRU_DIM), state_lru_conv.dtype),
        conv_out(res_s[4], state_lru_conv.dtype),
        stack(outs_p[4], (nbp,) + s5_shape, state_s5_re.dtype),
        res_s[5].reshape((depth, nbs) + s5_shape).astype(state_s5_re.dtype),
        stack(outs_p[5], (nbp,) + s5_shape, state_s5_im.dtype),
        res_s[6].reshape((depth, nbs) + s5_shape).astype(state_s5_im.dtype),
    )
```

```python
import functools

import jax
import jax.numpy as jnp
from jax import lax
from jax.experimental import pallas as pl
from jax.experimental.pallas import tpu as pltpu

f32 = jnp.float32
bf16 = jnp.bfloat16

D_MODEL = 1024
CONV_WIDTH = 4
SSD_DIM = 1024
SSD_HEADDIM = 64
SSD_HEADS = 16
SSD_GROUPS = 4
SSD_HPG = 4
SSD_STATE = 128
SSD_BC = SSD_GROUPS * SSD_STATE
SSD_CONV_DIM = SSD_DIM + 2 * SSD_BC
LRU_DIM = 512
LRU_BLOCKS = 8
LRU_BLOCK = LRU_DIM // LRU_BLOCKS
LRU_C = 8.0
S5_DIM = 512
S5_GROUP = 16
S5_NGROUPS = 32
S5_STATE = 64
S5_FLAT = S5_NGROUPS * S5_STATE
S5_HALF = S5_FLAT // 2
EPS = 1e-6

LANES = 128
SUBLANES = 8
CHUNK = 128
TILE_P = 256
TILES_PER_STEP = 1
TILE_S = 256
SUBTILE_S = 64
SUBTILE_SHIFT = 6
SEQ_S = 8
STATE_BUFS = 2
STATE_SEQS = 4
NEG = -1e30

SEG = CHUNK // SUBLANES
SEG_SHIFT = 4
SUB_SHIFT = 3
HALO = (CONV_WIDTH - 1) * SUBLANES

TAB_A = 0
TAB_Q = TAB_A + SUBLANES
TAB_ASEG = TAB_Q + 3 * SUBLANES
TAB_PW = TAB_ASEG + SUBLANES
TAB_ROWS = TAB_PW + SEG * SUBLANES

C_Z = 0
C_XBC = C_Z + SSD_DIM
C_LRU = C_XBC + SSD_CONV_DIM
C_LRU_G = C_LRU + LRU_DIM
C_S5 = C_LRU_G + LRU_DIM
C_S5_G = C_S5 + S5_DIM
C_DT = C_S5_G + S5_DIM
IN_COLS = C_DT + LANES

VMEM_LIMIT_BYTES = 56 * 1024 * 1024

N_WEIGHTS = 21


def _rms(x, g):
    return x * lax.rsqrt(jnp.mean(x * x, axis=-1, keepdims=True) + EPS) * g


def _silu(x):
    return x * jax.nn.sigmoid(x)


def _dot(a, b):
    return jnp.dot(a, b, preferred_element_type=f32)


def _dot_nt(a, b):
    return lax.dot_general(a, b, (((1,), (1,)), ((), ())), preferred_element_type=f32)


def _dot_tn(a, b):
    return lax.dot_general(a, b, (((0,), (0,)), ((), ())), preferred_element_type=f32)


def _dot_exact(a, b):
    return jnp.dot(a, b, preferred_element_type=f32, precision=lax.Precision.HIGHEST)


def _pair_expand(v, j, lane_lo):
    q = v.shape[0]
    lo = jnp.broadcast_to(v[:, 2 * j:2 * j + 1], (q, LANES))
    hi = jnp.broadcast_to(v[:, 2 * j + 1:2 * j + 2], (q, LANES))
    return jnp.where(lane_lo, lo, hi)


def _s5_prep_body(lre_ref, lim_ref, ldt_ref, lre_g_ref, lim_g_ref, ldt_g_ref,
                  bre_ref, bim_ref, cre_ref, cim_ref, wa_ref, wx_ref, tre_ref, tim_ref, bb_ref, cc_ref, lw_ref):
    def abar(lre, lim, ldt):
        delta = jnp.exp(ldt)
        mag = jnp.exp(lre * delta)
        return mag * jnp.cos(lim * delta), mag * jnp.sin(lim * delta)

    ar, ai = abar(lre_ref[...], lim_ref[...], ldt_ref[...])

    def cmul(xr, xi, yr, yi):
        return xr * yr - xi * yi, xr * yi + xi * yr

    pw = [(ar, ai)]
    for _ in range(SEG - 1):
        pw.append(cmul(*pw[-1], ar, ai))
    seg = [pw[SEG - 1]]
    for _ in range(2):
        seg.append(cmul(*seg[-1], *seg[-1]))
    zero = jnp.zeros_like(ar)

    def put(i, v):
        tre_ref[i] = v[0]
        tim_ref[i] = v[1]

    for r in range(SUBLANES):
        put(TAB_A + r, pw[0])
        put(TAB_ASEG + r, seg[0])
        for t, d in enumerate((1, 2, 4)):
            put(TAB_Q + t * SUBLANES + r, seg[t] if r >= d else (zero, zero))
        for k in range(SEG):
            put(TAB_PW + k * SUBLANES + r, pw[k])

    lre, lim = lre_g_ref[...], lim_g_ref[...]
    ar, ai = abar(lre, lim, ldt_g_ref[...])
    denom = lre * lre + lim * lim
    nr = ar - 1.0
    ni = ai

    def per_channel(v):
        return jnp.broadcast_to(v[:, None, :], (S5_NGROUPS, S5_GROUP, S5_STATE)).reshape(S5_DIM, S5_STATE)

    coef_re = per_channel((nr * lre + ni * lim) / denom)
    coef_im = per_channel((ni * lre - nr * lim) / denom)
    bre, bim = bre_ref[...], bim_ref[...]
    bbar = (coef_re * bre - coef_im * bim, coef_re * bim + coef_im * bre)
    bb_ref[...] = jnp.zeros(bb_ref.shape, bf16)
    gh = S5_NGROUPS // 2
    for part in range(2):
        for half in range(2):
            for g in range(gh):
                r0 = (half * gh + g) * S5_GROUP
                bb_ref[2 * part + half, S5_GROUP * g:S5_GROUP * (g + 1), S5_STATE * g:S5_STATE * (g + 1)] = \
                    bbar[part][r0:r0 + S5_GROUP, :].astype(bf16)
    cc_ref[...] = jnp.zeros(cc_ref.shape, bf16)
    for part, (c_ref, sign) in enumerate(((cre_ref, 1.0), (cim_ref, -1.0))):
        for half in range(2):
            for g in range(gh):
                row0 = part * S5_HALF + S5_STATE * g
                cc_ref[half, row0:row0 + S5_STATE, S5_GROUP * g:S5_GROUP * (g + 1)] = \
                    (sign * c_ref[half * gh + g]).astype(bf16)
    lw_ref[...] = jnp.zeros(lw_ref.shape, bf16)
    nb = LRU_BLOCKS // 2
    for m, w_ref in enumerate((wa_ref, wx_ref)):
        for half in range(2):
            for b in range(nb):
                lw_ref[2 * m + half, LRU_BLOCK * b:LRU_BLOCK * (b + 1), LRU_BLOCK * b:LRU_BLOCK * (b + 1)] = \
                    w_ref[half * nb + b].astype(bf16)


def _s5_prep(lam_re, lam_im, log_dt, b_re, b_im, c_re, c_im, lru_wa, lru_wx):
    depth = lam_re.shape[0]
    rows_c = S5_FLAT // LANES
    ldt = jnp.broadcast_to(log_dt[:, :, None], (depth, S5_NGROUPS, S5_STATE))

    def bt(v):
        return jnp.transpose(v, (0, 1, 3, 2)).reshape(depth, S5_DIM, S5_STATE)

    cspec = pl.BlockSpec((None, rows_c, LANES), lambda i: (i, 0, 0))
    gspec = pl.BlockSpec((None, S5_NGROUPS, S5_STATE), lambda i: (i, 0, 0))
    rspec = pl.BlockSpec((None, S5_DIM, S5_STATE), lambda i: (i, 0, 0))
    tspec = pl.BlockSpec((None, TAB_ROWS, rows_c, LANES), lambda i: (i, 0, 0, 0))
    def whole(shape):
        nd = len(shape)
        return pl.BlockSpec((None,) + shape, lambda i: (i,) + (0,) * nd)

    bb_shape = (4, S5_DIM // 2, S5_HALF)
    cc_shape = (2, 2 * S5_HALF, S5_DIM // 2)
    lw_shape = (4, LRU_DIM // 2, LRU_DIM // 2)
    c_shape = (S5_NGROUPS, S5_STATE, S5_GROUP)
    w_shape = (LRU_BLOCKS, LRU_BLOCK, LRU_BLOCK)
    tre, tim, bb, cc, lw = pl.pallas_call(
        _s5_prep_body,
        grid=(depth,),
        in_specs=[cspec, cspec, cspec, gspec, gspec, gspec, rspec, rspec,
                  whole(c_shape), whole(c_shape), whole(w_shape), whole(w_shape)],
        out_specs=[tspec, tspec, whole(bb_shape), whole(cc_shape), whole(lw_shape)],
        out_shape=[jax.ShapeDtypeStruct((depth, TAB_ROWS, rows_c, LANES), f32)] * 2
        + [jax.ShapeDtypeStruct((depth,) + s, bf16) for s in (bb_shape, cc_shape, lw_shape)],
        name="s5_prep",
    )(lam_re.reshape(depth, rows_c, LANES), lam_im.reshape(depth, rows_c, LANES),
      ldt.reshape(depth, rows_c, LANES), lam_re, lam_im, ldt, bt(b_re), bt(b_im),
      jnp.swapaxes(c_re, 2, 3), jnp.swapaxes(c_im, 2, 3), lru_wa, lru_wx)
    tab = jnp.stack([tre, tim], axis=1).reshape(depth, 2, TAB_ROWS, S5_FLAT)
    return tab, bb, cc, lw


def _layer_math(prompt, T, x, w, st, o, scr):
    (ng_ref, w_in_ref, cw_ssd_ref, cb_ssd_ref, dtb_ref, alog_ref, dfull_ref, sng_ref,
     cw_lru_ref, cb_lru_ref, lru_w_ref, lru_b_ref, lam_ref,
     tab_ref, bb_ref, cc_ref, s5d_ref, glu_w_ref, glu_b_ref, w_out_ref, _) = w
    o_ssd_ref, o_cssd_ref, o_lru_ref, o_clru_ref, o_s5r_ref, o_s5i_ref = o
    if prompt:
        h_ssd, prev_ssd, prev_lru, a_s, b_s, bur_s, bui_s, lru_c, s5_cr, s5_ci = scr
    else:
        h0_ssd_hbm, c0_ssd_ref, h0_lru_ref, c0_lru_ref, h0_s5r_ref, h0_s5i_ref = st
        a_s, b_s, bur_s, bui_s, c_s, bm_s, xw_s, yoff_s, eac_s, h_in, h_out, sem_in, sem_out = scr
    nseq = T // SEQ_S
    if not prompt:
        layer = pl.program_id(0)
        seq0 = pl.program_id(1) * nseq

        def in_copy(i, slot):
            return pltpu.make_async_copy(h0_ssd_hbm.at[layer, pl.ds(seq0 + i * STATE_SEQS, STATE_SEQS)],
                                         h_in.at[slot], sem_in.at[slot])

        def out_copy(i, slot):
            return pltpu.make_async_copy(h_out.at[slot],
                                         o_ssd_ref.at[layer, pl.ds(seq0 + i * STATE_SEQS, STATE_SEQS)],
                                         sem_out.at[slot])

        for j in range(STATE_BUFS - 1):
            in_copy(j, j).start()
    Q = CHUNK if prompt else SUBTILE_S

    hn = _rms(x, ng_ref[...]).astype(bf16)

    def proj(lo, hi):
        return _dot(hn, w_in_ref[:, lo:hi])

    def sub_iota(n):
        return lax.broadcasted_iota(jnp.int32, (SUBLANES, n), 0)

    def conv_taps(halo, rs, cw_ref, cb_ref):
        ext = jnp.concatenate([halo, rs], axis=0)
        n = rs.shape[0]
        acc = cb_ref[...] + cw_ref[3:4, :] * rs
        for j in range(1, CONV_WIDTH):
            acc = acc + cw_ref[3 - j:4 - j, :] * ext[HALO - SUBLANES * j:HALO - SUBLANES * j + n, :]
        return acc

    def conv(raw, prev_ref, c0_ref, cw_ref, cb_ref, o_ref):
        cdim = raw.shape[1]
        if not prompt:
            outs = []
            for s in range(T // Q):
                rs = raw[Q * s:Q * (s + 1), :]
                o_ref[s] = rs[Q - HALO:, :].reshape(CONV_WIDTH - 1, SUBLANES, cdim)
                outs.append(conv_taps(c0_ref[s].reshape(HALO, cdim), rs, cw_ref, cb_ref))
            return jnp.concatenate(outs, axis=0)
        first = sub_iota(cdim) == 0
        tail = prev_ref[...]
        outs = []
        for r0 in range(0, T, CHUNK):
            rs = raw[r0:r0 + CHUNK, :]
            cur = rs[CHUNK - HALO:, :]
            halo = jnp.concatenate(
                [jnp.where(first, pltpu.roll(tail[SUBLANES * k:SUBLANES * (k + 1), :], 1, 0),
                           pltpu.roll(cur[SUBLANES * k:SUBLANES * (k + 1), :], 1, 0))
                 for k in range(CONV_WIDTH - 1)], axis=0)
            outs.append(conv_taps(halo, rs, cw_ref, cb_ref))
            tail = cur
        prev_ref[...] = tail
        for k in range(CONV_WIDTH - 1):
            o_ref[k:k + 1, :] = tail[SUBLANES * k + SUBLANES - 1:SUBLANES * (k + 1), :]
        return jnp.concatenate(outs, axis=0)

    row = lax.broadcasted_iota(jnp.int32, (Q, Q), 0)
    col = lax.broadcasted_iota(jnp.int32, (Q, Q), 1)
    if prompt:
        def local_time(i):
            return jnp.bitwise_or(jnp.left_shift(jnp.bitwise_and(i, SUBLANES - 1), SEG_SHIFT),
                                  jnp.right_shift(i, SUB_SHIFT))
        causal = local_time(row) >= local_time(col)
    else:
        same_seq = jnp.bitwise_and(row, SUBLANES - 1) == jnp.bitwise_and(col, SUBLANES - 1)
        causal = jnp.logical_and(same_seq, jnp.right_shift(row, SUB_SHIFT) >= jnp.right_shift(col, SUB_SHIFT))
    tril = jnp.where(causal, 1.0, 0.0)
    lane_lo = lax.broadcasted_iota(jnp.int32, (Q, LANES), 1) < SSD_HEADDIM
    gsz = SSD_HPG * SSD_HEADDIM

    xbc = _silu(conv(proj(C_XBC, C_LRU), prev_ssd if prompt else None, None if prompt else c0_ssd_ref,
                     cw_ssd_ref, cb_ssd_ref, o_cssd_ref))
    dt_all = jax.nn.softplus(proj(C_DT, IN_COLS) + dtb_ref[...])
    a_neg = -jnp.exp(alog_ref[...])

    def ssd_state_io(rows_c, rows_xw, rows_b, e_last, h_get, h_set):
        outs = []
        for g in range(SSD_GROUPS):
            hp = h_get(g)
            outs.append(_dot_nt(rows_c[:, LANES * g:LANES * (g + 1)].astype(bf16), hp.astype(bf16)))
            sg = _dot_tn(rows_xw[:, gsz * g:gsz * (g + 1)].astype(bf16),
                         rows_b[:, LANES * g:LANES * (g + 1)].astype(bf16))
            dec = jnp.concatenate(
                [jnp.broadcast_to(e_last[:, SSD_HPG * g + k:SSD_HPG * g + k + 1], (SSD_HEADDIM, SSD_STATE))
                 for k in range(SSD_HPG)], axis=0)
            h_set(g, dec * hp + sg)
        return jnp.concatenate(outs, axis=1)

    def ssd_chunk(r0):
        xs = xbc[r0:r0 + Q, :SSD_DIM]
        bm = xbc[r0:r0 + Q, SSD_DIM:SSD_DIM + SSD_BC]
        cm = xbc[r0:r0 + Q, SSD_DIM + SSD_BC:]
        dt = dt_all[r0:r0 + Q, :]
        bm_b = bm.astype(bf16)
        cm_b = cm.astype(bf16)
        acum = _dot_exact(tril, dt * a_neg)
        acum_row = acum.T
        dt_row = dt.T
        scores = [_dot_nt(cm_b[:, LANES * g:LANES * (g + 1)], bm_b[:, LANES * g:LANES * (g + 1)])
                  for g in range(SSD_GROUPS)]
        if prompt:
            arow = acum_row[0:SSD_HEADS, :]
            w_row = jnp.exp(arow[:, Q - 1:Q] - arow) * dt_row[0:SSD_HEADS, :]
            e_end = jnp.exp(acum[Q - 1:Q, :])
            lane1 = lane_lo[0:1, :]
            bts = [bm[:, LANES * g:LANES * (g + 1)].T for g in range(SSD_GROUPS)]
            y_pairs = []
            for j in range(SSD_HEADS // 2):
                g = (2 * j) // SSD_HPG
                cm_g = cm[:, LANES * g:LANES * (g + 1)]
                bt_g = bts[g]
                lhs_y, lhs_s = [], []
                for h in (2 * j, 2 * j + 1):
                    colb = jnp.broadcast_to(acum[:, h:h + 1], (Q, LANES))
                    decay = jnp.exp(jnp.where(causal, colb - acum_row[h:h + 1, :], NEG))
                    lhs_y.append((scores[g] * decay * dt_row[h:h + 1, :]).astype(bf16))
                    lhs_s.append((bt_g * w_row[h:h + 1, :]).astype(bf16))
                for h in (2 * j, 2 * j + 1):
                    colb = jnp.broadcast_to(acum[:, h:h + 1], (Q, LANES))
                    lhs_y.append((jnp.exp(colb) * cm_g).astype(bf16))
                xp = xs[:, LANES * j:LANES * (j + 1)]
                hp = h_ssd[:, LANES * j:LANES * (j + 1)]
                xbd = jnp.concatenate([jnp.where(lane_lo, xp, 0.0), jnp.where(lane_lo, 0.0, xp)],
                                      axis=0).astype(bf16)
                hbd = jnp.concatenate([jnp.where(lane_lo, hp, 0.0), jnp.where(lane_lo, 0.0, hp)],
                                      axis=0).astype(bf16)
                y_pairs.append(_dot(jnp.concatenate(lhs_y, axis=1), jnp.concatenate([xbd, hbd], axis=0)))
                dec = jnp.where(lane1, jnp.broadcast_to(e_end[:, 2 * j:2 * j + 1], (1, LANES)),
                                jnp.broadcast_to(e_end[:, 2 * j + 1:2 * j + 2], (1, LANES)))
                h_ssd[:, LANES * j:LANES * (j + 1)] = dec * hp + _dot(jnp.concatenate(lhs_s, axis=1), xbd)
            return jnp.concatenate(y_pairs, axis=1) + dfull_ref[...] * xs

        eac = jnp.exp(acum)
        sel = jnp.where(col == jnp.bitwise_and(row, SUBLANES - 1) + (Q - SUBLANES), 1.0, 0.0)
        acum_end = _dot_exact(sel, acum)
        wgt = jnp.exp(acum_end - acum) * dt
        y_pairs, xw_pairs, ecol_pairs = [], [], []
        for j in range(SSD_HEADS // 2):
            g = (2 * j) // SSD_HPG
            ms = []
            for h in (2 * j, 2 * j + 1):
                diff = acum[:, h:h + 1] - acum_row[h:h + 1, :]
                decay = jnp.exp(jnp.where(causal, diff, NEG))
                ms.append((scores[g] * decay * dt_row[h:h + 1, :]).astype(bf16))
            xp = xs[:, LANES * j:LANES * (j + 1)]
            xbd = jnp.concatenate([jnp.where(lane_lo, xp, 0.0), jnp.where(lane_lo, 0.0, xp)], axis=0).astype(bf16)
            y_pairs.append(_dot(jnp.concatenate(ms, axis=1), xbd))
            xw_pairs.append(xp * _pair_expand(wgt, j, lane_lo))
            ecol_pairs.append(_pair_expand(eac, j, lane_lo))
        y_diag = jnp.concatenate(y_pairs, axis=1)
        xw = jnp.concatenate(xw_pairs, axis=1)
        ecol = jnp.concatenate(ecol_pairs, axis=1)

        c_s[r0:r0 + Q, :] = _dot(perm_b, cm_b)
        bm_s[r0:r0 + Q, :] = _dot(perm_b, bm_b)
        xw_s[r0:r0 + Q, :] = _dot(perm_b, xw.astype(bf16))
        eac_s[r0:r0 + Q, :] = eac
        return y_diag + dfull_ref[...] * xs, ecol

    if prompt:
        y = jnp.concatenate([ssd_chunk(r0) for r0 in range(0, T, Q)], axis=0)
    else:
        to_seq = jnp.bitwise_or(jnp.left_shift(jnp.bitwise_and(row, SUBLANES - 1), SUB_SHIFT),
                                jnp.right_shift(row, SUB_SHIFT)) == col
        perm_b = jnp.where(to_seq, 1.0, 0.0).astype(bf16)
        parts = [ssd_chunk(r0) for r0 in range(0, T, Q)]

        ngrp = nseq // STATE_SEQS

        def seq_step(i, carry):
            slot = jnp.bitwise_and(i, STATE_BUFS - 1)
            ahead = i + (STATE_BUFS - 1)

            @pl.when(ahead < ngrp)
            def _():
                in_copy(ahead, jnp.bitwise_and(ahead, STATE_BUFS - 1)).start()

            in_copy(i, slot).wait()

            @pl.when(i >= STATE_BUFS)
            def _():
                out_copy(i - STATE_BUFS, slot).wait()

            for q in range(STATE_SEQS):
                sq = i * STATE_SEQS + q
                s0 = pl.multiple_of(sq * SEQ_S, SEQ_S)

                def h_get(g):
                    return h_in[slot, q, pl.ds(gsz * g, gsz), :]

                def h_set(g, v):
                    h_out[slot, q, pl.ds(gsz * g, gsz), :] = v

                e_row = jnp.left_shift(jnp.right_shift(sq, SUB_SHIFT), SUBTILE_SHIFT) + (Q - SUBLANES) \
                    + jnp.bitwise_and(sq, SUBLANES - 1)
                yoff_s[pl.ds(s0, SEQ_S), :] = ssd_state_io(
                    c_s[pl.ds(s0, SEQ_S), :], xw_s[pl.ds(s0, SEQ_S), :], bm_s[pl.ds(s0, SEQ_S), :],
                    eac_s[pl.ds(e_row, 1), :], h_get, h_set)
            out_copy(i, slot).start()
            return carry

        lax.fori_loop(0, ngrp, seq_step, 0)
        for j in range(STATE_BUFS):
            out_copy(ngrp - STATE_BUFS + j, j).wait()
        perm_f = jnp.where(to_seq, 1.0, 0.0)
        y = jnp.concatenate(
            [part + _dot_exact(perm_f, yoff_s[r0:r0 + Q, :]) * ecol
             for r0, (part, ecol) in zip(range(0, T, Q), parts)], axis=0)
    y_ssd = _rms(y * _silu(proj(C_Z, C_XBC)), sng_ref[...])

    xr = conv(proj(C_LRU, C_LRU_G), prev_lru if prompt else None, None if prompt else c0_lru_ref,
              cw_lru_ref, cb_lru_ref, o_clru_ref)
    xr_b = xr.astype(bf16)
    hl = LRU_DIM // 2
    gates = jnp.concatenate([_dot(xr_b[:, hl * (k % 2):hl * (k % 2 + 1)], lru_w_ref[k]) for k in range(4)],
                            axis=1) + lru_b_ref[...]
    r_gate = jax.nn.sigmoid(gates[:, :LRU_DIM])
    i_gate = jax.nn.sigmoid(gates[:, LRU_DIM:])
    log_a = -LRU_C * r_gate * jax.nn.softplus(-lam_ref[...])
    a_t = jnp.exp(log_a)
    gain = jnp.sqrt(jnp.maximum(-jnp.tanh(log_a) * (a_t * a_t + 1.0), 0.0))
    a_s[...] = a_t
    b_s[...] = gain * i_gate * xr

    def vrow(ref, r0, k):
        return ref[r0 + SUBLANES * k:r0 + SUBLANES * (k + 1), :]

    def set_vrow(ref, r0, k, v):
        ref[r0 + SUBLANES * k:r0 + SUBLANES * (k + 1), :] = v

    if prompt:
        sub = sub_iota(LRU_DIM)
        carry = lru_c[...]
        for r0 in range(0, T, CHUNK):
            acc_a, acc_h = vrow(a_s, r0, 0), vrow(b_s, r0, 0)
            for k in range(1, SEG):
                a_k = vrow(a_s, r0, k)
                acc_h = a_k * acc_h + vrow(b_s, r0, k)
                acc_a = a_k * acc_a
                set_vrow(a_s, r0, k, acc_a)
                set_vrow(b_s, r0, k, acc_h)
            alpha = jnp.where(sub == 0, 0.0, pltpu.roll(acc_a, 1, 0))
            beta = jnp.where(sub == 0, jnp.broadcast_to(carry, (SUBLANES, LRU_DIM)), pltpu.roll(acc_h, 1, 0))
            for d in (1, 2, 4):
                a_sh = jnp.where(sub >= d, pltpu.roll(alpha, d, 0), 1.0)
                b_sh = jnp.where(sub >= d, pltpu.roll(beta, d, 0), 0.0)
                beta = alpha * b_sh + beta
                alpha = alpha * a_sh
            carry = (acc_a * beta + acc_h)[SUBLANES - 1:SUBLANES, :]
            for k in range(SEG):
                set_vrow(b_s, r0, k, vrow(b_s, r0, k) + vrow(a_s, r0, k) * beta)
        lru_c[...] = carry
        o_lru_ref[...] = carry
    else:
        for s in range(T // Q):
            h = h0_lru_ref[SUBLANES * s:SUBLANES * (s + 1), :]
            for k in range(Q // SUBLANES):
                h = vrow(a_s, Q * s, k) * h + vrow(b_s, Q * s, k)
                set_vrow(b_s, Q * s, k, h)
            o_lru_ref[SUBLANES * s:SUBLANES * (s + 1), :] = h
    y_lru = b_s[...] * _silu(proj(C_LRU_G, C_S5))

    u = proj(C_S5, C_S5_G)
    u_b = u.astype(bf16)
    half = S5_DIM // 2
    for k in range(2):
        uk = u_b[:, half * k:half * (k + 1)]
        bur_s[:, S5_HALF * k:S5_HALF * (k + 1)] = _dot(uk, bb_ref[k])
        bui_s[:, S5_HALF * k:S5_HALF * (k + 1)] = _dot(uk, bb_ref[2 + k])

    def tab(r0):
        return tab_ref[0, r0:r0 + SUBLANES, :], tab_ref[1, r0:r0 + SUBLANES, :]

    def cmul_add(pr, pi, xr, xi, yr, yi):
        return pr * xr - pi * xi + yr, pr * xi + pi * xr + yi

    ar, ai = tab(TAB_A)
    if prompt:
        sub = sub_iota(S5_FLAT)
        c_r, c_i = s5_cr[...], s5_ci[...]
        for r0 in range(0, T, CHUNK):
            hr, hi = vrow(bur_s, r0, 0), vrow(bui_s, r0, 0)
            for k in range(1, SEG):
                hr, hi = cmul_add(ar, ai, hr, hi, vrow(bur_s, r0, k), vrow(bui_s, r0, k))
                set_vrow(bur_s, r0, k, hr)
                set_vrow(bui_s, r0, k, hi)
            er = jnp.where(sub == 0, jnp.broadcast_to(c_r, (SUBLANES, S5_FLAT)), pltpu.roll(hr, 1, 0))
            ei = jnp.where(sub == 0, jnp.broadcast_to(c_i, (SUBLANES, S5_FLAT)), pltpu.roll(hi, 1, 0))
            for t, d in enumerate((1, 2, 4)):
                qr, qi = tab(TAB_Q + t * SUBLANES)
                er, ei = cmul_add(qr, qi, pltpu.roll(er, d, 0), pltpu.roll(ei, d, 0), er, ei)
            sr, si = tab(TAB_ASEG)
            nr, ni = cmul_add(sr, si, er, ei, hr, hi)
            c_r, c_i = nr[SUBLANES - 1:SUBLANES, :], ni[SUBLANES - 1:SUBLANES, :]
            for k in range(SEG):
                pr, pi = tab(TAB_PW + k * SUBLANES)
                vr, vi = cmul_add(pr, pi, er, ei, vrow(bur_s, r0, k), vrow(bui_s, r0, k))
                set_vrow(bur_s, r0, k, vr)
                set_vrow(bui_s, r0, k, vi)
        s5_cr[...] = c_r
        s5_ci[...] = c_i
        o_s5r_ref[...] = c_r
        o_s5i_ref[...] = c_i
    else:
        h0r = h0_s5r_ref[...].reshape(nseq, S5_FLAT)
        h0i = h0_s5i_ref[...].reshape(nseq, S5_FLAT)
        ends_r, ends_i = [], []
        for s in range(T // Q):
            rows = slice(SUBLANES * s, SUBLANES * (s + 1))
            hr, hi = h0r[rows, :], h0i[rows, :]
            for k in range(Q // SUBLANES):
                hr, hi = cmul_add(ar, ai, hr, hi, vrow(bur_s, Q * s, k), vrow(bui_s, Q * s, k))
                set_vrow(bur_s, Q * s, k, hr)
                set_vrow(bui_s, Q * s, k, hi)
            ends_r.append(hr)
            ends_i.append(hi)
        o_s5r_ref[...] = jnp.concatenate(ends_r, axis=0).reshape(nseq, S5_NGROUPS, S5_STATE)
        o_s5i_ref[...] = jnp.concatenate(ends_i, axis=0).reshape(nseq, S5_NGROUPS, S5_STATE)
    ys = []
    for k in range(2):
        hk = jnp.concatenate([bur_s[:, S5_HALF * k:S5_HALF * (k + 1)].astype(bf16),
                              bui_s[:, S5_HALF * k:S5_HALF * (k + 1)].astype(bf16)], axis=1)
        ys.append(_dot(hk, cc_ref[k]))
    ys5 = jnp.concatenate(ys, axis=1) + s5d_ref[...] * u
    ys5 = jax.nn.gelu(ys5)
    ys5 = ys5 * jax.nn.sigmoid(_dot(ys5.astype(bf16), glu_w_ref[...]) + glu_b_ref[...])
    y_s5 = ys5 * _silu(proj(C_S5_G, C_DT))

    ycat = jnp.concatenate([y_ssd.astype(bf16), y_lru.astype(bf16), y_s5.astype(bf16)], axis=1)
    return x + _dot(ycat, w_out_ref[...])


def _prompt_body(final, *refs):
    x_ref = refs[0]
    w = refs[1:1 + N_WEIGHTS]
    y_ref = refs[1 + N_WEIGHTS]
    o = refs[2 + N_WEIGHTS:8 + N_WEIGHTS]
    scr = refs[8 + N_WEIGHTS:]
    h_ssd, prev_ssd, prev_lru = scr[:3]
    lru_c, s5_cr, s5_ci = scr[-3:]

    @pl.when(pl.program_id(1) == 0)
    def _():
        h_ssd[...] = jnp.zeros_like(h_ssd)
        prev_ssd[...] = jnp.zeros_like(prev_ssd)
        prev_lru[...] = jnp.zeros_like(prev_lru)
        lru_c[...] = jnp.zeros_like(lru_c)
        s5_cr[...] = jnp.zeros_like(s5_cr)
        s5_ci[...] = jnp.zeros_like(s5_ci)

    for k in range(TILES_PER_STEP):
        rows = slice(TILE_P * k, TILE_P * (k + 1))
        scr_k = scr[:3] + scr[3 + 4 * k:7 + 4 * k] + scr[3 + 4 * TILES_PER_STEP:]
        out = _layer_math(True, TILE_P, x_ref[rows, :], w, None, o, scr_k)
        if final:
            out = _rms(out, w[-1][...])
        y_ref[rows, :] = out

    @pl.when(pl.program_id(1) == pl.num_programs(1) - 1)
    def _():
        o[0][...] = h_ssd[...].T


def _sample_body(*refs):
    x_ref = refs[0]
    st = refs[1:7]
    w = refs[7:7 + N_WEIGHTS]
    y_ref = refs[7 + N_WEIGHTS]
    o = refs[8 + N_WEIGHTS:14 + N_WEIGHTS]
    x_all = refs[14 + N_WEIGHTS]
    scr = refs[15 + N_WEIGHTS:]
    layer = pl.program_id(0)
    last_layer = layer == pl.num_programs(0) - 1
    r0 = pl.multiple_of(pl.program_id(1) * TILE_S, TILE_S)

    @pl.when(layer == 0)
    def _():
        x_all[pl.ds(r0, TILE_S), :] = x_ref[...]

    out = _layer_math(False, TILE_S, x_all[pl.ds(r0, TILE_S), :], w, st, o, scr)
    x_all[pl.ds(r0, TILE_S), :] = out

    @pl.when(last_layer)
    def _():
        y_ref[...] = _rms(out, w[-1][...])

    @pl.when(jnp.logical_not(last_layer))
    def _():
        y_ref[...] = out


def _prompt_call(layer, final, x, weights):
    T = TILE_P
    nb, seq, _ = x.shape

    def wspec(a):
        nd = a.ndim - 1
        return pl.BlockSpec((None,) + a.shape[1:], lambda b, c: (layer,) + (0,) * nd, pipeline_mode=pl.Buffered(1))

    def st(shape):
        nd = len(shape)
        return pl.BlockSpec((None,) + shape, lambda b, c: (b,) + (0,) * nd)

    step_rows = T * TILES_PER_STEP
    x_spec = pl.BlockSpec((None, step_rows, D_MODEL), lambda b, c: (b, c, 0))
    out_specs = [x_spec, st((SSD_DIM, SSD_STATE)), st((CONV_WIDTH - 1, SSD_CONV_DIM)), st((1, LRU_DIM)),
                 st((CONV_WIDTH - 1, LRU_DIM)), st((1, S5_FLAT)), st((1, S5_FLAT))]
    out_shape = [jax.ShapeDtypeStruct(x.shape, f32),
                 jax.ShapeDtypeStruct((nb, SSD_DIM, SSD_STATE), f32),
                 jax.ShapeDtypeStruct((nb, CONV_WIDTH - 1, SSD_CONV_DIM), f32),
                 jax.ShapeDtypeStruct((nb, 1, LRU_DIM), f32),
                 jax.ShapeDtypeStruct((nb, CONV_WIDTH - 1, LRU_DIM), f32),
                 jax.ShapeDtypeStruct((nb, 1, S5_FLAT), f32),
                 jax.ShapeDtypeStruct((nb, 1, S5_FLAT), f32)]
    scratch = [pltpu.VMEM((SSD_STATE, SSD_DIM), f32),
               pltpu.VMEM((HALO, SSD_CONV_DIM), f32),
               pltpu.VMEM((HALO, LRU_DIM), f32)]
    scratch += [pltpu.VMEM((T, LRU_DIM), f32), pltpu.VMEM((T, LRU_DIM), f32),
                pltpu.VMEM((T, S5_FLAT), f32), pltpu.VMEM((T, S5_FLAT), f32)] * TILES_PER_STEP
    scratch += [pltpu.VMEM((1, LRU_DIM), f32), pltpu.VMEM((1, S5_FLAT), f32), pltpu.VMEM((1, S5_FLAT), f32)]
    return pl.pallas_call(
        functools.partial(_prompt_body, final),
        grid=(nb, seq // step_rows), in_specs=[x_spec] + [wspec(a) for a in weights],
        out_specs=out_specs, out_shape=out_shape, scratch_shapes=scratch,
        compiler_params=pltpu.CompilerParams(dimension_semantics=("arbitrary", "arbitrary"),
                                             vmem_limit_bytes=VMEM_LIMIT_BYTES),
        name="layer_prompt",
    )(x, *weights)


def _sample_call(x, states, weights):
    T = TILE_S
    rows = x.shape[0]
    depth = weights[0].shape[0]
    nseq = T // SEQ_S

    def wspec(a):
        nd = a.ndim - 1
        return pl.BlockSpec((None,) + a.shape[1:], lambda l, i: (l,) + (0,) * nd, pipeline_mode=pl.Buffered(1))

    def st(a):
        nd = a.ndim - 2
        if a.ndim == 4 and a.shape[2:] == (SSD_DIM, SSD_STATE):
            return pl.BlockSpec(memory_space=pl.ANY)
        if a.ndim == 6:
            return pl.BlockSpec((None, None) + a.shape[2:], lambda l, i: (l, i) + (0,) * nd)
        return pl.BlockSpec((None, nseq) + a.shape[2:], lambda l, i: (l, i) + (0,) * nd)

    x_spec = pl.BlockSpec((T, D_MODEL), lambda l, i: (i, 0))
    st_specs = [st(a) for a in states]
    scratch = [pltpu.VMEM((rows, D_MODEL), f32),
               pltpu.VMEM((T, LRU_DIM), f32), pltpu.VMEM((T, LRU_DIM), f32),
               pltpu.VMEM((T, S5_FLAT), f32), pltpu.VMEM((T, S5_FLAT), f32),
               pltpu.VMEM((T, SSD_BC), f32), pltpu.VMEM((T, SSD_BC), f32),
               pltpu.VMEM((T, SSD_DIM), f32), pltpu.VMEM((T, SSD_DIM), f32),
               pltpu.VMEM((T, LANES), f32),
               pltpu.VMEM((STATE_BUFS, STATE_SEQS, SSD_DIM, SSD_STATE), f32),
               pltpu.VMEM((STATE_BUFS, STATE_SEQS, SSD_DIM, SSD_STATE), f32),
               pltpu.SemaphoreType.DMA((STATE_BUFS,)), pltpu.SemaphoreType.DMA((STATE_BUFS,))]
    return pl.pallas_call(
        _sample_body,
        grid=(depth, rows // T), in_specs=[x_spec] + st_specs + [wspec(a) for a in weights],
        out_specs=[pl.BlockSpec((None, T, D_MODEL), lambda l, i: (l, i, 0))] + st_specs,
        out_shape=[jax.ShapeDtypeStruct((depth,) + x.shape, f32)] + [jax.ShapeDtypeStruct(a.shape, f32) for a in states],
        scratch_shapes=scratch,
        compiler_params=pltpu.CompilerParams(dimension_semantics=("arbitrary", "arbitrary"),
                                             vmem_limit_bytes=VMEM_LIMIT_BYTES),
        name="layers_sample",
    )(x, *states, *weights)


def _pad_lanes(v):
    return jnp.pad(v, [(0, 0)] * (v.ndim - 1) + [(0, LANES - v.shape[-1])])


def kernel(x_prompt, x_sample, state_ssd, state_ssd_conv, state_lru, state_lru_conv, state_s5_re, state_s5_im, norm_g, w_in, ssd_conv_w, ssd_conv_b, ssd_dt_bias, ssd_a_log, ssd_d, ssd_norm_g, lru_conv_w, lru_conv_b, lru_wa, lru_ba, lru_wx, lru_bx, lru_lambda, s5_lambda_re, s5_lambda_im, s5_log_dt, s5_b_re, s5_b_im, s5_c_re, s5_c_im, s5_d, s5_glu_w, s5_glu_b, w_out, final_norm_g):
    depth = w_in.shape[0]
    nbp = x_prompt.shape[0]
    nbs, ls, _ = x_sample.shape
    assert ls == SEQ_S and x_prompt.shape[1] % (TILE_P * TILES_PER_STEP) == 0 and (nbs * ls) % TILE_S == 0

    tab, bb, cc, lru_w = _s5_prep(s5_lambda_re.astype(f32), s5_lambda_im.astype(f32), s5_log_dt.astype(f32),
                                  s5_b_re.astype(f32), s5_b_im.astype(f32), s5_c_re.astype(f32),
                                  s5_c_im.astype(f32), lru_wa.astype(f32), lru_wx.astype(f32))

    def row(v, n):
        return v.astype(f32).reshape(depth, 1, n)

    wi = w_in.astype(bf16)
    w_in_r = jnp.concatenate([wi[..., 0:3072], wi[..., 3088:5136], _pad_lanes(wi[..., 3072:3088])], axis=-1)

    weights = (
        row(norm_g, D_MODEL), w_in_r,
        ssd_conv_w.astype(f32), row(ssd_conv_b, SSD_CONV_DIM),
        _pad_lanes(row(ssd_dt_bias, SSD_HEADS)), _pad_lanes(row(ssd_a_log, SSD_HEADS)),
        jnp.repeat(ssd_d.astype(f32), SSD_HEADDIM, axis=-1).reshape(depth, 1, SSD_DIM),
        row(ssd_norm_g, SSD_DIM),
        lru_conv_w.astype(f32), row(lru_conv_b, LRU_DIM),
        lru_w,
        jnp.concatenate([lru_ba, lru_bx], axis=-1).astype(f32).reshape(depth, 1, 2 * LRU_DIM),
        row(lru_lambda, LRU_DIM),
        tab, bb, cc, row(s5_d, S5_DIM),
        s5_glu_w.astype(bf16), row(s5_glu_b, S5_DIM),
        w_out.astype(bf16),
        jnp.broadcast_to(final_norm_g.astype(f32).reshape(1, 1, D_MODEL), (depth, 1, D_MODEL)),
    )

    seq_p = x_prompt.shape[1]
    xp = jnp.swapaxes(x_prompt.astype(f32).reshape(nbp, seq_p // CHUNK, SUBLANES, SEG, D_MODEL), 2, 3)
    xp = xp.reshape(nbp, seq_p, D_MODEL)
    outs_p = [[] for _ in range(6)]
    for i in range(depth):
        res = _prompt_call(i, i == depth - 1, xp, weights)
        xp = res[0]
        for j in range(6):
            outs_p[j].append(res[1 + j])
    y_prompt = jnp.swapaxes(xp.reshape(nbp, seq_p // CHUNK, SEG, SUBLANES, D_MODEL), 2, 3).reshape(nbp, seq_p, D_MODEL)

    ntile = nbs // SUBLANES
    nsub = TILE_S // SUBTILE_S

    def conv_in(v):
        v = jnp.swapaxes(v.astype(f32).reshape(depth, ntile, SUBLANES, CONV_WIDTH - 1, v.shape[-1]), 2, 3)
        return v.reshape(depth, ntile // nsub, nsub, CONV_WIDTH - 1, SUBLANES, v.shape[-1])

    def conv_out(v, dtype):
        v = v.reshape(depth, ntile, CONV_WIDTH - 1, SUBLANES, v.shape[-1])
        return jnp.swapaxes(v, 2, 3).reshape(depth, nbs, CONV_WIDTH - 1, v.shape[-1]).astype(dtype)

    xs = jnp.swapaxes(x_sample.astype(f32).reshape(ntile, SUBLANES, ls, D_MODEL), 1, 2).reshape(nbs * ls, D_MODEL)
    states_s = (state_ssd.astype(f32).reshape(depth, nbs, SSD_DIM, SSD_STATE), conv_in(state_ssd_conv),
                state_lru.astype(f32), conv_in(state_lru_conv),
                state_s5_re.astype(f32), state_s5_im.astype(f32))
    weights_s = tuple(w[:, :, TAB_A:TAB_A + SUBLANES] if w is tab else w for w in weights)
    res_s = _sample_call(xs, states_s, weights_s)
    y_sample = jnp.swapaxes(res_s[0][depth - 1].reshape(ntile, ls, SUBLANES, D_MODEL), 1, 2).reshape(nbs, ls, D_MODEL)

    def stack(lst, shape, dtype):
        return jnp.stack(lst).reshape((depth,) + shape).astype(dtype)

    ssd_shape = (SSD_HEADS, SSD_HEADDIM, SSD_STATE)
    s5_shape = (S5_NGROUPS, S5_STATE)
    return (
        y_prompt.astype(x_prompt.dtype), y_sample.astype(x_sample.dtype),
        stack(outs_p[0], (nbp,) + ssd_shape, state_ssd.dtype),
        res_s[1].reshape((depth, nbs) + ssd_shape).astype(state_ssd.dtype),
        stack(outs_p[1], (nbp, CONV_WIDTH - 1, SSD_CONV_DIM), state_ssd_conv.dtype),
        conv_out(res_s[2], state_ssd_conv.dtype),
        stack(outs_p[2], (nbp, LRU_DIM), state_lru.dtype), res_s[3].astype(state_lru.dtype),
        stack(outs_p[3], (nbp, CONV_WIDTH - 1, LRU_DIM), state_lru_conv.dtype),
        conv_out(res_s[4], state_lru_conv.dtype),
        stack(outs_p[4], (nbp,) + s5_shape, state_s5_re.dtype),
        res_s[5].reshape((depth, nbs) + s5_shape).astype(state_s5_re.dtype),
        stack(outs_p[5], (nbp,) + s5_shape, state_s5_im.dtype),
        res_s[6].reshape((depth, nbs) + s5_shape).astype(state_s5_im.dtype),
    )
```

```python
import functools

import jax
import jax.numpy as jnp
from jax import lax
from jax.experimental import pallas as pl
from jax.experimental.pallas import tpu as pltpu

f32 = jnp.float32
bf16 = jnp.bfloat16

D_MODEL = 1024
CONV_WIDTH = 4
SSD_DIM = 1024
SSD_HEADDIM = 64
SSD_HEADS = 16
SSD_GROUPS = 4
SSD_HPG = 4
SSD_STATE = 128
SSD_BC = SSD_GROUPS * SSD_STATE
SSD_CONV_DIM = SSD_DIM + 2 * SSD_BC
LRU_DIM = 512
LRU_BLOCKS = 8
LRU_BLOCK = LRU_DIM // LRU_BLOCKS
LRU_C = 8.0
S5_DIM = 512
S5_GROUP = 16
S5_NGROUPS = 32
S5_STATE = 64
S5_FLAT = S5_NGROUPS * S5_STATE
S5_HALF = S5_FLAT // 2
EPS = 1e-6

LANES = 128
SUBLANES = 8
CHUNK = 128
TILE_P = 256
TILES_PER_STEP = 1
TILE_S = 256
SUBTILE_S = 64
SUBTILE_SHIFT = 6
SEQ_S = 8
STATE_BUFS = 4
STATE_SEQS = 2
NEG = -1e30

SEG = CHUNK // SUBLANES
SEG_SHIFT = 4
SUB_SHIFT = 3
HALO = (CONV_WIDTH - 1) * SUBLANES

TAB_A = 0
TAB_Q = TAB_A + SUBLANES
TAB_ASEG = TAB_Q + 3 * SUBLANES
TAB_PW = TAB_ASEG + SUBLANES
TAB_ROWS = TAB_PW + SEG * SUBLANES

C_Z = 0
C_XBC = C_Z + SSD_DIM
C_LRU = C_XBC + SSD_CONV_DIM
C_LRU_G = C_LRU + LRU_DIM
C_S5 = C_LRU_G + LRU_DIM
C_S5_G = C_S5 + S5_DIM
C_DT = C_S5_G + S5_DIM
IN_COLS = C_DT + LANES

VMEM_LIMIT_BYTES = 56 * 1024 * 1024

N_WEIGHTS = 21


def _rms(x, g):
    return x * lax.rsqrt(jnp.mean(x * x, axis=-1, keepdims=True) + EPS) * g


def _silu(x):
    return x * jax.nn.sigmoid(x)


def _dot(a, b):
    return jnp.dot(a, b, preferred_element_type=f32)


def _dot_nt(a, b):
    return lax.dot_general(a, b, (((1,), (1,)), ((), ())), preferred_element_type=f32)


def _dot_tn(a, b):
    return lax.dot_general(a, b, (((0,), (0,)), ((), ())), preferred_element_type=f32)


def _dot_exact(a, b):
    return jnp.dot(a, b, preferred_element_type=f32, precision=lax.Precision.HIGHEST)


def _pair_expand(v, j, lane_lo):
    q = v.shape[0]
    lo = jnp.broadcast_to(v[:, 2 * j:2 * j + 1], (q, LANES))
    hi = jnp.broadcast_to(v[:, 2 * j + 1:2 * j + 2], (q, LANES))
    return jnp.where(lane_lo, lo, hi)


def _s5_prep_body(lre_ref, lim_ref, ldt_ref, lre_g_ref, lim_g_ref, ldt_g_ref,
                  bre_ref, bim_ref, cre_ref, cim_ref, wa_ref, wx_ref, tre_ref, tim_ref, bb_ref, cc_ref, lw_ref):
    def abar(lre, lim, ldt):
        delta = jnp.exp(ldt)
        mag = jnp.exp(lre * delta)
        return mag * jnp.cos(lim * delta), mag * jnp.sin(lim * delta)

    ar, ai = abar(lre_ref[...], lim_ref[...], ldt_ref[...])

    def cmul(xr, xi, yr, yi):
        return xr * yr - xi * yi, xr * yi + xi * yr

    pw = [(ar, ai)]
    for _ in range(SEG - 1):
        pw.append(cmul(*pw[-1], ar, ai))
    seg = [pw[SEG - 1]]
    for _ in range(2):
        seg.append(cmul(*seg[-1], *seg[-1]))
    zero = jnp.zeros_like(ar)

    def put(i, v):
        tre_ref[i] = v[0]
        tim_ref[i] = v[1]

    for r in range(SUBLANES):
        put(TAB_A + r, pw[0])
        put(TAB_ASEG + r, seg[0])
        for t, d in enumerate((1, 2, 4)):
            put(TAB_Q + t * SUBLANES + r, seg[t] if r >= d else (zero, zero))
        for k in range(SEG):
            put(TAB_PW + k * SUBLANES + r, pw[k])

    lre, lim = lre_g_ref[...], lim_g_ref[...]
    ar, ai = abar(lre, lim, ldt_g_ref[...])
    denom = lre * lre + lim * lim
    nr = ar - 1.0
    ni = ai

    def per_channel(v):
        return jnp.broadcast_to(v[:, None, :], (S5_NGROUPS, S5_GROUP, S5_STATE)).reshape(S5_DIM, S5_STATE)

    coef_re = per_channel((nr * lre + ni * lim) / denom)
    coef_im = per_channel((ni * lre - nr * lim) / denom)
    bre, bim = bre_ref[...], bim_ref[...]
    bbar = (coef_re * bre - coef_im * bim, coef_re * bim + coef_im * bre)
    bb_ref[...] = jnp.zeros(bb_ref.shape, bf16)
    gh = S5_NGROUPS // 2
    for part in range(2):
        for half in range(2):
            for g in range(gh):
                r0 = (half * gh + g) * S5_GROUP
                bb_ref[2 * part + half, S5_GROUP * g:S5_GROUP * (g + 1), S5_STATE * g:S5_STATE * (g + 1)] = \
                    bbar[part][r0:r0 + S5_GROUP, :].astype(bf16)
    cc_ref[...] = jnp.zeros(cc_ref.shape, bf16)
    for part, (c_ref, sign) in enumerate(((cre_ref, 1.0), (cim_ref, -1.0))):
        for half in range(2):
            for g in range(gh):
                row0 = part * S5_HALF + S5_STATE * g
                cc_ref[half, row0:row0 + S5_STATE, S5_GROUP * g:S5_GROUP * (g + 1)] = \
                    (sign * c_ref[half * gh + g]).astype(bf16)
    lw_ref[...] = jnp.zeros(lw_ref.shape, bf16)
    nb = LRU_BLOCKS // 2
    for m, w_ref in enumerate((wa_ref, wx_ref)):
        for half in range(2):
            for b in range(nb):
                lw_ref[2 * m + half, LRU_BLOCK * b:LRU_BLOCK * (b + 1), LRU_BLOCK * b:LRU_BLOCK * (b + 1)] = \
                    w_ref[half * nb + b].astype(bf16)


def _s5_prep(lam_re, lam_im, log_dt, b_re, b_im, c_re, c_im, lru_wa, lru_wx):
    depth = lam_re.shape[0]
    rows_c = S5_FLAT // LANES
    ldt = jnp.broadcast_to(log_dt[:, :, None], (depth, S5_NGROUPS, S5_STATE))

    def bt(v):
        return jnp.transpose(v, (0, 1, 3, 2)).reshape(depth, S5_DIM, S5_STATE)

    cspec = pl.BlockSpec((None, rows_c, LANES), lambda i: (i, 0, 0))
    gspec = pl.BlockSpec((None, S5_NGROUPS, S5_STATE), lambda i: (i, 0, 0))
    rspec = pl.BlockSpec((None, S5_DIM, S5_STATE), lambda i: (i, 0, 0))
    tspec = pl.BlockSpec((None, TAB_ROWS, rows_c, LANES), lambda i: (i, 0, 0, 0))
    def whole(shape):
        nd = len(shape)
        return pl.BlockSpec((None,) + shape, lambda i: (i,) + (0,) * nd)

    bb_shape = (4, S5_DIM // 2, S5_HALF)
    cc_shape = (2, 2 * S5_HALF, S5_DIM // 2)
    lw_shape = (4, LRU_DIM // 2, LRU_DIM // 2)
    c_shape = (S5_NGROUPS, S5_STATE, S5_GROUP)
    w_shape = (LRU_BLOCKS, LRU_BLOCK, LRU_BLOCK)
    tre, tim, bb, cc, lw = pl.pallas_call(
        _s5_prep_body,
        grid=(depth,),
        in_specs=[cspec, cspec, cspec, gspec, gspec, gspec, rspec, rspec,
                  whole(c_shape), whole(c_shape), whole(w_shape), whole(w_shape)],
        out_specs=[tspec, tspec, whole(bb_shape), whole(cc_shape), whole(lw_shape)],
        out_shape=[jax.ShapeDtypeStruct((depth, TAB_ROWS, rows_c, LANES), f32)] * 2
        + [jax.ShapeDtypeStruct((depth,) + s, bf16) for s in (bb_shape, cc_shape, lw_shape)],
        name="s5_prep",
    )(lam_re.reshape(depth, rows_c, LANES), lam_im.reshape(depth, rows_c, LANES),
      ldt.reshape(depth, rows_c, LANES), lam_re, lam_im, ldt, bt(b_re), bt(b_im),
      jnp.swapaxes(c_re, 2, 3), jnp.swapaxes(c_im, 2, 3), lru_wa, lru_wx)
    tab = jnp.stack([tre, tim], axis=1).reshape(depth, 2, TAB_ROWS, S5_FLAT)
    return tab, bb, cc, lw


def _layer_math(prompt, T, x, w, st, o, scr):
    (ng_ref, w_in_ref, cw_ssd_ref, cb_ssd_ref, dtb_ref, alog_ref, dfull_ref, sng_ref,
     cw_lru_ref, cb_lru_ref, lru_w_ref, lru_b_ref, lam_ref,
     tab_ref, bb_ref, cc_ref, s5d_ref, glu_w_ref, glu_b_ref, w_out_ref, _) = w
    o_ssd_ref, o_cssd_ref, o_lru_ref, o_clru_ref, o_s5r_ref, o_s5i_ref = o
    if prompt:
        h_ssd, prev_ssd, prev_lru, a_s, b_s, bur_s, bui_s, lru_c, s5_cr, s5_ci = scr
    else:
        h0_ssd_hbm, c0_ssd_ref, h0_lru_ref, c0_lru_ref, h0_s5r_ref, h0_s5i_ref = st
        a_s, b_s, bur_s, bui_s, c_s, bm_s, xw_s, yoff_s, eac_s, h_in, h_out, sem_in, sem_out = scr
    nseq = T // SEQ_S
    if not prompt:
        layer = pl.program_id(0)
        seq0 = pl.program_id(1) * nseq

        def in_copy(i, slot):
            return pltpu.make_async_copy(h0_ssd_hbm.at[layer, pl.ds(seq0 + i * STATE_SEQS, STATE_SEQS)],
                                         h_in.at[slot], sem_in.at[slot])

        def out_copy(i, slot):
            return pltpu.make_async_copy(h_out.at[slot],
                                         o_ssd_ref.at[layer, pl.ds(seq0 + i * STATE_SEQS, STATE_SEQS)],
                                         sem_out.at[slot])

        for j in range(STATE_BUFS - 1):
            in_copy(j, j).start()
    Q = CHUNK if prompt else SUBTILE_S

    hn = _rms(x, ng_ref[...]).astype(bf16)

    def proj(lo, hi):
        return _dot(hn, w_in_ref[:, lo:hi])

    def sub_iota(n):
        return lax.broadcasted_iota(jnp.int32, (SUBLANES, n), 0)

    def conv_taps(halo, rs, cw_ref, cb_ref):
        ext = jnp.concatenate([halo, rs], axis=0)
        n = rs.shape[0]
        acc = cb_ref[...] + cw_ref[3:4, :] * rs
        for j in range(1, CONV_WIDTH):
            acc = acc + cw_ref[3 - j:4 - j, :] * ext[HALO - SUBLANES * j:HALO - SUBLANES * j + n, :]
        return acc

    def conv(raw, prev_ref, c0_ref, cw_ref, cb_ref, o_ref):
        cdim = raw.shape[1]
        if not prompt:
            outs = []
            for s in range(T // Q):
                rs = raw[Q * s:Q * (s + 1), :]
                o_ref[s] = rs[Q - HALO:, :].reshape(CONV_WIDTH - 1, SUBLANES, cdim)
                outs.append(conv_taps(c0_ref[s].reshape(HALO, cdim), rs, cw_ref, cb_ref))
            return jnp.concatenate(outs, axis=0)
        first = sub_iota(cdim) == 0
        tail = prev_ref[...]
        outs = []
        for r0 in range(0, T, CHUNK):
            rs = raw[r0:r0 + CHUNK, :]
            cur = rs[CHUNK - HALO:, :]
            halo = jnp.concatenate(
                [jnp.where(first, pltpu.roll(tail[SUBLANES * k:SUBLANES * (k + 1), :], 1, 0),
                           pltpu.roll(cur[SUBLANES * k:SUBLANES * (k + 1), :], 1, 0))
                 for k in range(CONV_WIDTH - 1)], axis=0)
            outs.append(conv_taps(halo, rs, cw_ref, cb_ref))
            tail = cur
        prev_ref[...] = tail
        for k in range(CONV_WIDTH - 1):
            o_ref[k:k + 1, :] = tail[SUBLANES * k + SUBLANES - 1:SUBLANES * (k + 1), :]
        return jnp.concatenate(outs, axis=0)

    row = lax.broadcasted_iota(jnp.int32, (Q, Q), 0)
    col = lax.broadcasted_iota(jnp.int32, (Q, Q), 1)
    if prompt:
        def local_time(i):
            return jnp.bitwise_or(jnp.left_shift(jnp.bitwise_and(i, SUBLANES - 1), SEG_SHIFT),
                                  jnp.right_shift(i, SUB_SHIFT))
        causal = local_time(row) >= local_time(col)
    else:
        same_seq = jnp.bitwise_and(row, SUBLANES - 1) == jnp.bitwise_and(col, SUBLANES - 1)
        causal = jnp.logical_and(same_seq, jnp.right_shift(row, SUB_SHIFT) >= jnp.right_shift(col, SUB_SHIFT))
    tril = jnp.where(causal, 1.0, 0.0)
    lane_lo = lax.broadcasted_iota(jnp.int32, (Q, LANES), 1) < SSD_HEADDIM
    gsz = SSD_HPG * SSD_HEADDIM

    xbc = _silu(conv(proj(C_XBC, C_LRU), prev_ssd if prompt else None, None if prompt else c0_ssd_ref,
                     cw_ssd_ref, cb_ssd_ref, o_cssd_ref))
    dt_all = jax.nn.softplus(proj(C_DT, IN_COLS) + dtb_ref[...])
    a_neg = -jnp.exp(alog_ref[...])

    def ssd_state_io(rows_c, rows_xw, rows_b, e_last, h_get, h_set):
        outs = []
        for g in range(SSD_GROUPS):
            hp = h_get(g)
            outs.append(_dot_nt(rows_c[:, LANES * g:LANES * (g + 1)].astype(bf16), hp.astype(bf16)))
            sg = _dot_tn(rows_xw[:, gsz * g:gsz * (g + 1)].astype(bf16),
                         rows_b[:, LANES * g:LANES * (g + 1)].astype(bf16))
            dec = jnp.concatenate(
                [jnp.broadcast_to(e_last[:, SSD_HPG * g + k:SSD_HPG * g + k + 1], (SSD_HEADDIM, SSD_STATE))
                 for k in range(SSD_HPG)], axis=0)
            h_set(g, dec * hp + sg)
        return jnp.concatenate(outs, axis=1)

    def ssd_chunk(r0):
        xs = xbc[r0:r0 + Q, :SSD_DIM]
        bm = xbc[r0:r0 + Q, SSD_DIM:SSD_DIM + SSD_BC]
        cm = xbc[r0:r0 + Q, SSD_DIM + SSD_BC:]
        dt = dt_all[r0:r0 + Q, :]
        bm_b = bm.astype(bf16)
        cm_b = cm.astype(bf16)
        acum = _dot_exact(tril, dt * a_neg)
        acum_row = acum.T
        dt_row = dt.T
        scores = [_dot_nt(cm_b[:, LANES * g:LANES * (g + 1)], bm_b[:, LANES * g:LANES * (g + 1)])
                  for g in range(SSD_GROUPS)]
        if prompt:
            arow = acum_row[0:SSD_HEADS, :]
            w_row = jnp.exp(arow[:, Q - 1:Q] - arow) * dt_row[0:SSD_HEADS, :]
            e_end = jnp.exp(acum[Q - 1:Q, :])
            lane1 = lane_lo[0:1, :]
            bts = [bm[:, LANES * g:LANES * (g + 1)].T for g in range(SSD_GROUPS)]
            y_pairs = []
            for j in range(SSD_HEADS // 2):
                g = (2 * j) // SSD_HPG
                cm_g = cm[:, LANES * g:LANES * (g + 1)]
                bt_g = bts[g]
                lhs_y, lhs_s = [], []
                for h in (2 * j, 2 * j + 1):
                    colb = jnp.broadcast_to(acum[:, h:h + 1], (Q, LANES))
                    decay = jnp.exp(jnp.where(causal, colb - acum_row[h:h + 1, :], NEG))
                    lhs_y.append((scores[g] * decay * dt_row[h:h + 1, :]).astype(bf16))
                    lhs_s.append((bt_g * w_row[h:h + 1, :]).astype(bf16))
                for h in (2 * j, 2 * j + 1):
                    colb = jnp.broadcast_to(acum[:, h:h + 1], (Q, LANES))
                    lhs_y.append((jnp.exp(colb) * cm_g).astype(bf16))
                xp = xs[:, LANES * j:LANES * (j + 1)]
                hp = h_ssd[:, LANES * j:LANES * (j + 1)]
                xbd = jnp.concatenate([jnp.where(lane_lo, xp, 0.0), jnp.where(lane_lo, 0.0, xp)],
                                      axis=0).astype(bf16)
                hbd = jnp.concatenate([jnp.where(lane_lo, hp, 0.0), jnp.where(lane_lo, 0.0, hp)],
                                      axis=0).astype(bf16)
                y_pairs.append(_dot(jnp.concatenate(lhs_y, axis=1), jnp.concatenate([xbd, hbd], axis=0)))
                dec = jnp.where(lane1, jnp.broadcast_to(e_end[:, 2 * j:2 * j + 1], (1, LANES)),
                                jnp.broadcast_to(e_end[:, 2 * j + 1:2 * j + 2], (1, LANES)))
                h_ssd[:, LANES * j:LANES * (j + 1)] = dec * hp + _dot(jnp.concatenate(lhs_s, axis=1), xbd)
            return jnp.concatenate(y_pairs, axis=1) + dfull_ref[...] * xs

        eac = jnp.exp(acum)
        sel = jnp.where(col == jnp.bitwise_and(row, SUBLANES - 1) + (Q - SUBLANES), 1.0, 0.0)
        acum_end = _dot_exact(sel, acum)
        wgt = jnp.exp(acum_end - acum) * dt
        y_pairs, xw_pairs, ecol_pairs = [], [], []
        for j in range(SSD_HEADS // 2):
            g = (2 * j) // SSD_HPG
            ms = []
            for h in (2 * j, 2 * j + 1):
                diff = acum[:, h:h + 1] - acum_row[h:h + 1, :]
                decay = jnp.exp(jnp.where(causal, diff, NEG))
                ms.append((scores[g] * decay * dt_row[h:h + 1, :]).astype(bf16))
            xp = xs[:, LANES * j:LANES * (j + 1)]
            xbd = jnp.concatenate([jnp.where(lane_lo, xp, 0.0), jnp.where(lane_lo, 0.0, xp)], axis=0).astype(bf16)
            y_pairs.append(_dot(jnp.concatenate(ms, axis=1), xbd))
            xw_pairs.append(xp * _pair_expand(wgt, j, lane_lo))
            ecol_pairs.append(_pair_expand(eac, j, lane_lo))
        y_diag = jnp.concatenate(y_pairs, axis=1)
        xw = jnp.concatenate(xw_pairs, axis=1)
        ecol = jnp.concatenate(ecol_pairs, axis=1)

        c_s[r0:r0 + Q, :] = _dot(perm_b, cm_b)
        bm_s[r0:r0 + Q, :] = _dot(perm_b, bm_b)
        xw_s[r0:r0 + Q, :] = _dot(perm_b, xw.astype(bf16))
        eac_s[r0:r0 + Q, :] = eac
        return y_diag + dfull_ref[...] * xs, ecol

    if prompt:
        y = jnp.concatenate([ssd_chunk(r0) for r0 in range(0, T, Q)], axis=0)
    else:
        to_seq = jnp.bitwise_or(jnp.left_shift(jnp.bitwise_and(row, SUBLANES - 1), SUB_SHIFT),
                                jnp.right_shift(row, SUB_SHIFT)) == col
        perm_b = jnp.where(to_seq, 1.0, 0.0).astype(bf16)
        parts = [ssd_chunk(r0) for r0 in range(0, T, Q)]

        ngrp = nseq // STATE_SEQS

        def seq_step(i, carry):
            slot = jnp.bitwise_and(i, STATE_BUFS - 1)
            ahead = i + (STATE_BUFS - 1)

            @pl.when(ahead < ngrp)
            def _():
                in_copy(ahead, jnp.bitwise_and(ahead, STATE_BUFS - 1)).start()

            in_copy(i, slot).wait()

            @pl.when(i >= STATE_BUFS)
            def _():
                out_copy(i - STATE_BUFS, slot).wait()

            for q in range(STATE_SEQS):
                sq = i * STATE_SEQS + q
                s0 = pl.multiple_of(sq * SEQ_S, SEQ_S)

                def h_get(g):
                    return h_in[slot, q, pl.ds(gsz * g, gsz), :]

                def h_set(g, v):
                    h_out[slot, q, pl.ds(gsz * g, gsz), :] = v

                e_row = jnp.left_shift(jnp.right_shift(sq, SUB_SHIFT), SUBTILE_SHIFT) + (Q - SUBLANES) \
                    + jnp.bitwise_and(sq, SUBLANES - 1)
                yoff_s[pl.ds(s0, SEQ_S), :] = ssd_state_io(
                    c_s[pl.ds(s0, SEQ_S), :], xw_s[pl.ds(s0, SEQ_S), :], bm_s[pl.ds(s0, SEQ_S), :],
                    eac_s[pl.ds(e_row, 1), :], h_get, h_set)
            out_copy(i, slot).start(priority=1)
            return carry

        lax.fori_loop(0, ngrp, seq_step, 0)
        for j in range(STATE_BUFS):
            out_copy(ngrp - STATE_BUFS + j, j).wait()
        perm_f = jnp.where(to_seq, 1.0, 0.0)
        y = jnp.concatenate(
            [part + _dot_exact(perm_f, yoff_s[r0:r0 + Q, :]) * ecol
             for r0, (part, ecol) in zip(range(0, T, Q), parts)], axis=0)
    y_ssd = _rms(y * _silu(proj(C_Z, C_XBC)), sng_ref[...])

    xr = conv(proj(C_LRU, C_LRU_G), prev_lru if prompt else None, None if prompt else c0_lru_ref,
              cw_lru_ref, cb_lru_ref, o_clru_ref)
    xr_b = xr.astype(bf16)
    hl = LRU_DIM // 2
    gates = jnp.concatenate([_dot(xr_b[:, hl * (k % 2):hl * (k % 2 + 1)], lru_w_ref[k]) for k in range(4)],
                            axis=1) + lru_b_ref[...]
    r_gate = jax.nn.sigmoid(gates[:, :LRU_DIM])
    i_gate = jax.nn.sigmoid(gates[:, LRU_DIM:])
    log_a = -LRU_C * r_gate * jax.nn.softplus(-lam_ref[...])
    a_t = jnp.exp(log_a)
    gain = jnp.sqrt(jnp.maximum(-jnp.tanh(log_a) * (a_t * a_t + 1.0), 0.0))
    a_s[...] = a_t
    b_s[...] = gain * i_gate * xr

    def vrow(ref, r0, k):
        return ref[r0 + SUBLANES * k:r0 + SUBLANES * (k + 1), :]

    def set_vrow(ref, r0, k, v):
        ref[r0 + SUBLANES * k:r0 + SUBLANES * (k + 1), :] = v

    if prompt:
        sub = sub_iota(LRU_DIM)
        carry = lru_c[...]
        for r0 in range(0, T, CHUNK):
            acc_a, acc_h = vrow(a_s, r0, 0), vrow(b_s, r0, 0)
            for k in range(1, SEG):
                a_k = vrow(a_s, r0, k)
                acc_h = a_k * acc_h + vrow(b_s, r0, k)
                acc_a = a_k * acc_a
                set_vrow(a_s, r0, k, acc_a)
                set_vrow(b_s, r0, k, acc_h)
            alpha = jnp.where(sub == 0, 0.0, pltpu.roll(acc_a, 1, 0))
            beta = jnp.where(sub == 0, jnp.broadcast_to(carry, (SUBLANES, LRU_DIM)), pltpu.roll(acc_h, 1, 0))
            for d in (1, 2, 4):
                a_sh = jnp.where(sub >= d, pltpu.roll(alpha, d, 0), 1.0)
                b_sh = jnp.where(sub >= d, pltpu.roll(beta, d, 0), 0.0)
                beta = alpha * b_sh + beta
                alpha = alpha * a_sh
            carry = (acc_a * beta + acc_h)[SUBLANES - 1:SUBLANES, :]
            for k in range(SEG):
                set_vrow(b_s, r0, k, vrow(b_s, r0, k) + vrow(a_s, r0, k) * beta)
        lru_c[...] = carry
        o_lru_ref[...] = carry
    else:
        for s in range(T // Q):
            h = h0_lru_ref[SUBLANES * s:SUBLANES * (s + 1), :]
            for k in range(Q // SUBLANES):
                h = vrow(a_s, Q * s, k) * h + vrow(b_s, Q * s, k)
                set_vrow(b_s, Q * s, k, h)
            o_lru_ref[SUBLANES * s:SUBLANES * (s + 1), :] = h
    y_lru = b_s[...] * _silu(proj(C_LRU_G, C_S5))

    u = proj(C_S5, C_S5_G)
    u_b = u.astype(bf16)
    half = S5_DIM // 2
    for k in range(2):
        uk = u_b[:, half * k:half * (k + 1)]
        bur_s[:, S5_HALF * k:S5_HALF * (k + 1)] = _dot(uk, bb_ref[k])
        bui_s[:, S5_HALF * k:S5_HALF * (k + 1)] = _dot(uk, bb_ref[2 + k])

    def tab(r0):
        return tab_ref[0, r0:r0 + SUBLANES, :], tab_ref[1, r0:r0 + SUBLANES, :]

    def cmul_add(pr, pi, xr, xi, yr, yi):
        return pr * xr - pi * xi + yr, pr * xi + pi * xr + yi

    ar, ai = tab(TAB_A)
    if prompt:
        sub = sub_iota(S5_FLAT)
        c_r, c_i = s5_cr[...], s5_ci[...]
        for r0 in range(0, T, CHUNK):
            hr, hi = vrow(bur_s, r0, 0), vrow(bui_s, r0, 0)
            for k in range(1, SEG):
                hr, hi = cmul_add(ar, ai, hr, hi, vrow(bur_s, r0, k), vrow(bui_s, r0, k))
                set_vrow(bur_s, r0, k, hr)
                set_vrow(bui_s, r0, k, hi)
            er = jnp.where(sub == 0, jnp.broadcast_to(c_r, (SUBLANES, S5_FLAT)), pltpu.roll(hr, 1, 0))
            ei = jnp.where(sub == 0, jnp.broadcast_to(c_i, (SUBLANES, S5_FLAT)), pltpu.roll(hi, 1, 0))
            for t, d in enumerate((1, 2, 4)):
                qr, qi = tab(TAB_Q + t * SUBLANES)
                er, ei = cmul_add(qr, qi, pltpu.roll(er, d, 0), pltpu.roll(ei, d, 0), er, ei)
            sr, si = tab(TAB_ASEG)
            nr, ni = cmul_add(sr, si, er, ei, hr, hi)
            c_r, c_i = nr[SUBLANES - 1:SUBLANES, :], ni[SUBLANES - 1:SUBLANES, :]
            for k in range(SEG):
                pr, pi = tab(TAB_PW + k * SUBLANES)
                vr, vi = cmul_add(pr, pi, er, ei, vrow(bur_s, r0, k), vrow(bui_s, r0, k))
                set_vrow(bur_s, r0, k, vr)
                set_vrow(bui_s, r0, k, vi)
        s5_cr[...] = c_r
        s5_ci[...] = c_i
        o_s5r_ref[...] = c_r
        o_s5i_ref[...] = c_i
    else:
        h0r = h0_s5r_ref[...].reshape(nseq, S5_FLAT)
        h0i = h0_s5i_ref[...].reshape(nseq, S5_FLAT)
        ends_r, ends_i = [], []
        for s in range(T // Q):
            rows = slice(SUBLANES * s, SUBLANES * (s + 1))
            hr, hi = h0r[rows, :], h0i[rows, :]
            for k in range(Q // SUBLANES):
                hr, hi = cmul_add(ar, ai, hr, hi, vrow(bur_s, Q * s, k), vrow(bui_s, Q * s, k))
                set_vrow(bur_s, Q * s, k, hr)
                set_vrow(bui_s, Q * s, k, hi)
            ends_r.append(hr)
            ends_i.append(hi)
        o_s5r_ref[...] = jnp.concatenate(ends_r, axis=0).reshape(nseq, S5_NGROUPS, S5_STATE)
        o_s5i_ref[...] = jnp.concatenate(ends_i, axis=0).reshape(nseq, S5_NGROUPS, S5_STATE)
    ys = []
    for k in range(2):
        hk = jnp.concatenate([bur_s[:, S5_HALF * k:S5_HALF * (k + 1)].astype(bf16),
                              bui_s[:, S5_HALF * k:S5_HALF * (k + 1)].astype(bf16)], axis=1)
        ys.append(_dot(hk, cc_ref[k]))
    ys5 = jnp.concatenate(ys, axis=1) + s5d_ref[...] * u
    ys5 = jax.nn.gelu(ys5)
    ys5 = ys5 * jax.nn.sigmoid(_dot(ys5.astype(bf16), glu_w_ref[...]) + glu_b_ref[...])
    y_s5 = ys5 * _silu(proj(C_S5_G, C_DT))

    ycat = jnp.concatenate([y_ssd.astype(bf16), y_lru.astype(bf16), y_s5.astype(bf16)], axis=1)
    return x + _dot(ycat, w_out_ref[...])


def _prompt_body(final, *refs):
    x_ref = refs[0]
    w = refs[1:1 + N_WEIGHTS]
    y_ref = refs[1 + N_WEIGHTS]
    o = refs[2 + N_WEIGHTS:8 + N_WEIGHTS]
    scr = refs[8 + N_WEIGHTS:]
    h_ssd, prev_ssd, prev_lru = scr[:3]
    lru_c, s5_cr, s5_ci = scr[-3:]

    @pl.when(pl.program_id(1) == 0)
    def _():
        h_ssd[...] = jnp.zeros_like(h_ssd)
        prev_ssd[...] = jnp.zeros_like(prev_ssd)
        prev_lru[...] = jnp.zeros_like(prev_lru)
        lru_c[...] = jnp.zeros_like(lru_c)
        s5_cr[...] = jnp.zeros_like(s5_cr)
        s5_ci[...] = jnp.zeros_like(s5_ci)

    for k in range(TILES_PER_STEP):
        rows = slice(TILE_P * k, TILE_P * (k + 1))
        scr_k = scr[:3] + scr[3 + 4 * k:7 + 4 * k] + scr[3 + 4 * TILES_PER_STEP:]
        out = _layer_math(True, TILE_P, x_ref[rows, :], w, None, o, scr_k)
        if final:
            out = _rms(out, w[-1][...])
        y_ref[rows, :] = out

    @pl.when(pl.program_id(1) == pl.num_programs(1) - 1)
    def _():
        o[0][...] = h_ssd[...].T


def _sample_body(*refs):
    x_ref = refs[0]
    st = refs[1:7]
    w = refs[7:7 + N_WEIGHTS]
    y_ref = refs[7 + N_WEIGHTS]
    o = refs[8 + N_WEIGHTS:14 + N_WEIGHTS]
    x_all = refs[14 + N_WEIGHTS]
    scr = refs[15 + N_WEIGHTS:]
    layer = pl.program_id(0)
    last_layer = layer == pl.num_programs(0) - 1
    r0 = pl.multiple_of(pl.program_id(1) * TILE_S, TILE_S)

    @pl.when(layer == 0)
    def _():
        x_all[pl.ds(r0, TILE_S), :] = x_ref[...]

    out = _layer_math(False, TILE_S, x_all[pl.ds(r0, TILE_S), :], w, st, o, scr)
    x_all[pl.ds(r0, TILE_S), :] = out

    @pl.when(last_layer)
    def _():
        y_ref[...] = _rms(out, w[-1][...])

    @pl.when(jnp.logical_not(last_layer))
    def _():
        y_ref[...] = out


def _prompt_call(layer, final, x, weights):
    T = TILE_P
    nb, seq, _ = x.shape

    def wspec(a):
        nd = a.ndim - 1
        return pl.BlockSpec((None,) + a.shape[1:], lambda b, c: (layer,) + (0,) * nd, pipeline_mode=pl.Buffered(1))

    def st(shape):
        nd = len(shape)
        return pl.BlockSpec((None,) + shape, lambda b, c: (b,) + (0,) * nd)

    step_rows = T * TILES_PER_STEP
    x_spec = pl.BlockSpec((None, step_rows, D_MODEL), lambda b, c: (b, c, 0))
    out_specs = [x_spec, st((SSD_DIM, SSD_STATE)), st((CONV_WIDTH - 1, SSD_CONV_DIM)), st((1, LRU_DIM)),
                 st((CONV_WIDTH - 1, LRU_DIM)), st((1, S5_FLAT)), st((1, S5_FLAT))]
    out_shape = [jax.ShapeDtypeStruct(x.shape, f32),
                 jax.ShapeDtypeStruct((nb, SSD_DIM, SSD_STATE), f32),
                 jax.ShapeDtypeStruct((nb, CONV_WIDTH - 1, SSD_CONV_DIM), f32),
                 jax.ShapeDtypeStruct((nb, 1, LRU_DIM), f32),
                 jax.ShapeDtypeStruct((nb, CONV_WIDTH - 1, LRU_DIM), f32),
                 jax.ShapeDtypeStruct((nb, 1, S5_FLAT), f32),
                 jax.ShapeDtypeStruct((nb, 1, S5_FLAT), f32)]
    scratch = [pltpu.VMEM((SSD_STATE, SSD_DIM), f32),
               pltpu.VMEM((HALO, SSD_CONV_DIM), f32),
               pltpu.VMEM((HALO, LRU_DIM), f32)]
    scratch += [pltpu.VMEM((T, LRU_DIM), f32), pltpu.VMEM((T, LRU_DIM), f32),
                pltpu.VMEM((T, S5_FLAT), f32), pltpu.VMEM((T, S5_FLAT), f32)] * TILES_PER_STEP
    scratch += [pltpu.VMEM((1, LRU_DIM), f32), pltpu.VMEM((1, S5_FLAT), f32), pltpu.VMEM((1, S5_FLAT), f32)]
    return pl.pallas_call(
        functools.partial(_prompt_body, final),
        grid=(nb, seq // step_rows), in_specs=[x_spec] + [wspec(a) for a in weights],
        out_specs=out_specs, out_shape=out_shape, scratch_shapes=scratch,
        compiler_params=pltpu.CompilerParams(dimension_semantics=("arbitrary", "arbitrary"),
                                             vmem_limit_bytes=VMEM_LIMIT_BYTES),
        name="layer_prompt",
    )(x, *weights)


def _sample_call(x, states, weights):
    T = TILE_S
    rows = x.shape[0]
    depth = weights[0].shape[0]
    nseq = T // SEQ_S

    def wspec(a):
        nd = a.ndim - 1
        return pl.BlockSpec((None,) + a.shape[1:], lambda l, i: (l,) + (0,) * nd, pipeline_mode=pl.Buffered(1))

    def st(a):
        nd = a.ndim - 2
        if a.ndim == 4 and a.shape[2:] == (SSD_DIM, SSD_STATE):
            return pl.BlockSpec(memory_space=pl.ANY)
        if a.ndim == 6:
            return pl.BlockSpec((None, None) + a.shape[2:], lambda l, i: (l, i) + (0,) * nd)
        return pl.BlockSpec((None, nseq) + a.shape[2:], lambda l, i: (l, i) + (0,) * nd)

    x_spec = pl.BlockSpec((T, D_MODEL), lambda l, i: (i, 0))
    st_specs = [st(a) for a in states]
    scratch = [pltpu.VMEM((rows, D_MODEL), f32),
               pltpu.VMEM((T, LRU_DIM), f32), pltpu.VMEM((T, LRU_DIM), f32),
               pltpu.VMEM((T, S5_FLAT), f32), pltpu.VMEM((T, S5_FLAT), f32),
               pltpu.VMEM((T, SSD_BC), f32), pltpu.VMEM((T, SSD_BC), f32),
               pltpu.VMEM((T, SSD_DIM), f32), pltpu.VMEM((T, SSD_DIM), f32),
               pltpu.VMEM((T, LANES), f32),
               pltpu.VMEM((STATE_BUFS, STATE_SEQS, SSD_DIM, SSD_STATE), f32),
               pltpu.VMEM((STATE_BUFS, STATE_SEQS, SSD_DIM, SSD_STATE), f32),
               pltpu.SemaphoreType.DMA((STATE_BUFS,)), pltpu.SemaphoreType.DMA((STATE_BUFS,))]
    return pl.pallas_call(
        _sample_body,
        grid=(depth, rows // T), in_specs=[x_spec] + st_specs + [wspec(a) for a in weights],
        out_specs=[pl.BlockSpec((None, T, D_MODEL), lambda l, i: (l, i, 0))] + st_specs,
        out_shape=[jax.ShapeDtypeStruct((depth,) + x.shape, f32)] + [jax.ShapeDtypeStruct(a.shape, f32) for a in states],
        scratch_shapes=scratch,
        compiler_params=pltpu.CompilerParams(dimension_semantics=("arbitrary", "arbitrary"),
                                             vmem_limit_bytes=VMEM_LIMIT_BYTES),
        name="layers_sample",
    )(x, *states, *weights)


def _pad_lanes(v):
    return jnp.pad(v, [(0, 0)] * (v.ndim - 1) + [(0, LANES - v.shape[-1])])


def kernel(x_prompt, x_sample, state_ssd, state_ssd_conv, state_lru, state_lru_conv, state_s5_re, state_s5_im, norm_g, w_in, ssd_conv_w, ssd_conv_b, ssd_dt_bias, ssd_a_log, ssd_d, ssd_norm_g, lru_conv_w, lru_conv_b, lru_wa, lru_ba, lru_wx, lru_bx, lru_lambda, s5_lambda_re, s5_lambda_im, s5_log_dt, s5_b_re, s5_b_im, s5_c_re, s5_c_im, s5_d, s5_glu_w, s5_glu_b, w_out, final_norm_g):
    depth = w_in.shape[0]
    nbp = x_prompt.shape[0]
    nbs, ls, _ = x_sample.shape
    assert ls == SEQ_S and x_prompt.shape[1] % (TILE_P * TILES_PER_STEP) == 0 and (nbs * ls) % TILE_S == 0

    tab, bb, cc, lru_w = _s5_prep(s5_lambda_re.astype(f32), s5_lambda_im.astype(f32), s5_log_dt.astype(f32),
                                  s5_b_re.astype(f32), s5_b_im.astype(f32), s5_c_re.astype(f32),
                                  s5_c_im.astype(f32), lru_wa.astype(f32), lru_wx.astype(f32))

    def row(v, n):
        return v.astype(f32).reshape(depth, 1, n)

    wi = w_in.astype(bf16)
    w_in_r = jnp.concatenate([wi[..., 0:3072], wi[..., 3088:5136], _pad_lanes(wi[..., 3072:3088])], axis=-1)

    weights = (
        row(norm_g, D_MODEL), w_in_r,
        ssd_conv_w.astype(f32), row(ssd_conv_b, SSD_CONV_DIM),
        _pad_lanes(row(ssd_dt_bias, SSD_HEADS)), _pad_lanes(row(ssd_a_log, SSD_HEADS)),
        jnp.repeat(ssd_d.astype(f32), SSD_HEADDIM, axis=-1).reshape(depth, 1, SSD_DIM),
        row(ssd_norm_g, SSD_DIM),
        lru_conv_w.astype(f32), row(lru_conv_b, LRU_DIM),
        lru_w,
        jnp.concatenate([lru_ba, lru_bx], axis=-1).astype(f32).reshape(depth, 1, 2 * LRU_DIM),
        row(lru_lambda, LRU_DIM),
        tab, bb, cc, row(s5_d, S5_DIM),
        s5_glu_w.astype(bf16), row(s5_glu_b, S5_DIM),
        w_out.astype(bf16),
        jnp.broadcast_to(final_norm_g.astype(f32).reshape(1, 1, D_MODEL), (depth, 1, D_MODEL)),
    )

    seq_p = x_prompt.shape[1]
    xp = jnp.swapaxes(x_prompt.astype(f32).reshape(nbp, seq_p // CHUNK, SUBLANES, SEG, D_MODEL), 2, 3)
    xp = xp.reshape(nbp, seq_p, D_MODEL)
    outs_p = [[] for _ in range(6)]
    for i in range(depth):
        res = _prompt_call(i, i == depth - 1, xp, weights)
        xp = res[0]
        for j in range(6):
            outs_p[j].append(res[1 + j])
    y_prompt = jnp.swapaxes(xp.reshape(nbp, seq_p // CHUNK, SEG, SUBLANES, D_MODEL), 2, 3).reshape(nbp, seq_p, D_MODEL)

    ntile = nbs // SUBLANES
    nsub = TILE_S // SUBTILE_S

    def conv_in(v):
        v = jnp.swapaxes(v.astype(f32).reshape(depth, ntile, SUBLANES, CONV_WIDTH - 1, v.shape[-1]), 2, 3)
        return v.reshape(depth, ntile // nsub, nsub, CONV_WIDTH - 1, SUBLANES, v.shape[-1])

    def conv_out(v, dtype):
        v = v.reshape(depth, ntile, CONV_WIDTH - 1, SUBLANES, v.shape[-1])
        return jnp.swapaxes(v, 2, 3).reshape(depth, nbs, CONV_WIDTH - 1, v.shape[-1]).astype(dtype)

    xs = jnp.swapaxes(x_sample.astype(f32).reshape(ntile, SUBLANES, ls, D_MODEL), 1, 2).reshape(nbs * ls, D_MODEL)
    states_s = (state_ssd.astype(f32).reshape(depth, nbs, SSD_DIM, SSD_STATE), conv_in(state_ssd_conv),
                state_lru.astype(f32), conv_in(state_lru_conv),
                state_s5_re.astype(f32), state_s5_im.astype(f32))
    weights_s = tuple(w[:, :, TAB_A:TAB_A + SUBLANES] if w is tab else w for w in weights)
    res_s = _sample_call(xs, states_s, weights_s)
    y_sample = jnp.swapaxes(res_s[0][depth - 1].reshape(ntile, ls, SUBLANES, D_MODEL), 1, 2).reshape(nbs, ls, D_MODEL)

    def stack(lst, shape, dtype):
        return jnp.stack(lst).reshape((depth,) + shape).astype(dtype)

    ssd_shape = (SSD_HEADS, SSD_HEADDIM, SSD_STATE)
    s5_shape = (S5_NGROUPS, S5_STATE)
    return (
        y_prompt.astype(x_prompt.dtype), y_sample.astype(x_sample.dtype),
        stack(outs_p[0], (nbp,) + ssd_shape, state_ssd.dtype),
        res_s[1].reshape((depth, nbs) + ssd_shape).astype(state_ssd.dtype),
        stack(outs_p[1], (nbp, CONV_WIDTH - 1, SSD_CONV_DIM), state_ssd_conv.dtype),
        conv_out(res_s[2], state_ssd_conv.dtype),
        stack(outs_p[2], (nbp, LRU_DIM), state_lru.dtype), res_s[3].astype(state_lru.dtype),
        stack(outs_p[3], (nbp, CONV_WIDTH - 1, LRU_DIM), state_lru_conv.dtype),
        conv_out(res_s[4], state_lru_conv.dtype),
        stack(outs_p[4], (nbp,) + s5_shape, state_s5_re.dtype),
        res_s[5].reshape((depth, nbs) + s5_shape).astype(state_s5_re.dtype),
        stack(outs_p[5], (nbp,) + s5_shape, state_s5_im.dtype),
        res_s[6].reshape((depth, nbs) + s5_shape).astype(state_s5_im.dtype),
    )
```

```python
import functools

import jax
import jax.numpy as jnp
from jax import lax
from jax.experimental import pallas as pl
from jax.experimental.pallas import tpu as pltpu

f32 = jnp.float32
bf16 = jnp.bfloat16

D_MODEL = 1024
CONV_WIDTH = 4
SSD_DIM = 1024
SSD_HEADDIM = 64
SSD_HEADS = 16
SSD_GROUPS = 4
SSD_HPG = 4
SSD_STATE = 128
SSD_BC = SSD_GROUPS * SSD_STATE
SSD_CONV_DIM = SSD_DIM + 2 * SSD_BC
LRU_DIM = 512
LRU_BLOCKS = 8
LRU_BLOCK = LRU_DIM // LRU_BLOCKS
LRU_C = 8.0
S5_DIM = 512
S5_GROUP = 16
S5_NGROUPS = 32
S5_STATE = 64
S5_FLAT = S5_NGROUPS * S5_STATE
S5_HALF = S5_FLAT // 2
EPS = 1e-6

LANES = 128
SUBLANES = 8
CHUNK = 128
TILE_P = 256
TILES_PER_STEP = 1
TILE_S = 256
SUBTILE_S = 64
SUBTILE_SHIFT = 6
SEQ_S = 8
STATE_BUFS = 4
STATE_SEQS = 2
NEG = -1e30

SEG = CHUNK // SUBLANES
SEG_SHIFT = 4
SUB_SHIFT = 3
HALO = (CONV_WIDTH - 1) * SUBLANES

TAB_A = 0
TAB_Q = TAB_A + SUBLANES
TAB_ASEG = TAB_Q + 3 * SUBLANES
TAB_PW = TAB_ASEG + SUBLANES
TAB_ROWS = TAB_PW + SEG * SUBLANES

C_Z = 0
C_XBC = C_Z + SSD_DIM
C_LRU = C_XBC + SSD_CONV_DIM
C_LRU_G = C_LRU + LRU_DIM
C_S5 = C_LRU_G + LRU_DIM
C_S5_G = C_S5 + S5_DIM
C_DT = C_S5_G + S5_DIM
IN_COLS = C_DT + LANES

VMEM_LIMIT_BYTES = 56 * 1024 * 1024

N_WEIGHTS = 21


def _rms(x, g):
    return x * lax.rsqrt(jnp.mean(x * x, axis=-1, keepdims=True) + EPS) * g


def _silu(x):
    return x * jax.nn.sigmoid(x)


def _dot(a, b):
    return jnp.dot(a, b, preferred_element_type=f32)


def _dot_nt(a, b):
    return lax.dot_general(a, b, (((1,), (1,)), ((), ())), preferred_element_type=f32)


def _dot_tn(a, b):
    return lax.dot_general(a, b, (((0,), (0,)), ((), ())), preferred_element_type=f32)


def _dot_exact(a, b):
    return jnp.dot(a, b, preferred_element_type=f32, precision=lax.Precision.HIGHEST)


def _pair_expand(v, j, lane_lo):
    q = v.shape[0]
    lo = jnp.broadcast_to(v[:, 2 * j:2 * j + 1], (q, LANES))
    hi = jnp.broadcast_to(v[:, 2 * j + 1:2 * j + 2], (q, LANES))
    return jnp.where(lane_lo, lo, hi)


def _s5_prep_body(lre_ref, lim_ref, ldt_ref, lre_g_ref, lim_g_ref, ldt_g_ref,
                  bre_ref, bim_ref, cre_ref, cim_ref, wa_ref, wx_ref, tre_ref, tim_ref, bb_ref, cc_ref, lw_ref):
    def abar(lre, lim, ldt):
        delta = jnp.exp(ldt)
        mag = jnp.exp(lre * delta)
        return mag * jnp.cos(lim * delta), mag * jnp.sin(lim * delta)

    ar, ai = abar(lre_ref[...], lim_ref[...], ldt_ref[...])

    def cmul(xr, xi, yr, yi):
        return xr * yr - xi * yi, xr * yi + xi * yr

    pw = [(ar, ai)]
    for _ in range(SEG - 1):
        pw.append(cmul(*pw[-1], ar, ai))
    seg = [pw[SEG - 1]]
    for _ in range(2):
        seg.append(cmul(*seg[-1], *seg[-1]))
    zero = jnp.zeros_like(ar)

    def put(i, v):
        tre_ref[i] = v[0]
        tim_ref[i] = v[1]

    for r in range(SUBLANES):
        put(TAB_A + r, pw[0])
        put(TAB_ASEG + r, seg[0])
        for t, d in enumerate((1, 2, 4)):
            put(TAB_Q + t * SUBLANES + r, seg[t] if r >= d else (zero, zero))
        for k in range(SEG):
            put(TAB_PW + k * SUBLANES + r, pw[k])

    lre, lim = lre_g_ref[...], lim_g_ref[...]
    ar, ai = abar(lre, lim, ldt_g_ref[...])
    denom = lre * lre + lim * lim
    nr = ar - 1.0
    ni = ai

    def per_channel(v):
        return jnp.broadcast_to(v[:, None, :], (S5_NGROUPS, S5_GROUP, S5_STATE)).reshape(S5_DIM, S5_STATE)

    coef_re = per_channel((nr * lre + ni * lim) / denom)
    coef_im = per_channel((ni * lre - nr * lim) / denom)
    bre, bim = bre_ref[...], bim_ref[...]
    bbar = (coef_re * bre - coef_im * bim, coef_re * bim + coef_im * bre)
    bb_ref[...] = jnp.zeros(bb_ref.shape, bf16)
    gh = S5_NGROUPS // 2
    for part in range(2):
        for half in range(2):
            for g in range(gh):
                r0 = (half * gh + g) * S5_GROUP
                bb_ref[2 * part + half, S5_GROUP * g:S5_GROUP * (g + 1), S5_STATE * g:S5_STATE * (g + 1)] = \
                    bbar[part][r0:r0 + S5_GROUP, :].astype(bf16)
    cc_ref[...] = jnp.zeros(cc_ref.shape, bf16)
    for part, (c_ref, sign) in enumerate(((cre_ref, 1.0), (cim_ref, -1.0))):
        for half in range(2):
            for g in range(gh):
                row0 = part * S5_HALF + S5_STATE * g
                cc_ref[half, row0:row0 + S5_STATE, S5_GROUP * g:S5_GROUP * (g + 1)] = \
                    (sign * c_ref[half * gh + g]).astype(bf16)
    lw_ref[...] = jnp.zeros(lw_ref.shape, bf16)
    nb = LRU_BLOCKS // 2
    for m, w_ref in enumerate((wa_ref, wx_ref)):
        for half in range(2):
            for b in range(nb):
                lw_ref[2 * m + half, LRU_BLOCK * b:LRU_BLOCK * (b + 1), LRU_BLOCK * b:LRU_BLOCK * (b + 1)] = \
                    w_ref[half * nb + b].astype(bf16)


def _s5_prep(lam_re, lam_im, log_dt, b_re, b_im, c_re, c_im, lru_wa, lru_wx):
    depth = lam_re.shape[0]
    rows_c = S5_FLAT // LANES
    ldt = jnp.broadcast_to(log_dt[:, :, None], (depth, S5_NGROUPS, S5_STATE))

    def bt(v):
        return jnp.transpose(v, (0, 1, 3, 2)).reshape(depth, S5_DIM, S5_STATE)

    cspec = pl.BlockSpec((None, rows_c, LANES), lambda i: (i, 0, 0))
    gspec = pl.BlockSpec((None, S5_NGROUPS, S5_STATE), lambda i: (i, 0, 0))
    rspec = pl.BlockSpec((None, S5_DIM, S5_STATE), lambda i: (i, 0, 0))
    tspec = pl.BlockSpec((None, TAB_ROWS, rows_c, LANES), lambda i: (i, 0, 0, 0))
    def whole(shape):
        nd = len(shape)
        return pl.BlockSpec((None,) + shape, lambda i: (i,) + (0,) * nd)

    bb_shape = (4, S5_DIM // 2, S5_HALF)
    cc_shape = (2, 2 * S5_HALF, S5_DIM // 2)
    lw_shape = (4, LRU_DIM // 2, LRU_DIM // 2)
    c_shape = (S5_NGROUPS, S5_STATE, S5_GROUP)
    w_shape = (LRU_BLOCKS, LRU_BLOCK, LRU_BLOCK)
    tre, tim, bb, cc, lw = pl.pallas_call(
        _s5_prep_body,
        grid=(depth,),
        in_specs=[cspec, cspec, cspec, gspec, gspec, gspec, rspec, rspec,
                  whole(c_shape), whole(c_shape), whole(w_shape), whole(w_shape)],
        out_specs=[tspec, tspec, whole(bb_shape), whole(cc_shape), whole(lw_shape)],
        out_shape=[jax.ShapeDtypeStruct((depth, TAB_ROWS, rows_c, LANES), f32)] * 2
        + [jax.ShapeDtypeStruct((depth,) + s, bf16) for s in (bb_shape, cc_shape, lw_shape)],
        name="s5_prep",
    )(lam_re.reshape(depth, rows_c, LANES), lam_im.reshape(depth, rows_c, LANES),
      ldt.reshape(depth, rows_c, LANES), lam_re, lam_im, ldt, bt(b_re), bt(b_im),
      jnp.swapaxes(c_re, 2, 3), jnp.swapaxes(c_im, 2, 3), lru_wa, lru_wx)
    tab = jnp.stack([tre, tim], axis=1).reshape(depth, 2, TAB_ROWS, S5_FLAT)
    return tab, bb, cc, lw


def _layer_math(prompt, T, x, w, st, o, scr):
    (ng_ref, w_in_ref, cw_ssd_ref, cb_ssd_ref, dtb_ref, alog_ref, dfull_ref, sng_ref,
     cw_lru_ref, cb_lru_ref, lru_w_ref, lru_b_ref, lam_ref,
     tab_ref, bb_ref, cc_ref, s5d_ref, glu_w_ref, glu_b_ref, w_out_ref, _) = w
    o_ssd_ref, o_cssd_ref, o_lru_ref, o_clru_ref, o_s5r_ref, o_s5i_ref = o
    if prompt:
        h_ssd, prev_ssd, prev_lru, a_s, b_s, bur_s, bui_s, lru_c, s5_cr, s5_ci = scr
    else:
        h0_ssd_hbm, c0_ssd_ref, h0_lru_ref, c0_lru_ref, h0_s5r_ref, h0_s5i_ref = st
        a_s, b_s, bur_s, bui_s, c_s, bm_s, xw_s, yoff_s, eac_s, h_in, h_out, sem_in, sem_out = scr
    nseq = T // SEQ_S
    if not prompt:
        layer = pl.program_id(0)
        seq0 = pl.program_id(1) * nseq

        def in_copy(i, slot):
            return pltpu.make_async_copy(h0_ssd_hbm.at[layer, pl.ds(seq0 + i * STATE_SEQS, STATE_SEQS)],
                                         h_in.at[slot], sem_in.at[slot])

        def out_copy(i, slot):
            return pltpu.make_async_copy(h_out.at[slot],
                                         o_ssd_ref.at[layer, pl.ds(seq0 + i * STATE_SEQS, STATE_SEQS)],
                                         sem_out.at[slot])

        for j in range(STATE_BUFS):
            in_copy(j, j).start()
    Q = CHUNK if prompt else SUBTILE_S

    hn = _rms(x, ng_ref[...]).astype(bf16)

    def proj(lo, hi):
        return _dot(hn, w_in_ref[:, lo:hi])

    def sub_iota(n):
        return lax.broadcasted_iota(jnp.int32, (SUBLANES, n), 0)

    def conv_taps(halo, rs, cw_ref, cb_ref):
        ext = jnp.concatenate([halo, rs], axis=0)
        n = rs.shape[0]
        acc = cb_ref[...] + cw_ref[3:4, :] * rs
        for j in range(1, CONV_WIDTH):
            acc = acc + cw_ref[3 - j:4 - j, :] * ext[HALO - SUBLANES * j:HALO - SUBLANES * j + n, :]
        return acc

    def conv(raw, prev_ref, c0_ref, cw_ref, cb_ref, o_ref):
        cdim = raw.shape[1]
        if not prompt:
            outs = []
            for s in range(T // Q):
                rs = raw[Q * s:Q * (s + 1), :]
                o_ref[s] = rs[Q - HALO:, :].reshape(CONV_WIDTH - 1, SUBLANES, cdim)
                outs.append(conv_taps(c0_ref[s].reshape(HALO, cdim), rs, cw_ref, cb_ref))
            return jnp.concatenate(outs, axis=0)
        first = sub_iota(cdim) == 0
        tail = prev_ref[...]
        outs = []
        for r0 in range(0, T, CHUNK):
            rs = raw[r0:r0 + CHUNK, :]
            cur = rs[CHUNK - HALO:, :]
            halo = jnp.concatenate(
                [jnp.where(first, pltpu.roll(tail[SUBLANES * k:SUBLANES * (k + 1), :], 1, 0),
                           pltpu.roll(cur[SUBLANES * k:SUBLANES * (k + 1), :], 1, 0))
                 for k in range(CONV_WIDTH - 1)], axis=0)
            outs.append(conv_taps(halo, rs, cw_ref, cb_ref))
            tail = cur
        prev_ref[...] = tail
        for k in range(CONV_WIDTH - 1):
            o_ref[k:k + 1, :] = tail[SUBLANES * k + SUBLANES - 1:SUBLANES * (k + 1), :]
        return jnp.concatenate(outs, axis=0)

    row = lax.broadcasted_iota(jnp.int32, (Q, Q), 0)
    col = lax.broadcasted_iota(jnp.int32, (Q, Q), 1)
    if prompt:
        def local_time(i):
            return jnp.bitwise_or(jnp.left_shift(jnp.bitwise_and(i, SUBLANES - 1), SEG_SHIFT),
                                  jnp.right_shift(i, SUB_SHIFT))
        causal = local_time(row) >= local_time(col)
    else:
        same_seq = jnp.bitwise_and(row, SUBLANES - 1) == jnp.bitwise_and(col, SUBLANES - 1)
        causal = jnp.logical_and(same_seq, jnp.right_shift(row, SUB_SHIFT) >= jnp.right_shift(col, SUB_SHIFT))
    tril = jnp.where(causal, 1.0, 0.0)
    lane_lo = lax.broadcasted_iota(jnp.int32, (Q, LANES), 1) < SSD_HEADDIM
    gsz = SSD_HPG * SSD_HEADDIM

    xbc = _silu(conv(proj(C_XBC, C_LRU), prev_ssd if prompt else None, None if prompt else c0_ssd_ref,
                     cw_ssd_ref, cb_ssd_ref, o_cssd_ref))
    dt_all = jax.nn.softplus(proj(C_DT, IN_COLS) + dtb_ref[...])
    a_neg = -jnp.exp(alog_ref[...])

    def ssd_state_io(rows_c, rows_xw, rows_b, e_last, h_get, h_set):
        outs = []
        for g in range(SSD_GROUPS):
            hp = h_get(g)
            outs.append(_dot_nt(rows_c[:, LANES * g:LANES * (g + 1)].astype(bf16), hp.astype(bf16)))
            sg = _dot_tn(rows_xw[:, gsz * g:gsz * (g + 1)].astype(bf16),
                         rows_b[:, LANES * g:LANES * (g + 1)].astype(bf16))
            dec = jnp.concatenate(
                [jnp.broadcast_to(e_last[:, SSD_HPG * g + k:SSD_HPG * g + k + 1], (SSD_HEADDIM, SSD_STATE))
                 for k in range(SSD_HPG)], axis=0)
            h_set(g, dec * hp + sg)
        return jnp.concatenate(outs, axis=1)

    def ssd_chunk(r0):
        xs = xbc[r0:r0 + Q, :SSD_DIM]
        bm = xbc[r0:r0 + Q, SSD_DIM:SSD_DIM + SSD_BC]
        cm = xbc[r0:r0 + Q, SSD_DIM + SSD_BC:]
        dt = dt_all[r0:r0 + Q, :]
        bm_b = bm.astype(bf16)
        cm_b = cm.astype(bf16)
        acum = _dot_exact(tril, dt * a_neg)
        acum_row = acum.T
        dt_row = dt.T
        scores = [_dot_nt(cm_b[:, LANES * g:LANES * (g + 1)], bm_b[:, LANES * g:LANES * (g + 1)])
                  for g in range(SSD_GROUPS)]
        if prompt:
            arow = acum_row[0:SSD_HEADS, :]
            w_row = jnp.exp(arow[:, Q - 1:Q] - arow) * dt_row[0:SSD_HEADS, :]
            e_end = jnp.exp(acum[Q - 1:Q, :])
            lane1 = lane_lo[0:1, :]
            bts = [bm[:, LANES * g:LANES * (g + 1)].T for g in range(SSD_GROUPS)]
            y_pairs = []
            for j in range(SSD_HEADS // 2):
                g = (2 * j) // SSD_HPG
                cm_g = cm[:, LANES * g:LANES * (g + 1)]
                bt_g = bts[g]
                lhs_y, lhs_s = [], []
                for h in (2 * j, 2 * j + 1):
                    colb = jnp.broadcast_to(acum[:, h:h + 1], (Q, LANES))
                    decay = jnp.exp(jnp.where(causal, colb - acum_row[h:h + 1, :], NEG))
                    lhs_y.append((scores[g] * decay * dt_row[h:h + 1, :]).astype(bf16))
                    lhs_s.append((bt_g * w_row[h:h + 1, :]).astype(bf16))
                for h in (2 * j, 2 * j + 1):
                    colb = jnp.broadcast_to(acum[:, h:h + 1], (Q, LANES))
                    lhs_y.append((jnp.exp(colb) * cm_g).astype(bf16))
                xp = xs[:, LANES * j:LANES * (j + 1)]
                hp = h_ssd[:, LANES * j:LANES * (j + 1)]
                xbd = jnp.concatenate([jnp.where(lane_lo, xp, 0.0), jnp.where(lane_lo, 0.0, xp)],
                                      axis=0).astype(bf16)
                hbd = jnp.concatenate([jnp.where(lane_lo, hp, 0.0), jnp.where(lane_lo, 0.0, hp)],
                                      axis=0).astype(bf16)
                y_pairs.append(_dot(jnp.concatenate(lhs_y, axis=1), jnp.concatenate([xbd, hbd], axis=0)))
                dec = jnp.where(lane1, jnp.broadcast_to(e_end[:, 2 * j:2 * j + 1], (1, LANES)),
                                jnp.broadcast_to(e_end[:, 2 * j + 1:2 * j + 2], (1, LANES)))
                h_ssd[:, LANES * j:LANES * (j + 1)] = dec * hp + _dot(jnp.concatenate(lhs_s, axis=1), xbd)
            return jnp.concatenate(y_pairs, axis=1) + dfull_ref[...] * xs

        eac = jnp.exp(acum)
        sel = jnp.where(col == jnp.bitwise_and(row, SUBLANES - 1) + (Q - SUBLANES), 1.0, 0.0)
        acum_end = _dot_exact(sel, acum)
        wgt = jnp.exp(acum_end - acum) * dt
        y_pairs, xw_pairs, ecol_pairs = [], [], []
        for j in range(SSD_HEADS // 2):
            g = (2 * j) // SSD_HPG
            ms = []
            for h in (2 * j, 2 * j + 1):
                diff = acum[:, h:h + 1] - acum_row[h:h + 1, :]
                decay = jnp.exp(jnp.where(causal, diff, NEG))
                ms.append((scores[g] * decay * dt_row[h:h + 1, :]).astype(bf16))
            xp = xs[:, LANES * j:LANES * (j + 1)]
            xbd = jnp.concatenate([jnp.where(lane_lo, xp, 0.0), jnp.where(lane_lo, 0.0, xp)], axis=0).astype(bf16)
            y_pairs.append(_dot(jnp.concatenate(ms, axis=1), xbd))
            xw_pairs.append(xp * _pair_expand(wgt, j, lane_lo))
            ecol_pairs.append(_pair_expand(eac, j, lane_lo))
        y_diag = jnp.concatenate(y_pairs, axis=1)
        xw = jnp.concatenate(xw_pairs, axis=1)
        ecol = jnp.concatenate(ecol_pairs, axis=1)

        c_s[r0:r0 + Q, :] = _dot(perm_b, cm_b)
        bm_s[r0:r0 + Q, :] = _dot(perm_b, bm_b)
        xw_s[r0:r0 + Q, :] = _dot(perm_b, xw.astype(bf16))
        eac_s[r0:r0 + Q, :] = eac
        return y_diag + dfull_ref[...] * xs, ecol

    if prompt:
        y = jnp.concatenate([ssd_chunk(r0) for r0 in range(0, T, Q)], axis=0)
    else:
        to_seq = jnp.bitwise_or(jnp.left_shift(jnp.bitwise_and(row, SUBLANES - 1), SUB_SHIFT),
                                jnp.right_shift(row, SUB_SHIFT)) == col
        perm_b = jnp.where(to_seq, 1.0, 0.0).astype(bf16)
        parts = [ssd_chunk(r0) for r0 in range(0, T, Q)]

        ngrp = nseq // STATE_SEQS

        def seq_step(i, carry):
            slot = jnp.bitwise_and(i, STATE_BUFS - 1)
            in_copy(i, slot).wait()

            @pl.when(i >= STATE_BUFS)
            def _():
                out_copy(i - STATE_BUFS, slot).wait()

            for q in range(STATE_SEQS):
                sq = i * STATE_SEQS + q
                s0 = pl.multiple_of(sq * SEQ_S, SEQ_S)

                def h_get(g):
                    return h_in[slot, q, pl.ds(gsz * g, gsz), :]

                def h_set(g, v):
                    h_out[slot, q, pl.ds(gsz * g, gsz), :] = v

                e_row = jnp.left_shift(jnp.right_shift(sq, SUB_SHIFT), SUBTILE_SHIFT) + (Q - SUBLANES) \
                    + jnp.bitwise_and(sq, SUBLANES - 1)
                yoff_s[pl.ds(s0, SEQ_S), :] = ssd_state_io(
                    c_s[pl.ds(s0, SEQ_S), :], xw_s[pl.ds(s0, SEQ_S), :], bm_s[pl.ds(s0, SEQ_S), :],
                    eac_s[pl.ds(e_row, 1), :], h_get, h_set)
            out_copy(i, slot).start()

            @pl.when(i + STATE_BUFS < ngrp)
            def _():
                in_copy(i + STATE_BUFS, slot).start()

            return carry

        lax.fori_loop(0, ngrp, seq_step, 0)
        for j in range(STATE_BUFS):
            out_copy(ngrp - STATE_BUFS + j, j).wait()
        perm_f = jnp.where(to_seq, 1.0, 0.0)
        y = jnp.concatenate(
            [part + _dot_exact(perm_f, yoff_s[r0:r0 + Q, :]) * ecol
             for r0, (part, ecol) in zip(range(0, T, Q), parts)], axis=0)
    y_ssd = _rms(y * _silu(proj(C_Z, C_XBC)), sng_ref[...])

    xr = conv(proj(C_LRU, C_LRU_G), prev_lru if prompt else None, None if prompt else c0_lru_ref,
              cw_lru_ref, cb_lru_ref, o_clru_ref)
    xr_b = xr.astype(bf16)
    hl = LRU_DIM // 2
    gates = jnp.concatenate([_dot(xr_b[:, hl * (k % 2):hl * (k % 2 + 1)], lru_w_ref[k]) for k in range(4)],
                            axis=1) + lru_b_ref[...]
    r_gate = jax.nn.sigmoid(gates[:, :LRU_DIM])
    i_gate = jax.nn.sigmoid(gates[:, LRU_DIM:])
    log_a = -LRU_C * r_gate * jax.nn.softplus(-lam_ref[...])
    a_t = jnp.exp(log_a)
    gain = jnp.sqrt(jnp.maximum(-jnp.tanh(log_a) * (a_t * a_t + 1.0), 0.0))
    a_s[...] = a_t
    b_s[...] = gain * i_gate * xr

    def vrow(ref, r0, k):
        return ref[r0 + SUBLANES * k:r0 + SUBLANES * (k + 1), :]

    def set_vrow(ref, r0, k, v):
        ref[r0 + SUBLANES * k:r0 + SUBLANES * (k + 1), :] = v

    if prompt:
        sub = sub_iota(LRU_DIM)
        carry = lru_c[...]
        for r0 in range(0, T, CHUNK):
            acc_a, acc_h = vrow(a_s, r0, 0), vrow(b_s, r0, 0)
            for k in range(1, SEG):
                a_k = vrow(a_s, r0, k)
                acc_h = a_k * acc_h + vrow(b_s, r0, k)
                acc_a = a_k * acc_a
                set_vrow(a_s, r0, k, acc_a)
                set_vrow(b_s, r0, k, acc_h)
            alpha = jnp.where(sub == 0, 0.0, pltpu.roll(acc_a, 1, 0))
            beta = jnp.where(sub == 0, jnp.broadcast_to(carry, (SUBLANES, LRU_DIM)), pltpu.roll(acc_h, 1, 0))
            for d in (1, 2, 4):
                a_sh = jnp.where(sub >= d, pltpu.roll(alpha, d, 0), 1.0)
                b_sh = jnp.where(sub >= d, pltpu.roll(beta, d, 0), 0.0)
                beta = alpha * b_sh + beta
                alpha = alpha * a_sh
            carry = (acc_a * beta + acc_h)[SUBLANES - 1:SUBLANES, :]
            for k in range(SEG):
                set_vrow(b_s, r0, k, vrow(b_s, r0, k) + vrow(a_s, r0, k) * beta)
        lru_c[...] = carry
        o_lru_ref[...] = carry
    else:
        for s in range(T // Q):
            h = h0_lru_ref[SUBLANES * s:SUBLANES * (s + 1), :]
            for k in range(Q // SUBLANES):
                h = vrow(a_s, Q * s, k) * h + vrow(b_s, Q * s, k)
                set_vrow(b_s, Q * s, k, h)
            o_lru_ref[SUBLANES * s:SUBLANES * (s + 1), :] = h
    y_lru = b_s[...] * _silu(proj(C_LRU_G, C_S5))

    u = proj(C_S5, C_S5_G)
    u_b = u.astype(bf16)
    half = S5_DIM // 2
    for k in range(2):
        uk = u_b[:, half * k:half * (k + 1)]
        bur_s[:, S5_HALF * k:S5_HALF * (k + 1)] = _dot(uk, bb_ref[k])
        bui_s[:, S5_HALF * k:S5_HALF * (k + 1)] = _dot(uk, bb_ref[2 + k])

    def tab(r0):
        return tab_ref[0, r0:r0 + SUBLANES, :], tab_ref[1, r0:r0 + SUBLANES, :]

    def cmul_add(pr, pi, xr, xi, yr, yi):
        return pr * xr - pi * xi + yr, pr * xi + pi * xr + yi

    ar, ai = tab(TAB_A)
    if prompt:
        sub = sub_iota(S5_FLAT)
        c_r, c_i = s5_cr[...], s5_ci[...]
        for r0 in range(0, T, CHUNK):
            hr, hi = vrow(bur_s, r0, 0), vrow(bui_s, r0, 0)
            for k in range(1, SEG):
                hr, hi = cmul_add(ar, ai, hr, hi, vrow(bur_s, r0, k), vrow(bui_s, r0, k))
                set_vrow(bur_s, r0, k, hr)
                set_vrow(bui_s, r0, k, hi)
            er = jnp.where(sub == 0, jnp.broadcast_to(c_r, (SUBLANES, S5_FLAT)), pltpu.roll(hr, 1, 0))
            ei = jnp.where(sub == 0, jnp.broadcast_to(c_i, (SUBLANES, S5_FLAT)), pltpu.roll(hi, 1, 0))
            for t, d in enumerate((1, 2, 4)):
                qr, qi = tab(TAB_Q + t * SUBLANES)
                er, ei = cmul_add(qr, qi, pltpu.roll(er, d, 0), pltpu.roll(ei, d, 0), er, ei)
            sr, si = tab(TAB_ASEG)
            nr, ni = cmul_add(sr, si, er, ei, hr, hi)
            c_r, c_i = nr[SUBLANES - 1:SUBLANES, :], ni[SUBLANES - 1:SUBLANES, :]
            for k in range(SEG):
                pr, pi = tab(TAB_PW + k * SUBLANES)
                vr, vi = cmul_add(pr, pi, er, ei, vrow(bur_s, r0, k), vrow(bui_s, r0, k))
                set_vrow(bur_s, r0, k, vr)
                set_vrow(bui_s, r0, k, vi)
        s5_cr[...] = c_r
        s5_ci[...] = c_i
        o_s5r_ref[...] = c_r
        o_s5i_ref[...] = c_i
    else:
        h0r = h0_s5r_ref[...].reshape(nseq, S5_FLAT)
        h0i = h0_s5i_ref[...].reshape(nseq, S5_FLAT)
        ends_r, ends_i = [], []
        for s in range(T // Q):
            rows = slice(SUBLANES * s, SUBLANES * (s + 1))
            hr, hi = h0r[rows, :], h0i[rows, :]
            for k in range(Q // SUBLANES):
                hr, hi = cmul_add(ar, ai, hr, hi, vrow(bur_s, Q * s, k), vrow(bui_s, Q * s, k))
                set_vrow(bur_s, Q * s, k, hr)
                set_vrow(bui_s, Q * s, k, hi)
            ends_r.append(hr)
            ends_i.append(hi)
        o_s5r_ref[...] = jnp.concatenate(ends_r, axis=0).reshape(nseq, S5_NGROUPS, S5_STATE)
        o_s5i_ref[...] = jnp.concatenate(ends_i, axis=0).reshape(nseq, S5_NGROUPS, S5_STATE)
    ys = []
    for k in range(2):
        hk = jnp.concatenate([bur_s[:, S5_HALF * k:S5_HALF * (k + 1)].astype(bf16),
                              bui_s[:, S5_HALF * k:S5_HALF * (k + 1)].astype(bf16)], axis=1)
        ys.append(_dot(hk, cc_ref[k]))
    ys5 = jnp.concatenate(ys, axis=1) + s5d_ref[...] * u
    ys5 = jax.nn.gelu(ys5)
    ys5 = ys5 * jax.nn.sigmoid(_dot(ys5.astype(bf16), glu_w_ref[...]) + glu_b_ref[...])
    y_s5 = ys5 * _silu(proj(C_S5_G, C_DT))

    ycat = jnp.concatenate([y_ssd.astype(bf16), y_lru.astype(bf16), y_s5.astype(bf16)], axis=1)
    return x + _dot(ycat, w_out_ref[...])


def _prompt_body(final, *refs):
    x_ref = refs[0]
    w = refs[1:1 + N_WEIGHTS]
    y_ref = refs[1 + N_WEIGHTS]
    o = refs[2 + N_WEIGHTS:8 + N_WEIGHTS]
    scr = refs[8 + N_WEIGHTS:]
    h_ssd, prev_ssd, prev_lru = scr[:3]
    lru_c, s5_cr, s5_ci = scr[-3:]

    @pl.when(pl.program_id(1) == 0)
    def _():
        h_ssd[...] = jnp.zeros_like(h_ssd)
        prev_ssd[...] = jnp.zeros_like(prev_ssd)
        prev_lru[...] = jnp.zeros_like(prev_lru)
        lru_c[...] = jnp.zeros_like(lru_c)
        s5_cr[...] = jnp.zeros_like(s5_cr)
        s5_ci[...] = jnp.zeros_like(s5_ci)

    for k in range(TILES_PER_STEP):
        rows = slice(TILE_P * k, TILE_P * (k + 1))
        scr_k = scr[:3] + scr[3 + 4 * k:7 + 4 * k] + scr[3 + 4 * TILES_PER_STEP:]
        out = _layer_math(True, TILE_P, x_ref[rows, :], w, None, o, scr_k)
        if final:
            out = _rms(out, w[-1][...])
        y_ref[rows, :] = out

    @pl.when(pl.program_id(1) == pl.num_programs(1) - 1)
    def _():
        o[0][...] = h_ssd[...].T


def _sample_body(*refs):
    x_ref = refs[0]
    st = refs[1:7]
    w = refs[7:7 + N_WEIGHTS]
    y_ref = refs[7 + N_WEIGHTS]
    o = refs[8 + N_WEIGHTS:14 + N_WEIGHTS]
    x_all = refs[14 + N_WEIGHTS]
    scr = refs[15 + N_WEIGHTS:]
    layer = pl.program_id(0)
    last_layer = layer == pl.num_programs(0) - 1
    r0 = pl.multiple_of(pl.program_id(1) * TILE_S, TILE_S)

    @pl.when(layer == 0)
    def _():
        x_all[pl.ds(r0, TILE_S), :] = x_ref[...]

    out = _layer_math(False, TILE_S, x_all[pl.ds(r0, TILE_S), :], w, st, o, scr)
    x_all[pl.ds(r0, TILE_S), :] = out

    @pl.when(last_layer)
    def _():
        y_ref[...] = _rms(out, w[-1][...])

    @pl.when(jnp.logical_not(last_layer))
    def _():
        y_ref[...] = out


def _prompt_call(layer, final, x, weights):
    T = TILE_P
    nb, seq, _ = x.shape

    def wspec(a):
        nd = a.ndim - 1
        return pl.BlockSpec((None,) + a.shape[1:], lambda b, c: (layer,) + (0,) * nd, pipeline_mode=pl.Buffered(1))

    def st(shape):
        nd = len(shape)
        return pl.BlockSpec((None,) + shape, lambda b, c: (b,) + (0,) * nd)

    step_rows = T * TILES_PER_STEP
    x_spec = pl.BlockSpec((None, step_rows, D_MODEL), lambda b, c: (b, c, 0))
    out_specs = [x_spec, st((SSD_DIM, SSD_STATE)), st((CONV_WIDTH - 1, SSD_CONV_DIM)), st((1, LRU_DIM)),
                 st((CONV_WIDTH - 1, LRU_DIM)), st((1, S5_FLAT)), st((1, S5_FLAT))]
    out_shape = [jax.ShapeDtypeStruct(x.shape, f32),
                 jax.ShapeDtypeStruct((nb, SSD_DIM, SSD_STATE), f32),
                 jax.ShapeDtypeStruct((nb, CONV_WIDTH - 1, SSD_CONV_DIM), f32),
                 jax.ShapeDtypeStruct((nb, 1, LRU_DIM), f32),
                 jax.ShapeDtypeStruct((nb, CONV_WIDTH - 1, LRU_DIM), f32),
                 jax.ShapeDtypeStruct((nb, 1, S5_FLAT), f32),
                 jax.ShapeDtypeStruct((nb, 1, S5_FLAT), f32)]
    scratch = [pltpu.VMEM((SSD_STATE, SSD_DIM), f32),
               pltpu.VMEM((HALO, SSD_CONV_DIM), f32),
               pltpu.VMEM((HALO, LRU_DIM), f32)]
    scratch += [pltpu.VMEM((T, LRU_DIM), f32), pltpu.VMEM((T, LRU_DIM), f32),
                pltpu.VMEM((T, S5_FLAT), f32), pltpu.VMEM((T, S5_FLAT), f32)] * TILES_PER_STEP
    scratch += [pltpu.VMEM((1, LRU_DIM), f32), pltpu.VMEM((1, S5_FLAT), f32), pltpu.VMEM((1, S5_FLAT), f32)]
    return pl.pallas_call(
        functools.partial(_prompt_body, final),
        grid=(nb, seq // step_rows), in_specs=[x_spec] + [wspec(a) for a in weights],
        out_specs=out_specs, out_shape=out_shape, scratch_shapes=scratch,
        compiler_params=pltpu.CompilerParams(dimension_semantics=("arbitrary", "arbitrary"),
                                             vmem_limit_bytes=VMEM_LIMIT_BYTES),
        name="layer_prompt",
    )(x, *weights)


def _sample_call(x, states, weights):
    T = TILE_S
    rows = x.shape[0]
    depth = weights[0].shape[0]
    nseq = T // SEQ_S

    def wspec(a):
        nd = a.ndim - 1
        return pl.BlockSpec((None,) + a.shape[1:], lambda l, i: (l,) + (0,) * nd, pipeline_mode=pl.Buffered(1))

    def st(a):
        nd = a.ndim - 2
        if a.ndim == 4 and a.shape[2:] == (SSD_DIM, SSD_STATE):
            return pl.BlockSpec(memory_space=pl.ANY)
        if a.ndim == 6:
            return pl.BlockSpec((None, None) + a.shape[2:], lambda l, i: (l, i) + (0,) * nd)
        return pl.BlockSpec((None, nseq) + a.shape[2:], lambda l, i: (l, i) + (0,) * nd)

    x_spec = pl.BlockSpec((T, D_MODEL), lambda l, i: (i, 0))
    st_specs = [st(a) for a in states]
    scratch = [pltpu.VMEM((rows, D_MODEL), f32),
               pltpu.VMEM((T, LRU_DIM), f32), pltpu.VMEM((T, LRU_DIM), f32),
               pltpu.VMEM((T, S5_FLAT), f32), pltpu.VMEM((T, S5_FLAT), f32),
               pltpu.VMEM((T, SSD_BC), f32), pltpu.VMEM((T, SSD_BC), f32),
               pltpu.VMEM((T, SSD_DIM), f32), pltpu.VMEM((T, SSD_DIM), f32),
               pltpu.VMEM((T, LANES), f32),
               pltpu.VMEM((STATE_BUFS, STATE_SEQS, SSD_DIM, SSD_STATE), f32),
               pltpu.VMEM((STATE_BUFS, STATE_SEQS, SSD_DIM, SSD_STATE), f32),
               pltpu.SemaphoreType.DMA((STATE_BUFS,)), pltpu.SemaphoreType.DMA((STATE_BUFS,))]
    return pl.pallas_call(
        _sample_body,
        grid=(depth, rows // T), in_specs=[x_spec] + st_specs + [wspec(a) for a in weights],
        out_specs=[pl.BlockSpec((None, T, D_MODEL), lambda l, i: (l, i, 0))] + st_specs,
        out_shape=[jax.ShapeDtypeStruct((depth,) + x.shape, f32)] + [jax.ShapeDtypeStruct(a.shape, f32) for a in states],
        scratch_shapes=scratch,
        compiler_params=pltpu.CompilerParams(dimension_semantics=("arbitrary", "arbitrary"),
                                             vmem_limit_bytes=VMEM_LIMIT_BYTES),
        name="layers_sample",
    )(x, *states, *weights)


def _pad_lanes(v):
    return jnp.pad(v, [(0, 0)] * (v.ndim - 1) + [(0, LANES - v.shape[-1])])


def kernel(x_prompt, x_sample, state_ssd, state_ssd_conv, state_lru, state_lru_conv, state_s5_re, state_s5_im, norm_g, w_in, ssd_conv_w, ssd_conv_b, ssd_dt_bias, ssd_a_log, ssd_d, ssd_norm_g, lru_conv_w, lru_conv_b, lru_wa, lru_ba, lru_wx, lru_bx, lru_lambda, s5_lambda_re, s5_lambda_im, s5_log_dt, s5_b_re, s5_b_im, s5_c_re, s5_c_im, s5_d, s5_glu_w, s5_glu_b, w_out, final_norm_g):
    depth = w_in.shape[0]
    nbp = x_prompt.shape[0]
    nbs, ls, _ = x_sample.shape
    assert ls == SEQ_S and x_prompt.shape[1] % (TILE_P * TILES_PER_STEP) == 0 and (nbs * ls) % TILE_S == 0

    tab, bb, cc, lru_w = _s5_prep(s5_lambda_re.astype(f32), s5_lambda_im.astype(f32), s5_log_dt.astype(f32),
                                  s5_b_re.astype(f32), s5_b_im.astype(f32), s5_c_re.astype(f32),
                                  s5_c_im.astype(f32), lru_wa.astype(f32), lru_wx.astype(f32))

    def row(v, n):
        return v.astype(f32).reshape(depth, 1, n)

    wi = w_in.astype(bf16)
    w_in_r = jnp.concatenate([wi[..., 0:3072], wi[..., 3088:5136], _pad_lanes(wi[..., 3072:3088])], axis=-1)

    weights = (
        row(norm_g, D_MODEL), w_in_r,
        ssd_conv_w.astype(f32), row(ssd_conv_b, SSD_CONV_DIM),
        _pad_lanes(row(ssd_dt_bias, SSD_HEADS)), _pad_lanes(row(ssd_a_log, SSD_HEADS)),
        jnp.repeat(ssd_d.astype(f32), SSD_HEADDIM, axis=-1).reshape(depth, 1, SSD_DIM),
        row(ssd_norm_g, SSD_DIM),
        lru_conv_w.astype(f32), row(lru_conv_b, LRU_DIM),
        lru_w,
        jnp.concatenate([lru_ba, lru_bx], axis=-1).astype(f32).reshape(depth, 1, 2 * LRU_DIM),
        row(lru_lambda, LRU_DIM),
        tab, bb, cc, row(s5_d, S5_DIM),
        s5_glu_w.astype(bf16), row(s5_glu_b, S5_DIM),
        w_out.astype(bf16),
        jnp.broadcast_to(final_norm_g.astype(f32).reshape(1, 1, D_MODEL), (depth, 1, D_MODEL)),
    )

    seq_p = x_prompt.shape[1]
    xp = jnp.swapaxes(x_prompt.astype(f32).reshape(nbp, seq_p // CHUNK, SUBLANES, SEG, D_MODEL), 2, 3)
    xp = xp.reshape(nbp, seq_p, D_MODEL)
    outs_p = [[] for _ in range(6)]
    for i in range(depth):
        res = _prompt_call(i, i == depth - 1, xp, weights)
        xp = res[0]
        for j in range(6):
            outs_p[j].append(res[1 + j])
    y_prompt = jnp.swapaxes(xp.reshape(nbp, seq_p // CHUNK, SEG, SUBLANES, D_MODEL), 2, 3).reshape(nbp, seq_p, D_MODEL)

    ntile = nbs // SUBLANES
    nsub = TILE_S // SUBTILE_S

    def conv_in(v):
        v = jnp.swapaxes(v.astype(f32).reshape(depth, ntile, SUBLANES, CONV_WIDTH - 1, v.shape[-1]), 2, 3)
        return v.reshape(depth, ntile // nsub, nsub, CONV_WIDTH - 1, SUBLANES, v.shape[-1])

    def conv_out(v, dtype):
        v = v.reshape(depth, ntile, CONV_WIDTH - 1, SUBLANES, v.shape[-1])
        return jnp.swapaxes(v, 2, 3).reshape(depth, nbs, CONV_WIDTH - 1, v.shape[-1]).astype(dtype)

    xs = jnp.swapaxes(x_sample.astype(f32).reshape(ntile, SUBLANES, ls, D_MODEL), 1, 2).reshape(nbs * ls, D_MODEL)
    states_s = (state_ssd.astype(f32).reshape(depth, nbs, SSD_DIM, SSD_STATE), conv_in(state_ssd_conv),
                state_lru.astype(f32), conv_in(state_lru_conv),
                state_s5_re.astype(f32), state_s5_im.astype(f32))
    weights_s = tuple(w[:, :, TAB_A:TAB_A + SUBLANES] if w is tab else w for w in weights)
    res_s = _sample_call(xs, states_s, weights_s)
    y_sample = jnp.swapaxes(res_s[0][depth - 1].reshape(ntile, ls, SUBLANES, D_MODEL), 1, 2).reshape(nbs, ls, D_MODEL)

    def stack(lst, shape, dtype):
        return jnp.stack(lst).reshape((depth,) + shape).astype(dtype)

    ssd_shape = (SSD_HEADS, SSD_HEADDIM, SSD_STATE)
    s5_shape = (S5_NGROUPS, S5_STATE)
    return (
        y_prompt.astype(x_prompt.dtype), y_sample.astype(x_sample.dtype),
        stack(outs_p[0], (nbp,) + ssd_shape, state_ssd.dtype),
        res_s[1].reshape((depth, nbs) + ssd_shape).astype(state_ssd.dtype),
        stack(outs_p[1], (nbp, CONV_WIDTH - 1, SSD_CONV_DIM), state_ssd_conv.dtype),
        conv_out(res_s[2], state_ssd_conv.dtype),
        stack(outs_p[2], (nbp, LRU_DIM), state_lru.dtype), res_s[3].astype(state_lru.dtype),
        stack(outs_p[3], (nbp, CONV_WIDTH - 1, LRU_DIM), state_lru_conv.dtype),
        conv_out(res_s[4], state_lru_conv.dtype),
        stack(outs_p[4], (nbp,) + s5_shape, state_s5_re.dtype),
        res_s[5].reshape((depth, nbs) + s5_shape).astype(state_s5_re.dtype),
        stack(outs_p[5], (nbp,) + s5_shape, state_s5_im.dtype),
        res_s[6].reshape((depth, nbs) + s5_shape).astype(state_s5_im.dtype),
    )
```
